```python
import jax, jax.numpy as jnp
from jax import lax
import numpy as np

D_MODEL = 1024
BATCH = 2
SEQ = 8192
DEPTH = 1

GRID_W = 64
NA_HEADS = 8
NA_HEAD_DIM = 64
NA_WIN_ROWS_MAX = 8
NA_WIN_COLS = 16
NA_WIDTH = NA_HEADS * NA_HEAD_DIM
MLA_HEADS = 8
MLA_Q_LORA = 384
MLA_KV_LORA = 256
MLA_NOPE = 64
MLA_ROPE = 32
MLA_V = 64
MLA_WIDTH = MLA_HEADS * MLA_V
ROPE_THETA = 100.0
Q_BLOCK = 128
IN_COLS = 3 * NA_WIDTH + MLA_Q_LORA + MLA_KV_LORA + MLA_ROPE + 2 * D_MODEL
N_EXPERTS = 32
TOP_K = 4
D_FF = 1024
SWIGLU_ALPHA = 1.702
SWIGLU_LIMIT = 7.0
MOE_BLOCK = 256
RMS_EPS = 1e-6

kernel_name = 'hybrid_na_mla_moe_block'


def rms_norm(x, g):
    xf = x.astype(jnp.float32)
    y = xf * lax.rsqrt(jnp.mean(xf * xf, axis=-1, keepdims=True) + RMS_EPS)
    return (y * g.astype(jnp.float32)).astype(x.dtype)


def rope_2d(x, rows, cols):
    half = MLA_ROPE // 2
    nf = half // 2
    inv = ROPE_THETA ** (-jnp.arange(nf, dtype=jnp.float32) / nf)

    def rot(xa, pos):
        ang = pos.astype(jnp.float32)[:, None] * inv
        cos, sin = jnp.cos(ang), jnp.sin(ang)
        xa = xa.astype(jnp.float32)
        x1, x2 = xa[..., :nf], xa[..., nf:]
        return jnp.concatenate([x1 * cos - x2 * sin, x1 * sin + x2 * cos], axis=-1)

    out = jnp.concatenate([rot(x[..., :half], rows), rot(x[..., half:], cols)], axis=-1)
    return out.astype(x.dtype)


def neighborhood_attention(q, k, v, rpb):
    B, H, S, Dh = q.shape
    rows = S // GRID_W
    kh = min(NA_WIN_ROWS_MAX, rows)
    kw = NA_WIN_COLS
    r = jnp.arange(rows)
    cidx = jnp.arange(GRID_W)
    row_start = jnp.clip(r - kh // 2, 0, rows - kh)
    col_start = jnp.clip(cidx - kw // 2, 0, GRID_W - kw)
    key_rows = row_start[:, None] + jnp.arange(kh)[None, :]
    qg = q.reshape(B, H, rows, GRID_W, Dh)
    kg = k.reshape(B, H, rows, GRID_W, Dh)[:, :, key_rows]
    vg = v.reshape(B, H, rows, GRID_W, Dh)[:, :, key_rows]
    s = jnp.einsum('bhrqd,bhrjkd->bhrqjk', qg, kg).astype(jnp.float32) * (Dh ** -0.5)
    dr = key_rows - r[:, None] + (NA_WIN_ROWS_MAX - 1)
    dc = jnp.clip(cidx[None, :] - cidx[:, None], -(kw - 1), kw - 1) + (kw - 1)
    bias = rpb.astype(jnp.float32)[:, dr][..., dc]
    s = s + bias.transpose(0, 1, 3, 2, 4)[None]
    col_in = (cidx[None, :] >= col_start[:, None]) & (cidx[None, :] < col_start[:, None] + kw)
    s = jnp.where(col_in[None, None, None, :, None, :], s, -1e30)
    p = jax.nn.softmax(s.reshape(B, H, rows, GRID_W, kh * GRID_W), axis=-1)
    p = p.reshape(B, H, rows, GRID_W, kh, GRID_W).astype(v.dtype)
    o = jnp.einsum('bhrqjk,bhrjkd->bhrqd', p, vg)
    return o.reshape(B, H, S, Dh)


def mla_attention(cq_lat, ckv_lat, k_rope, q_norm_g, kv_norm_g, w_uq, w_ukv, rows, cols):
    B, S, _ = cq_lat.shape
    c_q = rms_norm(cq_lat, q_norm_g)
    q = jnp.einsum('bsr,rhd->bhsd', c_q, w_uq)
    q_nope = q[..., :MLA_NOPE]
    q_rope = rope_2d(q[..., MLA_NOPE:], rows, cols)
    c_kv = rms_norm(ckv_lat, kv_norm_g)
    kv = jnp.einsum('bsr,rhd->bhsd', c_kv, w_ukv)
    k_nope, v = kv[..., :MLA_NOPE], kv[..., MLA_NOPE:]
    k_r = rope_2d(k_rope, rows, cols)
    scale = (MLA_NOPE + MLA_ROPE) ** -0.5
    nb = S // Q_BLOCK
    qn_b = q_nope.reshape(B, MLA_HEADS, nb, Q_BLOCK, MLA_NOPE).transpose(2, 0, 1, 3, 4)
    qr_b = q_rope.reshape(B, MLA_HEADS, nb, Q_BLOCK, MLA_ROPE).transpose(2, 0, 1, 3, 4)

    def block(args):
        qn, qr = args
        s = jnp.einsum('bhqd,bhkd->bhqk', qn, k_nope) + jnp.einsum('bhqd,bkd->bhqk', qr, k_r)
        p = jax.nn.softmax(s.astype(jnp.float32) * scale, axis=-1).astype(v.dtype)
        return jnp.einsum('bhqk,bhkd->bhqd', p, v)

    o = lax.map(block, (qn_b, qr_b))
    o = o.transpose(1, 0, 3, 2, 4).reshape(B, S, MLA_WIDTH)
    return o


def clamped_swiglu(hcat):
    x_glu = jnp.minimum(hcat[..., ::2], SWIGLU_LIMIT)
    x_lin = jnp.clip(hcat[..., 1::2], -SWIGLU_LIMIT, SWIGLU_LIMIT)
    return x_glu * jax.nn.sigmoid(SWIGLU_ALPHA * x_glu) * (x_lin + 1)


def moe_ffn(h, w_router, b_router, w1, b1, w2, b2):
    B, S, D = h.shape
    T = B * S
    xf = h.reshape(T, D)
    logits = (xf @ w_router + b_router).astype(jnp.float32)
    top_vals, top_idx = lax.top_k(logits, TOP_K)
    gates = jax.nn.softmax(top_vals, axis=-1)
    n_items = T * TOP_K
    flat_e = top_idx.reshape(-1)
    flat_tok = jnp.arange(n_items, dtype=jnp.int32) // TOP_K
    flat_g = gates.reshape(-1)
    order = jnp.argsort(flat_e)
    e_sorted = flat_e[order]
    counts = jnp.bincount(flat_e, length=N_EXPERTS)
    padded = ((counts + MOE_BLOCK - 1) // MOE_BLOCK) * MOE_BLOCK
    start = jnp.cumsum(counts) - counts
    pend = jnp.cumsum(padded)
    pstart = pend - padded
    dest = pstart[e_sorted] + (jnp.arange(n_items, dtype=jnp.int32) - start[e_sorted])
    nblk = -(-n_items // MOE_BLOCK) + N_EXPERTS
    cap = nblk * MOE_BLOCK
    buf_tok = jnp.zeros((cap,), jnp.int32).at[dest].set(flat_tok[order])
    buf_g = jnp.zeros((cap,), jnp.float32).at[dest].set(flat_g[order])
    blk_e = jnp.minimum(jnp.searchsorted(pend, jnp.arange(nblk) * MOE_BLOCK, side='right'), N_EXPERTS - 1)
    xb = xf[buf_tok].reshape(nblk, MOE_BLOCK, D)

    def expert_block(args):
        xblk, e = args
        hcat = xblk @ w1[e] + b1[e]
        return clamped_swiglu(hcat) @ w2[e] + b2[e]

    yb = lax.map(expert_block, (xb, blk_e)).reshape(cap, D)
    out = jnp.zeros((T, D), h.dtype).at[buf_tok].add(yb * buf_g[:, None].astype(yb.dtype))
    return out.reshape(B, S, D)


def setup_inputs(seed: int = 0) -> dict:
    key = jax.random.key(seed)
    ks = jax.random.split(key, 24)
    L, D, E = DEPTH, D_MODEL, N_EXPERTS
    nrm = lambda k, shape, s: jax.random.normal(k, shape, jnp.float32) * s
    gain = lambda k: 1.0 + nrm(k, (L, D), 0.02)
    return {
        'x': nrm(ks[0], (BATCH, SEQ, D), 1.0),
        'c': nrm(ks[1], (BATCH, D), 1.0),
        'w_ada': nrm(ks[2], (L, D, 6 * D), D ** -0.5),
        'b_ada': nrm(ks[3], (L, 6 * D), 0.02),
        'g_attn_pre': gain(ks[4]),
        'g_attn_post': gain(ks[5]),
        'w_in': nrm(ks[6], (L, D, IN_COLS), D ** -0.5),
        'b_gate': nrm(ks[7], (L, 2 * D), 0.02),
        'na_rpb': nrm(ks[8], (L, NA_HEADS, 2 * NA_WIN_ROWS_MAX - 1, 2 * NA_WIN_COLS - 1), 0.02),
        'q_norm_g': 1.0 + nrm(ks[9], (L, MLA_Q_LORA), 0.02),
        'kv_norm_g': 1.0 + nrm(ks[10], (L, MLA_KV_LORA), 0.02),
        'w_uq': nrm(ks[11], (L, MLA_Q_LORA, MLA_HEADS, MLA_NOPE + MLA_ROPE), MLA_Q_LORA ** -0.5),
        'w_ukv': nrm(ks[12], (L, MLA_KV_LORA, MLA_HEADS, MLA_NOPE + MLA_V), MLA_KV_LORA ** -0.5),
        'w_na_up': nrm(ks[13], (L, NA_WIDTH, D), NA_WIDTH ** -0.5),
        'w_mla_up': nrm(ks[14], (L, MLA_WIDTH, D), MLA_WIDTH ** -0.5),
        'w_out': nrm(ks[15], (L, D, D), D ** -0.5),
        'g_ffn_pre': gain(ks[16]),
        'g_ffn_post': gain(ks[17]),
        'w_router': nrm(ks[18], (L, D, E), D ** -0.5),
        'b_router': nrm(ks[19], (L, E), 0.01),
        'w1': nrm(ks[20], (L, E, D, 2 * D_FF), D ** -0.5),
        'b1': nrm(ks[21], (L, E, 2 * D_FF), 0.01),
        'w2': nrm(ks[22], (L, E, D_FF, D), D_FF ** -0.5),
        'b2': nrm(ks[23], (L, E, D), 0.01),
    }


def reference(x, c, w_ada, b_ada, g_attn_pre, g_attn_post, w_in, b_gate, na_rpb, q_norm_g, kv_norm_g,
              w_uq, w_ukv, w_na_up, w_mla_up, w_out, g_ffn_pre, g_ffn_post, w_router, b_router,
              w1, b1, w2, b2):
    B, S, D = x.shape
    t = jnp.arange(S, dtype=jnp.int32)
    rows, cols = t // GRID_W, t % GRID_W
    split_at = np.cumsum([3 * NA_WIDTH, MLA_Q_LORA, MLA_KV_LORA, MLA_ROPE, D_MODEL]).tolist()
    sc = jax.nn.silu(c)
    for l in range(DEPTH):
        mod = sc @ w_ada[l] + b_ada[l]
        shift_a, scale_a, gate_a, shift_f, scale_f, gate_f = [m[:, None, :] for m in jnp.split(mod, 6, axis=-1)]
        h = rms_norm(x, g_attn_pre[l]) * (1 + scale_a) + shift_a
        proj = h @ w_in[l]
        na_qkv, cq_lat, ckv_lat, k_rope, g_na, g_mla = jnp.split(proj, split_at, axis=-1)
        na_qkv = na_qkv.reshape(B, S, 3, NA_HEADS, NA_HEAD_DIM).transpose(2, 0, 3, 1, 4)
        o_na = neighborhood_attention(na_qkv[0], na_qkv[1], na_qkv[2], na_rpb[l])
        o_na = o_na.transpose(0, 2, 1, 3).reshape(B, S, NA_WIDTH)
        o_mla = mla_attention(cq_lat, ckv_lat, k_rope, q_norm_g[l], kv_norm_g[l], w_uq[l], w_ukv[l], rows, cols)
        gates = jax.nn.sigmoid(jnp.concatenate([g_na, g_mla], axis=-1) + b_gate[l])
        merged = gates[..., :D_MODEL] * (o_na @ w_na_up[l]) + gates[..., D_MODEL:] * (o_mla @ w_mla_up[l])
        y = merged @ w_out[l]
        x = x + gate_a * rms_norm(y, g_attn_post[l])
        h = rms_norm(x, g_ffn_pre[l]) * (1 + scale_f) + shift_f
        y = moe_ffn(h, w_router[l], b_router[l], w1[l], b1[l], w2[l], b2[l])
        x = x + gate_f * rms_norm(y, g_ffn_post[l])
    return x
```

```python
import functools
import math

import numpy as np
import jax
import jax.numpy as jnp
from jax import lax
from jax.experimental import pallas as pl
from jax.experimental.pallas import tpu as pltpu

GRID_W = 64
NA_HEADS = 8
NA_HEAD_DIM = 64
NA_WIN_ROWS = 8
NA_WIN_COLS = 16
NA_WIDTH = NA_HEADS * NA_HEAD_DIM
MLA_HEADS = 8
MLA_Q_LORA = 384
MLA_KV_LORA = 256
MLA_NOPE = 64
MLA_ROPE = 32
MLA_V = 64
MLA_WIDTH = MLA_HEADS * MLA_V
ROPE_THETA = 100.0
TOP_K = 4
SWIGLU_ALPHA = 1.702
SWIGLU_LIMIT = 7.0
MOE_BLOCK = 256
RMS_EPS = 1e-6
NEG_BIG = -1e30

LANES = 128
VMEM_LIMIT_BYTES = 56 * 1024 * 1024

Q_LORA_PAD = 512
COL_GATES = 0
NA_GROUP_ROWS = 8
NA_KEY_ROWS = 16


def _f32(x):
    return x.astype(jnp.float32)


def _bf16(x):
    return x.astype(jnp.bfloat16)


def _dot(a, b):
    return jnp.dot(a, b, preferred_element_type=jnp.float32)


def _dot_nt(a, b):
    return lax.dot_general(a, b, (((1,), (1,)), ((), ())), preferred_element_type=jnp.float32)


def _rms(x, g, n=None):
    n = x.shape[-1] if n is None else n
    ms = jnp.sum(x * x, axis=-1, keepdims=True) * (1.0 / n)
    return x * lax.rsqrt(ms + RMS_EPS) * g


def _cparams(sem):
    return pltpu.CompilerParams(dimension_semantics=sem, vmem_limit_bytes=VMEM_LIMIT_BYTES)


def _ada_kernel(c_ref, w_ref, b_ref, o_ref):
    c = c_ref[...]
    sc = c * jax.nn.sigmoid(c)
    o_ref[...] = _dot(_bf16(sc), _bf16(w_ref[...])) + b_ref[...]


def _ada_mod(c, w_ada, b_ada):
    B, D = c.shape
    n_out = w_ada.shape[1]
    return pl.pallas_call(
        _ada_kernel,
        out_shape=jax.ShapeDtypeStruct((B, n_out), jnp.float32),
        grid=(n_out // D,),
        in_specs=[
            pl.BlockSpec((B, D), lambda j: (0, 0)),
            pl.BlockSpec((D, D), lambda j: (0, j)),
            pl.BlockSpec((1, D), lambda j: (0, j)),
        ],
        out_specs=pl.BlockSpec((B, D), lambda j: (0, j)),
        compiler_params=_cparams(("arbitrary",)),
        name="ada_mod",
    )(c, w_ada, b_ada.reshape(1, n_out))


def _in_proj_kernel(x_ref, mod_ref, g_ref, w_ref, bg_ref, o_ref, *, d_model, n_gate, chunk):
    x = x_ref[...]
    shift = mod_ref[0, :, 0:d_model]
    scale = mod_ref[0, :, d_model:2 * d_model]
    h = _rms(x, g_ref[...]) * (1.0 + scale) + shift
    hb = _bf16(h)
    n_total = w_ref.shape[1]
    for c0 in range(0, n_total, chunk):
        c1 = min(c0 + chunk, n_total)
        acc = _dot(hb, w_ref[:, c0:c1])
        if c0 < n_gate:
            acc = jax.nn.sigmoid(acc + bg_ref[:, c0:c1])
        o_ref[:, c0:c1] = _bf16(acc)


def _in_proj(x2, mod3, g_pre, w_ext, b_gate, seq, tm=512):
    T, D = x2.shape
    n_total = w_ext.shape[1]
    n_gate = b_gate.shape[-1]
    kern = functools.partial(_in_proj_kernel, d_model=D, n_gate=n_gate, chunk=512)
    return pl.pallas_call(
        kern,
        out_shape=jax.ShapeDtypeStruct((T, n_total), jnp.bfloat16),
        grid=(T // tm,),
        in_specs=[
            pl.BlockSpec((tm, D), lambda i: (i, 0)),
            pl.BlockSpec((1, 1, mod3.shape[-1]), lambda i: ((i * tm) // seq, 0, 0)),
            pl.BlockSpec((1, D), lambda i: (0, 0)),
            pl.BlockSpec((D, n_total), lambda i: (0, 0)),
            pl.BlockSpec((1, n_gate), lambda i: (0, 0)),
        ],
        out_specs=pl.BlockSpec((tm, n_total), lambda i: (i, 0)),
        compiler_params=_cparams(("arbitrary",)),
        name="in_proj",
    )(x2, mod3, g_pre.reshape(1, D), w_ext, b_gate.reshape(1, n_gate))


def _na_kernel(q_ref, k_ref, v_ref, bias_ref, o_ref, *, n_rows):
    g = pl.program_id(2)
    kb = jnp.clip(g * NA_GROUP_ROWS - NA_WIN_ROWS // 2, 0, n_rows - NA_KEY_ROWS) * GRID_W
    kb = pl.multiple_of(kb, 256)
    nk = NA_KEY_ROWS * GRID_W
    q = q_ref[...]
    k = k_ref[pl.ds(kb, nk), :]
    v = v_ref[pl.ds(kb, nk), :]
    lane = lax.broadcasted_iota(jnp.int32, (1, LANES), 1)
    scale = NA_HEAD_DIM ** -0.5
    outs = []
    for h in range(2):
        in_head = (lane >= h * NA_HEAD_DIM) & (lane < (h + 1) * NA_HEAD_DIM)
        qh = jnp.where(in_head, q * scale, 0).astype(jnp.bfloat16)
        s = _dot_nt(qh, k) + bias_ref[0, h]
        m = jnp.max(s, axis=-1, keepdims=True)
        p = jnp.exp(s - m)
        l = jnp.sum(p, axis=-1, keepdims=True)
        pv = _dot(_bf16(p), v)
        outs.append(pv / l)
    o_ref[...] = _bf16(jnp.where(lane < NA_HEAD_DIM, outs[0], outs[1]))


def _na_attention(proj, bias_tab, batch, seq, col_q, col_k, col_v):
    T = proj.shape[0]
    n_rows = seq // GRID_W
    n_groups = n_rows // NA_GROUP_ROWS
    tq = NA_GROUP_ROWS * GRID_W
    n_pairs = NA_HEADS // 2

    def cls(g):
        return jnp.where(g == 0, 0, jnp.where(g == n_groups - 1, 2, 1))

    kern = functools.partial(_na_kernel, n_rows=n_rows)
    return pl.pallas_call(
        kern,
        out_shape=jax.ShapeDtypeStruct((T, NA_WIDTH), jnp.bfloat16),
        grid=(n_pairs, batch, n_groups),
        in_specs=[
            pl.BlockSpec((tq, LANES), lambda hp, b, g: (b * n_groups + g, col_q // LANES + hp)),
            pl.BlockSpec((seq, LANES), lambda hp, b, g: (b, col_k // LANES + hp)),
            pl.BlockSpec((seq, LANES), lambda hp, b, g: (b, col_v // LANES + hp)),
            pl.BlockSpec((1, 2, tq, NA_KEY_ROWS * GRID_W), lambda hp, b, g: (cls(g), hp, 0, 0)),
        ],
        out_specs=pl.BlockSpec((tq, LANES), lambda hp, b, g: (b * n_groups + g, hp)),
        compiler_params=_cparams(("arbitrary", "arbitrary", "arbitrary")),
        name="na_attn",
    )(proj, proj, proj, bias_tab)


def _na_bias_tables(rpb, n_rows):
    H = rpb.shape[0]
    kw = NA_WIN_COLS
    W = GRID_W
    pad = W - kw
    rp = jnp.pad(rpb.astype(jnp.float32), ((0, 0), (0, 0), (pad, pad)), constant_values=NEG_BIG)
    toep = jnp.stack([rp[:, :, W - 1 - qc: 2 * W - 1 - qc] for qc in range(W)], axis=2)
    cidx = np.arange(W)
    col_start = np.clip(cidx - kw // 2, 0, W - kw)
    col_in = (cidx[None, :] >= col_start[:, None]) & (cidx[None, :] < col_start[:, None] + kw)
    toep = jnp.where(jnp.asarray(col_in)[None, None], toep, NEG_BIG)
    neg_blk = jnp.full((H, W, W), NEG_BIG, jnp.float32)
    tabs = []
    for c in range(3):
        rows_i = []
        for i in range(NA_GROUP_ROWS):
            if c == 0:
                j0, off = max(i - NA_WIN_ROWS // 2, 0), NA_WIN_ROWS - 1
            elif c == 1:
                j0, off = i, NA_WIN_ROWS - 1 - NA_WIN_ROWS // 2
            else:
                j0, off = min(i + NA_WIN_ROWS // 2, NA_KEY_ROWS - NA_WIN_ROWS), -1
            blks = []
            for j in range(NA_KEY_ROWS):
                if j0 <= j < j0 + NA_WIN_ROWS:
                    blks.append(toep[:, j - i + off])
                else:
                    blks.append(neg_blk)
            rows_i.append(jnp.concatenate(blks, axis=-1))
        tabs.append(jnp.concatenate(rows_i, axis=1))
    return jnp.stack(tabs, axis=0)


def _mla_prep_kernel(cq_ref, ckv_ref, kr_ref, gq_ref, gkv_ref, wqm_ref, wqs_ref, wk_ref, wv_ref,
                     pm_ref, cq_tab_ref, sq_tab_ref, csk_tab_ref, q_out, k_out, v_out, *, qscale):
    cq = _f32(cq_ref[...])
    cqn = _bf16(_rms(cq, gq_ref[...], n=MLA_Q_LORA))
    qm = _dot(cqn, wqm_ref[...])
    qs = _dot(cqn, wqs_ref[...])
    ctab = cq_tab_ref[...]
    stab = sq_tab_ref[...]
    for h in range(MLA_HEADS):
        sl = slice(h * LANES, (h + 1) * LANES)
        q_out[:, sl] = _bf16((qm[:, sl] * ctab + qs[:, sl] * stab) * qscale)

    ckv = _f32(ckv_ref[...])
    ckvn = _bf16(_rms(ckv, gkv_ref[...]))
    kk = _dot(ckvn, wk_ref[...])
    v_out[...] = _bf16(_dot(ckvn, wv_ref[...]))
    krr = _f32(kr_ref[...]) * csk_tab_ref[...]
    kplace = _dot(_bf16(krr), pm_ref[...])
    for h in range(MLA_HEADS):
        sl = slice(h * LANES, (h + 1) * LANES)
        k_out[:, sl] = _bf16(kk[:, sl] + kplace)


def _mla_prep(proj, col_cq, col_ckv, col_kr, gq, gkv, wqm, wqs, wk, wv, pm, cq_tab, sq_tab, csk_tab,
              seq, qscale, tm=512):
    T = proj.shape[0]
    n_s = seq // tm
    full = lambda a: pl.BlockSpec(a.shape, lambda i: (0,) * a.ndim)
    tab = pl.BlockSpec((tm, LANES), lambda i: (i % n_s, 0))
    kern = functools.partial(_mla_prep_kernel, qscale=qscale)
    return pl.pallas_call(
        kern,
        out_shape=(
            jax.ShapeDtypeStruct((T, MLA_HEADS * LANES), jnp.bfloat16),
            jax.ShapeDtypeStruct((T, MLA_HEADS * LANES), jnp.bfloat16),
            jax.ShapeDtypeStruct((T, MLA_WIDTH), jnp.bfloat16),
        ),
        grid=(T // tm,),
        in_specs=[
            pl.BlockSpec((tm, Q_LORA_PAD), lambda i: (i, col_cq // Q_LORA_PAD)),
            pl.BlockSpec((tm, MLA_KV_LORA), lambda i: (i, col_ckv // MLA_KV_LORA)),
            pl.BlockSpec((tm, LANES), lambda i: (i, col_kr // LANES)),
            full(gq), full(gkv), full(wqm), full(wqs), full(wk), full(wv), full(pm),
            tab, tab, tab,
        ],
        out_specs=(
            pl.BlockSpec((tm, MLA_HEADS * LANES), lambda i: (i, 0)),
            pl.BlockSpec((tm, MLA_HEADS * LANES), lambda i: (i, 0)),
            pl.BlockSpec((tm, MLA_WIDTH), lambda i: (i, 0)),
        ),
        compiler_params=_cparams(("arbitrary",)),
        name="mla_prep",
    )(proj, proj, proj, gq, gkv, wqm, wqs, wk, wv, pm, cq_tab, sq_tab, csk_tab)


def _mla_attn_kernel(q_ref, k_ref, v_ref, o_ref, m_scr, l_scr, acc_scr, *, tk):
    seq = k_ref.shape[0]
    tq = q_ref.shape[0]
    m_scr[...] = jnp.full(m_scr.shape, -jnp.inf, jnp.float32)
    l_scr[...] = jnp.zeros(l_scr.shape, jnp.float32)
    acc_scr[...] = jnp.zeros(acc_scr.shape, jnp.float32)

    def body(c, carry):
        k0 = pl.multiple_of(c * tk, tk)
        vc = v_ref[pl.ds(k0, tk), :]
        for h in range(2):
            qh = q_ref[:, h * LANES:(h + 1) * LANES]
            kc = k_ref[pl.ds(k0, tk), h * LANES:(h + 1) * LANES]
            s = _dot_nt(qh, kc)
            m_old = m_scr[h]
            m_new = jnp.maximum(m_old, jnp.max(s, axis=-1, keepdims=True))
            alpha = jnp.exp2(m_old - m_new)
            p = jnp.exp2(s - m_new[:, 0:1])
            l_scr[h] = alpha * l_scr[h] + jnp.sum(p, axis=-1, keepdims=True)
            acc_scr[h] = alpha * acc_scr[h] + _dot(_bf16(p), vc)
            m_scr[h] = m_new
        return carry

    lax.fori_loop(0, seq // tk, body, 0)
    lane = lax.broadcasted_iota(jnp.int32, (1, LANES), 1)
    o0 = acc_scr[0] / l_scr[0]
    o1 = acc_scr[1] / l_scr[1]
    o_ref[...] = _bf16(jnp.where(lane < MLA_V, o0, o1))


def _mla_attention(q_cat, k_cat, v, batch, seq, tq=512, tk=1024):
    T = q_cat.shape[0]
    n_q = seq // tq
    n_pairs = MLA_HEADS // 2
    kern = functools.partial(_mla_attn_kernel, tk=tk)
    return pl.pallas_call(
        kern,
        out_shape=jax.ShapeDtypeStruct((T, MLA_WIDTH), jnp.bfloat16),
        grid=(batch, n_pairs, n_q),
        in_specs=[
            pl.BlockSpec((tq, 2 * LANES), lambda b, hp, i: (b * n_q + i, hp)),
            pl.BlockSpec((seq, 2 * LANES), lambda b, hp, i: (b, hp)),
            pl.BlockSpec((seq, LANES), lambda b, hp, i: (b, hp)),
        ],
        out_specs=pl.BlockSpec((tq, LANES), lambda b, hp, i: (b * n_q + i, hp)),
        scratch_shapes=[
            pltpu.VMEM((2, tq, LANES), jnp.float32),
            pltpu.VMEM((2, tq, LANES), jnp.float32),
            pltpu.VMEM((2, tq, LANES), jnp.float32),
        ],
        compiler_params=_cparams(("arbitrary", "arbitrary", "arbitrary")),
        name="mla_attn",
    )(q_cat, k_cat, v)


def _out_proj_kernel(ona_ref, omla_ref, gate_ref, x_ref, mod_ref, gpost_ref, gpre_ref, wna_ref,
                     wmla_ref, wout_ref, wr_ref, br_ref, x1_ref, h2_ref, lg_ref, *, d_model):
    D = d_model
    gates = gate_ref[...]
    merged = (_f32(gates[:, 0:D]) * _dot(ona_ref[...], wna_ref[...])
              + _f32(gates[:, D:2 * D]) * _dot(omla_ref[...], wmla_ref[...]))
    y = _dot(_bf16(merged), wout_ref[...])
    gate_a = mod_ref[0, :, 2 * D:3 * D]
    shift_f = mod_ref[0, :, 3 * D:4 * D]
    scale_f = mod_ref[0, :, 4 * D:5 * D]
    x1 = x_ref[...] + gate_a * _rms(y, gpost_ref[...])
    x1_ref[...] = x1
    h2 = _rms(x1, gpre_ref[...]) * (1.0 + scale_f) + shift_f
    h2_ref[...] = h2
    lg_ref[...] = _dot(_bf16(h2), wr_ref[...]) + br_ref[...]


def _out_proj(o_na, o_mla, proj, x2, mod3, g_post, g_pre, w_na, w_mla, w_out, w_r, b_r, seq, tm=512):
    T, D = x2.shape
    full = lambda a: pl.BlockSpec(a.shape, lambda i: (0,) * a.ndim)
    row = lambda w: pl.BlockSpec((tm, w), lambda i: (i, 0))
    kern = functools.partial(_out_proj_kernel, d_model=D)
    g_post = g_post.reshape(1, D)
    g_pre = g_pre.reshape(1, D)
    return pl.pallas_call(
        kern,
        out_shape=(
            jax.ShapeDtypeStruct((T, D), jnp.float32),
            jax.ShapeDtypeStruct((T, D), jnp.float32),
            jax.ShapeDtypeStruct((T, LANES), jnp.float32),
        ),
        grid=(T // tm,),
        in_specs=[
            row(NA_WIDTH), row(MLA_WIDTH),
            pl.BlockSpec((tm, 2 * D), lambda i: (i, COL_GATES // (2 * D))),
            row(D),
            pl.BlockSpec((1, 1, mod3.shape[-1]), lambda i: ((i * tm) // seq, 0, 0)),
            full(g_post), full(g_pre), full(w_na), full(w_mla), full(w_out), full(w_r), full(b_r),
        ],
        out_specs=(row(D), row(D), row(LANES)),
        compiler_params=_cparams(("arbitrary",)),
        name="out_proj",
    )(o_na, o_mla, proj, x2, mod3, g_post, g_pre, w_na, w_mla, w_out, w_r, b_r)


def _route_kernel(lg_ref, r_ref, cnt_ref, carry_scr, *, sub):
    @pl.when(pl.program_id(0) == 0)
    def _():
        carry_scr[...] = jnp.zeros(carry_scr.shape, jnp.float32)

    tr = lg_ref.shape[0]
    lane = lax.broadcasted_iota(jnp.int32, (sub, LANES), 1).astype(jnp.float32)
    ri = lax.broadcasted_iota(jnp.int32, (sub, sub), 0)
    ci = lax.broadcasted_iota(jnp.int32, (sub, sub), 1)
    tri = jnp.where(ri >= ci, 1.0, 0.0).astype(jnp.bfloat16)
    for s0 in range(0, tr, sub):
        work = lg_ref[s0:s0 + sub, :]
        sels, vals, idxs = [], [], []
        for _k in range(TOP_K):
            mk = jnp.max(work, axis=-1, keepdims=True)
            ik = jnp.min(jnp.where(work == mk, lane, float(LANES)), axis=-1, keepdims=True)
            sk = lane == ik
            work = jnp.where(sk, -jnp.inf, work)
            sels.append(sk)
            vals.append(mk)
            idxs.append(ik)
        es = [jnp.exp(v - vals[0]) for v in vals]
        denom = es[0] + es[1] + es[2] + es[3]
        onehot = jnp.zeros((sub, LANES), jnp.float32)
        for sk in sels:
            onehot = jnp.where(sk, 1.0, onehot)
        prefix = _dot(tri, _bf16(onehot))
        carry = carry_scr[...]
        rank_mat = carry + prefix - 1.0
        res = jnp.zeros((sub, LANES), jnp.float32)
        for kk in range(TOP_K):
            rank_k = jnp.sum(jnp.where(sels[kk], rank_mat, 0.0), axis=-1, keepdims=True)
            res = jnp.where(lane == kk, idxs[kk], res)
            res = jnp.where(lane == TOP_K + kk, rank_k, res)
            res = jnp.where(lane == 2 * TOP_K + kk, es[kk] / denom, res)
        r_ref[s0:s0 + sub, :] = res
        carry_scr[...] = carry + jnp.sum(onehot, axis=0, keepdims=True)
    cnt_ref[...] = carry_scr[...]


def _route(logits, tr=2048, sub=256):
    T = logits.shape[0]
    kern = functools.partial(_route_kernel, sub=sub)
    return pl.pallas_call(
        kern,
        out_shape=(
            jax.ShapeDtypeStruct((T, LANES), jnp.float32),
            jax.ShapeDtypeStruct((1, LANES), jnp.float32),
        ),
        grid=(T // tr,),
        in_specs=[pl.BlockSpec((tr, LANES), lambda i: (i, 0))],
        out_specs=(
            pl.BlockSpec((tr, LANES), lambda i: (i, 0)),
            pl.BlockSpec((1, LANES), lambda i: (0, 0)),
        ),
        scratch_shapes=[pltpu.VMEM((1, LANES), jnp.float32)],
        compiler_params=_cparams(("arbitrary",)),
        name="route",
    )(logits)


def _dispatch_kernel(dest_ref, h_ref, xs_in_ref, xs_ref, sem):
    del xs_in_ref
    ts = h_ref.shape[0]

    def row_copy(j, d):
        return pltpu.make_async_copy(h_ref.at[pl.ds(j, 1), :], xs_ref.at[pl.ds(d, 1), :], sem)

    def issue(j, carry):
        for kk in range(TOP_K):
            row_copy(j, dest_ref[j * TOP_K + kk]).start()
        return carry

    lax.fori_loop(0, ts, issue, 0)

    def drain(j, carry):
        for _kk in range(TOP_K):
            row_copy(0, 0).wait()
        return carry

    lax.fori_loop(0, ts, drain, 0)


def _dispatch(h2, dest, cap, ts=256):
    T, D = h2.shape
    xs0 = jnp.zeros((cap, D), h2.dtype)
    return pl.pallas_call(
        _dispatch_kernel,
        out_shape=jax.ShapeDtypeStruct((cap, D), h2.dtype),
        grid=(T // ts,),
        in_specs=[
            pl.BlockSpec((ts * TOP_K,), lambda i: (i,), memory_space=pltpu.SMEM),
            pl.BlockSpec((ts, D), lambda i: (i, 0)),
            pl.BlockSpec(memory_space=pl.ANY),
        ],
        out_specs=pl.BlockSpec(memory_space=pl.ANY),
        scratch_shapes=[pltpu.SemaphoreType.DMA(())],
        input_output_aliases={2: 0},
        compiler_params=_cparams(("arbitrary",)),
        name="dispatch",
    )(dest, h2, xs0)


def _moe_kernel(blk_e_ref, nvalid_ref, xs_ref, w1_ref, b1_ref, w2_ref, b2_ref, y_ref, w1p_scr, w2b_scr):
    i = pl.program_id(0)
    d_ff = w2_ref.shape[1]
    n_groups = (2 * d_ff) // (2 * LANES)

    @pl.when(i < nvalid_ref[0])
    def _():
        e = blk_e_ref[i]
        e_prev = blk_e_ref[jnp.maximum(i - 1, 0)]

        @pl.when((i == 0) | (e != e_prev))
        def _():
            r = lax.broadcasted_iota(jnp.int32, (2 * LANES, 2 * LANES), 0)
            c = lax.broadcasted_iota(jnp.int32, (2 * LANES, 2 * LANES), 1)
            src = jnp.where(c < LANES, 2 * c, 2 * (c - LANES) + 1)
            perm = jnp.where(r == src, 1.0, 0.0).astype(jnp.bfloat16)
            for gI in range(n_groups):
                sl = slice(gI * 2 * LANES, (gI + 1) * 2 * LANES)
                w1p_scr[:, sl] = _bf16(_dot(_bf16(w1_ref[0, :, sl]), perm))
            w2b_scr[...] = _bf16(w2_ref[0])

        xb = _bf16(xs_ref[...])
        hcat = _dot(xb, w1p_scr[...]) + b1_ref[0]
        acts = []
        for gI in range(n_groups):
            glu = jnp.minimum(hcat[:, gI * 2 * LANES: gI * 2 * LANES + LANES], SWIGLU_LIMIT)
            lin = jnp.clip(hcat[:, gI * 2 * LANES + LANES: (gI + 1) * 2 * LANES], -SWIGLU_LIMIT, SWIGLU_LIMIT)
            acts.append(_bf16(glu * jax.nn.sigmoid(SWIGLU_ALPHA * glu) * (lin + 1.0)))
        act = jnp.concatenate(acts, axis=-1)
        y_ref[...] = _dot(act, w2b_scr[...]) + b2_ref[0]

    @pl.when(i >= nvalid_ref[0])
    def _():
        y_ref[...] = jnp.zeros(y_ref.shape, y_ref.dtype)


def _moe_ffn(xs, blk_e, nvalid, w1, b1p, w2, b2):
    cap, D = xs.shape
    E, _, F2 = w1.shape
    F = w2.shape[1]
    nblk = cap // MOE_BLOCK

    def xmap(i, be, nv):
        return (jnp.minimum(i, nv[0] - 1), 0)

    grid_spec = pltpu.PrefetchScalarGridSpec(
        num_scalar_prefetch=2,
        grid=(nblk,),
        in_specs=[
            pl.BlockSpec((MOE_BLOCK, D), xmap),
            pl.BlockSpec((1, D, F2), lambda i, be, nv: (be[i], 0, 0)),
            pl.BlockSpec((1, 1, F2), lambda i, be, nv: (be[i], 0, 0)),
            pl.BlockSpec((1, F, D), lambda i, be, nv: (be[i], 0, 0)),
            pl.BlockSpec((1, 1, D), lambda i, be, nv: (be[i], 0, 0)),
        ],
        out_specs=pl.BlockSpec((MOE_BLOCK, D), lambda i, be, nv: (i, 0)),
        scratch_shapes=[
            pltpu.VMEM((D, F2), jnp.bfloat16),
            pltpu.VMEM((F, D), jnp.bfloat16),
        ],
    )
    return pl.pallas_call(
        _moe_kernel,
        out_shape=jax.ShapeDtypeStruct((cap, D), jnp.float32),
        grid_spec=grid_spec,
        compiler_params=_cparams(("arbitrary",)),
        name="moe_ffn",
    )(blk_e, nvalid, xs, w1, b1p, w2, b2)


def _combine_kernel(dest_ref, y_ref, r_ref, x1_ref, mod_ref, g_ref, o_ref, ybuf, sem, *, d_model):
    D = d_model
    tc = x1_ref.shape[0]

    def row_copy(j, kk, d):
        return pltpu.make_async_copy(y_ref.at[pl.ds(d, 1), :], ybuf.at[kk, pl.ds(j, 1), :], sem)

    def issue(j, carry):
        for kk in range(TOP_K):
            row_copy(j, kk, dest_ref[j * TOP_K + kk]).start()
        return carry

    lax.fori_loop(0, tc, issue, 0)

    def drain(j, carry):
        for kk in range(TOP_K):
            row_copy(0, kk, 0).wait()
        return carry

    lax.fori_loop(0, tc, drain, 0)

    r = r_ref[...]
    moe = jnp.zeros((tc, D), jnp.float32)
    for kk in range(TOP_K):
        moe = moe + r[:, 2 * TOP_K + kk: 2 * TOP_K + kk + 1] * ybuf[kk]
    gate_f = mod_ref[0, :, 5 * D:6 * D]
    o_ref[...] = x1_ref[...] + gate_f * _rms(moe, g_ref[...])


def _combine(y, dest, route, x1, mod3, g_post, seq, tc=256):
    T, D = x1.shape
    kern = functools.partial(_combine_kernel, d_model=D)
    return pl.pallas_call(
        kern,
        out_shape=jax.ShapeDtypeStruct((T, D), jnp.float32),
        grid=(T // tc,),
        in_specs=[
            pl.BlockSpec((tc * TOP_K,), lambda i: (i,), memory_space=pltpu.SMEM),
            pl.BlockSpec(memory_space=pl.ANY),
            pl.BlockSpec((tc, LANES), lambda i: (i, 0)),
            pl.BlockSpec((tc, D), lambda i: (i, 0)),
            pl.BlockSpec((1, 1, mod3.shape[-1]), lambda i: ((i * tc) // seq, 0, 0)),
            pl.BlockSpec((1, D), lambda i: (0, 0)),
        ],
        out_specs=pl.BlockSpec((tc, D), lambda i: (i, 0)),
        scratch_shapes=[
            pltpu.VMEM((TOP_K, tc, D), jnp.float32),
            pltpu.SemaphoreType.DMA(()),
        ],
        compiler_params=_cparams(("arbitrary",)),
        name="combine",
    )(dest, y, route, x1, mod3, g_post.reshape(1, D))


def _rope_swap_cols(w):
    nf = MLA_ROPE // 4
    return jnp.concatenate([-w[..., nf:2 * nf], w[..., 0:nf], -w[..., 3 * nf:4 * nf], w[..., 2 * nf:3 * nf]],
                           axis=-1)


def _rope_tables(seq):
    t = jnp.arange(seq, dtype=jnp.int32)
    rows, cols = t // GRID_W, t % GRID_W
    nf = MLA_ROPE // 4
    inv = ROPE_THETA ** (-jnp.arange(nf, dtype=jnp.float32) / nf)
    ar = rows.astype(jnp.float32)[:, None] * inv
    ac = cols.astype(jnp.float32)[:, None] * inv
    c32 = jnp.concatenate([jnp.cos(ar), jnp.cos(ar), jnp.cos(ac), jnp.cos(ac)], axis=-1)
    s32 = jnp.concatenate([jnp.sin(ar), jnp.sin(ar), jnp.sin(ac), jnp.sin(ac)], axis=-1)
    one = jnp.ones((seq, MLA_NOPE), jnp.float32)
    z64 = jnp.zeros((seq, MLA_NOPE), jnp.float32)
    z32 = jnp.zeros((seq, LANES - MLA_NOPE - MLA_ROPE), jnp.float32)
    cq_tab = jnp.concatenate([one, c32, z32], axis=-1)
    sq_tab = jnp.concatenate([z64, s32, z32], axis=-1)
    csk_tab = jnp.concatenate([c32, s32, z64], axis=-1)
    return cq_tab, sq_tab, csk_tab


def kernel(x, c, w_ada, b_ada, g_attn_pre, g_attn_post, w_in, b_gate, na_rpb, q_norm_g, kv_norm_g,
           w_uq, w_ukv, w_na_up, w_mla_up, w_out, g_ffn_pre, g_ffn_post, w_router, b_router,
           w1, b1, w2, b2):
    B, S, D = x.shape
    T = B * S
    depth = w_ada.shape[0]
    E = w_router.shape[-1]
    n_rows = S // GRID_W
    assert S % (NA_GROUP_ROWS * GRID_W) == 0 and n_rows >= NA_KEY_ROWS
    assert E <= LANES and T % 2048 == 0

    bf = jnp.bfloat16
    cq_tab, sq_tab, csk_tab = _rope_tables(S)
    pm_np = np.zeros((LANES, LANES), np.float32)
    for cidx in range(MLA_ROPE):
        pm_np[cidx, MLA_NOPE + cidx] = 1.0
        pm_np[MLA_ROPE + cidx, MLA_NOPE + cidx] = 1.0
    pm = jnp.asarray(pm_np, bf)
    qscale = float((MLA_NOPE + MLA_ROPE) ** -0.5 * math.log2(math.e))

    col_na = 2 * D
    col_cq = col_na + 3 * NA_WIDTH
    col_ckv = col_cq + Q_LORA_PAD
    col_kr = col_ckv + MLA_KV_LORA
    assert col_cq % Q_LORA_PAD == 0 and col_ckv % MLA_KV_LORA == 0

    x2 = x.reshape(T, D)
    for l in range(depth):
        mod = _ada_mod(c, w_ada[l], b_ada[l])
        mod3 = mod.reshape(B, 1, 6 * D)

        o_na, o_cq, o_ckv, o_kr, o_g = np.cumsum([0, 3 * NA_WIDTH, MLA_Q_LORA, MLA_KV_LORA, MLA_ROPE]).tolist()
        wi = w_in[l]
        w_kr = wi[:, o_kr:o_g]
        w_ext = jnp.concatenate([
            wi[:, o_g:o_g + 2 * D],
            wi[:, o_na:o_cq],
            wi[:, o_cq:o_ckv], jnp.zeros((D, Q_LORA_PAD - MLA_Q_LORA), wi.dtype),
            wi[:, o_ckv:o_kr],
            w_kr, _rope_swap_cols(w_kr), jnp.zeros((D, LANES - 2 * MLA_ROPE), wi.dtype),
        ], axis=1).astype(bf)
        proj = _in_proj(x2, mod3, g_attn_pre[l], w_ext, b_gate[l], S)

        bias_tab = _na_bias_tables(na_rpb[l], n_rows)
        o_na_tok = _na_attention(proj, bias_tab, B, S, col_na, col_na + NA_WIDTH, col_na + 2 * NA_WIDTH)

        wq = w_uq[l]
        zq = jnp.zeros((MLA_Q_LORA, MLA_HEADS, LANES - MLA_NOPE - MLA_ROPE), wq.dtype)
        wqm = jnp.concatenate([wq, zq], axis=-1).reshape(MLA_Q_LORA, MLA_HEADS * LANES)
        wqs = jnp.concatenate([jnp.zeros((MLA_Q_LORA, MLA_HEADS, MLA_NOPE), wq.dtype),
                               _rope_swap_cols(wq[..., MLA_NOPE:]), zq], axis=-1
                              ).reshape(MLA_Q_LORA, MLA_HEADS * LANES)
        rpad = ((0, Q_LORA_PAD - MLA_Q_LORA), (0, 0))
        wqm = jnp.pad(wqm, rpad).astype(bf)
        wqs = jnp.pad(wqs, rpad).astype(bf)
        wkv = w_ukv[l]
        wk = jnp.concatenate([wkv[..., :MLA_NOPE], jnp.zeros((MLA_KV_LORA, MLA_HEADS, LANES - MLA_NOPE), wkv.dtype)],
                             axis=-1).reshape(MLA_KV_LORA, MLA_HEADS * LANES).astype(bf)
        wv = wkv[..., MLA_NOPE:].reshape(MLA_KV_LORA, MLA_WIDTH).astype(bf)
        gq = jnp.pad(q_norm_g[l], (0, Q_LORA_PAD - MLA_Q_LORA)).reshape(1, Q_LORA_PAD)
        gkv = kv_norm_g[l].reshape(1, MLA_KV_LORA)
        q_cat, k_cat, v_mla = _mla_prep(proj, col_cq, col_ckv, col_kr, gq, gkv, wqm, wqs, wk, wv, pm,
                                        cq_tab, sq_tab, csk_tab, S, qscale)
        o_mla_tok = _mla_attention(q_cat, k_cat, v_mla, B, S)

        w_r = jnp.pad(w_router[l], ((0, 0), (0, LANES - E))).astype(bf)
        b_r = jnp.concatenate([b_router[l], jnp.full((LANES - E,), NEG_BIG, jnp.float32)]).reshape(1, LANES)
        x1, h2, logits = _out_proj(o_na_tok, o_mla_tok, proj, x2, mod3, g_attn_post[l], g_ffn_pre[l],
                                   w_na_up[l].astype(bf), w_mla_up[l].astype(bf), w_out[l].astype(bf),
                                   w_r, b_r, S)

        route, counts = _route(logits)
        e_idx = route[:, 0:TOP_K].astype(jnp.int32)
        rank = route[:, TOP_K:2 * TOP_K].astype(jnp.int32)
        cnt = counts[0, :E].astype(jnp.int32)
        padded = ((cnt + MOE_BLOCK - 1) // MOE_BLOCK) * MOE_BLOCK
        pend = jnp.cumsum(padded)
        pstart = pend - padded
        n_items = T * TOP_K
        nblk = -(-n_items // MOE_BLOCK) + E
        cap = nblk * MOE_BLOCK
        dest = (rank + jnp.sum(jnp.where(e_idx[..., None] == jnp.arange(E, dtype=jnp.int32), pstart, 0), axis=-1)
                ).reshape(n_items)
        blk_off = jnp.arange(nblk, dtype=jnp.int32) * MOE_BLOCK
        blk_e = jnp.minimum(jnp.sum((pend[None, :] <= blk_off[:, None]).astype(jnp.int32), axis=-1), E - 1)
        nvalid = (pend[-1:] // MOE_BLOCK).astype(jnp.int32)

        xs = _dispatch(h2, dest, cap)
        F2 = w1.shape[-1]
        b1p = b1[l].reshape(E, F2 // (2 * LANES), LANES, 2).transpose(0, 1, 3, 2).reshape(E, 1, F2)
        y_sorted = _moe_ffn(xs, blk_e, nvalid, w1[l], b1p, w2[l], b2[l].reshape(E, 1, D))
        x2 = _combine(y_sorted, dest, route, x1, mod3, g_ffn_post[l], S)
    return x2.reshape(B, S, D)
```

```python
import functools
import math

import numpy as np
import jax
import jax.numpy as jnp
from jax import lax
from jax.experimental import pallas as pl
from jax.experimental.pallas import tpu as pltpu

GRID_W = 64
NA_HEADS = 8
NA_HEAD_DIM = 64
NA_WIN_ROWS = 8
NA_WIN_COLS = 16
NA_WIDTH = NA_HEADS * NA_HEAD_DIM
MLA_HEADS = 8
MLA_Q_LORA = 384
MLA_KV_LORA = 256
MLA_NOPE = 64
MLA_ROPE = 32
MLA_V = 64
MLA_WIDTH = MLA_HEADS * MLA_V
MLA_KEY_CHUNK = 256
MLA_VROWS = MLA_V + 16
MLA_STEPS_PER_TRIP = 8
ROPE_THETA = 100.0
TOP_K = 4
SWIGLU_ALPHA = 1.702
SWIGLU_LIMIT = 7.0
MOE_BLOCK = 256
RMS_EPS = 1e-6
NEG_BIG = -1e30

LANES = 128
VMEM_LIMIT_BYTES = 56 * 1024 * 1024

Q_LORA_PAD = 512
COL_GATES = 0
NA_GROUP_ROWS = 8
NA_KEY_ROWS = 16


def _f32(x):
    return x.astype(jnp.float32)


def _bf16(x):
    return x.astype(jnp.bfloat16)


def _dot(a, b):
    return jnp.dot(a, b, preferred_element_type=jnp.float32)


def _dot_nt(a, b):
    return lax.dot_general(a, b, (((1,), (1,)), ((), ())), preferred_element_type=jnp.float32)


def _rms(x, g, n=None):
    n = x.shape[-1] if n is None else n
    ms = jnp.sum(x * x, axis=-1, keepdims=True) * (1.0 / n)
    return x * lax.rsqrt(ms + RMS_EPS) * g


def _cparams(sem):
    return pltpu.CompilerParams(dimension_semantics=sem, vmem_limit_bytes=VMEM_LIMIT_BYTES)


def _ada_kernel(c_ref, w_ref, b_ref, o_ref):
    c = c_ref[...]
    sc = c * jax.nn.sigmoid(c)
    o_ref[...] = _dot(_bf16(sc), _bf16(w_ref[...])) + b_ref[...]


def _ada_mod(c, w_ada, b_ada):
    B, D = c.shape
    n_out = w_ada.shape[1]
    return pl.pallas_call(
        _ada_kernel,
        out_shape=jax.ShapeDtypeStruct((B, n_out), jnp.float32),
        grid=(n_out // D,),
        in_specs=[
            pl.BlockSpec((B, D), lambda j: (0, 0)),
            pl.BlockSpec((D, D), lambda j: (0, j)),
            pl.BlockSpec((1, D), lambda j: (0, j)),
        ],
        out_specs=pl.BlockSpec((B, D), lambda j: (0, j)),
        compiler_params=_cparams(("arbitrary",)),
        name="ada_mod",
    )(c, w_ada, b_ada.reshape(1, n_out))


def _in_proj_kernel(x_ref, mod_ref, g_ref, w_ref, bg_ref, o_ref, *, d_model, n_gate, chunk):
    x = x_ref[...]
    shift = mod_ref[0, :, 0:d_model]
    scale = mod_ref[0, :, d_model:2 * d_model]
    h = _rms(x, g_ref[...]) * (1.0 + scale) + shift
    hb = _bf16(h)
    n_total = w_ref.shape[1]
    for c0 in range(0, n_total, chunk):
        c1 = min(c0 + chunk, n_total)
        acc = _dot(hb, w_ref[:, c0:c1])
        if c0 < n_gate:
            acc = jax.nn.sigmoid(acc + bg_ref[:, c0:c1])
        o_ref[:, c0:c1] = _bf16(acc)


def _in_proj(x2, mod3, g_pre, w_ext, b_gate, seq, tm=512):
    T, D = x2.shape
    n_total = w_ext.shape[1]
    n_gate = b_gate.shape[-1]
    kern = functools.partial(_in_proj_kernel, d_model=D, n_gate=n_gate, chunk=512)
    return pl.pallas_call(
        kern,
        out_shape=jax.ShapeDtypeStruct((T, n_total), jnp.bfloat16),
        grid=(T // tm,),
        in_specs=[
            pl.BlockSpec((tm, D), lambda i: (i, 0)),
            pl.BlockSpec((1, 1, mod3.shape[-1]), lambda i: ((i * tm) // seq, 0, 0)),
            pl.BlockSpec((1, D), lambda i: (0, 0)),
            pl.BlockSpec((D, n_total), lambda i: (0, 0)),
            pl.BlockSpec((1, n_gate), lambda i: (0, 0)),
        ],
        out_specs=pl.BlockSpec((tm, n_total), lambda i: (i, 0)),
        compiler_params=_cparams(("arbitrary",)),
        name="in_proj",
    )(x2, mod3, g_pre.reshape(1, D), w_ext, b_gate.reshape(1, n_gate))


def _na_kernel(q_ref, k_ref, v_ref, bias_ref, o_ref, *, n_rows):
    g = pl.program_id(2)
    kb = jnp.clip(g * NA_GROUP_ROWS - NA_WIN_ROWS // 2, 0, n_rows - NA_KEY_ROWS) * GRID_W
    kb = pl.multiple_of(kb, 256)
    nk = NA_KEY_ROWS * GRID_W
    q = q_ref[...]
    k = k_ref[pl.ds(kb, nk), :]
    v = v_ref[pl.ds(kb, nk), :]
    lane = lax.broadcasted_iota(jnp.int32, (1, LANES), 1)
    scale = NA_HEAD_DIM ** -0.5
    outs = []
    for h in range(2):
        in_head = (lane >= h * NA_HEAD_DIM) & (lane < (h + 1) * NA_HEAD_DIM)
        qh = jnp.where(in_head, q * scale, 0).astype(jnp.bfloat16)
        s = _dot_nt(qh, k) + bias_ref[0, h]
        m = jnp.max(s, axis=-1, keepdims=True)
        p = jnp.exp(s - m)
        l = jnp.sum(p, axis=-1, keepdims=True)
        pv = _dot(_bf16(p), v)
        outs.append(pv / l)
    o_ref[...] = _bf16(jnp.where(lane < NA_HEAD_DIM, outs[0], outs[1]))


def _na_attention(proj, bias_tab, batch, seq, col_q, col_k, col_v):
    T = proj.shape[0]
    n_rows = seq // GRID_W
    n_groups = n_rows // NA_GROUP_ROWS
    tq = NA_GROUP_ROWS * GRID_W
    n_pairs = NA_HEADS // 2

    def cls(g):
        return jnp.where(g == 0, 0, jnp.where(g == n_groups - 1, 2, 1))

    kern = functools.partial(_na_kernel, n_rows=n_rows)
    return pl.pallas_call(
        kern,
        out_shape=jax.ShapeDtypeStruct((T, NA_WIDTH), jnp.bfloat16),
        grid=(n_pairs, batch, n_groups),
        in_specs=[
            pl.BlockSpec((tq, LANES), lambda hp, b, g: (b * n_groups + g, col_q // LANES + hp)),
            pl.BlockSpec((seq, LANES), lambda hp, b, g: (b, col_k // LANES + hp)),
            pl.BlockSpec((seq, LANES), lambda hp, b, g: (b, col_v // LANES + hp)),
            pl.BlockSpec((1, 2, tq, NA_KEY_ROWS * GRID_W), lambda hp, b, g: (cls(g), hp, 0, 0)),
        ],
        out_specs=pl.BlockSpec((tq, LANES), lambda hp, b, g: (b * n_groups + g, hp)),
        compiler_params=_cparams(("arbitrary", "arbitrary", "arbitrary")),
        name="na_attn",
    )(proj, proj, proj, bias_tab)


def _na_bias_tables(rpb, n_rows):
    H = rpb.shape[0]
    kw = NA_WIN_COLS
    W = GRID_W
    pad = W - kw
    rp = jnp.pad(rpb.astype(jnp.float32), ((0, 0), (0, 0), (pad, pad)), constant_values=NEG_BIG)
    toep = jnp.stack([rp[:, :, W - 1 - qc: 2 * W - 1 - qc] for qc in range(W)], axis=2)
    cidx = np.arange(W)
    col_start = np.clip(cidx - kw // 2, 0, W - kw)
    col_in = (cidx[None, :] >= col_start[:, None]) & (cidx[None, :] < col_start[:, None] + kw)
    toep = jnp.where(jnp.asarray(col_in)[None, None], toep, NEG_BIG)
    neg_blk = jnp.full((H, W, W), NEG_BIG, jnp.float32)
    tabs = []
    for c in range(3):
        rows_i = []
        for i in range(NA_GROUP_ROWS):
            if c == 0:
                j0, off = max(i - NA_WIN_ROWS // 2, 0), NA_WIN_ROWS - 1
            elif c == 1:
                j0, off = i, NA_WIN_ROWS - 1 - NA_WIN_ROWS // 2
            else:
                j0, off = min(i + NA_WIN_ROWS // 2, NA_KEY_ROWS - NA_WIN_ROWS), -1
            blks = []
            for j in range(NA_KEY_ROWS):
                if j0 <= j < j0 + NA_WIN_ROWS:
                    blks.append(toep[:, j - i + off])
                else:
                    blks.append(neg_blk)
            rows_i.append(jnp.concatenate(blks, axis=-1))
        tabs.append(jnp.concatenate(rows_i, axis=1))
    return jnp.stack(tabs, axis=0)


def _mla_prep_kernel(cq_ref, ckv_ref, kr_ref, gq_ref, gkv_ref, wqm_ref, wqs_ref, wk_ref, wv_ref,
                     pm_ref, cq_tab_ref, sq_tab_ref, csk_tab_ref, q_out, k_out, v_out, *, qscale):
    cq = _f32(cq_ref[...])
    cqn = _bf16(_rms(cq, gq_ref[...], n=MLA_Q_LORA))
    qm = _dot(cqn, wqm_ref[...])
    qs = _dot(cqn, wqs_ref[...])
    ctab = cq_tab_ref[...]
    stab = sq_tab_ref[...]
    for h in range(MLA_HEADS):
        sl = slice(h * LANES, (h + 1) * LANES)
        qh = (qm[:, sl] * ctab + qs[:, sl] * stab) * qscale
        q_out[0, h] = _bf16(qh.T)

    ckv = _f32(ckv_ref[...])
    ckvn = _bf16(_rms(ckv, gkv_ref[...]))
    kk = _dot(ckvn, wk_ref[...])
    vv = _dot(ckvn, wv_ref[...])
    n_chunk = v_out.shape[2]
    ones = jnp.ones((MLA_VROWS - MLA_V, MLA_KEY_CHUNK), jnp.bfloat16)
    for hp in range(MLA_HEADS // 2):
        vt = _bf16(vv[:, hp * LANES:(hp + 1) * LANES].T)
        for cc in range(n_chunk):
            ks = slice(cc * MLA_KEY_CHUNK, (cc + 1) * MLA_KEY_CHUNK)
            for h in range(2):
                r0 = h * MLA_VROWS
                v_out[0, hp, cc, r0:r0 + MLA_V, :] = vt[h * MLA_V:(h + 1) * MLA_V, ks]
                v_out[0, hp, cc, r0 + MLA_V:r0 + MLA_VROWS, :] = ones
    krr = _f32(kr_ref[...]) * csk_tab_ref[...]
    kplace = _dot(_bf16(krr), pm_ref[...])
    for h in range(MLA_HEADS):
        sl = slice(h * LANES, (h + 1) * LANES)
        k_out[:, sl] = _bf16(kk[:, sl] + kplace)


def _mla_prep(proj, col_cq, col_ckv, col_kr, gq, gkv, wqm, wqs, wk, wv, pm, cq_tab, sq_tab, csk_tab,
              seq, qscale, tm=512):
    T = proj.shape[0]
    batch = T // seq
    n_s = seq // tm
    n_pairs = MLA_HEADS // 2
    cpt = tm // MLA_KEY_CHUNK
    full = lambda a: pl.BlockSpec(a.shape, lambda i: (0,) * a.ndim)
    tab = pl.BlockSpec((tm, LANES), lambda i: (i % n_s, 0))
    kern = functools.partial(_mla_prep_kernel, qscale=qscale)
    return pl.pallas_call(
        kern,
        out_shape=(
            jax.ShapeDtypeStruct((batch, MLA_HEADS, LANES, seq), jnp.bfloat16),
            jax.ShapeDtypeStruct((T, MLA_HEADS * LANES), jnp.bfloat16),
            jax.ShapeDtypeStruct((batch, n_pairs, seq // MLA_KEY_CHUNK, 2 * MLA_VROWS, MLA_KEY_CHUNK),
                                 jnp.bfloat16),
        ),
        grid=(T // tm,),
        in_specs=[
            pl.BlockSpec((tm, Q_LORA_PAD), lambda i: (i, col_cq // Q_LORA_PAD)),
            pl.BlockSpec((tm, MLA_KV_LORA), lambda i: (i, col_ckv // MLA_KV_LORA)),
            pl.BlockSpec((tm, LANES), lambda i: (i, col_kr // LANES)),
            full(gq), full(gkv), full(wqm), full(wqs), full(wk), full(wv), full(pm),
            tab, tab, tab,
        ],
        out_specs=(
            pl.BlockSpec((1, MLA_HEADS, LANES, tm), lambda i: (i // n_s, 0, 0, i % n_s)),
            pl.BlockSpec((tm, MLA_HEADS * LANES), lambda i: (i, 0)),
            pl.BlockSpec((1, n_pairs, cpt, 2 * MLA_VROWS, MLA_KEY_CHUNK),
                         lambda i: (i // n_s, 0, i % n_s, 0, 0)),
        ),
        compiler_params=_cparams(("arbitrary",)),
        name="mla_prep",
    )(proj, proj, proj, gq, gkv, wqm, wqs, wk, wv, pm, cq_tab, sq_tab, csk_tab)


def _mla_attn_kernel(qt_ref, k_ref, vt_ref, o_ref, m_scr, acc_scr, st_scr):
    n_chunks = vt_ref.shape[2]
    tk = MLA_KEY_CHUNK
    m_scr[...] = jnp.full(m_scr.shape, -jnp.inf, jnp.float32)
    acc_scr[...] = jnp.zeros(acc_scr.shape, jnp.float32)

    def scores(c, slot):
        k0 = pl.multiple_of(c * tk, tk)
        for h in range(2):
            kc = k_ref[pl.ds(k0, tk), h * LANES:(h + 1) * LANES]
            st_scr[slot, h] = _dot(kc, qt_ref[0, h])

    def step(c, slot):
        scores(jnp.minimum(c + 1, n_chunks - 1), 1 - slot)
        vt = vt_ref[0, 0, c]
        pts, alphas = [], []
        for h in range(2):
            st = st_scr[slot, h]
            m_old = m_scr[h]
            m_new = jnp.maximum(m_old, jnp.max(st, axis=0, keepdims=True))
            alpha = jnp.exp2(m_old - m_new)
            pt = jnp.exp2(st - m_new)
            m_scr[h] = m_new
            pts.append(_bf16(pt))
            alphas.append(alpha)
        for h in range(2):
            acc_scr[h] = alphas[h] * acc_scr[h] + _dot(vt[h * MLA_VROWS:(h + 1) * MLA_VROWS, :], pts[h])

    def body(j, carry):
        for u in range(MLA_STEPS_PER_TRIP):
            step(MLA_STEPS_PER_TRIP * j + u, u % 2)
        return carry

    scores(0, 0)
    lax.fori_loop(0, n_chunks // MLA_STEPS_PER_TRIP, body, 0)
    outs = []
    for h in range(2):
        acc = acc_scr[h]
        outs.append(acc[0:MLA_V] / acc[MLA_V:MLA_V + 1])
    o_ref[...] = _bf16(jnp.concatenate(outs, axis=0).T)


def _mla_attention(q_t, k_cat, v_t, batch, seq, tq=512):
    T = k_cat.shape[0]
    n_q = seq // tq
    n_pairs = MLA_HEADS // 2
    n_chunks = seq // MLA_KEY_CHUNK
    return pl.pallas_call(
        _mla_attn_kernel,
        out_shape=jax.ShapeDtypeStruct((T, MLA_WIDTH), jnp.bfloat16),
        grid=(batch, n_pairs, n_q),
        in_specs=[
            pl.BlockSpec((1, 2, LANES, tq), lambda b, hp, i: (b, hp, 0, i)),
            pl.BlockSpec((seq, 2 * LANES), lambda b, hp, i: (b, hp)),
            pl.BlockSpec((1, 1, n_chunks, 2 * MLA_VROWS, MLA_KEY_CHUNK), lambda b, hp, i: (b, hp, 0, 0, 0)),
        ],
        out_specs=pl.BlockSpec((tq, LANES), lambda b, hp, i: (b * n_q + i, hp)),
        scratch_shapes=[
            pltpu.VMEM((2, 1, tq), jnp.float32),
            pltpu.VMEM((2, MLA_VROWS, tq), jnp.float32),
            pltpu.VMEM((2, 2, MLA_KEY_CHUNK, tq), jnp.float32),
        ],
        compiler_params=_cparams(("arbitrary", "arbitrary", "arbitrary")),
        name="mla_attn",
    )(q_t, k_cat, v_t)


def _out_proj_kernel(ona_ref, omla_ref, gate_ref, x_ref, mod_ref, gpost_ref, gpre_ref, wna_ref,
                     wmla_ref, wout_ref, wr_ref, br_ref, x1_ref, h2_ref, lg_ref, *, d_model):
    D = d_model
    gates = gate_ref[...]
    merged = (_f32(gates[:, 0:D]) * _dot(ona_ref[...], wna_ref[...])
              + _f32(gates[:, D:2 * D]) * _dot(omla_ref[...], wmla_ref[...]))
    y = _dot(_bf16(merged), wout_ref[...])
    gate_a = mod_ref[0, :, 2 * D:3 * D]
    shift_f = mod_ref[0, :, 3 * D:4 * D]
    scale_f = mod_ref[0, :, 4 * D:5 * D]
    x1 = x_ref[...] + gate_a * _rms(y, gpost_ref[...])
    x1_ref[...] = x1
    h2 = _rms(x1, gpre_ref[...]) * (1.0 + scale_f) + shift_f
    h2_ref[...] = h2
    lg_ref[...] = _dot(_bf16(h2), wr_ref[...]) + br_ref[...]


def _out_proj(o_na, o_mla, proj, x2, mod3, g_post, g_pre, w_na, w_mla, w_out, w_r, b_r, seq, tm=512):
    T, D = x2.shape
    full = lambda a: pl.BlockSpec(a.shape, lambda i: (0,) * a.ndim)
    row = lambda w: pl.BlockSpec((tm, w), lambda i: (i, 0))
    kern = functools.partial(_out_proj_kernel, d_model=D)
    g_post = g_post.reshape(1, D)
    g_pre = g_pre.reshape(1, D)
    return pl.pallas_call(
        kern,
        out_shape=(
            jax.ShapeDtypeStruct((T, D), jnp.float32),
            jax.ShapeDtypeStruct((T, D), jnp.float32),
            jax.ShapeDtypeStruct((T, LANES), jnp.float32),
        ),
        grid=(T // tm,),
        in_specs=[
            row(NA_WIDTH), row(MLA_WIDTH),
            pl.BlockSpec((tm, 2 * D), lambda i: (i, COL_GATES // (2 * D))),
            row(D),
            pl.BlockSpec((1, 1, mod3.shape[-1]), lambda i: ((i * tm) // seq, 0, 0)),
            full(g_post), full(g_pre), full(w_na), full(w_mla), full(w_out), full(w_r), full(b_r),
        ],
        out_specs=(row(D), row(D), row(LANES)),
        compiler_params=_cparams(("arbitrary",)),
        name="out_proj",
    )(o_na, o_mla, proj, x2, mod3, g_post, g_pre, w_na, w_mla, w_out, w_r, b_r)


def _route_kernel(lg_ref, r_ref, cnt_ref, carry_scr, *, sub):
    @pl.when(pl.program_id(0) == 0)
    def _():
        carry_scr[...] = jnp.zeros(carry_scr.shape, jnp.float32)

    tr = lg_ref.shape[0]
    lane = lax.broadcasted_iota(jnp.int32, (sub, LANES), 1).astype(jnp.float32)
    ri = lax.broadcasted_iota(jnp.int32, (sub, sub), 0)
    ci = lax.broadcasted_iota(jnp.int32, (sub, sub), 1)
    tri = jnp.where(ri >= ci, 1.0, 0.0).astype(jnp.bfloat16)
    for s0 in range(0, tr, sub):
        work = lg_ref[s0:s0 + sub, :]
        sels, vals, idxs = [], [], []
        for _k in range(TOP_K):
            mk = jnp.max(work, axis=-1, keepdims=True)
            ik = jnp.min(jnp.where(work == mk, lane, float(LANES)), axis=-1, keepdims=True)
            sk = lane == ik
            work = jnp.where(sk, -jnp.inf, work)
            sels.append(sk)
            vals.append(mk)
            idxs.append(ik)
        es = [jnp.exp(v - vals[0]) for v in vals]
        denom = es[0] + es[1] + es[2] + es[3]
        onehot = jnp.zeros((sub, LANES), jnp.float32)
        for sk in sels:
            onehot = jnp.where(sk, 1.0, onehot)
        prefix = _dot(tri, _bf16(onehot))
        carry = carry_scr[...]
        rank_mat = carry + prefix - 1.0
        res = jnp.zeros((sub, LANES), jnp.float32)
        for kk in range(TOP_K):
            rank_k = jnp.sum(jnp.where(sels[kk], rank_mat, 0.0), axis=-1, keepdims=True)
            res = jnp.where(lane == kk, idxs[kk], res)
            res = jnp.where(lane == TOP_K + kk, rank_k, res)
            res = jnp.where(lane == 2 * TOP_K + kk, es[kk] / denom, res)
        r_ref[s0:s0 + sub, :] = res
        carry_scr[...] = carry + jnp.sum(onehot, axis=0, keepdims=True)
    cnt_ref[...] = carry_scr[...]


def _route(logits, tr=2048, sub=256):
    T = logits.shape[0]
    kern = functools.partial(_route_kernel, sub=sub)
    return pl.pallas_call(
        kern,
        out_shape=(
            jax.ShapeDtypeStruct((T, LANES), jnp.float32),
            jax.ShapeDtypeStruct((1, LANES), jnp.float32),
        ),
        grid=(T // tr,),
        in_specs=[pl.BlockSpec((tr, LANES), lambda i: (i, 0))],
        out_specs=(
            pl.BlockSpec((tr, LANES), lambda i: (i, 0)),
            pl.BlockSpec((1, LANES), lambda i: (0, 0)),
        ),
        scratch_shapes=[pltpu.VMEM((1, LANES), jnp.float32)],
        compiler_params=_cparams(("arbitrary",)),
        name="route",
    )(logits)


def _dispatch_kernel(dest_ref, h_ref, xs_in_ref, xs_ref, sem):
    del xs_in_ref
    ts = h_ref.shape[0]

    def row_copy(j, d):
        return pltpu.make_async_copy(h_ref.at[pl.ds(j, 1), :], xs_ref.at[pl.ds(d, 1), :], sem)

    def issue(j, carry):
        for kk in range(TOP_K):
            row_copy(j, dest_ref[j * TOP_K + kk]).start()
        return carry

    lax.fori_loop(0, ts, issue, 0)

    def drain(j, carry):
        for _kk in range(TOP_K):
            row_copy(0, 0).wait()
        return carry

    lax.fori_loop(0, ts, drain, 0)


def _dispatch(h2, dest, cap, ts=256):
    T, D = h2.shape
    xs0 = jnp.zeros((cap, D), h2.dtype)
    return pl.pallas_call(
        _dispatch_kernel,
        out_shape=jax.ShapeDtypeStruct((cap, D), h2.dtype),
        grid=(T // ts,),
        in_specs=[
            pl.BlockSpec((ts * TOP_K,), lambda i: (i,), memory_space=pltpu.SMEM),
            pl.BlockSpec((ts, D), lambda i: (i, 0)),
            pl.BlockSpec(memory_space=pl.ANY),
        ],
        out_specs=pl.BlockSpec(memory_space=pl.ANY),
        scratch_shapes=[pltpu.SemaphoreType.DMA(())],
        input_output_aliases={2: 0},
        compiler_params=_cparams(("arbitrary",)),
        name="dispatch",
    )(dest, h2, xs0)


def _moe_kernel(blk_e_ref, nvalid_ref, xs_ref, w1_ref, b1_ref, w2_ref, b2_ref, y_ref, w1p_scr, w2b_scr):
    i = pl.program_id(0)
    d_ff = w2_ref.shape[1]
    n_groups = (2 * d_ff) // (2 * LANES)

    @pl.when(i < nvalid_ref[0])
    def _():
        e = blk_e_ref[i]
        e_prev = blk_e_ref[jnp.maximum(i - 1, 0)]

        @pl.when((i == 0) | (e != e_prev))
        def _():
            r = lax.broadcasted_iota(jnp.int32, (2 * LANES, 2 * LANES), 0)
            c = lax.broadcasted_iota(jnp.int32, (2 * LANES, 2 * LANES), 1)
            src = jnp.where(c < LANES, 2 * c, 2 * (c - LANES) + 1)
            perm = jnp.where(r == src, 1.0, 0.0).astype(jnp.bfloat16)
            for gI in range(n_groups):
                sl = slice(gI * 2 * LANES, (gI + 1) * 2 * LANES)
                w1p_scr[:, sl] = _bf16(_dot(_bf16(w1_ref[0, :, sl]), perm))
            w2b_scr[...] = _bf16(w2_ref[0])

        xb = _bf16(xs_ref[...])
        hcat = _dot(xb, w1p_scr[...]) + b1_ref[0]
        acts = []
        for gI in range(n_groups):
            glu = jnp.minimum(hcat[:, gI * 2 * LANES: gI * 2 * LANES + LANES], SWIGLU_LIMIT)
            lin = jnp.clip(hcat[:, gI * 2 * LANES + LANES: (gI + 1) * 2 * LANES], -SWIGLU_LIMIT, SWIGLU_LIMIT)
            acts.append(_bf16(glu * jax.nn.sigmoid(SWIGLU_ALPHA * glu) * (lin + 1.0)))
        act = jnp.concatenate(acts, axis=-1)
        y_ref[...] = _dot(act, w2b_scr[...]) + b2_ref[0]

    @pl.when(i >= nvalid_ref[0])
    def _():
        y_ref[...] = jnp.zeros(y_ref.shape, y_ref.dtype)


def _moe_ffn(xs, blk_e, nvalid, w1, b1p, w2, b2):
    cap, D = xs.shape
    E, _, F2 = w1.shape
    F = w2.shape[1]
    nblk = cap // MOE_BLOCK

    def xmap(i, be, nv):
        return (jnp.minimum(i, nv[0] - 1), 0)

    grid_spec = pltpu.PrefetchScalarGridSpec(
        num_scalar_prefetch=2,
        grid=(nblk,),
        in_specs=[
            pl.BlockSpec((MOE_BLOCK, D), xmap),
            pl.BlockSpec((1, D, F2), lambda i, be, nv: (be[i], 0, 0)),
            pl.BlockSpec((1, 1, F2), lambda i, be, nv: (be[i], 0, 0)),
            pl.BlockSpec((1, F, D), lambda i, be, nv: (be[i], 0, 0)),
            pl.BlockSpec((1, 1, D), lambda i, be, nv: (be[i], 0, 0)),
        ],
        out_specs=pl.BlockSpec((MOE_BLOCK, D), lambda i, be, nv: (i, 0)),
        scratch_shapes=[
            pltpu.VMEM((D, F2), jnp.bfloat16),
            pltpu.VMEM((F, D), jnp.bfloat16),
        ],
    )
    return pl.pallas_call(
        _moe_kernel,
        out_shape=jax.ShapeDtypeStruct((cap, D), jnp.float32),
        grid_spec=grid_spec,
        compiler_params=_cparams(("arbitrary",)),
        name="moe_ffn",
    )(blk_e, nvalid, xs, w1, b1p, w2, b2)


def _combine_kernel(dest_ref, y_ref, r_ref, x1_ref, mod_ref, g_ref, o_ref, ybuf, sem, *, d_model):
    D = d_model
    tc = x1_ref.shape[0]

    def row_copy(j, kk, d):
        return pltpu.make_async_copy(y_ref.at[pl.ds(d, 1), :], ybuf.at[kk, pl.ds(j, 1), :], sem)

    def issue(j, carry):
        for kk in range(TOP_K):
            row_copy(j, kk, dest_ref[j * TOP_K + kk]).start()
        return carry

    lax.fori_loop(0, tc, issue, 0)

    def drain(j, carry):
        for kk in range(TOP_K):
            row_copy(0, kk, 0).wait()
        return carry

    lax.fori_loop(0, tc, drain, 0)

    r = r_ref[...]
    moe = jnp.zeros((tc, D), jnp.float32)
    for kk in range(TOP_K):
        moe = moe + r[:, 2 * TOP_K + kk: 2 * TOP_K + kk + 1] * ybuf[kk]
    gate_f = mod_ref[0, :, 5 * D:6 * D]
    o_ref[...] = x1_ref[...] + gate_f * _rms(moe, g_ref[...])


def _combine(y, dest, route, x1, mod3, g_post, seq, tc=256):
    T, D = x1.shape
    kern = functools.partial(_combine_kernel, d_model=D)
    return pl.pallas_call(
        kern,
        out_shape=jax.ShapeDtypeStruct((T, D), jnp.float32),
        grid=(T // tc,),
        in_specs=[
            pl.BlockSpec((tc * TOP_K,), lambda i: (i,), memory_space=pltpu.SMEM),
            pl.BlockSpec(memory_space=pl.ANY),
            pl.BlockSpec((tc, LANES), lambda i: (i, 0)),
            pl.BlockSpec((tc, D), lambda i: (i, 0)),
            pl.BlockSpec((1, 1, mod3.shape[-1]), lambda i: ((i * tc) // seq, 0, 0)),
            pl.BlockSpec((1, D), lambda i: (0, 0)),
        ],
        out_specs=pl.BlockSpec((tc, D), lambda i: (i, 0)),
        scratch_shapes=[
            pltpu.VMEM((TOP_K, tc, D), jnp.float32),
            pltpu.SemaphoreType.DMA(()),
        ],
        compiler_params=_cparams(("arbitrary",)),
        name="combine",
    )(dest, y, route, x1, mod3, g_post.reshape(1, D))


def _rope_swap_cols(w):
    nf = MLA_ROPE // 4
    return jnp.concatenate([-w[..., nf:2 * nf], w[..., 0:nf], -w[..., 3 * nf:4 * nf], w[..., 2 * nf:3 * nf]],
                           axis=-1)


def _rope_tables(seq):
    t = jnp.arange(seq, dtype=jnp.int32)
    rows, cols = t // GRID_W, t % GRID_W
    nf = MLA_ROPE // 4
    inv = ROPE_THETA ** (-jnp.arange(nf, dtype=jnp.float32) / nf)
    ar = rows.astype(jnp.float32)[:, None] * inv
    ac = cols.astype(jnp.float32)[:, None] * inv
    c32 = jnp.concatenate([jnp.cos(ar), jnp.cos(ar), jnp.cos(ac), jnp.cos(ac)], axis=-1)
    s32 = jnp.concatenate([jnp.sin(ar), jnp.sin(ar), jnp.sin(ac), jnp.sin(ac)], axis=-1)
    one = jnp.ones((seq, MLA_NOPE), jnp.float32)
    z64 = jnp.zeros((seq, MLA_NOPE), jnp.float32)
    z32 = jnp.zeros((seq, LANES - MLA_NOPE - MLA_ROPE), jnp.float32)
    cq_tab = jnp.concatenate([one, c32, z32], axis=-1)
    sq_tab = jnp.concatenate([z64, s32, z32], axis=-1)
    csk_tab = jnp.concatenate([c32, s32, z64], axis=-1)
    return cq_tab, sq_tab, csk_tab


def kernel(x, c, w_ada, b_ada, g_attn_pre, g_attn_post, w_in, b_gate, na_rpb, q_norm_g, kv_norm_g,
           w_uq, w_ukv, w_na_up, w_mla_up, w_out, g_ffn_pre, g_ffn_post, w_router, b_router,
           w1, b1, w2, b2):
    B, S, D = x.shape
    T = B * S
    depth = w_ada.shape[0]
    E = w_router.shape[-1]
    n_rows = S // GRID_W
    assert S % (NA_GROUP_ROWS * GRID_W) == 0 and n_rows >= NA_KEY_ROWS
    assert E <= LANES and T % 2048 == 0

    bf = jnp.bfloat16
    cq_tab, sq_tab, csk_tab = _rope_tables(S)
    pm_np = np.zeros((LANES, LANES), np.float32)
    for cidx in range(MLA_ROPE):
        pm_np[cidx, MLA_NOPE + cidx] = 1.0
        pm_np[MLA_ROPE + cidx, MLA_NOPE + cidx] = 1.0
    pm = jnp.asarray(pm_np, bf)
    qscale = float((MLA_NOPE + MLA_ROPE) ** -0.5 * math.log2(math.e))

    col_na = 2 * D
    col_cq = col_na + 3 * NA_WIDTH
    col_ckv = col_cq + Q_LORA_PAD
    col_kr = col_ckv + MLA_KV_LORA
    assert col_cq % Q_LORA_PAD == 0 and col_ckv % MLA_KV_LORA == 0

    x2 = x.reshape(T, D)
    for l in range(depth):
        mod = _ada_mod(c, w_ada[l], b_ada[l])
        mod3 = mod.reshape(B, 1, 6 * D)

        o_na, o_cq, o_ckv, o_kr, o_g = np.cumsum([0, 3 * NA_WIDTH, MLA_Q_LORA, MLA_KV_LORA, MLA_ROPE]).tolist()
        wi = w_in[l]
        w_kr = wi[:, o_kr:o_g]
        w_ext = jnp.concatenate([
            wi[:, o_g:o_g + 2 * D],
            wi[:, o_na:o_cq],
            wi[:, o_cq:o_ckv], jnp.zeros((D, Q_LORA_PAD - MLA_Q_LORA), wi.dtype),
            wi[:, o_ckv:o_kr],
            w_kr, _rope_swap_cols(w_kr), jnp.zeros((D, LANES - 2 * MLA_ROPE), wi.dtype),
        ], axis=1).astype(bf)
        proj = _in_proj(x2, mod3, g_attn_pre[l], w_ext, b_gate[l], S)

        bias_tab = _na_bias_tables(na_rpb[l], n_rows)
        o_na_tok = _na_attention(proj, bias_tab, B, S, col_na, col_na + NA_WIDTH, col_na + 2 * NA_WIDTH)

        wq = w_uq[l]
        zq = jnp.zeros((MLA_Q_LORA, MLA_HEADS, LANES - MLA_NOPE - MLA_ROPE), wq.dtype)
        wqm = jnp.concatenate([wq, zq], axis=-1).reshape(MLA_Q_LORA, MLA_HEADS * LANES)
        wqs = jnp.concatenate([jnp.zeros((MLA_Q_LORA, MLA_HEADS, MLA_NOPE), wq.dtype),
                               _rope_swap_cols(wq[..., MLA_NOPE:]), zq], axis=-1
                              ).reshape(MLA_Q_LORA, MLA_HEADS * LANES)
        rpad = ((0, Q_LORA_PAD - MLA_Q_LORA), (0, 0))
        wqm = jnp.pad(wqm, rpad).astype(bf)
        wqs = jnp.pad(wqs, rpad).astype(bf)
        wkv = w_ukv[l]
        wk = jnp.concatenate([wkv[..., :MLA_NOPE], jnp.zeros((MLA_KV_LORA, MLA_HEADS, LANES - MLA_NOPE), wkv.dtype)],
                             axis=-1).reshape(MLA_KV_LORA, MLA_HEADS * LANES).astype(bf)
        wv = wkv[..., MLA_NOPE:].reshape(MLA_KV_LORA, MLA_WIDTH).astype(bf)
        gq = jnp.pad(q_norm_g[l], (0, Q_LORA_PAD - MLA_Q_LORA)).reshape(1, Q_LORA_PAD)
        gkv = kv_norm_g[l].reshape(1, MLA_KV_LORA)
        q_cat, k_cat, v_mla = _mla_prep(proj, col_cq, col_ckv, col_kr, gq, gkv, wqm, wqs, wk, wv, pm,
                                        cq_tab, sq_tab, csk_tab, S, qscale)
        o_mla_tok = _mla_attention(q_cat, k_cat, v_mla, B, S)

        w_r = jnp.pad(w_router[l], ((0, 0), (0, LANES - E))).astype(bf)
        b_r = jnp.concatenate([b_router[l], jnp.full((LANES - E,), NEG_BIG, jnp.float32)]).reshape(1, LANES)
        x1, h2, logits = _out_proj(o_na_tok, o_mla_tok, proj, x2, mod3, g_attn_post[l], g_ffn_pre[l],
                                   w_na_up[l].astype(bf), w_mla_up[l].astype(bf), w_out[l].astype(bf),
                                   w_r, b_r, S)

        route, counts = _route(logits)
        e_idx = route[:, 0:TOP_K].astype(jnp.int32)
        rank = route[:, TOP_K:2 * TOP_K].astype(jnp.int32)
        cnt = counts[0, :E].astype(jnp.int32)
        padded = ((cnt + MOE_BLOCK - 1) // MOE_BLOCK) * MOE_BLOCK
        pend = jnp.cumsum(padded)
        pstart = pend - padded
        n_items = T * TOP_K
        nblk = -(-n_items // MOE_BLOCK) + E
        cap = nblk * MOE_BLOCK
        dest = (rank + jnp.sum(jnp.where(e_idx[..., None] == jnp.arange(E, dtype=jnp.int32), pstart, 0), axis=-1)
                ).reshape(n_items)
        blk_off = jnp.arange(nblk, dtype=jnp.int32) * MOE_BLOCK
        blk_e = jnp.minimum(jnp.sum((pend[None, :] <= blk_off[:, None]).astype(jnp.int32), axis=-1), E - 1)
        nvalid = (pend[-1:] // MOE_BLOCK).astype(jnp.int32)

        xs = _dispatch(h2, dest, cap)
        F2 = w1.shape[-1]
        b1p = b1[l].reshape(E, F2 // (2 * LANES), LANES, 2).transpose(0, 1, 3, 2).reshape(E, 1, F2)
        y_sorted = _moe_ffn(xs, blk_e, nvalid, w1[l], b1p, w2[l], b2[l].reshape(E, 1, D))
        x2 = _combine(y_sorted, dest, route, x1, mod3, g_ffn_post[l], S)
    return x2.reshape(B, S, D)
```

```python
import functools
import math

import numpy as np
import jax
import jax.numpy as jnp
from jax import lax
from jax.experimental import pallas as pl
from jax.experimental.pallas import tpu as pltpu

GRID_W = 64
NA_HEADS = 8
NA_HEAD_DIM = 64
NA_WIN_ROWS = 8
NA_WIN_COLS = 16
NA_WIDTH = NA_HEADS * NA_HEAD_DIM
MLA_HEADS = 8
MLA_Q_LORA = 384
MLA_KV_LORA = 256
MLA_NOPE = 64
MLA_ROPE = 32
MLA_V = 64
MLA_WIDTH = MLA_HEADS * MLA_V
MLA_KEY_CHUNK = 256
MLA_VROWS = MLA_V + 16
MLA_STEPS_PER_TRIP = 8
ROPE_THETA = 100.0
TOP_K = 4
SWIGLU_ALPHA = 1.702
SWIGLU_LIMIT = 7.0
MOE_BLOCK = 256
RMS_EPS = 1e-6
NEG_BIG = -1e30

LANES = 128
VMEM_LIMIT_BYTES = 56 * 1024 * 1024

Q_LORA_PAD = 512
COL_GATES = 0
NA_GROUP_ROWS = 8
NA_KEY_ROWS = 16


def _f32(x):
    return x.astype(jnp.float32)


def _bf16(x):
    return x.astype(jnp.bfloat16)


def _dot(a, b):
    return jnp.dot(a, b, preferred_element_type=jnp.float32)


def _dot_nt(a, b):
    return lax.dot_general(a, b, (((1,), (1,)), ((), ())), preferred_element_type=jnp.float32)


def _rms(x, g, n=None):
    n = x.shape[-1] if n is None else n
    ms = jnp.sum(x * x, axis=-1, keepdims=True) * (1.0 / n)
    return x * lax.rsqrt(ms + RMS_EPS) * g


SUBLANES = 8


def _store_row_tiles(ref, x, base=0):
    n = x.shape[0]
    for c in range(x.shape[1] // LANES):
        ref[pl.ds(base + c, n, stride=SUBLANES), :] = x[:, c * LANES:(c + 1) * LANES]


def _load_row_tiles(ref, n, base=0):
    chunks = [ref[pl.ds(base + c, n, stride=SUBLANES), :] for c in range(SUBLANES)]
    return jnp.concatenate(chunks, axis=-1)


def _cparams(sem):
    return pltpu.CompilerParams(dimension_semantics=sem, vmem_limit_bytes=VMEM_LIMIT_BYTES)


def _ada_kernel(c_ref, w_ref, b_ref, o_ref):
    c = c_ref[...]
    sc = c * jax.nn.sigmoid(c)
    o_ref[...] = _dot(_bf16(sc), _bf16(w_ref[...])) + b_ref[...]


def _ada_mod(c, w_ada, b_ada):
    B, D = c.shape
    n_out = w_ada.shape[1]
    return pl.pallas_call(
        _ada_kernel,
        out_shape=jax.ShapeDtypeStruct((B, n_out), jnp.float32),
        grid=(n_out // D,),
        in_specs=[
            pl.BlockSpec((B, D), lambda j: (0, 0)),
            pl.BlockSpec((D, D), lambda j: (0, j)),
            pl.BlockSpec((1, D), lambda j: (0, j)),
        ],
        out_specs=pl.BlockSpec((B, D), lambda j: (0, j)),
        compiler_params=_cparams(("arbitrary",)),
        name="ada_mod",
    )(c, w_ada, b_ada.reshape(1, n_out))


def _in_proj_kernel(x_ref, mod_ref, g_ref, w_ref, bg_ref, o_ref, *, d_model, n_gate, chunk):
    x = x_ref[...]
    shift = mod_ref[0, :, 0:d_model]
    scale = mod_ref[0, :, d_model:2 * d_model]
    h = _rms(x, g_ref[...]) * (1.0 + scale) + shift
    hb = _bf16(h)
    n_total = w_ref.shape[1]
    for c0 in range(0, n_total, chunk):
        c1 = min(c0 + chunk, n_total)
        acc = _dot(hb, w_ref[:, c0:c1])
        if c0 < n_gate:
            acc = jax.nn.sigmoid(acc + bg_ref[:, c0:c1])
        o_ref[:, c0:c1] = _bf16(acc)


def _in_proj(x2, mod3, g_pre, w_ext, b_gate, seq, tm=512):
    T, D = x2.shape
    n_total = w_ext.shape[1]
    n_gate = b_gate.shape[-1]
    kern = functools.partial(_in_proj_kernel, d_model=D, n_gate=n_gate, chunk=512)
    return pl.pallas_call(
        kern,
        out_shape=jax.ShapeDtypeStruct((T, n_total), jnp.bfloat16),
        grid=(T // tm,),
        in_specs=[
            pl.BlockSpec((tm, D), lambda i: (i, 0)),
            pl.BlockSpec((1, 1, mod3.shape[-1]), lambda i: ((i * tm) // seq, 0, 0)),
            pl.BlockSpec((1, D), lambda i: (0, 0)),
            pl.BlockSpec((D, n_total), lambda i: (0, 0)),
            pl.BlockSpec((1, n_gate), lambda i: (0, 0)),
        ],
        out_specs=pl.BlockSpec((tm, n_total), lambda i: (i, 0)),
        compiler_params=_cparams(("arbitrary",)),
        name="in_proj",
    )(x2, mod3, g_pre.reshape(1, D), w_ext, b_gate.reshape(1, n_gate))


def _na_kernel(q_ref, k_ref, v_ref, bias_ref, o_ref, *, n_rows):
    g = pl.program_id(2)
    kb = jnp.clip(g * NA_GROUP_ROWS - NA_WIN_ROWS // 2, 0, n_rows - NA_KEY_ROWS) * GRID_W
    kb = pl.multiple_of(kb, 256)
    nk = NA_KEY_ROWS * GRID_W
    q = q_ref[...]
    k = k_ref[pl.ds(kb, nk), :]
    v = v_ref[pl.ds(kb, nk), :]
    lane = lax.broadcasted_iota(jnp.int32, (1, LANES), 1)
    scale = NA_HEAD_DIM ** -0.5
    outs = []
    for h in range(2):
        in_head = (lane >= h * NA_HEAD_DIM) & (lane < (h + 1) * NA_HEAD_DIM)
        qh = jnp.where(in_head, q * scale, 0).astype(jnp.bfloat16)
        s = _dot_nt(qh, k) + bias_ref[0, h]
        m = jnp.max(s, axis=-1, keepdims=True)
        p = jnp.exp(s - m)
        l = jnp.sum(p, axis=-1, keepdims=True)
        pv = _dot(_bf16(p), v)
        outs.append(pv / l)
    o_ref[...] = _bf16(jnp.where(lane < NA_HEAD_DIM, outs[0], outs[1]))


def _na_attention(proj, bias_tab, batch, seq, col_q, col_k, col_v):
    T = proj.shape[0]
    n_rows = seq // GRID_W
    n_groups = n_rows // NA_GROUP_ROWS
    tq = NA_GROUP_ROWS * GRID_W
    n_pairs = NA_HEADS // 2

    def cls(g):
        return jnp.where(g == 0, 0, jnp.where(g == n_groups - 1, 2, 1))

    kern = functools.partial(_na_kernel, n_rows=n_rows)
    return pl.pallas_call(
        kern,
        out_shape=jax.ShapeDtypeStruct((T, NA_WIDTH), jnp.bfloat16),
        grid=(n_pairs, batch, n_groups),
        in_specs=[
            pl.BlockSpec((tq, LANES), lambda hp, b, g: (b * n_groups + g, col_q // LANES + hp)),
            pl.BlockSpec((seq, LANES), lambda hp, b, g: (b, col_k // LANES + hp)),
            pl.BlockSpec((seq, LANES), lambda hp, b, g: (b, col_v // LANES + hp)),
            pl.BlockSpec((1, 2, tq, NA_KEY_ROWS * GRID_W), lambda hp, b, g: (cls(g), hp, 0, 0)),
        ],
        out_specs=pl.BlockSpec((tq, LANES), lambda hp, b, g: (b * n_groups + g, hp)),
        compiler_params=_cparams(("arbitrary", "arbitrary", "arbitrary")),
        name="na_attn",
    )(proj, proj, proj, bias_tab)


def _na_bias_tables(rpb, n_rows):
    H = rpb.shape[0]
    kw = NA_WIN_COLS
    W = GRID_W
    pad = W - kw
    rp = jnp.pad(rpb.astype(jnp.float32), ((0, 0), (0, 0), (pad, pad)), constant_values=NEG_BIG)
    toep = jnp.stack([rp[:, :, W - 1 - qc: 2 * W - 1 - qc] for qc in range(W)], axis=2)
    cidx = np.arange(W)
    col_start = np.clip(cidx - kw // 2, 0, W - kw)
    col_in = (cidx[None, :] >= col_start[:, None]) & (cidx[None, :] < col_start[:, None] + kw)
    toep = jnp.where(jnp.asarray(col_in)[None, None], toep, NEG_BIG)
    neg_blk = jnp.full((H, W, W), NEG_BIG, jnp.float32)
    tabs = []
    for c in range(3):
        rows_i = []
        for i in range(NA_GROUP_ROWS):
            if c == 0:
                j0, off = max(i - NA_WIN_ROWS // 2, 0), NA_WIN_ROWS - 1
            elif c == 1:
                j0, off = i, NA_WIN_ROWS - 1 - NA_WIN_ROWS // 2
            else:
                j0, off = min(i + NA_WIN_ROWS // 2, NA_KEY_ROWS - NA_WIN_ROWS), -1
            blks = []
            for j in range(NA_KEY_ROWS):
                if j0 <= j < j0 + NA_WIN_ROWS:
                    blks.append(toep[:, j - i + off])
                else:
                    blks.append(neg_blk)
            rows_i.append(jnp.concatenate(blks, axis=-1))
        tabs.append(jnp.concatenate(rows_i, axis=1))
    return jnp.stack(tabs, axis=0)


def _mla_prep_kernel(cq_ref, ckv_ref, kr_ref, gq_ref, gkv_ref, wqm_ref, wqs_ref, wk_ref, wv_ref,
                     pm_ref, cq_tab_ref, sq_tab_ref, csk_tab_ref, q_out, k_out, v_out, *, qscale):
    cq = _f32(cq_ref[...])
    cqn = _bf16(_rms(cq, gq_ref[...], n=MLA_Q_LORA))
    qm = _dot(cqn, wqm_ref[...])
    qs = _dot(cqn, wqs_ref[...])
    ctab = cq_tab_ref[...]
    stab = sq_tab_ref[...]
    for h in range(MLA_HEADS):
        sl = slice(h * LANES, (h + 1) * LANES)
        qh = (qm[:, sl] * ctab + qs[:, sl] * stab) * qscale
        q_out[0, h] = _bf16(qh.T)

    ckv = _f32(ckv_ref[...])
    ckvn = _bf16(_rms(ckv, gkv_ref[...]))
    kk = _dot(ckvn, wk_ref[...])
    vv = _dot(ckvn, wv_ref[...])
    n_chunk = v_out.shape[2]
    ones = jnp.ones((MLA_VROWS - MLA_V, MLA_KEY_CHUNK), jnp.bfloat16)
    for hp in range(MLA_HEADS // 2):
        vt = _bf16(vv[:, hp * LANES:(hp + 1) * LANES].T)
        for cc in range(n_chunk):
            ks = slice(cc * MLA_KEY_CHUNK, (cc + 1) * MLA_KEY_CHUNK)
            for h in range(2):
                r0 = h * MLA_VROWS
                v_out[0, hp, cc, r0:r0 + MLA_V, :] = vt[h * MLA_V:(h + 1) * MLA_V, ks]
                v_out[0, hp, cc, r0 + MLA_V:r0 + MLA_VROWS, :] = ones
    krr = _f32(kr_ref[...]) * csk_tab_ref[...]
    kplace = _dot(_bf16(krr), pm_ref[...])
    for h in range(MLA_HEADS):
        sl = slice(h * LANES, (h + 1) * LANES)
        k_out[:, sl] = _bf16(kk[:, sl] + kplace)


def _mla_prep(proj, col_cq, col_ckv, col_kr, gq, gkv, wqm, wqs, wk, wv, pm, cq_tab, sq_tab, csk_tab,
              seq, qscale, tm=512):
    T = proj.shape[0]
    batch = T // seq
    n_s = seq // tm
    n_pairs = MLA_HEADS // 2
    cpt = tm // MLA_KEY_CHUNK
    full = lambda a: pl.BlockSpec(a.shape, lambda i: (0,) * a.ndim)
    tab = pl.BlockSpec((tm, LANES), lambda i: (i % n_s, 0))
    kern = functools.partial(_mla_prep_kernel, qscale=qscale)
    return pl.pallas_call(
        kern,
        out_shape=(
            jax.ShapeDtypeStruct((batch, MLA_HEADS, LANES, seq), jnp.bfloat16),
            jax.ShapeDtypeStruct((T, MLA_HEADS * LANES), jnp.bfloat16),
            jax.ShapeDtypeStruct((batch, n_pairs, seq // MLA_KEY_CHUNK, 2 * MLA_VROWS, MLA_KEY_CHUNK),
                                 jnp.bfloat16),
        ),
        grid=(T // tm,),
        in_specs=[
            pl.BlockSpec((tm, Q_LORA_PAD), lambda i: (i, col_cq // Q_LORA_PAD)),
            pl.BlockSpec((tm, MLA_KV_LORA), lambda i: (i, col_ckv // MLA_KV_LORA)),
            pl.BlockSpec((tm, LANES), lambda i: (i, col_kr // LANES)),
            full(gq), full(gkv), full(wqm), full(wqs), full(wk), full(wv), full(pm),
            tab, tab, tab,
        ],
        out_specs=(
            pl.BlockSpec((1, MLA_HEADS, LANES, tm), lambda i: (i // n_s, 0, 0, i % n_s)),
            pl.BlockSpec((tm, MLA_HEADS * LANES), lambda i: (i, 0)),
            pl.BlockSpec((1, n_pairs, cpt, 2 * MLA_VROWS, MLA_KEY_CHUNK),
                         lambda i: (i // n_s, 0, i % n_s, 0, 0)),
        ),
        compiler_params=_cparams(("arbitrary",)),
        name="mla_prep",
    )(proj, proj, proj, gq, gkv, wqm, wqs, wk, wv, pm, cq_tab, sq_tab, csk_tab)


def _mla_attn_kernel(qt_ref, k_ref, vt_ref, o_ref, m_scr, acc_scr, st_scr):
    n_chunks = vt_ref.shape[2]
    tk = MLA_KEY_CHUNK
    m_scr[...] = jnp.full(m_scr.shape, -jnp.inf, jnp.float32)
    acc_scr[...] = jnp.zeros(acc_scr.shape, jnp.float32)

    def scores(c, slot):
        k0 = pl.multiple_of(c * tk, tk)
        for h in range(2):
            kc = k_ref[pl.ds(k0, tk), h * LANES:(h + 1) * LANES]
            st_scr[slot, h] = _dot(kc, qt_ref[0, h])

    def step(c, slot):
        scores(jnp.minimum(c + 1, n_chunks - 1), 1 - slot)
        vt = vt_ref[0, 0, c]
        pts, alphas = [], []
        for h in range(2):
            st = st_scr[slot, h]
            m_old = m_scr[h]
            m_new = jnp.maximum(m_old, jnp.max(st, axis=0, keepdims=True))
            alpha = jnp.exp2(m_old - m_new)
            pt = jnp.exp2(st - m_new)
            m_scr[h] = m_new
            pts.append(_bf16(pt))
            alphas.append(alpha)
        for h in range(2):
            acc_scr[h] = alphas[h] * acc_scr[h] + _dot(vt[h * MLA_VROWS:(h + 1) * MLA_VROWS, :], pts[h])

    def body(j, carry):
        for u in range(MLA_STEPS_PER_TRIP):
            step(MLA_STEPS_PER_TRIP * j + u, u % 2)
        return carry

    scores(0, 0)
    lax.fori_loop(0, n_chunks // MLA_STEPS_PER_TRIP, body, 0)
    outs = []
    for h in range(2):
        acc = acc_scr[h]
        outs.append(acc[0:MLA_V] / acc[MLA_V:MLA_V + 1])
    o_ref[...] = _bf16(jnp.concatenate(outs, axis=0).T)


def _mla_attention(q_t, k_cat, v_t, batch, seq, tq=512):
    T = k_cat.shape[0]
    n_q = seq // tq
    n_pairs = MLA_HEADS // 2
    n_chunks = seq // MLA_KEY_CHUNK
    assert n_chunks % MLA_STEPS_PER_TRIP == 0
    return pl.pallas_call(
        _mla_attn_kernel,
        out_shape=jax.ShapeDtypeStruct((T, MLA_WIDTH), jnp.bfloat16),
        grid=(batch, n_pairs, n_q),
        in_specs=[
            pl.BlockSpec((1, 2, LANES, tq), lambda b, hp, i: (b, hp, 0, i)),
            pl.BlockSpec((seq, 2 * LANES), lambda b, hp, i: (b, hp)),
            pl.BlockSpec((1, 1, n_chunks, 2 * MLA_VROWS, MLA_KEY_CHUNK), lambda b, hp, i: (b, hp, 0, 0, 0)),
        ],
        out_specs=pl.BlockSpec((tq, LANES), lambda b, hp, i: (b * n_q + i, hp)),
        scratch_shapes=[
            pltpu.VMEM((2, 1, tq), jnp.float32),
            pltpu.VMEM((2, MLA_VROWS, tq), jnp.float32),
            pltpu.VMEM((2, 2, MLA_KEY_CHUNK, tq), jnp.float32),
        ],
        compiler_params=_cparams(("arbitrary", "arbitrary", "arbitrary")),
        name="mla_attn",
    )(q_t, k_cat, v_t)


def _out_proj_kernel(ona_ref, omla_ref, gate_ref, x_ref, mod_ref, gpost_ref, gpre_ref, wna_ref,
                     wmla_ref, wout_ref, wr_ref, br_ref, x1_ref, h2_ref, lg_ref, *, d_model):
    D = d_model
    gates = gate_ref[...]
    merged = (_f32(gates[:, 0:D]) * _dot(ona_ref[...], wna_ref[...])
              + _f32(gates[:, D:2 * D]) * _dot(omla_ref[...], wmla_ref[...]))
    y = _dot(_bf16(merged), wout_ref[...])
    gate_a = mod_ref[0, :, 2 * D:3 * D]
    shift_f = mod_ref[0, :, 3 * D:4 * D]
    scale_f = mod_ref[0, :, 4 * D:5 * D]
    x1 = x_ref[...] + gate_a * _rms(y, gpost_ref[...])
    x1_ref[...] = x1
    h2 = _rms(x1, gpre_ref[...]) * (1.0 + scale_f) + shift_f
    _store_row_tiles(h2_ref, h2)
    lg_ref[...] = _dot(_bf16(h2), wr_ref[...]) + br_ref[...]


def _out_proj(o_na, o_mla, proj, x2, mod3, g_post, g_pre, w_na, w_mla, w_out, w_r, b_r, seq, tm=512):
    T, D = x2.shape
    full = lambda a: pl.BlockSpec(a.shape, lambda i: (0,) * a.ndim)
    row = lambda w: pl.BlockSpec((tm, w), lambda i: (i, 0))
    kern = functools.partial(_out_proj_kernel, d_model=D)
    g_post = g_post.reshape(1, D)
    g_pre = g_pre.reshape(1, D)
    return pl.pallas_call(
        kern,
        out_shape=(
            jax.ShapeDtypeStruct((T, D), jnp.float32),
            jax.ShapeDtypeStruct((T * (D // LANES), LANES), jnp.float32),
            jax.ShapeDtypeStruct((T, LANES), jnp.float32),
        ),
        grid=(T // tm,),
        in_specs=[
            row(NA_WIDTH), row(MLA_WIDTH),
            pl.BlockSpec((tm, 2 * D), lambda i: (i, COL_GATES // (2 * D))),
            row(D),
            pl.BlockSpec((1, 1, mod3.shape[-1]), lambda i: ((i * tm) // seq, 0, 0)),
            full(g_post), full(g_pre), full(w_na), full(w_mla), full(w_out), full(w_r), full(b_r),
        ],
        out_specs=(row(D), pl.BlockSpec((tm * (D // LANES), LANES), lambda i: (i, 0)), row(LANES)),
        compiler_params=_cparams(("arbitrary",)),
        name="out_proj",
    )(o_na, o_mla, proj, x2, mod3, g_post, g_pre, w_na, w_mla, w_out, w_r, b_r)


def _route_kernel(lg_ref, r_ref, cnt_ref, carry_scr, *, sub):
    @pl.when(pl.program_id(0) == 0)
    def _():
        carry_scr[...] = jnp.zeros(carry_scr.shape, jnp.float32)

    tr = lg_ref.shape[0]
    lane = lax.broadcasted_iota(jnp.int32, (sub, LANES), 1).astype(jnp.float32)
    ri = lax.broadcasted_iota(jnp.int32, (sub, sub), 0)
    ci = lax.broadcasted_iota(jnp.int32, (sub, sub), 1)
    tri = jnp.where(ri >= ci, 1.0, 0.0).astype(jnp.bfloat16)
    for s0 in range(0, tr, sub):
        work = lg_ref[s0:s0 + sub, :]
        sels, vals, idxs = [], [], []
        for _k in range(TOP_K):
            mk = jnp.max(work, axis=-1, keepdims=True)
            ik = jnp.min(jnp.where(work == mk, lane, float(LANES)), axis=-1, keepdims=True)
            sk = lane == ik
            work = jnp.where(sk, -jnp.inf, work)
            sels.append(sk)
            vals.append(mk)
            idxs.append(ik)
        es = [jnp.exp(v - vals[0]) for v in vals]
        denom = es[0] + es[1] + es[2] + es[3]
        onehot = jnp.zeros((sub, LANES), jnp.float32)
        for sk in sels:
            onehot = jnp.where(sk, 1.0, onehot)
        prefix = _dot(tri, _bf16(onehot))
        carry = carry_scr[...]
        rank_mat = carry + prefix - 1.0
        res = jnp.zeros((sub, LANES), jnp.float32)
        for kk in range(TOP_K):
            rank_k = jnp.sum(jnp.where(sels[kk], rank_mat, 0.0), axis=-1, keepdims=True)
            res = jnp.where(lane == kk, idxs[kk], res)
            res = jnp.where(lane == TOP_K + kk, rank_k, res)
            res = jnp.where(lane == 2 * TOP_K + kk, es[kk] / denom, res)
        r_ref[s0:s0 + sub, :] = res
        carry_scr[...] = carry + jnp.sum(onehot, axis=0, keepdims=True)
    cnt_ref[...] = carry_scr[...]


def _route(logits, tr=2048, sub=256):
    T = logits.shape[0]
    kern = functools.partial(_route_kernel, sub=sub)
    return pl.pallas_call(
        kern,
        out_shape=(
            jax.ShapeDtypeStruct((T, LANES), jnp.float32),
            jax.ShapeDtypeStruct((1, LANES), jnp.float32),
        ),
        grid=(T // tr,),
        in_specs=[pl.BlockSpec((tr, LANES), lambda i: (i, 0))],
        out_specs=(
            pl.BlockSpec((tr, LANES), lambda i: (i, 0)),
            pl.BlockSpec((1, LANES), lambda i: (0, 0)),
        ),
        scratch_shapes=[pltpu.VMEM((1, LANES), jnp.float32)],
        compiler_params=_cparams(("arbitrary",)),
        name="route",
    )(logits)


def _slot_map_kernel(dest_ref, tok0_ref, row0_ref, tok_ref, row_ref, sem, *, n_tokens):
    i = pl.program_id(0)
    ts = dest_ref.shape[0] // TOP_K

    @pl.when(i == 0)
    def _():
        fill_tok = pltpu.make_async_copy(tok0_ref, tok_ref, sem.at[0])
        fill_row = pltpu.make_async_copy(row0_ref, row_ref, sem.at[1])
        fill_tok.start()
        fill_row.start()
        fill_tok.wait()
        fill_row.wait()

    def place(j, carry):
        t = i * ts + j
        for kk in range(TOP_K):
            d = dest_ref[j * TOP_K + kk]
            tok_ref[d] = t
            row_ref[d] = kk * n_tokens + t
        return carry

    lax.fori_loop(0, ts, place, 0, unroll=4)


def _slot_map(dest, cap, n_tokens, ts=256):
    kern = functools.partial(_slot_map_kernel, n_tokens=n_tokens)
    smem_full = pl.BlockSpec(memory_space=pltpu.SMEM)
    return pl.pallas_call(
        kern,
        out_shape=(jax.ShapeDtypeStruct((cap,), jnp.int32), jax.ShapeDtypeStruct((cap,), jnp.int32)),
        grid=(n_tokens // ts,),
        in_specs=[pl.BlockSpec((ts * TOP_K,), lambda i: (i,), memory_space=pltpu.SMEM),
                  pl.BlockSpec(memory_space=pl.ANY), pl.BlockSpec(memory_space=pl.ANY)],
        out_specs=(smem_full, smem_full),
        scratch_shapes=[pltpu.SemaphoreType.DMA((2,))],
        compiler_params=_cparams(("arbitrary",)),
        name="slot_map",
    )(dest, jnp.zeros((cap,), jnp.int32), jnp.full((cap,), -1, jnp.int32))


def _moe_kernel(blk_e_ref, nvalid_ref, tok_ref, tok_next_ref, row_ref, h_ref, w1_ref, b1_ref, w2_ref,
                b2_ref, out_ref, w1p_scr, w2b_scr, xbuf, ybuf, gsem, ssem, *, n_items):
    i = pl.program_id(0)
    nvalid = nvalid_ref[0]
    d_ff = w2_ref.shape[1]
    n_groups = (2 * d_ff) // (2 * LANES)

    def tile(r):
        return pl.ds(pl.multiple_of(r * SUBLANES, SUBLANES), SUBLANES)

    def gather_copy(t, j, slot):
        return pltpu.make_async_copy(h_ref.at[tile(t), :], xbuf.at[slot, pl.ds(j * SUBLANES, SUBLANES), :],
                                     gsem.at[slot])

    def scatter_copy(j, r, slot):
        return pltpu.make_async_copy(ybuf.at[slot, pl.ds(j * SUBLANES, SUBLANES), :], out_ref.at[tile(r), :],
                                     ssem.at[slot])

    def wait_gathers(slot):
        for _j in range(MOE_BLOCK):
            gather_copy(0, 0, slot).wait()

    def wait_scatters(slot):
        for _j in range(MOE_BLOCK):
            scatter_copy(0, 0, slot).wait()

    @pl.when(i == 0)
    def _():
        ybuf[1] = jnp.zeros(ybuf.shape[1:], ybuf.dtype)
        for half in range(2):
            spare = pltpu.make_async_copy(
                ybuf.at[1],
                out_ref.at[pl.ds((n_items + half * MOE_BLOCK) * SUBLANES, MOE_BLOCK * SUBLANES), :],
                ssem.at[1])
            spare.start()
            spare.wait()
        for j in range(MOE_BLOCK):
            gather_copy(tok_ref[0, 0, j], j, 0).start()

    @pl.when(i < nvalid)
    def _():
        e = blk_e_ref[i]
        e_prev = blk_e_ref[jnp.maximum(i - 1, 0)]

        @pl.when((i == 0) | (e != e_prev))
        def _():
            r = lax.broadcasted_iota(jnp.int32, (2 * LANES, 2 * LANES), 0)
            c = lax.broadcasted_iota(jnp.int32, (2 * LANES, 2 * LANES), 1)
            src = jnp.where(c < LANES, 2 * c, 2 * (c - LANES) + 1)
            perm = jnp.where(r == src, 1.0, 0.0).astype(jnp.bfloat16)
            for gI in range(n_groups):
                sl = slice(gI * 2 * LANES, (gI + 1) * 2 * LANES)
                w1p_scr[:, sl] = _bf16(_dot(_bf16(w1_ref[0, :, sl]), perm))
            w2b_scr[...] = _bf16(w2_ref[0])

        def block_step(slot):
            wait_gathers(slot)
            for j in range(MOE_BLOCK):
                gather_copy(tok_next_ref[0, 0, j], j, 1 - slot).start()

            @pl.when(i >= 2)
            def _():
                wait_scatters(slot)

            xb = _bf16(_load_row_tiles(xbuf.at[slot], MOE_BLOCK))
            hcat = _dot(xb, w1p_scr[...]) + b1_ref[0]
            acts = []
            for gI in range(n_groups):
                glu = jnp.minimum(hcat[:, gI * 2 * LANES: gI * 2 * LANES + LANES], SWIGLU_LIMIT)
                lin = jnp.clip(hcat[:, gI * 2 * LANES + LANES: (gI + 1) * 2 * LANES],
                               -SWIGLU_LIMIT, SWIGLU_LIMIT)
                acts.append(_bf16(glu * jax.nn.sigmoid(SWIGLU_ALPHA * glu) * (lin + 1.0)))
            act = jnp.concatenate(acts, axis=-1)
            _store_row_tiles(ybuf.at[slot], _dot(act, w2b_scr[...]) + b2_ref[0])

            spare0 = n_items + slot * MOE_BLOCK
            for j in range(MOE_BLOCK):
                r = row_ref[0, 0, j]
                scatter_copy(j, jnp.where(r >= 0, r, spare0 + j), slot).start(priority=1)

            @pl.when(i == nvalid - 1)
            def _():
                wait_gathers(1 - slot)
                wait_scatters(slot)

                @pl.when(i >= 1)
                def _():
                    wait_scatters(1 - slot)

        parity = lax.rem(i, 2)
        for slot in range(2):
            pl.when(parity == slot)(functools.partial(block_step, slot))


def _moe_ffn(h2_tiles, slot_tok, slot_row, blk_e, nvalid, w1, b1p, w2, b2):
    D = w1.shape[1]
    assert D == SUBLANES * LANES
    T = h2_tiles.shape[0] // SUBLANES
    E, _, F2 = w1.shape
    F = w2.shape[1]
    nblk = slot_tok.shape[0]
    n_items = T * TOP_K

    def cur(i, be, nv):
        return (jnp.minimum(i, nv[0] - 1), 0, 0)

    def nxt(i, be, nv):
        return (jnp.minimum(i + 1, nv[0] - 1), 0, 0)

    slot_spec = lambda f: pl.BlockSpec((1, 1, MOE_BLOCK), f, memory_space=pltpu.SMEM)
    grid_spec = pltpu.PrefetchScalarGridSpec(
        num_scalar_prefetch=2,
        grid=(nblk,),
        in_specs=[
            slot_spec(cur), slot_spec(nxt), slot_spec(cur),
            pl.BlockSpec(memory_space=pl.ANY),
            pl.BlockSpec((1, D, F2), lambda i, be, nv: (be[i], 0, 0)),
            pl.BlockSpec((1, 1, F2), lambda i, be, nv: (be[i], 0, 0)),
            pl.BlockSpec((1, F, D), lambda i, be, nv: (be[i], 0, 0)),
            pl.BlockSpec((1, 1, D), lambda i, be, nv: (be[i], 0, 0)),
        ],
        out_specs=pl.BlockSpec(memory_space=pl.ANY),
        scratch_shapes=[
            pltpu.VMEM((D, F2), jnp.bfloat16),
            pltpu.VMEM((F, D), jnp.bfloat16),
            pltpu.VMEM((2, MOE_BLOCK * SUBLANES, LANES), jnp.float32),
            pltpu.VMEM((2, MOE_BLOCK * SUBLANES, LANES), jnp.float32),
            pltpu.SemaphoreType.DMA((2,)),
            pltpu.SemaphoreType.DMA((2,)),
        ],
    )
    kern = functools.partial(_moe_kernel, n_items=n_items)
    return pl.pallas_call(
        kern,
        out_shape=jax.ShapeDtypeStruct(((n_items + 2 * MOE_BLOCK) * SUBLANES, LANES), jnp.float32),
        grid_spec=grid_spec,
        compiler_params=_cparams(("arbitrary",)),
        name="moe_ffn",
    )(blk_e, nvalid, slot_tok, slot_tok, slot_row, h2_tiles, w1, b1p, w2, b2)


def _combine_kernel(y0_ref, y1_ref, y2_ref, y3_ref, r_ref, x1_ref, mod_ref, g_ref, o_ref, *, d_model):
    D = d_model
    tc = x1_ref.shape[0]
    r = r_ref[...]
    moe = None
    for kk, y_ref in enumerate((y0_ref, y1_ref, y2_ref, y3_ref)):
        term = r[:, 2 * TOP_K + kk: 2 * TOP_K + kk + 1] * _load_row_tiles(y_ref, tc)
        moe = term if moe is None else moe + term
    gate_f = mod_ref[0, :, 5 * D:6 * D]
    o_ref[...] = x1_ref[...] + gate_f * _rms(moe, g_ref[...])


def _combine(y_k, route, x1, mod3, g_post, seq, tc=512):
    T, D = x1.shape
    n_t = T // tc
    kern = functools.partial(_combine_kernel, d_model=D)
    y_spec = lambda kk: pl.BlockSpec((tc * SUBLANES, LANES), lambda i: (kk * n_t + i, 0))
    return pl.pallas_call(
        kern,
        out_shape=jax.ShapeDtypeStruct((T, D), jnp.float32),
        grid=(n_t,),
        in_specs=[
            y_spec(0), y_spec(1), y_spec(2), y_spec(3),
            pl.BlockSpec((tc, LANES), lambda i: (i, 0)),
            pl.BlockSpec((tc, D), lambda i: (i, 0)),
            pl.BlockSpec((1, 1, mod3.shape[-1]), lambda i: ((i * tc) // seq, 0, 0)),
            pl.BlockSpec((1, D), lambda i: (0, 0)),
        ],
        out_specs=pl.BlockSpec((tc, D), lambda i: (i, 0)),
        compiler_params=_cparams(("arbitrary",)),
        name="combine",
    )(y_k, y_k, y_k, y_k, route, x1, mod3, g_post.reshape(1, D))


def _rope_swap_cols(w):
    nf = MLA_ROPE // 4
    return jnp.concatenate([-w[..., nf:2 * nf], w[..., 0:nf], -w[..., 3 * nf:4 * nf], w[..., 2 * nf:3 * nf]],
                           axis=-1)


def _rope_tables(seq):
    t = jnp.arange(seq, dtype=jnp.int32)
    rows, cols = t // GRID_W, t % GRID_W
    nf = MLA_ROPE // 4
    inv = ROPE_THETA ** (-jnp.arange(nf, dtype=jnp.float32) / nf)
    ar = rows.astype(jnp.float32)[:, None] * inv
    ac = cols.astype(jnp.float32)[:, None] * inv
    c32 = jnp.concatenate([jnp.cos(ar), jnp.cos(ar), jnp.cos(ac), jnp.cos(ac)], axis=-1)
    s32 = jnp.concatenate([jnp.sin(ar), jnp.sin(ar), jnp.sin(ac), jnp.sin(ac)], axis=-1)
    one = jnp.ones((seq, MLA_NOPE), jnp.float32)
    z64 = jnp.zeros((seq, MLA_NOPE), jnp.float32)
    z32 = jnp.zeros((seq, LANES - MLA_NOPE - MLA_ROPE), jnp.float32)
    cq_tab = jnp.concatenate([one, c32, z32], axis=-1)
    sq_tab = jnp.concatenate([z64, s32, z32], axis=-1)
    csk_tab = jnp.concatenate([c32, s32, z64], axis=-1)
    return cq_tab, sq_tab, csk_tab


def kernel(x, c, w_ada, b_ada, g_attn_pre, g_attn_post, w_in, b_gate, na_rpb, q_norm_g, kv_norm_g,
           w_uq, w_ukv, w_na_up, w_mla_up, w_out, g_ffn_pre, g_ffn_post, w_router, b_router,
           w1, b1, w2, b2):
    B, S, D = x.shape
    T = B * S
    depth = w_ada.shape[0]
    E = w_router.shape[-1]
    n_rows = S // GRID_W
    assert S % (NA_GROUP_ROWS * GRID_W) == 0 and n_rows >= NA_KEY_ROWS
    assert E <= LANES and T % 2048 == 0

    bf = jnp.bfloat16
    cq_tab, sq_tab, csk_tab = _rope_tables(S)
    pm_np = np.zeros((LANES, LANES), np.float32)
    for cidx in range(MLA_ROPE):
        pm_np[cidx, MLA_NOPE + cidx] = 1.0
        pm_np[MLA_ROPE + cidx, MLA_NOPE + cidx] = 1.0
    pm = jnp.asarray(pm_np, bf)
    qscale = float((MLA_NOPE + MLA_ROPE) ** -0.5 * math.log2(math.e))

    col_na = 2 * D
    col_cq = col_na + 3 * NA_WIDTH
    col_ckv = col_cq + Q_LORA_PAD
    col_kr = col_ckv + MLA_KV_LORA
    assert col_cq % Q_LORA_PAD == 0 and col_ckv % MLA_KV_LORA == 0

    x2 = x.reshape(T, D)
    for l in range(depth):
        mod = _ada_mod(c, w_ada[l], b_ada[l])
        mod3 = mod.reshape(B, 1, 6 * D)

        o_na, o_cq, o_ckv, o_kr, o_g = np.cumsum([0, 3 * NA_WIDTH, MLA_Q_LORA, MLA_KV_LORA, MLA_ROPE]).tolist()
        wi = w_in[l]
        w_kr = wi[:, o_kr:o_g]
        w_ext = jnp.concatenate([
            wi[:, o_g:o_g + 2 * D],
            wi[:, o_na:o_cq],
            wi[:, o_cq:o_ckv], jnp.zeros((D, Q_LORA_PAD - MLA_Q_LORA), wi.dtype),
            wi[:, o_ckv:o_kr],
            w_kr, _rope_swap_cols(w_kr), jnp.zeros((D, LANES - 2 * MLA_ROPE), wi.dtype),
        ], axis=1).astype(bf)
        proj = _in_proj(x2, mod3, g_attn_pre[l], w_ext, b_gate[l], S)

        bias_tab = _na_bias_tables(na_rpb[l], n_rows)
        o_na_tok = _na_attention(proj, bias_tab, B, S, col_na, col_na + NA_WIDTH, col_na + 2 * NA_WIDTH)

        wq = w_uq[l]
        zq = jnp.zeros((MLA_Q_LORA, MLA_HEADS, LANES - MLA_NOPE - MLA_ROPE), wq.dtype)
        wqm = jnp.concatenate([wq, zq], axis=-1).reshape(MLA_Q_LORA, MLA_HEADS * LANES)
        wqs = jnp.concatenate([jnp.zeros((MLA_Q_LORA, MLA_HEADS, MLA_NOPE), wq.dtype),
                               _rope_swap_cols(wq[..., MLA_NOPE:]), zq], axis=-1
                              ).reshape(MLA_Q_LORA, MLA_HEADS * LANES)
        rpad = ((0, Q_LORA_PAD - MLA_Q_LORA), (0, 0))
        wqm = jnp.pad(wqm, rpad).astype(bf)
        wqs = jnp.pad(wqs, rpad).astype(bf)
        wkv = w_ukv[l]
        wk = jnp.concatenate([wkv[..., :MLA_NOPE], jnp.zeros((MLA_KV_LORA, MLA_HEADS, LANES - MLA_NOPE), wkv.dtype)],
                             axis=-1).reshape(MLA_KV_LORA, MLA_HEADS * LANES).astype(bf)
        wv = wkv[..., MLA_NOPE:].reshape(MLA_KV_LORA, MLA_WIDTH).astype(bf)
        gq = jnp.pad(q_norm_g[l], (0, Q_LORA_PAD - MLA_Q_LORA)).reshape(1, Q_LORA_PAD)
        gkv = kv_norm_g[l].reshape(1, MLA_KV_LORA)
        q_cat, k_cat, v_mla = _mla_prep(proj, col_cq, col_ckv, col_kr, gq, gkv, wqm, wqs, wk, wv, pm,
                                        cq_tab, sq_tab, csk_tab, S, qscale)
        o_mla_tok = _mla_attention(q_cat, k_cat, v_mla, B, S)

        w_r = jnp.pad(w_router[l], ((0, 0), (0, LANES - E))).astype(bf)
        b_r = jnp.concatenate([b_router[l], jnp.full((LANES - E,), NEG_BIG, jnp.float32)]).reshape(1, LANES)
        x1, h2, logits = _out_proj(o_na_tok, o_mla_tok, proj, x2, mod3, g_attn_post[l], g_ffn_pre[l],
                                   w_na_up[l].astype(bf), w_mla_up[l].astype(bf), w_out[l].astype(bf),
                                   w_r, b_r, S)

        route, counts = _route(logits)
        e_idx = route[:, 0:TOP_K].astype(jnp.int32)
        rank = route[:, TOP_K:2 * TOP_K].astype(jnp.int32)
        cnt = counts[0, :E].astype(jnp.int32)
        padded = ((cnt + MOE_BLOCK - 1) // MOE_BLOCK) * MOE_BLOCK
        pend = jnp.cumsum(padded)
        pstart = pend - padded
        n_items = T * TOP_K
        nblk = -(-n_items // MOE_BLOCK) + E
        cap = nblk * MOE_BLOCK
        dest = (rank + jnp.sum(jnp.where(e_idx[..., None] == jnp.arange(E, dtype=jnp.int32), pstart, 0), axis=-1)
                ).reshape(n_items)
        blk_off = jnp.arange(nblk, dtype=jnp.int32) * MOE_BLOCK
        blk_e = jnp.minimum(jnp.sum((pend[None, :] <= blk_off[:, None]).astype(jnp.int32), axis=-1), E - 1)
        nvalid = (pend[-1:] // MOE_BLOCK).astype(jnp.int32)

        slot_tok, slot_row = _slot_map(dest, cap, T)
        slot_tok = slot_tok.reshape(nblk, 1, MOE_BLOCK)
        slot_row = slot_row.reshape(nblk, 1, MOE_BLOCK)
        F2 = w1.shape[-1]
        b1p = b1[l].reshape(E, F2 // (2 * LANES), LANES, 2).transpose(0, 1, 3, 2).reshape(E, 1, F2)
        y_k = _moe_ffn(h2, slot_tok, slot_row, blk_e, nvalid, w1[l], b1p, w2[l], b2[l].reshape(E, 1, D))
        x2 = _combine(y_k, route, x1, mod3, g_ffn_post[l], S)
    return x2.reshape(B, S, D)
```

```python
import functools
import math

import numpy as np
import jax
import jax.numpy as jnp
from jax import lax
from jax.experimental import pallas as pl
from jax.experimental.pallas import tpu as pltpu

GRID_W = 64
NA_HEADS = 8
NA_HEAD_DIM = 64
NA_WIN_ROWS = 8
NA_WIN_COLS = 16
NA_WIDTH = NA_HEADS * NA_HEAD_DIM
MLA_HEADS = 8
MLA_Q_LORA = 384
MLA_KV_LORA = 256
MLA_NOPE = 64
MLA_ROPE = 32
MLA_V = 64
MLA_WIDTH = MLA_HEADS * MLA_V
MLA_KEY_CHUNK = 256
MLA_VROWS = MLA_V + 16
MLA_STEPS_PER_TRIP = 8
ROPE_THETA = 100.0
TOP_K = 4
SWIGLU_ALPHA = 1.702
SWIGLU_LIMIT = 7.0
MOE_BLOCK = 256
RMS_EPS = 1e-6
NEG_BIG = -1e30

LANES = 128
VMEM_LIMIT_BYTES = 56 * 1024 * 1024

Q_LORA_PAD = 512
COL_GATES = 0
NA_GROUP_ROWS = 8
NA_KEY_ROWS = 16


def _f32(x):
    return x.astype(jnp.float32)


def _bf16(x):
    return x.astype(jnp.bfloat16)


def _dot(a, b):
    return jnp.dot(a, b, preferred_element_type=jnp.float32)


def _dot_nt(a, b):
    return lax.dot_general(a, b, (((1,), (1,)), ((), ())), preferred_element_type=jnp.float32)


def _rms(x, g, n=None):
    n = x.shape[-1] if n is None else n
    ms = jnp.sum(x * x, axis=-1, keepdims=True) * (1.0 / n)
    return x * lax.rsqrt(ms + RMS_EPS) * g


SUBLANES = 8


def _store_row_tiles(ref, x, base=0):
    n = x.shape[0]
    for c in range(x.shape[1] // LANES):
        ref[pl.ds(base + c, n, stride=SUBLANES), :] = x[:, c * LANES:(c + 1) * LANES]


def _load_row_tiles(ref, n, base=0):
    chunks = [ref[pl.ds(base + c, n, stride=SUBLANES), :] for c in range(SUBLANES)]
    return jnp.concatenate(chunks, axis=-1)


def _cparams(sem):
    return pltpu.CompilerParams(dimension_semantics=sem, vmem_limit_bytes=VMEM_LIMIT_BYTES)


def _ada_kernel(c_ref, w_ref, b_ref, o_ref):
    c = c_ref[...]
    sc = c * jax.nn.sigmoid(c)
    o_ref[...] = _dot(_bf16(sc), _bf16(w_ref[...])) + b_ref[...]


def _ada_mod(c, w_ada, b_ada):
    B, D = c.shape
    n_out = w_ada.shape[1]
    return pl.pallas_call(
        _ada_kernel,
        out_shape=jax.ShapeDtypeStruct((B, n_out), jnp.float32),
        grid=(n_out // D,),
        in_specs=[
            pl.BlockSpec((B, D), lambda j: (0, 0)),
            pl.BlockSpec((D, D), lambda j: (0, j)),
            pl.BlockSpec((1, D), lambda j: (0, j)),
        ],
        out_specs=pl.BlockSpec((B, D), lambda j: (0, j)),
        compiler_params=_cparams(("arbitrary",)),
        name="ada_mod",
    )(c, w_ada, b_ada.reshape(1, n_out))


def _in_proj_kernel(x_ref, mod_ref, g_ref, w_ref, bg_ref, o_ref, *, d_model, n_gate, chunk):
    x = x_ref[...]
    shift = mod_ref[0, :, 0:d_model]
    scale = mod_ref[0, :, d_model:2 * d_model]
    h = _rms(x, g_ref[...]) * (1.0 + scale) + shift
    hb = _bf16(h)
    n_total = w_ref.shape[1]
    for c0 in range(0, n_total, chunk):
        c1 = min(c0 + chunk, n_total)
        acc = _dot(hb, w_ref[:, c0:c1])
        if c0 < n_gate:
            acc = jax.nn.sigmoid(acc + bg_ref[:, c0:c1])
        o_ref[:, c0:c1] = _bf16(acc)


def _in_proj(x2, mod3, g_pre, w_ext, b_gate, seq, tm=512):
    T, D = x2.shape
    n_total = w_ext.shape[1]
    n_gate = b_gate.shape[-1]
    kern = functools.partial(_in_proj_kernel, d_model=D, n_gate=n_gate, chunk=512)
    return pl.pallas_call(
        kern,
        out_shape=jax.ShapeDtypeStruct((T, n_total), jnp.bfloat16),
        grid=(T // tm,),
        in_specs=[
            pl.BlockSpec((tm, D), lambda i: (i, 0)),
            pl.BlockSpec((1, 1, mod3.shape[-1]), lambda i: ((i * tm) // seq, 0, 0)),
            pl.BlockSpec((1, D), lambda i: (0, 0)),
            pl.BlockSpec((D, n_total), lambda i: (0, 0)),
            pl.BlockSpec((1, n_gate), lambda i: (0, 0)),
        ],
        out_specs=pl.BlockSpec((tm, n_total), lambda i: (i, 0)),
        compiler_params=_cparams(("arbitrary",)),
        name="in_proj",
    )(x2, mod3, g_pre.reshape(1, D), w_ext, b_gate.reshape(1, n_gate))


def _na_kernel(q_ref, k_ref, v_ref, bias_ref, o_ref, *, n_rows):
    g = pl.program_id(2)
    kb = jnp.clip(g * NA_GROUP_ROWS - NA_WIN_ROWS // 2, 0, n_rows - NA_KEY_ROWS) * GRID_W
    kb = pl.multiple_of(kb, 256)
    nk = NA_KEY_ROWS * GRID_W
    q = q_ref[...]
    k = k_ref[pl.ds(kb, nk), :]
    v = v_ref[pl.ds(kb, nk), :]
    lane = lax.broadcasted_iota(jnp.int32, (1, LANES), 1)
    scale = NA_HEAD_DIM ** -0.5
    scores = []
    for h in range(2):
        in_head = (lane >= h * NA_HEAD_DIM) & (lane < (h + 1) * NA_HEAD_DIM)
        qh = jnp.where(in_head, q * scale, 0).astype(jnp.bfloat16)
        scores.append(_dot_nt(qh, k))
    outs = []
    for h in range(2):
        s = scores[h] + bias_ref[0, h]
        m = jnp.max(s, axis=-1, keepdims=True)
        p = jnp.exp(s - m)
        l = jnp.sum(p, axis=-1, keepdims=True)
        pv = _dot(_bf16(p), v)
        outs.append(pv / l)
    o_ref[...] = _bf16(jnp.where(lane < NA_HEAD_DIM, outs[0], outs[1]))


def _na_attention(proj, bias_tab, batch, seq, col_q, col_k, col_v):
    T = proj.shape[0]
    n_rows = seq // GRID_W
    n_groups = n_rows // NA_GROUP_ROWS
    tq = NA_GROUP_ROWS * GRID_W
    n_pairs = NA_HEADS // 2

    def cls(g):
        return jnp.where(g == 0, 0, jnp.where(g == n_groups - 1, 2, 1))

    kern = functools.partial(_na_kernel, n_rows=n_rows)
    return pl.pallas_call(
        kern,
        out_shape=jax.ShapeDtypeStruct((T, NA_WIDTH), jnp.bfloat16),
        grid=(n_pairs, batch, n_groups),
        in_specs=[
            pl.BlockSpec((tq, LANES), lambda hp, b, g: (b * n_groups + g, col_q // LANES + hp)),
            pl.BlockSpec((seq, LANES), lambda hp, b, g: (b, col_k // LANES + hp)),
            pl.BlockSpec((seq, LANES), lambda hp, b, g: (b, col_v // LANES + hp)),
            pl.BlockSpec((1, 2, tq, NA_KEY_ROWS * GRID_W), lambda hp, b, g: (cls(g), hp, 0, 0)),
        ],
        out_specs=pl.BlockSpec((tq, LANES), lambda hp, b, g: (b * n_groups + g, hp)),
        compiler_params=_cparams(("arbitrary", "arbitrary", "arbitrary")),
        name="na_attn",
    )(proj, proj, proj, bias_tab)


def _na_bias_tables(rpb, n_rows):
    H = rpb.shape[0]
    kw = NA_WIN_COLS
    W = GRID_W
    pad = W - kw
    rp = jnp.pad(rpb.astype(jnp.float32), ((0, 0), (0, 0), (pad, pad)), constant_values=NEG_BIG)
    toep = jnp.stack([rp[:, :, W - 1 - qc: 2 * W - 1 - qc] for qc in range(W)], axis=2)
    cidx = np.arange(W)
    col_start = np.clip(cidx - kw // 2, 0, W - kw)
    col_in = (cidx[None, :] >= col_start[:, None]) & (cidx[None, :] < col_start[:, None] + kw)
    toep = jnp.where(jnp.asarray(col_in)[None, None], toep, NEG_BIG)
    neg_blk = jnp.full((H, W, W), NEG_BIG, jnp.float32)
    tabs = []
    for c in range(3):
        rows_i = []
        for i in range(NA_GROUP_ROWS):
            if c == 0:
                j0, off = max(i - NA_WIN_ROWS // 2, 0), NA_WIN_ROWS - 1
            elif c == 1:
                j0, off = i, NA_WIN_ROWS - 1 - NA_WIN_ROWS // 2
            else:
                j0, off = min(i + NA_WIN_ROWS // 2, NA_KEY_ROWS - NA_WIN_ROWS), -1
            blks = []
            for j in range(NA_KEY_ROWS):
                if j0 <= j < j0 + NA_WIN_ROWS:
                    blks.append(toep[:, j - i + off])
                else:
                    blks.append(neg_blk)
            rows_i.append(jnp.concatenate(blks, axis=-1))
        tabs.append(jnp.concatenate(rows_i, axis=1))
    return jnp.stack(tabs, axis=0)


def _mla_prep_kernel(cq_ref, ckv_ref, kr_ref, gq_ref, gkv_ref, wqm_ref, wqs_ref, wk_ref, wv_ref,
                     pm_ref, cq_tab_ref, sq_tab_ref, csk_tab_ref, q_out, k_out, v_out, *, qscale):
    cq = _f32(cq_ref[...])
    cqn = _bf16(_rms(cq, gq_ref[...], n=MLA_Q_LORA))
    qm = _dot(cqn, wqm_ref[...])
    qs = _dot(cqn, wqs_ref[...])
    ctab = cq_tab_ref[...]
    stab = sq_tab_ref[...]
    for h in range(MLA_HEADS):
        sl = slice(h * LANES, (h + 1) * LANES)
        qh = (qm[:, sl] * ctab + qs[:, sl] * stab) * qscale
        q_out[0, h] = _bf16(qh.T)

    ckv = _f32(ckv_ref[...])
    ckvn = _bf16(_rms(ckv, gkv_ref[...]))
    kk = _dot(ckvn, wk_ref[...])
    vv = _dot(ckvn, wv_ref[...])
    n_chunk = v_out.shape[2]
    ones = jnp.ones((MLA_VROWS - MLA_V, MLA_KEY_CHUNK), jnp.bfloat16)
    for hp in range(MLA_HEADS // 2):
        vt = _bf16(vv[:, hp * LANES:(hp + 1) * LANES].T)
        for cc in range(n_chunk):
            ks = slice(cc * MLA_KEY_CHUNK, (cc + 1) * MLA_KEY_CHUNK)
            for h in range(2):
                r0 = h * MLA_VROWS
                v_out[0, hp, cc, r0:r0 + MLA_V, :] = vt[h * MLA_V:(h + 1) * MLA_V, ks]
                v_out[0, hp, cc, r0 + MLA_V:r0 + MLA_VROWS, :] = ones
    krr = _f32(kr_ref[...]) * csk_tab_ref[...]
    kplace = _dot(_bf16(krr), pm_ref[...])
    for h in range(MLA_HEADS):
        sl = slice(h * LANES, (h + 1) * LANES)
        k_out[:, sl] = _bf16(kk[:, sl] + kplace)


def _mla_prep(proj, col_cq, col_ckv, col_kr, gq, gkv, wqm, wqs, wk, wv, pm, cq_tab, sq_tab, csk_tab,
              seq, qscale, tm=512):
    T = proj.shape[0]
    batch = T // seq
    n_s = seq // tm
    n_pairs = MLA_HEADS // 2
    cpt = tm // MLA_KEY_CHUNK
    full = lambda a: pl.BlockSpec(a.shape, lambda i: (0,) * a.ndim)
    tab = pl.BlockSpec((tm, LANES), lambda i: (i % n_s, 0))
    kern = functools.partial(_mla_prep_kernel, qscale=qscale)
    return pl.pallas_call(
        kern,
        out_shape=(
            jax.ShapeDtypeStruct((batch, MLA_HEADS, LANES, seq), jnp.bfloat16),
            jax.ShapeDtypeStruct((T, MLA_HEADS * LANES), jnp.bfloat16),
            jax.ShapeDtypeStruct((batch, n_pairs, seq // MLA_KEY_CHUNK, 2 * MLA_VROWS, MLA_KEY_CHUNK),
                                 jnp.bfloat16),
        ),
        grid=(T // tm,),
        in_specs=[
            pl.BlockSpec((tm, Q_LORA_PAD), lambda i: (i, col_cq // Q_LORA_PAD)),
            pl.BlockSpec((tm, MLA_KV_LORA), lambda i: (i, col_ckv // MLA_KV_LORA)),
            pl.BlockSpec((tm, LANES), lambda i: (i, col_kr // LANES)),
            full(gq), full(gkv), full(wqm), full(wqs), full(wk), full(wv), full(pm),
            tab, tab, tab,
        ],
        out_specs=(
            pl.BlockSpec((1, MLA_HEADS, LANES, tm), lambda i: (i // n_s, 0, 0, i % n_s)),
            pl.BlockSpec((tm, MLA_HEADS * LANES), lambda i: (i, 0)),
            pl.BlockSpec((1, n_pairs, cpt, 2 * MLA_VROWS, MLA_KEY_CHUNK),
                         lambda i: (i // n_s, 0, i % n_s, 0, 0)),
        ),
        compiler_params=_cparams(("arbitrary",)),
        name="mla_prep",
    )(proj, proj, proj, gq, gkv, wqm, wqs, wk, wv, pm, cq_tab, sq_tab, csk_tab)


def _mla_attn_kernel(qt_ref, k_ref, vt_ref, o_ref, m_scr, acc_scr, st_scr):
    n_chunks = vt_ref.shape[2]
    tk = MLA_KEY_CHUNK
    m_scr[...] = jnp.full(m_scr.shape, -jnp.inf, jnp.float32)
    acc_scr[...] = jnp.zeros(acc_scr.shape, jnp.float32)

    def scores(c, slot):
        k0 = pl.multiple_of(c * tk, tk)
        for h in range(2):
            kc = k_ref[pl.ds(k0, tk), h * LANES:(h + 1) * LANES]
            st_scr[slot, h] = _dot(kc, qt_ref[0, h])

    def step(c, slot):
        scores(jnp.minimum(c + 1, n_chunks - 1), 1 - slot)
        vt = vt_ref[0, 0, c]
        pts, alphas = [], []
        for h in range(2):
            st = st_scr[slot, h]
            m_old = m_scr[h]
            m_new = jnp.maximum(m_old, jnp.max(st, axis=0, keepdims=True))
            alpha = jnp.exp2(m_old - m_new)
            pt = jnp.exp2(st - m_new)
            m_scr[h] = m_new
            pts.append(_bf16(pt))
            alphas.append(alpha)
        for h in range(2):
            acc_scr[h] = alphas[h] * acc_scr[h] + _dot(vt[h * MLA_VROWS:(h + 1) * MLA_VROWS, :], pts[h])

    def body(j, carry):
        for u in range(MLA_STEPS_PER_TRIP):
            step(MLA_STEPS_PER_TRIP * j + u, u % 2)
        return carry

    scores(0, 0)
    lax.fori_loop(0, n_chunks // MLA_STEPS_PER_TRIP, body, 0)
    outs = []
    for h in range(2):
        acc = acc_scr[h]
        outs.append(acc[0:MLA_V] / acc[MLA_V:MLA_V + 1])
    o_ref[...] = _bf16(jnp.concatenate(outs, axis=0).T)


def _mla_attention(q_t, k_cat, v_t, batch, seq, tq=512):
    T = k_cat.shape[0]
    n_q = seq // tq
    n_pairs = MLA_HEADS // 2
    n_chunks = seq // MLA_KEY_CHUNK
    assert n_chunks % MLA_STEPS_PER_TRIP == 0
    return pl.pallas_call(
        _mla_attn_kernel,
        out_shape=jax.ShapeDtypeStruct((T, MLA_WIDTH), jnp.bfloat16),
        grid=(batch, n_pairs, n_q),
        in_specs=[
            pl.BlockSpec((1, 2, LANES, tq), lambda b, hp, i: (b, hp, 0, i)),
            pl.BlockSpec((seq, 2 * LANES), lambda b, hp, i: (b, hp)),
            pl.BlockSpec((1, 1, n_chunks, 2 * MLA_VROWS, MLA_KEY_CHUNK), lambda b, hp, i: (b, hp, 0, 0, 0)),
        ],
        out_specs=pl.BlockSpec((tq, LANES), lambda b, hp, i: (b * n_q + i, hp)),
        scratch_shapes=[
            pltpu.VMEM((2, 1, tq), jnp.float32),
            pltpu.VMEM((2, MLA_VROWS, tq), jnp.float32),
            pltpu.VMEM((2, 2, MLA_KEY_CHUNK, tq), jnp.float32),
        ],
        compiler_params=_cparams(("arbitrary", "arbitrary", "arbitrary")),
        name="mla_attn",
    )(q_t, k_cat, v_t)


def _out_proj_kernel(ona_ref, omla_ref, gate_ref, x_ref, mod_ref, gpost_ref, gpre_ref, wna_ref,
                     wmla_ref, wout_ref, wr_ref, br_ref, x1_ref, h2_ref, lg_ref, *, d_model):
    D = d_model
    gates = gate_ref[...]
    merged = (_f32(gates[:, 0:D]) * _dot(ona_ref[...], wna_ref[...])
              + _f32(gates[:, D:2 * D]) * _dot(omla_ref[...], wmla_ref[...]))
    y = _dot(_bf16(merged), wout_ref[...])
    gate_a = mod_ref[0, :, 2 * D:3 * D]
    shift_f = mod_ref[0, :, 3 * D:4 * D]
    scale_f = mod_ref[0, :, 4 * D:5 * D]
    x1 = x_ref[...] + gate_a * _rms(y, gpost_ref[...])
    x1_ref[...] = x1
    h2 = _rms(x1, gpre_ref[...]) * (1.0 + scale_f) + shift_f
    _store_row_tiles(h2_ref, h2)
    lg_ref[...] = _dot(_bf16(h2), wr_ref[...]) + br_ref[...]


def _out_proj(o_na, o_mla, proj, x2, mod3, g_post, g_pre, w_na, w_mla, w_out, w_r, b_r, seq, tm=512):
    T, D = x2.shape
    full = lambda a: pl.BlockSpec(a.shape, lambda i: (0,) * a.ndim)
    row = lambda w: pl.BlockSpec((tm, w), lambda i: (i, 0))
    kern = functools.partial(_out_proj_kernel, d_model=D)
    g_post = g_post.reshape(1, D)
    g_pre = g_pre.reshape(1, D)
    return pl.pallas_call(
        kern,
        out_shape=(
            jax.ShapeDtypeStruct((T, D), jnp.float32),
            jax.ShapeDtypeStruct((T * (D // LANES), LANES), jnp.float32),
            jax.ShapeDtypeStruct((T, LANES), jnp.float32),
        ),
        grid=(T // tm,),
        in_specs=[
            row(NA_WIDTH), row(MLA_WIDTH),
            pl.BlockSpec((tm, 2 * D), lambda i: (i, COL_GATES // (2 * D))),
            row(D),
            pl.BlockSpec((1, 1, mod3.shape[-1]), lambda i: ((i * tm) // seq, 0, 0)),
            full(g_post), full(g_pre), full(w_na), full(w_mla), full(w_out), full(w_r), full(b_r),
        ],
        out_specs=(row(D), pl.BlockSpec((tm * (D // LANES), LANES), lambda i: (i, 0)), row(LANES)),
        compiler_params=_cparams(("arbitrary",)),
        name="out_proj",
    )(o_na, o_mla, proj, x2, mod3, g_post, g_pre, w_na, w_mla, w_out, w_r, b_r)


def _route_kernel(lg_ref, r_ref, cnt_ref, carry_scr, *, sub):
    @pl.when(pl.program_id(0) == 0)
    def _():
        carry_scr[...] = jnp.zeros(carry_scr.shape, jnp.float32)

    tr = lg_ref.shape[0]
    lane = lax.broadcasted_iota(jnp.int32, (sub, LANES), 1).astype(jnp.float32)
    ri = lax.broadcasted_iota(jnp.int32, (sub, sub), 0)
    ci = lax.broadcasted_iota(jnp.int32, (sub, sub), 1)
    tri = jnp.where(ri >= ci, 1.0, 0.0).astype(jnp.bfloat16)
    for s0 in range(0, tr, sub):
        work = lg_ref[s0:s0 + sub, :]
        sels, vals, idxs = [], [], []
        for _k in range(TOP_K):
            mk = jnp.max(work, axis=-1, keepdims=True)
            ik = jnp.min(jnp.where(work == mk, lane, float(LANES)), axis=-1, keepdims=True)
            sk = lane == ik
            work = jnp.where(sk, -jnp.inf, work)
            sels.append(sk)
            vals.append(mk)
            idxs.append(ik)
        es = [jnp.exp(v - vals[0]) for v in vals]
        denom = es[0] + es[1] + es[2] + es[3]
        onehot = jnp.zeros((sub, LANES), jnp.float32)
        for sk in sels:
            onehot = jnp.where(sk, 1.0, onehot)
        prefix = _dot(tri, _bf16(onehot))
        carry = carry_scr[...]
        rank_mat = carry + prefix - 1.0
        res = jnp.zeros((sub, LANES), jnp.float32)
        for kk in range(TOP_K):
            rank_k = jnp.sum(jnp.where(sels[kk], rank_mat, 0.0), axis=-1, keepdims=True)
            res = jnp.where(lane == kk, idxs[kk], res)
            res = jnp.where(lane == TOP_K + kk, rank_k, res)
            res = jnp.where(lane == 2 * TOP_K + kk, es[kk] / denom, res)
        r_ref[s0:s0 + sub, :] = res
        carry_scr[...] = carry + jnp.sum(onehot, axis=0, keepdims=True)
    cnt_ref[...] = carry_scr[...]


def _route(logits, tr=2048, sub=256):
    T = logits.shape[0]
    kern = functools.partial(_route_kernel, sub=sub)
    return pl.pallas_call(
        kern,
        out_shape=(
            jax.ShapeDtypeStruct((T, LANES), jnp.float32),
            jax.ShapeDtypeStruct((1, LANES), jnp.float32),
        ),
        grid=(T // tr,),
        in_specs=[pl.BlockSpec((tr, LANES), lambda i: (i, 0))],
        out_specs=(
            pl.BlockSpec((tr, LANES), lambda i: (i, 0)),
            pl.BlockSpec((1, LANES), lambda i: (0, 0)),
        ),
        scratch_shapes=[pltpu.VMEM((1, LANES), jnp.float32)],
        compiler_params=_cparams(("arbitrary",)),
        name="route",
    )(logits)


def _slot_map_kernel(dest_ref, tok0_ref, row0_ref, tok_ref, row_ref, sem, *, n_tokens):
    i = pl.program_id(0)
    ts = dest_ref.shape[0] // TOP_K

    @pl.when(i == 0)
    def _():
        fill_tok = pltpu.make_async_copy(tok0_ref, tok_ref, sem.at[0])
        fill_row = pltpu.make_async_copy(row0_ref, row_ref, sem.at[1])
        fill_tok.start()
        fill_row.start()
        fill_tok.wait()
        fill_row.wait()

    def place(j, carry):
        t = i * ts + j
        for kk in range(TOP_K):
            d = dest_ref[j * TOP_K + kk]
            tok_ref[d] = t
            row_ref[d] = kk * n_tokens + t
        return carry

    lax.fori_loop(0, ts, place, 0, unroll=4)


def _slot_map(dest, cap, n_tokens, ts=256):
    kern = functools.partial(_slot_map_kernel, n_tokens=n_tokens)
    smem_full = pl.BlockSpec(memory_space=pltpu.SMEM)
    return pl.pallas_call(
        kern,
        out_shape=(jax.ShapeDtypeStruct((cap,), jnp.int32), jax.ShapeDtypeStruct((cap,), jnp.int32)),
        grid=(n_tokens // ts,),
        in_specs=[pl.BlockSpec((ts * TOP_K,), lambda i: (i,), memory_space=pltpu.SMEM),
                  pl.BlockSpec(memory_space=pl.ANY), pl.BlockSpec(memory_space=pl.ANY)],
        out_specs=(smem_full, smem_full),
        scratch_shapes=[pltpu.SemaphoreType.DMA((2,))],
        compiler_params=_cparams(("arbitrary",)),
        name="slot_map",
    )(dest, jnp.zeros((cap,), jnp.int32), jnp.full((cap,), -1, jnp.int32))


def _moe_kernel(blk_e_ref, nvalid_ref, tok_ref, tok_next_ref, row_prev_ref, row_ref, h_ref, w1_ref,
                b1_ref, w2_ref, b2_ref, out_ref, w1p_scr, w2b_scr, xbuf, ybuf, gsem, ssem, *, n_items):
    i = pl.program_id(0)
    nvalid = nvalid_ref[0]
    d_ff = w2_ref.shape[1]
    n_groups = (2 * d_ff) // (2 * LANES)

    def tile(r):
        return pl.ds(pl.multiple_of(r * SUBLANES, SUBLANES), SUBLANES)

    def gather_copy(t, j, slot):
        return pltpu.make_async_copy(h_ref.at[tile(t), :], xbuf.at[slot, pl.ds(j * SUBLANES, SUBLANES), :],
                                     gsem.at[slot])

    def scatter_copy(j, r, slot):
        return pltpu.make_async_copy(ybuf.at[slot, pl.ds(j * SUBLANES, SUBLANES), :], out_ref.at[tile(r), :],
                                     ssem.at[slot])

    def wait_gathers(slot):
        for _j in range(MOE_BLOCK):
            gather_copy(0, 0, slot).wait()

    def wait_scatters(slot):
        for _j in range(MOE_BLOCK):
            scatter_copy(0, 0, slot).wait()

    def scatter_rows(rows_ref, slot):
        spare0 = n_items + slot * MOE_BLOCK
        for j in range(MOE_BLOCK):
            r = rows_ref[0, 0, j]
            scatter_copy(j, jnp.where(r >= 0, r, spare0 + j), slot).start(priority=j % 2)

    @pl.when(i == 0)
    def _():
        ybuf[...] = jnp.zeros(ybuf.shape, ybuf.dtype)
        for j in range(MOE_BLOCK):
            scatter_copy(j, n_items + j, 0).start(priority=j % 2)
        for j in range(MOE_BLOCK):
            gather_copy(tok_ref[0, 0, j], j, 0).start(priority=j % 2)

    @pl.when(i < nvalid)
    def _():
        e = blk_e_ref[i]
        e_prev = blk_e_ref[jnp.maximum(i - 1, 0)]

        @pl.when((i == 0) | (e != e_prev))
        def _():
            r = lax.broadcasted_iota(jnp.int32, (2 * LANES, 2 * LANES), 0)
            c = lax.broadcasted_iota(jnp.int32, (2 * LANES, 2 * LANES), 1)
            src = jnp.where(c < LANES, 2 * c, 2 * (c - LANES) + 1)
            perm = jnp.where(r == src, 1.0, 0.0).astype(jnp.bfloat16)
            for gI in range(n_groups):
                sl = slice(gI * 2 * LANES, (gI + 1) * 2 * LANES)
                w1p_scr[:, sl] = _bf16(_dot(_bf16(w1_ref[0, :, sl]), perm))
            w2b_scr[...] = _bf16(w2_ref[0])

        def block_step(slot):
            wait_gathers(slot)
            wait_scatters(slot)
            for j in range(MOE_BLOCK):
                gather_copy(tok_next_ref[0, 0, j], j, 1 - slot).start(priority=j % 2)
            scatter_rows(row_prev_ref, 1 - slot)

            xb =_bf16(_load_row_tiles(xbuf.at[slot], MOE_BLOCK))
            hcat = _dot(xb, w1p_scr[...]) + b1_ref[0]
            acts = []
            for gI in range(n_groups):
                glu = jnp.minimum(hcat[:, gI * 2 * LANES: gI * 2 * LANES + LANES], SWIGLU_LIMIT)
                lin = jnp.clip(hcat[:, gI * 2 * LANES + LANES: (gI + 1) * 2 * LANES],
                               -SWIGLU_LIMIT, SWIGLU_LIMIT)
                acts.append(_bf16(glu * jax.nn.sigmoid(SWIGLU_ALPHA * glu) * (lin + 1.0)))
            act = jnp.concatenate(acts, axis=-1)
            _store_row_tiles(ybuf.at[slot], _dot(act, w2b_scr[...]) + b2_ref[0])

            @pl.when(i == nvalid - 1)
            def _():
                scatter_rows(row_ref, slot)
                wait_gathers(1 - slot)
                wait_scatters(1 - slot)
                wait_scatters(slot)

        parity = lax.rem(i, 2)
        for slot in range(2):
            pl.when(parity == slot)(functools.partial(block_step, slot))


def _moe_ffn(h2_tiles, slot_tok, slot_row, blk_e, nvalid, w1, b1p, w2, b2):
    D = w1.shape[1]
    assert D == SUBLANES * LANES
    T = h2_tiles.shape[0] // SUBLANES
    E, _, F2 = w1.shape
    F = w2.shape[1]
    nblk = slot_tok.shape[0]
    n_items = T * TOP_K

    def cur(i, be, nv):
        return (jnp.minimum(i, nv[0] - 1), 0, 0)

    def nxt(i, be, nv):
        return (jnp.minimum(i + 1, nv[0] - 1), 0, 0)

    def prv(i, be, nv):
        return (jnp.where(i == 0, nblk, jnp.maximum(jnp.minimum(i, nv[0] - 1) - 1, 0)), 0, 0)

    slot_row = jnp.concatenate([slot_row, jnp.full((1, 1, MOE_BLOCK), -1, slot_row.dtype)], axis=0)
    slot_spec = lambda f: pl.BlockSpec((1, 1, MOE_BLOCK), f, memory_space=pltpu.SMEM)
    grid_spec = pltpu.PrefetchScalarGridSpec(
        num_scalar_prefetch=2,
        grid=(nblk,),
        in_specs=[
            slot_spec(cur), slot_spec(nxt), slot_spec(prv), slot_spec(cur),
            pl.BlockSpec(memory_space=pl.ANY),
            pl.BlockSpec((1, D, F2), lambda i, be, nv: (be[i], 0, 0)),
            pl.BlockSpec((1, 1, F2), lambda i, be, nv: (be[i], 0, 0)),
            pl.BlockSpec((1, F, D), lambda i, be, nv: (be[i], 0, 0)),
            pl.BlockSpec((1, 1, D), lambda i, be, nv: (be[i], 0, 0)),
        ],
        out_specs=pl.BlockSpec(memory_space=pl.ANY),
        scratch_shapes=[
            pltpu.VMEM((D, F2), jnp.bfloat16),
            pltpu.VMEM((F, D), jnp.bfloat16),
            pltpu.VMEM((2, MOE_BLOCK * SUBLANES, LANES), jnp.float32),
            pltpu.VMEM((2, MOE_BLOCK * SUBLANES, LANES), jnp.float32),
            pltpu.SemaphoreType.DMA((2,)),
            pltpu.SemaphoreType.DMA((2,)),
        ],
    )
    kern = functools.partial(_moe_kernel, n_items=n_items)
    return pl.pallas_call(
        kern,
        out_shape=jax.ShapeDtypeStruct(((n_items + 2 * MOE_BLOCK) * SUBLANES, LANES), jnp.float32),
        grid_spec=grid_spec,
        compiler_params=_cparams(("arbitrary",)),
        name="moe_ffn",
    )(blk_e, nvalid, slot_tok, slot_tok, slot_row, slot_row, h2_tiles, w1, b1p, w2, b2)


def _combine_kernel(y0_ref, y1_ref, y2_ref, y3_ref, r_ref, x1_ref, mod_ref, g_ref, o_ref, *, d_model):
    D = d_model
    tc = x1_ref.shape[0]
    r = r_ref[...]
    moe = None
    for kk, y_ref in enumerate((y0_ref, y1_ref, y2_ref, y3_ref)):
        term = r[:, 2 * TOP_K + kk: 2 * TOP_K + kk + 1] * _load_row_tiles(y_ref, tc)
        moe = term if moe is None else moe + term
    gate_f = mod_ref[0, :, 5 * D:6 * D]
    o_ref[...] = x1_ref[...] + gate_f * _rms(moe, g_ref[...])


def _combine(y_k, route, x1, mod3, g_post, seq, tc=512):
    T, D = x1.shape
    n_t = T // tc
    kern = functools.partial(_combine_kernel, d_model=D)
    y_spec = lambda kk: pl.BlockSpec((tc * SUBLANES, LANES), lambda i: (kk * n_t + i, 0))
    return pl.pallas_call(
        kern,
        out_shape=jax.ShapeDtypeStruct((T, D), jnp.float32),
        grid=(n_t,),
        in_specs=[
            y_spec(0), y_spec(1), y_spec(2), y_spec(3),
            pl.BlockSpec((tc, LANES), lambda i: (i, 0)),
            pl.BlockSpec((tc, D), lambda i: (i, 0)),
            pl.BlockSpec((1, 1, mod3.shape[-1]), lambda i: ((i * tc) // seq, 0, 0)),
            pl.BlockSpec((1, D), lambda i: (0, 0)),
        ],
        out_specs=pl.BlockSpec((tc, D), lambda i: (i, 0)),
        compiler_params=_cparams(("arbitrary",)),
        name="combine",
    )(y_k, y_k, y_k, y_k, route, x1, mod3, g_post.reshape(1, D))


def _rope_swap_cols(w):
    nf = MLA_ROPE // 4
    return jnp.concatenate([-w[..., nf:2 * nf], w[..., 0:nf], -w[..., 3 * nf:4 * nf], w[..., 2 * nf:3 * nf]],
                           axis=-1)


def _rope_tables(seq):
    n_rows = seq // GRID_W
    nf = MLA_ROPE // 4
    inv = ROPE_THETA ** (-jnp.arange(nf, dtype=jnp.float32) / nf)
    ar = jnp.arange(n_rows, dtype=jnp.float32)[:, None] * inv
    ac = jnp.arange(GRID_W, dtype=jnp.float32)[:, None] * inv
    per_row = lambda a: jnp.broadcast_to(a[:, None, :], (n_rows, GRID_W, nf)).reshape(seq, nf)
    per_col = lambda a: jnp.broadcast_to(a[None, :, :], (n_rows, GRID_W, nf)).reshape(seq, nf)
    cr, sr, cc, sc = per_row(jnp.cos(ar)), per_row(jnp.sin(ar)), per_col(jnp.cos(ac)), per_col(jnp.sin(ac))
    c32 = jnp.concatenate([cr, cr, cc, cc], axis=-1)
    s32 = jnp.concatenate([sr, sr, sc, sc], axis=-1)
    one = jnp.ones((seq, MLA_NOPE), jnp.float32)
    z64 = jnp.zeros((seq, MLA_NOPE), jnp.float32)
    z32 = jnp.zeros((seq, LANES - MLA_NOPE - MLA_ROPE), jnp.float32)
    cq_tab = jnp.concatenate([one, c32, z32], axis=-1)
    sq_tab = jnp.concatenate([z64, s32, z32], axis=-1)
    csk_tab = jnp.concatenate([c32, s32, z64], axis=-1)
    return cq_tab, sq_tab, csk_tab


def kernel(x, c, w_ada, b_ada, g_attn_pre, g_attn_post, w_in, b_gate, na_rpb, q_norm_g, kv_norm_g,
           w_uq, w_ukv, w_na_up, w_mla_up, w_out, g_ffn_pre, g_ffn_post, w_router, b_router,
           w1, b1, w2, b2):
    B, S, D = x.shape
    T = B * S
    depth = w_ada.shape[0]
    E = w_router.shape[-1]
    n_rows = S // GRID_W
    assert S % (NA_GROUP_ROWS * GRID_W) == 0 and n_rows >= NA_KEY_ROWS
    assert E <= LANES and T % 2048 == 0

    bf = jnp.bfloat16
    cq_tab, sq_tab, csk_tab = _rope_tables(S)
    pm_np = np.zeros((LANES, LANES), np.float32)
    for cidx in range(MLA_ROPE):
        pm_np[cidx, MLA_NOPE + cidx] = 1.0
        pm_np[MLA_ROPE + cidx, MLA_NOPE + cidx] = 1.0
    pm = jnp.asarray(pm_np, bf)
    qscale = float((MLA_NOPE + MLA_ROPE) ** -0.5 * math.log2(math.e))

    col_na = 2 * D
    col_cq = col_na + 3 * NA_WIDTH
    col_ckv = col_cq + Q_LORA_PAD
    col_kr = col_ckv + MLA_KV_LORA
    assert col_cq % Q_LORA_PAD == 0 and col_ckv % MLA_KV_LORA == 0

    x2 = x.reshape(T, D)
    for l in range(depth):
        mod = _ada_mod(c, w_ada[l], b_ada[l])
        mod3 = mod.reshape(B, 1, 6 * D)

        o_na, o_cq, o_ckv, o_kr, o_g = np.cumsum([0, 3 * NA_WIDTH, MLA_Q_LORA, MLA_KV_LORA, MLA_ROPE]).tolist()
        wi = w_in[l]
        w_kr = wi[:, o_kr:o_g]
        w_ext = jnp.concatenate([
            wi[:, o_g:o_g + 2 * D],
            wi[:, o_na:o_cq],
            wi[:, o_cq:o_ckv], jnp.zeros((D, Q_LORA_PAD - MLA_Q_LORA), wi.dtype),
            wi[:, o_ckv:o_kr],
            w_kr, _rope_swap_cols(w_kr), jnp.zeros((D, LANES - 2 * MLA_ROPE), wi.dtype),
        ], axis=1).astype(bf)
        proj = _in_proj(x2, mod3, g_attn_pre[l], w_ext, b_gate[l], S)

        bias_tab = _na_bias_tables(na_rpb[l], n_rows)
        o_na_tok = _na_attention(proj, bias_tab, B, S, col_na, col_na + NA_WIDTH, col_na + 2 * NA_WIDTH)

        wq = w_uq[l]
        zq = jnp.zeros((MLA_Q_LORA, MLA_HEADS, LANES - MLA_NOPE - MLA_ROPE), wq.dtype)
        wqm = jnp.concatenate([wq, zq], axis=-1).reshape(MLA_Q_LORA, MLA_HEADS * LANES)
        wqs = jnp.concatenate([jnp.zeros((MLA_Q_LORA, MLA_HEADS, MLA_NOPE), wq.dtype),
                               _rope_swap_cols(wq[..., MLA_NOPE:]), zq], axis=-1
                              ).reshape(MLA_Q_LORA, MLA_HEADS * LANES)
        rpad = ((0, Q_LORA_PAD - MLA_Q_LORA), (0, 0))
        wqm = jnp.pad(wqm, rpad).astype(bf)
        wqs = jnp.pad(wqs, rpad).astype(bf)
        wkv = w_ukv[l]
        wk = jnp.concatenate([wkv[..., :MLA_NOPE], jnp.zeros((MLA_KV_LORA, MLA_HEADS, LANES - MLA_NOPE), wkv.dtype)],
                             axis=-1).reshape(MLA_KV_LORA, MLA_HEADS * LANES).astype(bf)
        wv = wkv[..., MLA_NOPE:].reshape(MLA_KV_LORA, MLA_WIDTH).astype(bf)
        gq = jnp.pad(q_norm_g[l], (0, Q_LORA_PAD - MLA_Q_LORA)).reshape(1, Q_LORA_PAD)
        gkv = kv_norm_g[l].reshape(1, MLA_KV_LORA)
        q_cat, k_cat, v_mla = _mla_prep(proj, col_cq, col_ckv, col_kr, gq, gkv, wqm, wqs, wk, wv, pm,
                                        cq_tab, sq_tab, csk_tab, S, qscale)
        o_mla_tok = _mla_attention(q_cat, k_cat, v_mla, B, S)

        w_r = jnp.pad(w_router[l], ((0, 0), (0, LANES - E))).astype(bf)
        b_r = jnp.concatenate([b_router[l], jnp.full((LANES - E,), NEG_BIG, jnp.float32)]).reshape(1, LANES)
        x1, h2, logits = _out_proj(o_na_tok, o_mla_tok, proj, x2, mod3, g_attn_post[l], g_ffn_pre[l],
                                   w_na_up[l].astype(bf), w_mla_up[l].astype(bf), w_out[l].astype(bf),
                                   w_r, b_r, S)

        route, counts = _route(logits)
        e_idx = route[:, 0:TOP_K].astype(jnp.int32)
        rank = route[:, TOP_K:2 * TOP_K].astype(jnp.int32)
        cnt = counts[0, :E].astype(jnp.int32)
        padded = ((cnt + MOE_BLOCK - 1) // MOE_BLOCK) * MOE_BLOCK
        pend = jnp.cumsum(padded)
        pstart = pend - padded
        n_items = T * TOP_K
        nblk = -(-n_items // MOE_BLOCK) + E
        cap = nblk * MOE_BLOCK
        dest = (rank + jnp.sum(jnp.where(e_idx[..., None] == jnp.arange(E, dtype=jnp.int32), pstart, 0), axis=-1)
                ).reshape(n_items)
        blk_off = jnp.arange(nblk, dtype=jnp.int32) * MOE_BLOCK
        blk_e = jnp.minimum(jnp.sum((pend[None, :] <= blk_off[:, None]).astype(jnp.int32), axis=-1), E - 1)
        nvalid = (pend[-1:] // MOE_BLOCK).astype(jnp.int32)

        slot_tok, slot_row = _slot_map(dest, cap, T)
        slot_tok = slot_tok.reshape(nblk, 1, MOE_BLOCK)
        slot_row = slot_row.reshape(nblk, 1, MOE_BLOCK)
        F2 = w1.shape[-1]
        b1p = b1[l].reshape(E, F2 // (2 * LANES), LANES, 2).transpose(0, 1, 3, 2).reshape(E, 1, F2)
        y_k = _moe_ffn(h2, slot_tok, slot_row, blk_e, nvalid, w1[l], b1p, w2[l], b2[l].reshape(E, 1, D))
        x2 = _combine(y_k, route, x1, mod3, g_ffn_post[l], S)
    return x2.reshape(B, S, D)
```

```python
import functools
import math

import numpy as np
import jax
import jax.numpy as jnp
from jax import lax
from jax.experimental import pallas as pl
from jax.experimental.pallas import tpu as pltpu

GRID_W = 64
NA_HEADS = 8
NA_HEAD_DIM = 64
NA_WIN_ROWS = 8
NA_WIN_COLS = 16
NA_WIDTH = NA_HEADS * NA_HEAD_DIM
MLA_HEADS = 8
MLA_Q_LORA = 384
MLA_KV_LORA = 256
MLA_NOPE = 64
MLA_ROPE = 32
MLA_V = 64
MLA_WIDTH = MLA_HEADS * MLA_V
MLA_KEY_CHUNK = 256
MLA_VROWS = MLA_V + 16
MLA_STEPS_PER_TRIP = 8
ROPE_THETA = 100.0
TOP_K = 4
SWIGLU_ALPHA = 1.702
SWIGLU_LIMIT = 7.0
MOE_BLOCK = 256
RMS_EPS = 1e-6
NEG_BIG = -1e30

LANES = 128
VMEM_LIMIT_BYTES = 56 * 1024 * 1024

Q_LORA_PAD = 512
COL_GATES = 0
NA_GROUP_ROWS = 8
NA_KEY_ROWS = 16


def _f32(x):
    return x.astype(jnp.float32)


def _bf16(x):
    return x.astype(jnp.bfloat16)


def _dot(a, b):
    return jnp.dot(a, b, preferred_element_type=jnp.float32)


def _dot_nt(a, b):
    return lax.dot_general(a, b, (((1,), (1,)), ((), ())), preferred_element_type=jnp.float32)


def _rms(x, g, n=None):
    n = x.shape[-1] if n is None else n
    ms = jnp.sum(x * x, axis=-1, keepdims=True) * (1.0 / n)
    return x * lax.rsqrt(ms + RMS_EPS) * g


SUBLANES = 8


def _store_row_tiles(ref, x, base=0):
    n = x.shape[0]
    for c in range(x.shape[1] // LANES):
        ref[pl.ds(base + c, n, stride=SUBLANES), :] = x[:, c * LANES:(c + 1) * LANES]


def _load_row_tiles(ref, n, base=0):
    chunks = [ref[pl.ds(base + c, n, stride=SUBLANES), :] for c in range(SUBLANES)]
    return jnp.concatenate(chunks, axis=-1)


def _cparams(sem):
    return pltpu.CompilerParams(dimension_semantics=sem, vmem_limit_bytes=VMEM_LIMIT_BYTES)


def _ada_kernel(c_ref, w_ref, b_ref, o_ref):
    c = c_ref[...]
    sc = c * jax.nn.sigmoid(c)
    o_ref[...] = _dot(_bf16(sc), _bf16(w_ref[...])) + b_ref[...]


def _ada_mod(c, w_ada, b_ada):
    B, D = c.shape
    n_out = w_ada.shape[1]
    return pl.pallas_call(
        _ada_kernel,
        out_shape=jax.ShapeDtypeStruct((B, n_out), jnp.float32),
        grid=(n_out // D,),
        in_specs=[
            pl.BlockSpec((B, D), lambda j: (0, 0)),
            pl.BlockSpec((D, D), lambda j: (0, j)),
            pl.BlockSpec((1, D), lambda j: (0, j)),
        ],
        out_specs=pl.BlockSpec((B, D), lambda j: (0, j)),
        compiler_params=_cparams(("arbitrary",)),
        name="ada_mod",
    )(c, w_ada, b_ada.reshape(1, n_out))


def _in_proj_kernel(x_ref, mod_ref, g_ref, w_ref, bg_ref, o_ref, *, d_model, n_gate, chunk):
    x = x_ref[...]
    shift = mod_ref[0, :, 0:d_model]
    scale = mod_ref[0, :, d_model:2 * d_model]
    h = _rms(x, g_ref[...]) * (1.0 + scale) + shift
    hb = _bf16(h)
    n_total = w_ref.shape[1]
    for c0 in range(0, n_total, chunk):
        c1 = min(c0 + chunk, n_total)
        acc = _dot(hb, w_ref[:, c0:c1])
        if c0 < n_gate:
            acc = jax.nn.sigmoid(acc + bg_ref[:, c0:c1])
        o_ref[:, c0:c1] = _bf16(acc)


def _in_proj(x2, mod3, g_pre, w_ext, b_gate, seq, tm=512):
    T, D = x2.shape
    n_total = w_ext.shape[1]
    n_gate = b_gate.shape[-1]
    kern = functools.partial(_in_proj_kernel, d_model=D, n_gate=n_gate, chunk=512)
    return pl.pallas_call(
        kern,
        out_shape=jax.ShapeDtypeStruct((T, n_total), jnp.bfloat16),
        grid=(T // tm,),
        in_specs=[
            pl.BlockSpec((tm, D), lambda i: (i, 0)),
            pl.BlockSpec((1, 1, mod3.shape[-1]), lambda i: ((i * tm) // seq, 0, 0)),
            pl.BlockSpec((1, D), lambda i: (0, 0)),
            pl.BlockSpec((D, n_total), lambda i: (0, 0)),
            pl.BlockSpec((1, n_gate), lambda i: (0, 0)),
        ],
        out_specs=pl.BlockSpec((tm, n_total), lambda i: (i, 0)),
        compiler_params=_cparams(("arbitrary",)),
        name="in_proj",
    )(x2, mod3, g_pre.reshape(1, D), w_ext, b_gate.reshape(1, n_gate))


def _na_kernel(q_ref, k_ref, v_ref, bias_ref, o_ref, *, n_rows):
    g = pl.program_id(2)
    kb = jnp.clip(g * NA_GROUP_ROWS - NA_WIN_ROWS // 2, 0, n_rows - NA_KEY_ROWS) * GRID_W
    kb = pl.multiple_of(kb, 256)
    nk = NA_KEY_ROWS * GRID_W
    q = q_ref[...]
    k = k_ref[pl.ds(kb, nk), :]
    v = v_ref[pl.ds(kb, nk), :]
    lane = lax.broadcasted_iota(jnp.int32, (1, LANES), 1)
    scale = NA_HEAD_DIM ** -0.5
    scores = []
    for h in range(2):
        in_head = (lane >= h * NA_HEAD_DIM) & (lane < (h + 1) * NA_HEAD_DIM)
        qh = jnp.where(in_head, q * scale, 0).astype(jnp.bfloat16)
        scores.append(_dot_nt(qh, k))
    outs = []
    for h in range(2):
        s = scores[h] + bias_ref[0, h]
        m = jnp.max(s, axis=-1, keepdims=True)
        p = jnp.exp(s - m)
        l = jnp.sum(p, axis=-1, keepdims=True)
        pv = _dot(_bf16(p), v)
        outs.append(pv / l)
    o_ref[...] = _bf16(jnp.where(lane < NA_HEAD_DIM, outs[0], outs[1]))


def _na_attention(proj, bias_tab, batch, seq, col_q, col_k, col_v):
    T = proj.shape[0]
    n_rows = seq // GRID_W
    n_groups = n_rows // NA_GROUP_ROWS
    tq = NA_GROUP_ROWS * GRID_W
    n_pairs = NA_HEADS // 2

    def cls(g):
        return jnp.where(g == 0, 0, jnp.where(g == n_groups - 1, 2, 1))

    kern = functools.partial(_na_kernel, n_rows=n_rows)
    return pl.pallas_call(
        kern,
        out_shape=jax.ShapeDtypeStruct((T, NA_WIDTH), jnp.bfloat16),
        grid=(n_pairs, batch, n_groups),
        in_specs=[
            pl.BlockSpec((tq, LANES), lambda hp, b, g: (b * n_groups + g, col_q // LANES + hp)),
            pl.BlockSpec((seq, LANES), lambda hp, b, g: (b, col_k // LANES + hp)),
            pl.BlockSpec((seq, LANES), lambda hp, b, g: (b, col_v // LANES + hp)),
            pl.BlockSpec((1, 2, tq, NA_KEY_ROWS * GRID_W), lambda hp, b, g: (cls(g), hp, 0, 0)),
        ],
        out_specs=pl.BlockSpec((tq, LANES), lambda hp, b, g: (b * n_groups + g, hp)),
        compiler_params=_cparams(("arbitrary", "arbitrary", "arbitrary")),
        name="na_attn",
    )(proj, proj, proj, bias_tab)


def _na_bias_tables(rpb, n_rows):
    H = rpb.shape[0]
    kw = NA_WIN_COLS
    W = GRID_W
    pad = W - kw
    rp = jnp.pad(rpb.astype(jnp.float32), ((0, 0), (0, 0), (pad, pad)), constant_values=NEG_BIG)
    toep = jnp.stack([rp[:, :, W - 1 - qc: 2 * W - 1 - qc] for qc in range(W)], axis=2)
    cidx = np.arange(W)
    col_start = np.clip(cidx - kw // 2, 0, W - kw)
    col_in = (cidx[None, :] >= col_start[:, None]) & (cidx[None, :] < col_start[:, None] + kw)
    toep = jnp.where(jnp.asarray(col_in)[None, None], toep, NEG_BIG)
    neg_blk = jnp.full((H, W, W), NEG_BIG, jnp.float32)
    tabs = []
    for c in range(3):
        rows_i = []
        for i in range(NA_GROUP_ROWS):
            if c == 0:
                j0, off = max(i - NA_WIN_ROWS // 2, 0), NA_WIN_ROWS - 1
            elif c == 1:
                j0, off = i, NA_WIN_ROWS - 1 - NA_WIN_ROWS // 2
            else:
                j0, off = min(i + NA_WIN_ROWS // 2, NA_KEY_ROWS - NA_WIN_ROWS), -1
            blks = []
            for j in range(NA_KEY_ROWS):
                if j0 <= j < j0 + NA_WIN_ROWS:
                    blks.append(toep[:, j - i + off])
                else:
                    blks.append(neg_blk)
            rows_i.append(jnp.concatenate(blks, axis=-1))
        tabs.append(jnp.concatenate(rows_i, axis=1))
    return jnp.stack(tabs, axis=0)


def _mla_prep_kernel(cq_ref, ckv_ref, kr_ref, gq_ref, gkv_ref, wqm_ref, wqs_ref, wk_ref, wv_ref,
                     pm_ref, cq_tab_ref, sq_tab_ref, csk_tab_ref, q_out, k_out, v_out, *, qscale):
    cq = _f32(cq_ref[...])
    cqn = _bf16(_rms(cq, gq_ref[...], n=MLA_Q_LORA))
    qm = _dot(cqn, wqm_ref[...])
    qs = _dot(cqn, wqs_ref[...])
    ctab = cq_tab_ref[...]
    stab = sq_tab_ref[...]
    for h in range(MLA_HEADS):
        sl = slice(h * LANES, (h + 1) * LANES)
        qh = (qm[:, sl] * ctab + qs[:, sl] * stab) * qscale
        q_out[0, h] = _bf16(qh.T)

    ckv = _f32(ckv_ref[...])
    ckvn = _bf16(_rms(ckv, gkv_ref[...]))
    kk = _dot(ckvn, wk_ref[...])
    vv = _dot(ckvn, wv_ref[...])
    n_chunk = v_out.shape[2]
    ones = jnp.ones((MLA_VROWS - MLA_V, MLA_KEY_CHUNK), jnp.bfloat16)
    for hp in range(MLA_HEADS // 2):
        vt = _bf16(vv[:, hp * LANES:(hp + 1) * LANES].T)
        for cc in range(n_chunk):
            ks = slice(cc * MLA_KEY_CHUNK, (cc + 1) * MLA_KEY_CHUNK)
            for h in range(2):
                r0 = h * MLA_VROWS
                v_out[0, hp, cc, r0:r0 + MLA_V, :] = vt[h * MLA_V:(h + 1) * MLA_V, ks]
                v_out[0, hp, cc, r0 + MLA_V:r0 + MLA_VROWS, :] = ones
    krr = _f32(kr_ref[...]) * csk_tab_ref[...]
    kplace = _dot(_bf16(krr), pm_ref[...])
    for h in range(MLA_HEADS):
        sl = slice(h * LANES, (h + 1) * LANES)
        k_out[:, sl] = _bf16(kk[:, sl] + kplace)


def _mla_prep(proj, col_cq, col_ckv, col_kr, gq, gkv, wqm, wqs, wk, wv, pm, cq_tab, sq_tab, csk_tab,
              seq, qscale, tm=512):
    T = proj.shape[0]
    batch = T // seq
    n_s = seq // tm
    n_pairs = MLA_HEADS // 2
    cpt = tm // MLA_KEY_CHUNK
    full = lambda a: pl.BlockSpec(a.shape, lambda i: (0,) * a.ndim)
    tab = pl.BlockSpec((tm, LANES), lambda i: (i % n_s, 0))
    kern = functools.partial(_mla_prep_kernel, qscale=qscale)
    return pl.pallas_call(
        kern,
        out_shape=(
            jax.ShapeDtypeStruct((batch, MLA_HEADS, LANES, seq), jnp.bfloat16),
            jax.ShapeDtypeStruct((T, MLA_HEADS * LANES), jnp.bfloat16),
            jax.ShapeDtypeStruct((batch, n_pairs, seq // MLA_KEY_CHUNK, 2 * MLA_VROWS, MLA_KEY_CHUNK),
                                 jnp.bfloat16),
        ),
        grid=(T // tm,),
        in_specs=[
            pl.BlockSpec((tm, Q_LORA_PAD), lambda i: (i, col_cq // Q_LORA_PAD)),
            pl.BlockSpec((tm, MLA_KV_LORA), lambda i: (i, col_ckv // MLA_KV_LORA)),
            pl.BlockSpec((tm, LANES), lambda i: (i, col_kr // LANES)),
            full(gq), full(gkv), full(wqm), full(wqs), full(wk), full(wv), full(pm),
            tab, tab, tab,
        ],
        out_specs=(
            pl.BlockSpec((1, MLA_HEADS, LANES, tm), lambda i: (i // n_s, 0, 0, i % n_s)),
            pl.BlockSpec((tm, MLA_HEADS * LANES), lambda i: (i, 0)),
            pl.BlockSpec((1, n_pairs, cpt, 2 * MLA_VROWS, MLA_KEY_CHUNK),
                         lambda i: (i // n_s, 0, i % n_s, 0, 0)),
        ),
        compiler_params=_cparams(("arbitrary",)),
        name="mla_prep",
    )(proj, proj, proj, gq, gkv, wqm, wqs, wk, wv, pm, cq_tab, sq_tab, csk_tab)


def _mla_attn_kernel(qt_ref, k_ref, vt_ref, o_ref, m_scr, acc_scr, st_scr):
    n_chunks = vt_ref.shape[2]
    tk = MLA_KEY_CHUNK
    m_scr[...] = jnp.full(m_scr.shape, -jnp.inf, jnp.float32)
    acc_scr[...] = jnp.zeros(acc_scr.shape, jnp.float32)

    def scores(c, slot):
        k0 = pl.multiple_of(c * tk, tk)
        for h in range(2):
            kc = k_ref[pl.ds(k0, tk), h * LANES:(h + 1) * LANES]
            st_scr[slot, h] = _dot(kc, qt_ref[0, h])

    def step(c, slot):
        scores(jnp.minimum(c + 1, n_chunks - 1), 1 - slot)
        vt = vt_ref[0, 0, c]
        pts, alphas = [], []
        for h in range(2):
            st = st_scr[slot, h]
            m_old = m_scr[h]
            m_new = jnp.maximum(m_old, jnp.max(st, axis=0, keepdims=True))
            alpha = jnp.exp2(m_old - m_new)
            pt = jnp.exp2(st - m_new)
            m_scr[h] = m_new
            pts.append(_bf16(pt))
            alphas.append(alpha)
        for h in range(2):
            acc_scr[h] = alphas[h] * acc_scr[h] + _dot(vt[h * MLA_VROWS:(h + 1) * MLA_VROWS, :], pts[h])

    def body(j, carry):
        for u in range(MLA_STEPS_PER_TRIP):
            step(MLA_STEPS_PER_TRIP * j + u, u % 2)
        return carry

    scores(0, 0)
    lax.fori_loop(0, n_chunks // MLA_STEPS_PER_TRIP, body, 0)
    outs = []
    for h in range(2):
        acc = acc_scr[h]
        outs.append(acc[0:MLA_V] / acc[MLA_V:MLA_V + 1])
    o_ref[...] = _bf16(jnp.concatenate(outs, axis=0).T)


def _mla_attention(q_t, k_cat, v_t, batch, seq, tq=512):
    T = k_cat.shape[0]
    n_q = seq // tq
    n_pairs = MLA_HEADS // 2
    n_chunks = seq // MLA_KEY_CHUNK
    assert n_chunks % MLA_STEPS_PER_TRIP == 0
    return pl.pallas_call(
        _mla_attn_kernel,
        out_shape=jax.ShapeDtypeStruct((T, MLA_WIDTH), jnp.bfloat16),
        grid=(batch, n_pairs, n_q),
        in_specs=[
            pl.BlockSpec((1, 2, LANES, tq), lambda b, hp, i: (b, hp, 0, i)),
            pl.BlockSpec((seq, 2 * LANES), lambda b, hp, i: (b, hp)),
            pl.BlockSpec((1, 1, n_chunks, 2 * MLA_VROWS, MLA_KEY_CHUNK), lambda b, hp, i: (b, hp, 0, 0, 0)),
        ],
        out_specs=pl.BlockSpec((tq, LANES), lambda b, hp, i: (b * n_q + i, hp)),
        scratch_shapes=[
            pltpu.VMEM((2, 1, tq), jnp.float32),
            pltpu.VMEM((2, MLA_VROWS, tq), jnp.float32),
            pltpu.VMEM((2, 2, MLA_KEY_CHUNK, tq), jnp.float32),
        ],
        compiler_params=_cparams(("arbitrary", "arbitrary", "arbitrary")),
        name="mla_attn",
    )(q_t, k_cat, v_t)


def _out_proj_kernel(ona_ref, omla_ref, gate_ref, x_ref, mod_ref, gpost_ref, gpre_ref, wna_ref,
                     wmla_ref, wout_ref, wr_ref, br_ref, x1_ref, h2_ref, lg_ref, *, d_model):
    D = d_model
    gates = gate_ref[...]
    merged = (_f32(gates[:, 0:D]) * _dot(ona_ref[...], wna_ref[...])
              + _f32(gates[:, D:2 * D]) * _dot(omla_ref[...], wmla_ref[...]))
    y = _dot(_bf16(merged), wout_ref[...])
    gate_a = mod_ref[0, :, 2 * D:3 * D]
    shift_f = mod_ref[0, :, 3 * D:4 * D]
    scale_f = mod_ref[0, :, 4 * D:5 * D]
    x1 = x_ref[...] + gate_a * _rms(y, gpost_ref[...])
    x1_ref[...] = x1
    h2 = _rms(x1, gpre_ref[...]) * (1.0 + scale_f) + shift_f
    _store_row_tiles(h2_ref, h2)
    lg_ref[...] = _dot(_bf16(h2), wr_ref[...]) + br_ref[...]


def _out_proj(o_na, o_mla, proj, x2, mod3, g_post, g_pre, w_na, w_mla, w_out, w_r, b_r, seq, tm=512):
    T, D = x2.shape
    full = lambda a: pl.BlockSpec(a.shape, lambda i: (0,) * a.ndim)
    row = lambda w: pl.BlockSpec((tm, w), lambda i: (i, 0))
    kern = functools.partial(_out_proj_kernel, d_model=D)
    g_post = g_post.reshape(1, D)
    g_pre = g_pre.reshape(1, D)
    return pl.pallas_call(
        kern,
        out_shape=(
            jax.ShapeDtypeStruct((T, D), jnp.float32),
            jax.ShapeDtypeStruct((T * (D // LANES), LANES), jnp.float32),
            jax.ShapeDtypeStruct((T, LANES), jnp.float32),
        ),
        grid=(T // tm,),
        in_specs=[
            row(NA_WIDTH), row(MLA_WIDTH),
            pl.BlockSpec((tm, 2 * D), lambda i: (i, COL_GATES // (2 * D))),
            row(D),
            pl.BlockSpec((1, 1, mod3.shape[-1]), lambda i: ((i * tm) // seq, 0, 0)),
            full(g_post), full(g_pre), full(w_na), full(w_mla), full(w_out), full(w_r), full(b_r),
        ],
        out_specs=(row(D), pl.BlockSpec((tm * (D // LANES), LANES), lambda i: (i, 0)), row(LANES)),
        compiler_params=_cparams(("arbitrary",)),
        name="out_proj",
    )(o_na, o_mla, proj, x2, mod3, g_post, g_pre, w_na, w_mla, w_out, w_r, b_r)


def _route_kernel(lg_ref, r_ref, cnt_ref, carry_scr, *, sub):
    @pl.when(pl.program_id(0) == 0)
    def _():
        carry_scr[...] = jnp.zeros(carry_scr.shape, jnp.float32)

    tr = lg_ref.shape[0]
    lane = lax.broadcasted_iota(jnp.int32, (sub, LANES), 1).astype(jnp.float32)
    ri = lax.broadcasted_iota(jnp.int32, (sub, sub), 0)
    ci = lax.broadcasted_iota(jnp.int32, (sub, sub), 1)
    tri = jnp.where(ri >= ci, 1.0, 0.0).astype(jnp.bfloat16)
    for s0 in range(0, tr, sub):
        work = lg_ref[s0:s0 + sub, :]
        sels, vals, idxs = [], [], []
        for _k in range(TOP_K):
            mk = jnp.max(work, axis=-1, keepdims=True)
            ik = jnp.min(jnp.where(work == mk, lane, float(LANES)), axis=-1, keepdims=True)
            sk = lane == ik
            work = jnp.where(sk, -jnp.inf, work)
            sels.append(sk)
            vals.append(mk)
            idxs.append(ik)
        es = [jnp.exp(v - vals[0]) for v in vals]
        denom = es[0] + es[1] + es[2] + es[3]
        onehot = jnp.zeros((sub, LANES), jnp.float32)
        for sk in sels:
            onehot = jnp.where(sk, 1.0, onehot)
        prefix = _dot(tri, _bf16(onehot))
        carry = carry_scr[...]
        rank_mat = carry + prefix - 1.0
        res = jnp.zeros((sub, LANES), jnp.float32)
        for kk in range(TOP_K):
            rank_k = jnp.sum(jnp.where(sels[kk], rank_mat, 0.0), axis=-1, keepdims=True)
            res = jnp.where(lane == kk, idxs[kk], res)
            res = jnp.where(lane == TOP_K + kk, rank_k, res)
            res = jnp.where(lane == 2 * TOP_K + kk, es[kk] / denom, res)
        r_ref[s0:s0 + sub, :] = res
        carry_scr[...] = carry + jnp.sum(onehot, axis=0, keepdims=True)
    cnt_ref[...] = carry_scr[...]


def _route(logits, tr=2048, sub=256):
    T = logits.shape[0]
    kern = functools.partial(_route_kernel, sub=sub)
    return pl.pallas_call(
        kern,
        out_shape=(
            jax.ShapeDtypeStruct((T, LANES), jnp.float32),
            jax.ShapeDtypeStruct((1, LANES), jnp.float32),
        ),
        grid=(T // tr,),
        in_specs=[pl.BlockSpec((tr, LANES), lambda i: (i, 0))],
        out_specs=(
            pl.BlockSpec((tr, LANES), lambda i: (i, 0)),
            pl.BlockSpec((1, LANES), lambda i: (0, 0)),
        ),
        scratch_shapes=[pltpu.VMEM((1, LANES), jnp.float32)],
        compiler_params=_cparams(("arbitrary",)),
        name="route",
    )(logits)


def _slot_map_kernel(dest_ref, tok0_ref, row0_ref, tok_ref, row_ref, sem, *, n_tokens):
    i = pl.program_id(0)
    ts = dest_ref.shape[0] // TOP_K

    @pl.when(i == 0)
    def _():
        fill_tok = pltpu.make_async_copy(tok0_ref, tok_ref, sem.at[0])
        fill_row = pltpu.make_async_copy(row0_ref, row_ref, sem.at[1])
        fill_tok.start()
        fill_row.start()
        fill_tok.wait()
        fill_row.wait()

    def place(j, carry):
        t = i * ts + j
        for kk in range(TOP_K):
            d = dest_ref[j * TOP_K + kk]
            tok_ref[d] = t
            row_ref[d] = kk * n_tokens + t
        return carry

    lax.fori_loop(0, ts, place, 0, unroll=4)


def _slot_map(dest, cap, n_tokens, ts=256):
    kern = functools.partial(_slot_map_kernel, n_tokens=n_tokens)
    smem_full = pl.BlockSpec(memory_space=pltpu.SMEM)
    return pl.pallas_call(
        kern,
        out_shape=(jax.ShapeDtypeStruct((cap,), jnp.int32), jax.ShapeDtypeStruct((cap,), jnp.int32)),
        grid=(n_tokens // ts,),
        in_specs=[pl.BlockSpec((ts * TOP_K,), lambda i: (i,), memory_space=pltpu.SMEM),
                  pl.BlockSpec(memory_space=pl.ANY), pl.BlockSpec(memory_space=pl.ANY)],
        out_specs=(smem_full, smem_full),
        scratch_shapes=[pltpu.SemaphoreType.DMA((2,))],
        compiler_params=_cparams(("arbitrary",)),
        name="slot_map",
    )(dest, jnp.zeros((cap,), jnp.int32), jnp.full((cap,), -1, jnp.int32))


def _moe_kernel(blk_e_ref, nvalid_ref, tok_ref, tok_next_ref, row_prev_ref, row_ref, h_ref, w1_ref,
                b1_ref, w2_ref, b2_ref, out_ref, w1p_scr, w2b_scr, xbuf0, xbuf1, ybuf0, ybuf1, gsem, ssem,
                *, n_items):
    i = pl.program_id(0)
    nvalid = nvalid_ref[0]
    d_ff = w2_ref.shape[1]
    n_groups = (2 * d_ff) // (2 * LANES)

    def tile(r):
        return pl.ds(pl.multiple_of(r * SUBLANES, SUBLANES), SUBLANES)

    xbufs = (xbuf0, xbuf1)
    ybufs = (ybuf0, ybuf1)

    def gather_copy(t, j, slot):
        return pltpu.make_async_copy(h_ref.at[tile(t), :], xbufs[slot].at[pl.ds(j * SUBLANES, SUBLANES), :],
                                     gsem.at[slot])

    def scatter_copy(j, r, slot):
        return pltpu.make_async_copy(ybufs[slot].at[pl.ds(j * SUBLANES, SUBLANES), :], out_ref.at[tile(r), :],
                                     ssem.at[slot])

    def wait_gathers(slot):
        for _j in range(MOE_BLOCK):
            gather_copy(0, 0, slot).wait()

    def wait_scatters(slot):
        for _j in range(MOE_BLOCK):
            scatter_copy(0, 0, slot).wait()

    def scatter_rows(rows_ref, slot):
        spare0 = n_items + slot * MOE_BLOCK
        for j in range(MOE_BLOCK):
            r = rows_ref[0, 0, j]
            scatter_copy(j, jnp.where(r >= 0, r, spare0 + j), slot).start(priority=j % 2)

    @pl.when(i == 0)
    def _():
        ybuf0[...] = jnp.zeros(ybuf0.shape, ybuf0.dtype)
        ybuf1[...] = jnp.zeros(ybuf1.shape, ybuf1.dtype)
        for j in range(MOE_BLOCK):
            scatter_copy(j, n_items + j, 0).start(priority=j % 2)
        for j in range(MOE_BLOCK):
            gather_copy(tok_ref[0, 0, j], j, 0).start(priority=j % 2)

    @pl.when(i < nvalid)
    def _():
        e = blk_e_ref[i]
        e_prev = blk_e_ref[jnp.maximum(i - 1, 0)]

        @pl.when((i == 0) | (e != e_prev))
        def _():
            r = lax.broadcasted_iota(jnp.int32, (2 * LANES, 2 * LANES), 0)
            c = lax.broadcasted_iota(jnp.int32, (2 * LANES, 2 * LANES), 1)
            src = jnp.where(c < LANES, 2 * c, 2 * (c - LANES) + 1)
            perm = jnp.where(r == src, 1.0, 0.0).astype(jnp.bfloat16)
            for gI in range(n_groups):
                sl = slice(gI * 2 * LANES, (gI + 1) * 2 * LANES)
                w1p_scr[:, sl] = _bf16(_dot(_bf16(w1_ref[0, :, sl]), perm))
            w2b_scr[...] = _bf16(w2_ref[0])

        def block_step(slot):
            wait_gathers(slot)
            wait_scatters(slot)
            for j in range(MOE_BLOCK):
                gather_copy(tok_next_ref[0, 0, j], j, 1 - slot).start(priority=j % 2)
            scatter_rows(row_prev_ref, 1 - slot)

            xb = _bf16(_load_row_tiles(xbufs[slot], MOE_BLOCK))
            hcat = _dot(xb, w1p_scr[...]) + b1_ref[0]
            acts = []
            for gI in range(n_groups):
                glu = jnp.minimum(hcat[:, gI * 2 * LANES: gI * 2 * LANES + LANES], SWIGLU_LIMIT)
                lin = jnp.clip(hcat[:, gI * 2 * LANES + LANES: (gI + 1) * 2 * LANES],
                               -SWIGLU_LIMIT, SWIGLU_LIMIT)
                acts.append(_bf16(glu * jax.nn.sigmoid(SWIGLU_ALPHA * glu) * (lin + 1.0)))
            act = jnp.concatenate(acts, axis=-1)
            _store_row_tiles(ybufs[slot], _dot(act, w2b_scr[...]) + b2_ref[0])

            @pl.when(i == nvalid - 1)
            def _():
                scatter_rows(row_ref, slot)
                wait_gathers(1 - slot)
                wait_scatters(1 - slot)
                wait_scatters(slot)

        parity = lax.rem(i, 2)
        for slot in range(2):
            pl.when(parity == slot)(functools.partial(block_step, slot))


def _moe_ffn(h2_tiles, slot_tok, slot_row, blk_e, nvalid, w1, b1p, w2, b2):
    D = w1.shape[1]
    assert D == SUBLANES * LANES
    T = h2_tiles.shape[0] // SUBLANES
    E, _, F2 = w1.shape
    F = w2.shape[1]
    nblk = slot_tok.shape[0]
    n_items = T * TOP_K

    def cur(i, be, nv):
        return (jnp.minimum(i, nv[0] - 1), 0, 0)

    def nxt(i, be, nv):
        return (jnp.minimum(i + 1, nv[0] - 1), 0, 0)

    def prv(i, be, nv):
        return (jnp.where(i == 0, nblk, jnp.maximum(jnp.minimum(i, nv[0] - 1) - 1, 0)), 0, 0)

    slot_row = jnp.concatenate([slot_row, jnp.full((1, 1, MOE_BLOCK), -1, slot_row.dtype)], axis=0)
    slot_spec = lambda f: pl.BlockSpec((1, 1, MOE_BLOCK), f, memory_space=pltpu.SMEM)
    grid_spec = pltpu.PrefetchScalarGridSpec(
        num_scalar_prefetch=2,
        grid=(nblk,),
        in_specs=[
            slot_spec(cur), slot_spec(nxt), slot_spec(prv), slot_spec(cur),
            pl.BlockSpec(memory_space=pl.ANY),
            pl.BlockSpec((1, D, F2), lambda i, be, nv: (be[i], 0, 0)),
            pl.BlockSpec((1, 1, F2), lambda i, be, nv: (be[i], 0, 0)),
            pl.BlockSpec((1, F, D), lambda i, be, nv: (be[i], 0, 0)),
            pl.BlockSpec((1, 1, D), lambda i, be, nv: (be[i], 0, 0)),
        ],
        out_specs=pl.BlockSpec(memory_space=pl.ANY),
        scratch_shapes=[
            pltpu.VMEM((D, F2), jnp.bfloat16),
            pltpu.VMEM((F, D), jnp.bfloat16),
            pltpu.VMEM((MOE_BLOCK * SUBLANES, LANES), jnp.float32),
            pltpu.VMEM((MOE_BLOCK * SUBLANES, LANES), jnp.float32),
            pltpu.VMEM((MOE_BLOCK * SUBLANES, LANES), jnp.float32),
            pltpu.VMEM((MOE_BLOCK * SUBLANES, LANES), jnp.float32),
            pltpu.SemaphoreType.DMA((2,)),
            pltpu.SemaphoreType.DMA((2,)),
        ],
    )
    kern = functools.partial(_moe_kernel, n_items=n_items)
    return pl.pallas_call(
        kern,
        out_shape=jax.ShapeDtypeStruct(((n_items + 2 * MOE_BLOCK) * SUBLANES, LANES), jnp.float32),
        grid_spec=grid_spec,
        compiler_params=_cparams(("arbitrary",)),
        name="moe_ffn",
    )(blk_e, nvalid, slot_tok, slot_tok, slot_row, slot_row, h2_tiles, w1, b1p, w2, b2)


def _combine_kernel(y0_ref, y1_ref, y2_ref, y3_ref, r_ref, x1_ref, mod_ref, g_ref, o_ref, *, d_model):
    D = d_model
    tc = x1_ref.shape[0]
    r = r_ref[...]
    moe = None
    for kk, y_ref in enumerate((y0_ref, y1_ref, y2_ref, y3_ref)):
        term = r[:, 2 * TOP_K + kk: 2 * TOP_K + kk + 1] * _load_row_tiles(y_ref, tc)
        moe = term if moe is None else moe + term
    gate_f = mod_ref[0, :, 5 * D:6 * D]
    o_ref[...] = x1_ref[...] + gate_f * _rms(moe, g_ref[...])


def _combine(y_k, route, x1, mod3, g_post, seq, tc=512):
    T, D = x1.shape
    n_t = T // tc
    kern = functools.partial(_combine_kernel, d_model=D)
    y_spec = lambda kk: pl.BlockSpec((tc * SUBLANES, LANES), lambda i: (kk * n_t + i, 0))
    return pl.pallas_call(
        kern,
        out_shape=jax.ShapeDtypeStruct((T, D), jnp.float32),
        grid=(n_t,),
        in_specs=[
            y_spec(0), y_spec(1), y_spec(2), y_spec(3),
            pl.BlockSpec((tc, LANES), lambda i: (i, 0)),
            pl.BlockSpec((tc, D), lambda i: (i, 0)),
            pl.BlockSpec((1, 1, mod3.shape[-1]), lambda i: ((i * tc) // seq, 0, 0)),
            pl.BlockSpec((1, D), lambda i: (0, 0)),
        ],
        out_specs=pl.BlockSpec((tc, D), lambda i: (i, 0)),
        compiler_params=_cparams(("arbitrary",)),
        name="combine",
    )(y_k, y_k, y_k, y_k, route, x1, mod3, g_post.reshape(1, D))


def _rope_swap_cols(w):
    nf = MLA_ROPE // 4
    return jnp.concatenate([-w[..., nf:2 * nf], w[..., 0:nf], -w[..., 3 * nf:4 * nf], w[..., 2 * nf:3 * nf]],
                           axis=-1)


def _rope_tables(seq):
    n_rows = seq // GRID_W
    nf = MLA_ROPE // 4
    inv = ROPE_THETA ** (-jnp.arange(nf, dtype=jnp.float32) / nf)
    ar = jnp.arange(n_rows, dtype=jnp.float32)[:, None] * inv
    ac = jnp.arange(GRID_W, dtype=jnp.float32)[:, None] * inv
    per_row = lambda a: jnp.broadcast_to(a[:, None, :], (n_rows, GRID_W, nf)).reshape(seq, nf)
    per_col = lambda a: jnp.broadcast_to(a[None, :, :], (n_rows, GRID_W, nf)).reshape(seq, nf)
    cr, sr, cc, sc = per_row(jnp.cos(ar)), per_row(jnp.sin(ar)), per_col(jnp.cos(ac)), per_col(jnp.sin(ac))
    c32 = jnp.concatenate([cr, cr, cc, cc], axis=-1)
    s32 = jnp.concatenate([sr, sr, sc, sc], axis=-1)
    one = jnp.ones((seq, MLA_NOPE), jnp.float32)
    z64 = jnp.zeros((seq, MLA_NOPE), jnp.float32)
    z32 = jnp.zeros((seq, LANES - MLA_NOPE - MLA_ROPE), jnp.float32)
    cq_tab = jnp.concatenate([one, c32, z32], axis=-1)
    sq_tab = jnp.concatenate([z64, s32, z32], axis=-1)
    csk_tab = jnp.concatenate([c32, s32, z64], axis=-1)
    return cq_tab, sq_tab, csk_tab


def kernel(x, c, w_ada, b_ada, g_attn_pre, g_attn_post, w_in, b_gate, na_rpb, q_norm_g, kv_norm_g,
           w_uq, w_ukv, w_na_up, w_mla_up, w_out, g_ffn_pre, g_ffn_post, w_router, b_router,
           w1, b1, w2, b2):
    B, S, D = x.shape
    T = B * S
    depth = w_ada.shape[0]
    E = w_router.shape[-1]
    n_rows = S // GRID_W
    assert S % (NA_GROUP_ROWS * GRID_W) == 0 and n_rows >= NA_KEY_ROWS
    assert E <= LANES and T % 2048 == 0

    bf = jnp.bfloat16
    cq_tab, sq_tab, csk_tab = _rope_tables(S)
    pm_np = np.zeros((LANES, LANES), np.float32)
    for cidx in range(MLA_ROPE):
        pm_np[cidx, MLA_NOPE + cidx] = 1.0
        pm_np[MLA_ROPE + cidx, MLA_NOPE + cidx] = 1.0
    pm = jnp.asarray(pm_np, bf)
    qscale = float((MLA_NOPE + MLA_ROPE) ** -0.5 * math.log2(math.e))

    col_na = 2 * D
    col_cq = col_na + 3 * NA_WIDTH
    col_ckv = col_cq + Q_LORA_PAD
    col_kr = col_ckv + MLA_KV_LORA
    assert col_cq % Q_LORA_PAD == 0 and col_ckv % MLA_KV_LORA == 0

    x2 = x.reshape(T, D)
    for l in range(depth):
        mod = _ada_mod(c, w_ada[l], b_ada[l])
        mod3 = mod.reshape(B, 1, 6 * D)

        o_na, o_cq, o_ckv, o_kr, o_g = np.cumsum([0, 3 * NA_WIDTH, MLA_Q_LORA, MLA_KV_LORA, MLA_ROPE]).tolist()
        wi = w_in[l]
        w_kr = wi[:, o_kr:o_g]
        w_ext = jnp.concatenate([
            wi[:, o_g:o_g + 2 * D],
            wi[:, o_na:o_cq],
            wi[:, o_cq:o_ckv], jnp.zeros((D, Q_LORA_PAD - MLA_Q_LORA), wi.dtype),
            wi[:, o_ckv:o_kr],
            w_kr, _rope_swap_cols(w_kr), jnp.zeros((D, LANES - 2 * MLA_ROPE), wi.dtype),
        ], axis=1).astype(bf)
        proj = _in_proj(x2, mod3, g_attn_pre[l], w_ext, b_gate[l], S)

        bias_tab = _na_bias_tables(na_rpb[l], n_rows)
        o_na_tok = _na_attention(proj, bias_tab, B, S, col_na, col_na + NA_WIDTH, col_na + 2 * NA_WIDTH)

        wq = w_uq[l]
        zq = jnp.zeros((MLA_Q_LORA, MLA_HEADS, LANES - MLA_NOPE - MLA_ROPE), wq.dtype)
        wqm = jnp.concatenate([wq, zq], axis=-1).reshape(MLA_Q_LORA, MLA_HEADS * LANES)
        wqs = jnp.concatenate([jnp.zeros((MLA_Q_LORA, MLA_HEADS, MLA_NOPE), wq.dtype),
                               _rope_swap_cols(wq[..., MLA_NOPE:]), zq], axis=-1
                              ).reshape(MLA_Q_LORA, MLA_HEADS * LANES)
        rpad = ((0, Q_LORA_PAD - MLA_Q_LORA), (0, 0))
        wqm = jnp.pad(wqm, rpad).astype(bf)
        wqs = jnp.pad(wqs, rpad).astype(bf)
        wkv = w_ukv[l]
        wk = jnp.concatenate([wkv[..., :MLA_NOPE], jnp.zeros((MLA_KV_LORA, MLA_HEADS, LANES - MLA_NOPE), wkv.dtype)],
                             axis=-1).reshape(MLA_KV_LORA, MLA_HEADS * LANES).astype(bf)
        wv = wkv[..., MLA_NOPE:].reshape(MLA_KV_LORA, MLA_WIDTH).astype(bf)
        gq = jnp.pad(q_norm_g[l], (0, Q_LORA_PAD - MLA_Q_LORA)).reshape(1, Q_LORA_PAD)
        gkv = kv_norm_g[l].reshape(1, MLA_KV_LORA)
        q_cat, k_cat, v_mla = _mla_prep(proj, col_cq, col_ckv, col_kr, gq, gkv, wqm, wqs, wk, wv, pm,
                                        cq_tab, sq_tab, csk_tab, S, qscale)
        o_mla_tok = _mla_attention(q_cat, k_cat, v_mla, B, S)

        w_r = jnp.pad(w_router[l], ((0, 0), (0, LANES - E))).astype(bf)
        b_r = jnp.concatenate([b_router[l], jnp.full((LANES - E,), NEG_BIG, jnp.float32)]).reshape(1, LANES)
        x1, h2, logits = _out_proj(o_na_tok, o_mla_tok, proj, x2, mod3, g_attn_post[l], g_ffn_pre[l],
                                   w_na_up[l].astype(bf), w_mla_up[l].astype(bf), w_out[l].astype(bf),
                                   w_r, b_r, S)

        route, counts = _route(logits)
        e_idx = route[:, 0:TOP_K].astype(jnp.int32)
        rank = route[:, TOP_K:2 * TOP_K].astype(jnp.int32)
        cnt = counts[0, :E].astype(jnp.int32)
        padded = ((cnt + MOE_BLOCK - 1) // MOE_BLOCK) * MOE_BLOCK
        pend = jnp.cumsum(padded)
        pstart = pend - padded
        n_items = T * TOP_K
        nblk = -(-n_items // MOE_BLOCK) + E
        cap = nblk * MOE_BLOCK
        dest = (rank + jnp.sum(jnp.where(e_idx[..., None] == jnp.arange(E, dtype=jnp.int32), pstart, 0), axis=-1)
                ).reshape(n_items)
        blk_off = jnp.arange(nblk, dtype=jnp.int32) * MOE_BLOCK
        blk_e = jnp.minimum(jnp.sum((pend[None, :] <= blk_off[:, None]).astype(jnp.int32), axis=-1), E - 1)
        nvalid = (pend[-1:] // MOE_BLOCK).astype(jnp.int32)

        slot_tok, slot_row = _slot_map(dest, cap, T)
        slot_tok = slot_tok.reshape(nblk, 1, MOE_BLOCK)
        slot_row = slot_row.reshape(nblk, 1, MOE_BLOCK)
        F2 = w1.shape[-1]
        b1p = b1[l].reshape(E, F2 // (2 * LANES), LANES, 2).transpose(0, 1, 3, 2).reshape(E, 1, F2)
        y_k = _moe_ffn(h2, slot_tok, slot_row, blk_e, nvalid, w1[l], b1p, w2[l], b2[l].reshape(E, 1, D))
        x2 = _combine(y_k, route, x1, mod3, g_ffn_post[l], S)
    return x2.reshape(B, S, D)
```

```python
import functools
import math

import numpy as np
import jax
import jax.numpy as jnp
from jax import lax
from jax.experimental import pallas as pl
from jax.experimental.pallas import tpu as pltpu

GRID_W = 64
NA_HEADS = 8
NA_HEAD_DIM = 64
NA_WIN_ROWS = 8
NA_WIN_COLS = 16
NA_WIDTH = NA_HEADS * NA_HEAD_DIM
MLA_HEADS = 8
MLA_Q_LORA = 384
MLA_KV_LORA = 256
MLA_NOPE = 64
MLA_ROPE = 32
MLA_V = 64
MLA_WIDTH = MLA_HEADS * MLA_V
MLA_KEY_CHUNK = 256
MLA_VROWS = MLA_V + 16
MLA_STEPS_PER_TRIP = 8
ROPE_THETA = 100.0
TOP_K = 4
SWIGLU_ALPHA = 1.702
SWIGLU_LIMIT = 7.0
MOE_BLOCK = 256
RMS_EPS = 1e-6
NEG_BIG = -1e30

LANES = 128
VMEM_LIMIT_BYTES = 56 * 1024 * 1024

Q_LORA_PAD = 512
COL_GATES = 0
NA_GROUP_ROWS = 8
NA_KEY_ROWS = 16


def _f32(x):
    return x.astype(jnp.float32)


def _bf16(x):
    return x.astype(jnp.bfloat16)


def _dot(a, b):
    return jnp.dot(a, b, preferred_element_type=jnp.float32)


def _dot_nt(a, b):
    return lax.dot_general(a, b, (((1,), (1,)), ((), ())), preferred_element_type=jnp.float32)


def _rms(x, g, n=None):
    n = x.shape[-1] if n is None else n
    ms = jnp.sum(x * x, axis=-1, keepdims=True) * (1.0 / n)
    return x * lax.rsqrt(ms + RMS_EPS) * g


SUBLANES = 8


def _store_row_tiles(ref, x, base=0):
    n = x.shape[0]
    for c in range(x.shape[1] // LANES):
        ref[pl.ds(base + c, n, stride=SUBLANES), :] = x[:, c * LANES:(c + 1) * LANES]


def _load_row_tiles(ref, n, base=0):
    chunks = [ref[pl.ds(base + c, n, stride=SUBLANES), :] for c in range(SUBLANES)]
    return jnp.concatenate(chunks, axis=-1)


def _cparams(sem):
    return pltpu.CompilerParams(dimension_semantics=sem, vmem_limit_bytes=VMEM_LIMIT_BYTES)


def _ada_kernel(c_ref, w_ref, b_ref, o_ref):
    c = c_ref[...]
    sc = c * jax.nn.sigmoid(c)
    o_ref[...] = _dot(_bf16(sc), _bf16(w_ref[...])) + b_ref[...]


def _ada_mod(c, w_ada, b_ada):
    B, D = c.shape
    n_out = w_ada.shape[1]
    return pl.pallas_call(
        _ada_kernel,
        out_shape=jax.ShapeDtypeStruct((B, n_out), jnp.float32),
        grid=(n_out // D,),
        in_specs=[
            pl.BlockSpec((B, D), lambda j: (0, 0)),
            pl.BlockSpec((D, D), lambda j: (0, j)),
            pl.BlockSpec((1, D), lambda j: (0, j)),
        ],
        out_specs=pl.BlockSpec((B, D), lambda j: (0, j)),
        compiler_params=_cparams(("arbitrary",)),
        name="ada_mod",
    )(c, w_ada, b_ada.reshape(1, n_out))


def _in_proj_kernel(x_ref, mod_ref, g_ref, w_ref, bg_ref, o_ref, *, d_model, n_gate, chunk):
    x = x_ref[...]
    shift = mod_ref[0, :, 0:d_model]
    scale = mod_ref[0, :, d_model:2 * d_model]
    h = _rms(x, g_ref[...]) * (1.0 + scale) + shift
    hb = _bf16(h)
    n_total = w_ref.shape[1]
    for c0 in range(0, n_total, chunk):
        c1 = min(c0 + chunk, n_total)
        acc = _dot(hb, w_ref[:, c0:c1])
        if c0 < n_gate:
            acc = jax.nn.sigmoid(acc + bg_ref[:, c0:c1])
        o_ref[:, c0:c1] = _bf16(acc)


def _in_proj(x2, mod3, g_pre, w_ext, b_gate, seq, tm=512):
    T, D = x2.shape
    n_total = w_ext.shape[1]
    n_gate = b_gate.shape[-1]
    kern = functools.partial(_in_proj_kernel, d_model=D, n_gate=n_gate, chunk=512)
    return pl.pallas_call(
        kern,
        out_shape=jax.ShapeDtypeStruct((T, n_total), jnp.bfloat16),
        grid=(T // tm,),
        in_specs=[
            pl.BlockSpec((tm, D), lambda i: (i, 0)),
            pl.BlockSpec((1, 1, mod3.shape[-1]), lambda i: ((i * tm) // seq, 0, 0)),
            pl.BlockSpec((1, D), lambda i: (0, 0)),
            pl.BlockSpec((D, n_total), lambda i: (0, 0)),
            pl.BlockSpec((1, n_gate), lambda i: (0, 0)),
        ],
        out_specs=pl.BlockSpec((tm, n_total), lambda i: (i, 0)),
        compiler_params=_cparams(("arbitrary",)),
        name="in_proj",
    )(x2, mod3, g_pre.reshape(1, D), w_ext, b_gate.reshape(1, n_gate))


def _na_kernel(q_ref, k_ref, v_ref, bias_ref, o_ref, *, n_rows):
    g = pl.program_id(2)
    kb = jnp.clip(g * NA_GROUP_ROWS - NA_WIN_ROWS // 2, 0, n_rows - NA_KEY_ROWS) * GRID_W
    kb = pl.multiple_of(kb, 256)
    nk = NA_KEY_ROWS * GRID_W
    q = q_ref[...]
    k = k_ref[pl.ds(kb, nk), :]
    v = v_ref[pl.ds(kb, nk), :]
    lane = lax.broadcasted_iota(jnp.int32, (1, LANES), 1)
    scale = NA_HEAD_DIM ** -0.5
    scores = []
    for h in range(2):
        in_head = (lane >= h * NA_HEAD_DIM) & (lane < (h + 1) * NA_HEAD_DIM)
        qh = jnp.where(in_head, q * scale, 0).astype(jnp.bfloat16)
        scores.append(_dot_nt(qh, k))
    outs = []
    for h in range(2):
        s = scores[h] + bias_ref[0, h]
        m = jnp.max(s, axis=-1, keepdims=True)
        p = jnp.exp(s - m)
        l = jnp.sum(p, axis=-1, keepdims=True)
        pv = _dot(_bf16(p), v)
        outs.append(pv / l)
    o_ref[...] = _bf16(jnp.where(lane < NA_HEAD_DIM, outs[0], outs[1]))


def _na_attention(proj, bias_tab, batch, seq, col_q, col_k, col_v):
    T = proj.shape[0]
    n_rows = seq // GRID_W
    n_groups = n_rows // NA_GROUP_ROWS
    tq = NA_GROUP_ROWS * GRID_W
    n_pairs = NA_HEADS // 2

    def cls(g):
        return jnp.where(g == 0, 0, jnp.where(g == n_groups - 1, 2, 1))

    kern = functools.partial(_na_kernel, n_rows=n_rows)
    return pl.pallas_call(
        kern,
        out_shape=jax.ShapeDtypeStruct((T, NA_WIDTH), jnp.bfloat16),
        grid=(n_pairs, batch, n_groups),
        in_specs=[
            pl.BlockSpec((tq, LANES), lambda hp, b, g: (b * n_groups + g, col_q // LANES + hp)),
            pl.BlockSpec((seq, LANES), lambda hp, b, g: (b, col_k // LANES + hp)),
            pl.BlockSpec((seq, LANES), lambda hp, b, g: (b, col_v // LANES + hp)),
            pl.BlockSpec((1, 2, tq, NA_KEY_ROWS * GRID_W), lambda hp, b, g: (cls(g), hp, 0, 0)),
        ],
        out_specs=pl.BlockSpec((tq, LANES), lambda hp, b, g: (b * n_groups + g, hp)),
        compiler_params=_cparams(("arbitrary", "arbitrary", "arbitrary")),
        name="na_attn",
    )(proj, proj, proj, bias_tab)


def _na_bias_tables(rpb, n_rows):
    H = rpb.shape[0]
    kw = NA_WIN_COLS
    W = GRID_W
    pad = W - kw
    rp = jnp.pad(rpb.astype(jnp.float32), ((0, 0), (0, 0), (pad, pad)), constant_values=NEG_BIG)
    toep = jnp.stack([rp[:, :, W - 1 - qc: 2 * W - 1 - qc] for qc in range(W)], axis=2)
    cidx = np.arange(W)
    col_start = np.clip(cidx - kw // 2, 0, W - kw)
    col_in = (cidx[None, :] >= col_start[:, None]) & (cidx[None, :] < col_start[:, None] + kw)
    toep = jnp.where(jnp.asarray(col_in)[None, None], toep, NEG_BIG)
    neg_blk = jnp.full((H, W, W), NEG_BIG, jnp.float32)
    tabs = []
    for c in range(3):
        rows_i = []
        for i in range(NA_GROUP_ROWS):
            if c == 0:
                j0, off = max(i - NA_WIN_ROWS // 2, 0), NA_WIN_ROWS - 1
            elif c == 1:
                j0, off = i, NA_WIN_ROWS - 1 - NA_WIN_ROWS // 2
            else:
                j0, off = min(i + NA_WIN_ROWS // 2, NA_KEY_ROWS - NA_WIN_ROWS), -1
            blks = []
            for j in range(NA_KEY_ROWS):
                if j0 <= j < j0 + NA_WIN_ROWS:
                    blks.append(toep[:, j - i + off])
                else:
                    blks.append(neg_blk)
            rows_i.append(jnp.concatenate(blks, axis=-1))
        tabs.append(jnp.concatenate(rows_i, axis=1))
    return jnp.stack(tabs, axis=0)


def _mla_prep_kernel(cq_ref, ckv_ref, kr_ref, gq_ref, gkv_ref, wqm_ref, wqs_ref, wk_ref, wv_ref,
                     pm_ref, cq_tab_ref, sq_tab_ref, csk_tab_ref, q_out, k_out, v_out, *, qscale):
    cq = _f32(cq_ref[...])
    cqn = _bf16(_rms(cq, gq_ref[...], n=MLA_Q_LORA))
    qm = _dot(cqn, wqm_ref[...])
    qs = _dot(cqn, wqs_ref[...])
    ctab = cq_tab_ref[...]
    stab = sq_tab_ref[...]
    for h in range(MLA_HEADS):
        sl = slice(h * LANES, (h + 1) * LANES)
        qh = (qm[:, sl] * ctab + qs[:, sl] * stab) * qscale
        q_out[0, h] = _bf16(qh.T)

    ckv = _f32(ckv_ref[...])
    ckvn = _bf16(_rms(ckv, gkv_ref[...]))
    kk = _dot(ckvn, wk_ref[...])
    vv = _dot(ckvn, wv_ref[...])
    n_chunk = v_out.shape[2]
    ones = jnp.ones((MLA_VROWS - MLA_V, MLA_KEY_CHUNK), jnp.bfloat16)
    for hp in range(MLA_HEADS // 2):
        vt = _bf16(vv[:, hp * LANES:(hp + 1) * LANES].T)
        for cc in range(n_chunk):
            ks = slice(cc * MLA_KEY_CHUNK, (cc + 1) * MLA_KEY_CHUNK)
            for h in range(2):
                r0 = h * MLA_VROWS
                v_out[0, hp, cc, r0:r0 + MLA_V, :] = vt[h * MLA_V:(h + 1) * MLA_V, ks]
                v_out[0, hp, cc, r0 + MLA_V:r0 + MLA_VROWS, :] = ones
    krr = _f32(kr_ref[...]) * csk_tab_ref[...]
    kplace = _dot(_bf16(krr), pm_ref[...])
    for h in range(MLA_HEADS):
        sl = slice(h * LANES, (h + 1) * LANES)
        k_out[:, sl] = _bf16(kk[:, sl] + kplace)


def _mla_prep(proj, col_cq, col_ckv, col_kr, gq, gkv, wqm, wqs, wk, wv, pm, cq_tab, sq_tab, csk_tab,
              seq, qscale, tm=512):
    T = proj.shape[0]
    batch = T // seq
    n_s = seq // tm
    n_pairs = MLA_HEADS // 2
    cpt = tm // MLA_KEY_CHUNK
    full = lambda a: pl.BlockSpec(a.shape, lambda i: (0,) * a.ndim)
    tab = pl.BlockSpec((tm, LANES), lambda i: (i % n_s, 0))
    kern = functools.partial(_mla_prep_kernel, qscale=qscale)
    return pl.pallas_call(
        kern,
        out_shape=(
            jax.ShapeDtypeStruct((batch, MLA_HEADS, LANES, seq), jnp.bfloat16),
            jax.ShapeDtypeStruct((T, MLA_HEADS * LANES), jnp.bfloat16),
            jax.ShapeDtypeStruct((batch, n_pairs, seq // MLA_KEY_CHUNK, 2 * MLA_VROWS, MLA_KEY_CHUNK),
                                 jnp.bfloat16),
        ),
        grid=(T // tm,),
        in_specs=[
            pl.BlockSpec((tm, Q_LORA_PAD), lambda i: (i, col_cq // Q_LORA_PAD)),
            pl.BlockSpec((tm, MLA_KV_LORA), lambda i: (i, col_ckv // MLA_KV_LORA)),
            pl.BlockSpec((tm, LANES), lambda i: (i, col_kr // LANES)),
            full(gq), full(gkv), full(wqm), full(wqs), full(wk), full(wv), full(pm),
            tab, tab, tab,
        ],
        out_specs=(
            pl.BlockSpec((1, MLA_HEADS, LANES, tm), lambda i: (i // n_s, 0, 0, i % n_s)),
            pl.BlockSpec((tm, MLA_HEADS * LANES), lambda i: (i, 0)),
            pl.BlockSpec((1, n_pairs, cpt, 2 * MLA_VROWS, MLA_KEY_CHUNK),
                         lambda i: (i // n_s, 0, i % n_s, 0, 0)),
        ),
        compiler_params=_cparams(("arbitrary",)),
        name="mla_prep",
    )(proj, proj, proj, gq, gkv, wqm, wqs, wk, wv, pm, cq_tab, sq_tab, csk_tab)


def _mla_attn_kernel(qt_ref, k_ref, vt_ref, o_ref, m_scr, acc_scr, st_scr):
    n_chunks = vt_ref.shape[2]
    tk = MLA_KEY_CHUNK
    m_scr[...] = jnp.full(m_scr.shape, -jnp.inf, jnp.float32)
    acc_scr[...] = jnp.zeros(acc_scr.shape, jnp.float32)

    def scores(c, slot):
        k0 = pl.multiple_of(c * tk, tk)
        for h in range(2):
            kc = k_ref[pl.ds(k0, tk), h * LANES:(h + 1) * LANES]
            st_scr[slot, h] = _dot(kc, qt_ref[0, h])

    def step(c, slot):
        scores(jnp.minimum(c + 1, n_chunks - 1), 1 - slot)
        vt = vt_ref[0, 0, c]
        pts, alphas = [], []
        for h in range(2):
            st = st_scr[slot, h]
            m_old = m_scr[h]
            m_new = jnp.maximum(m_old, jnp.max(st, axis=0, keepdims=True))
            alpha = jnp.exp2(m_old - m_new)
            pt = jnp.exp2(st - m_new)
            m_scr[h] = m_new
            pts.append(_bf16(pt))
            alphas.append(alpha)
        for h in range(2):
            acc_scr[h] = alphas[h] * acc_scr[h] + _dot(vt[h * MLA_VROWS:(h + 1) * MLA_VROWS, :], pts[h])

    def body(j, carry):
        for u in range(MLA_STEPS_PER_TRIP):
            step(MLA_STEPS_PER_TRIP * j + u, u % 2)
        return carry

    scores(0, 0)
    lax.fori_loop(0, n_chunks // MLA_STEPS_PER_TRIP, body, 0)
    outs = []
    for h in range(2):
        acc = acc_scr[h]
        outs.append(acc[0:MLA_V] / acc[MLA_V:MLA_V + 1])
    o_ref[...] = _bf16(jnp.concatenate(outs, axis=0).T)


def _mla_attention(q_t, k_cat, v_t, batch, seq, tq=512):
    T = k_cat.shape[0]
    n_q = seq // tq
    n_pairs = MLA_HEADS // 2
    n_chunks = seq // MLA_KEY_CHUNK
    assert n_chunks % MLA_STEPS_PER_TRIP == 0
    return pl.pallas_call(
        _mla_attn_kernel,
        out_shape=jax.ShapeDtypeStruct((T, MLA_WIDTH), jnp.bfloat16),
        grid=(batch, n_pairs, n_q),
        in_specs=[
            pl.BlockSpec((1, 2, LANES, tq), lambda b, hp, i: (b, hp, 0, i)),
            pl.BlockSpec((seq, 2 * LANES), lambda b, hp, i: (b, hp)),
            pl.BlockSpec((1, 1, n_chunks, 2 * MLA_VROWS, MLA_KEY_CHUNK), lambda b, hp, i: (b, hp, 0, 0, 0)),
        ],
        out_specs=pl.BlockSpec((tq, LANES), lambda b, hp, i: (b * n_q + i, hp)),
        scratch_shapes=[
            pltpu.VMEM((2, 1, tq), jnp.float32),
            pltpu.VMEM((2, MLA_VROWS, tq), jnp.float32),
            pltpu.VMEM((2, 2, MLA_KEY_CHUNK, tq), jnp.float32),
        ],
        compiler_params=_cparams(("arbitrary", "arbitrary", "arbitrary")),
        name="mla_attn",
    )(q_t, k_cat, v_t)


def _out_proj_kernel(ona_ref, omla_ref, gate_ref, x_ref, mod_ref, gpost_ref, gpre_ref, wna_ref,
                     wmla_ref, wout_ref, wr_ref, br_ref, x1_ref, h2_ref, lg_ref, *, d_model):
    D = d_model
    gates = gate_ref[...]
    merged = (_f32(gates[:, 0:D]) * _dot(ona_ref[...], wna_ref[...])
              + _f32(gates[:, D:2 * D]) * _dot(omla_ref[...], wmla_ref[...]))
    y = _dot(_bf16(merged), wout_ref[...])
    gate_a = mod_ref[0, :, 2 * D:3 * D]
    shift_f = mod_ref[0, :, 3 * D:4 * D]
    scale_f = mod_ref[0, :, 4 * D:5 * D]
    x1 = x_ref[...] + gate_a * _rms(y, gpost_ref[...])
    x1_ref[...] = x1
    h2 = _rms(x1, gpre_ref[...]) * (1.0 + scale_f) + shift_f
    h2_ref[...] = h2
    lg_ref[...] = _dot(_bf16(h2), wr_ref[...]) + br_ref[...]


def _out_proj(o_na, o_mla, proj, x2, mod3, g_post, g_pre, w_na, w_mla, w_out, w_r, b_r, seq, tm=512):
    T, D = x2.shape
    full = lambda a: pl.BlockSpec(a.shape, lambda i: (0,) * a.ndim)
    row = lambda w: pl.BlockSpec((tm, w), lambda i: (i, 0))
    kern = functools.partial(_out_proj_kernel, d_model=D)
    g_post = g_post.reshape(1, D)
    g_pre = g_pre.reshape(1, D)
    return pl.pallas_call(
        kern,
        out_shape=(
            jax.ShapeDtypeStruct((T, D), jnp.float32),
            jax.ShapeDtypeStruct((T, D), jnp.float32),
            jax.ShapeDtypeStruct((T, LANES), jnp.float32),
        ),
        grid=(T // tm,),
        in_specs=[
            row(NA_WIDTH), row(MLA_WIDTH),
            pl.BlockSpec((tm, 2 * D), lambda i: (i, COL_GATES // (2 * D))),
            row(D),
            pl.BlockSpec((1, 1, mod3.shape[-1]), lambda i: ((i * tm) // seq, 0, 0)),
            full(g_post), full(g_pre), full(w_na), full(w_mla), full(w_out), full(w_r), full(b_r),
        ],
        out_specs=(row(D), row(D), row(LANES)),
        compiler_params=_cparams(("arbitrary",)),
        name="out_proj",
    )(o_na, o_mla, proj, x2, mod3, g_post, g_pre, w_na, w_mla, w_out, w_r, b_r)


def _route_kernel(lg_ref, r_ref, cnt_ref, carry_scr, *, sub):
    @pl.when(pl.program_id(0) == 0)
    def _():
        carry_scr[...] = jnp.zeros(carry_scr.shape, jnp.float32)

    tr = lg_ref.shape[0]
    lane = lax.broadcasted_iota(jnp.int32, (sub, LANES), 1).astype(jnp.float32)
    ri = lax.broadcasted_iota(jnp.int32, (sub, sub), 0)
    ci = lax.broadcasted_iota(jnp.int32, (sub, sub), 1)
    tri = jnp.where(ri >= ci, 1.0, 0.0).astype(jnp.bfloat16)
    for s0 in range(0, tr, sub):
        work = lg_ref[s0:s0 + sub, :]
        sels, vals, idxs = [], [], []
        for _k in range(TOP_K):
            mk = jnp.max(work, axis=-1, keepdims=True)
            ik = jnp.min(jnp.where(work == mk, lane, float(LANES)), axis=-1, keepdims=True)
            sk = lane == ik
            work = jnp.where(sk, -jnp.inf, work)
            sels.append(sk)
            vals.append(mk)
            idxs.append(ik)
        es = [jnp.exp(v - vals[0]) for v in vals]
        denom = es[0] + es[1] + es[2] + es[3]
        onehot = jnp.zeros((sub, LANES), jnp.float32)
        for sk in sels:
            onehot = jnp.where(sk, 1.0, onehot)
        prefix = _dot(tri, _bf16(onehot))
        carry = carry_scr[...]
        rank_mat = carry + prefix - 1.0
        res = jnp.zeros((sub, LANES), jnp.float32)
        for kk in range(TOP_K):
            rank_k = jnp.sum(jnp.where(sels[kk], rank_mat, 0.0), axis=-1, keepdims=True)
            res = jnp.where(lane == kk, idxs[kk], res)
            res = jnp.where(lane == TOP_K + kk, rank_k, res)
            res = jnp.where(lane == 2 * TOP_K + kk, es[kk] / denom, res)
        r_ref[s0:s0 + sub, :] = res
        carry_scr[...] = carry + jnp.sum(onehot, axis=0, keepdims=True)
    cnt_ref[...] = carry_scr[...]


def _route(logits, tr=2048, sub=256):
    T = logits.shape[0]
    kern = functools.partial(_route_kernel, sub=sub)
    return pl.pallas_call(
        kern,
        out_shape=(
            jax.ShapeDtypeStruct((T, LANES), jnp.float32),
            jax.ShapeDtypeStruct((1, LANES), jnp.float32),
        ),
        grid=(T // tr,),
        in_specs=[pl.BlockSpec((tr, LANES), lambda i: (i, 0))],
        out_specs=(
            pl.BlockSpec((tr, LANES), lambda i: (i, 0)),
            pl.BlockSpec((1, LANES), lambda i: (0, 0)),
        ),
        scratch_shapes=[pltpu.VMEM((1, LANES), jnp.float32)],
        compiler_params=_cparams(("arbitrary",)),
        name="route",
    )(logits)


def _dispatch_kernel(zero_blk_ref, dest_ref, h_ref, row0_ref, xs_ref, row_ref, zbuf, sem, fill_sem,
                     *, n_tokens):
    i = pl.program_id(0)
    ts = h_ref.shape[0]

    @pl.when(i == 0)
    def _():
        fill_row = pltpu.make_async_copy(row0_ref, row_ref, fill_sem.at[0])
        fill_row.start()
        zbuf[...] = jnp.zeros(zbuf.shape, zbuf.dtype)
        n_zero = zero_blk_ref.shape[0]

        def zero_copy(z):
            r0 = pl.multiple_of(zero_blk_ref[z] * MOE_BLOCK, MOE_BLOCK)
            return pltpu.make_async_copy(zbuf, xs_ref.at[pl.ds(r0, MOE_BLOCK), :], fill_sem.at[1])

        for z in range(n_zero):
            pl.when(zero_blk_ref[z] >= 0)(lambda z=z: zero_copy(z).start())
        for z in range(n_zero):
            pl.when(zero_blk_ref[z] >= 0)(lambda z=z: zero_copy(z).wait())
        fill_row.wait()

    def row_copy(j, d):
        return pltpu.make_async_copy(h_ref.at[pl.ds(j, 1), :], xs_ref.at[pl.ds(d, 1), :], sem)

    def issue(j, carry):
        t = i * ts + j
        for kk in range(TOP_K):
            d = dest_ref[j * TOP_K + kk]
            row_copy(j, d).start(priority=kk % 2)
            row_ref[d] = kk * n_tokens + t
        return carry

    lax.fori_loop(0, ts, issue, 0, unroll=2)

    def drain(j, carry):
        for _kk in range(TOP_K):
            row_copy(0, 0).wait()
        return carry

    lax.fori_loop(0, ts, drain, 0)


def _dispatch(h2, dest, zero_blk, cap, ts=256):
    T, D = h2.shape
    kern = functools.partial(_dispatch_kernel, n_tokens=T)
    grid_spec = pltpu.PrefetchScalarGridSpec(
        num_scalar_prefetch=1,
        grid=(T // ts,),
        in_specs=[
            pl.BlockSpec((ts * TOP_K,), lambda i, zb: (i,), memory_space=pltpu.SMEM),
            pl.BlockSpec((ts, D), lambda i, zb: (i, 0)),
            pl.BlockSpec(memory_space=pl.ANY),
        ],
        out_specs=(pl.BlockSpec(memory_space=pl.ANY), pl.BlockSpec(memory_space=pltpu.SMEM)),
        scratch_shapes=[
            pltpu.VMEM((MOE_BLOCK, D), h2.dtype),
            pltpu.SemaphoreType.DMA(()),
            pltpu.SemaphoreType.DMA((2,)),
        ],
    )
    return pl.pallas_call(
        kern,
        out_shape=(jax.ShapeDtypeStruct((cap, D), h2.dtype), jax.ShapeDtypeStruct((cap,), jnp.int32)),
        grid_spec=grid_spec,
        compiler_params=_cparams(("arbitrary",)),
        name="dispatch",
    )(zero_blk, dest, h2, jnp.full((cap,), -1, jnp.int32))


def _moe_kernel(blk_e_ref, nvalid_ref, row_prev_ref, row_ref, x_ref, w1_ref, b1_ref, w2_ref, b2_ref,
                out_ref, w1p_scr, w2b_scr, ybuf0, ybuf1, ssem, *, n_items):
    i = pl.program_id(0)
    nvalid = nvalid_ref[0]
    d_ff = w2_ref.shape[1]
    n_groups = (2 * d_ff) // (2 * LANES)

    def tile(r):
        return pl.ds(pl.multiple_of(r * SUBLANES, SUBLANES), SUBLANES)

    ybufs = (ybuf0, ybuf1)

    def scatter_copy(j, r, slot):
        return pltpu.make_async_copy(ybufs[slot].at[pl.ds(j * SUBLANES, SUBLANES), :], out_ref.at[tile(r), :],
                                     ssem.at[slot])

    def wait_scatters(slot):
        for _j in range(MOE_BLOCK):
            scatter_copy(0, 0, slot).wait()

    def scatter_rows(rows_ref, slot):
        spare0 = n_items + slot * MOE_BLOCK
        for j in range(MOE_BLOCK):
            r = rows_ref[0, 0, j]
            scatter_copy(j, jnp.where(r >= 0, r, spare0 + j), slot).start(priority=j % 2)

    @pl.when(i == 0)
    def _():
        ybuf0[...] = jnp.zeros(ybuf0.shape, ybuf0.dtype)
        ybuf1[...] = jnp.zeros(ybuf1.shape, ybuf1.dtype)
        for j in range(MOE_BLOCK):
            scatter_copy(j, n_items + j, 0).start(priority=j % 2)

    @pl.when(i < nvalid)
    def _():
        e = blk_e_ref[i]
        e_prev = blk_e_ref[jnp.maximum(i - 1, 0)]

        @pl.when((i == 0) | (e != e_prev))
        def _():
            r = lax.broadcasted_iota(jnp.int32, (2 * LANES, 2 * LANES), 0)
            c = lax.broadcasted_iota(jnp.int32, (2 * LANES, 2 * LANES), 1)
            src = jnp.where(c < LANES, 2 * c, 2 * (c - LANES) + 1)
            perm = jnp.where(r == src, 1.0, 0.0).astype(jnp.bfloat16)
            for gI in range(n_groups):
                sl = slice(gI * 2 * LANES, (gI + 1) * 2 * LANES)
                w1p_scr[:, sl] = _bf16(_dot(_bf16(w1_ref[0, :, sl]), perm))
            w2b_scr[...] = _bf16(w2_ref[0])

        def block_step(slot):
            wait_scatters(slot)
            scatter_rows(row_prev_ref, 1 - slot)

            xb = _bf16(x_ref[...])
            hcat = _dot(xb, w1p_scr[...]) + b1_ref[0]
            acts = []
            for gI in range(n_groups):
                glu = jnp.minimum(hcat[:, gI * 2 * LANES: gI * 2 * LANES + LANES], SWIGLU_LIMIT)
                lin = jnp.clip(hcat[:, gI * 2 * LANES + LANES: (gI + 1) * 2 * LANES],
                               -SWIGLU_LIMIT, SWIGLU_LIMIT)
                acts.append(_bf16(glu * jax.nn.sigmoid(SWIGLU_ALPHA * glu) * (lin + 1.0)))
            act = jnp.concatenate(acts, axis=-1)
            _store_row_tiles(ybufs[slot], _dot(act, w2b_scr[...]) + b2_ref[0])

            @pl.when(i == nvalid - 1)
            def _():
                scatter_rows(row_ref, slot)
                wait_scatters(1 - slot)
                wait_scatters(slot)

        parity = lax.rem(i, 2)
        for slot in range(2):
            pl.when(parity == slot)(functools.partial(block_step, slot))


def _moe_ffn(xs, slot_row, n_items, blk_e, nvalid, w1, b1p, w2, b2):
    D = w1.shape[1]
    assert D == SUBLANES * LANES
    E, _, F2 = w1.shape
    F = w2.shape[1]
    nblk = slot_row.shape[0]

    def cur(i, be, nv):
        return (jnp.minimum(i, nv[0] - 1), 0, 0)

    def prv(i, be, nv):
        return (jnp.where(i == 0, nblk, jnp.maximum(jnp.minimum(i, nv[0] - 1) - 1, 0)), 0, 0)

    slot_row = jnp.concatenate([slot_row, jnp.full((1, 1, MOE_BLOCK), -1, slot_row.dtype)], axis=0)
    slot_spec = lambda f: pl.BlockSpec((1, 1, MOE_BLOCK), f, memory_space=pltpu.SMEM)
    grid_spec = pltpu.PrefetchScalarGridSpec(
        num_scalar_prefetch=2,
        grid=(nblk,),
        in_specs=[
            slot_spec(prv), slot_spec(cur),
            pl.BlockSpec((MOE_BLOCK, D), lambda i, be, nv: (jnp.minimum(i, nv[0] - 1), 0)),
            pl.BlockSpec((1, D, F2), lambda i, be, nv: (be[i], 0, 0)),
            pl.BlockSpec((1, 1, F2), lambda i, be, nv: (be[i], 0, 0)),
            pl.BlockSpec((1, F, D), lambda i, be, nv: (be[i], 0, 0)),
            pl.BlockSpec((1, 1, D), lambda i, be, nv: (be[i], 0, 0)),
        ],
        out_specs=pl.BlockSpec(memory_space=pl.ANY),
        scratch_shapes=[
            pltpu.VMEM((D, F2), jnp.bfloat16),
            pltpu.VMEM((F, D), jnp.bfloat16),
            pltpu.VMEM((MOE_BLOCK * SUBLANES, LANES), jnp.float32),
            pltpu.VMEM((MOE_BLOCK * SUBLANES, LANES), jnp.float32),
            pltpu.SemaphoreType.DMA((2,)),
        ],
    )
    kern = functools.partial(_moe_kernel, n_items=n_items)
    return pl.pallas_call(
        kern,
        out_shape=jax.ShapeDtypeStruct(((n_items + 2 * MOE_BLOCK) * SUBLANES, LANES), jnp.float32),
        grid_spec=grid_spec,
        compiler_params=_cparams(("arbitrary",)),
        name="moe_ffn",
    )(blk_e, nvalid, slot_row, slot_row, xs, w1, b1p, w2, b2)


def _combine_kernel(y0_ref, y1_ref, y2_ref, y3_ref, r_ref, x1_ref, mod_ref, g_ref, o_ref, *, d_model):
    D = d_model
    tc = x1_ref.shape[0]
    r = r_ref[...]
    moe = None
    for kk, y_ref in enumerate((y0_ref, y1_ref, y2_ref, y3_ref)):
        term = r[:, 2 * TOP_K + kk: 2 * TOP_K + kk + 1] * _load_row_tiles(y_ref, tc)
        moe = term if moe is None else moe + term
    gate_f = mod_ref[0, :, 5 * D:6 * D]
    o_ref[...] = x1_ref[...] + gate_f * _rms(moe, g_ref[...])


def _combine(y_k, route, x1, mod3, g_post, seq, tc=512):
    T, D = x1.shape
    n_t = T // tc
    kern = functools.partial(_combine_kernel, d_model=D)
    y_spec = lambda kk: pl.BlockSpec((tc * SUBLANES, LANES), lambda i: (kk * n_t + i, 0))
    return pl.pallas_call(
        kern,
        out_shape=jax.ShapeDtypeStruct((T, D), jnp.float32),
        grid=(n_t,),
        in_specs=[
            y_spec(0), y_spec(1), y_spec(2), y_spec(3),
            pl.BlockSpec((tc, LANES), lambda i: (i, 0)),
            pl.BlockSpec((tc, D), lambda i: (i, 0)),
            pl.BlockSpec((1, 1, mod3.shape[-1]), lambda i: ((i * tc) // seq, 0, 0)),
            pl.BlockSpec((1, D), lambda i: (0, 0)),
        ],
        out_specs=pl.BlockSpec((tc, D), lambda i: (i, 0)),
        compiler_params=_cparams(("arbitrary",)),
        name="combine",
    )(y_k, y_k, y_k, y_k, route, x1, mod3, g_post.reshape(1, D))


def _rope_swap_cols(w):
    nf = MLA_ROPE // 4
    return jnp.concatenate([-w[..., nf:2 * nf], w[..., 0:nf], -w[..., 3 * nf:4 * nf], w[..., 2 * nf:3 * nf]],
                           axis=-1)


def _rope_tables(seq):
    n_rows = seq // GRID_W
    nf = MLA_ROPE // 4
    inv = ROPE_THETA ** (-jnp.arange(nf, dtype=jnp.float32) / nf)
    ar = jnp.arange(n_rows, dtype=jnp.float32)[:, None] * inv
    ac = jnp.arange(GRID_W, dtype=jnp.float32)[:, None] * inv
    per_row = lambda a: jnp.broadcast_to(a[:, None, :], (n_rows, GRID_W, nf)).reshape(seq, nf)
    per_col = lambda a: jnp.broadcast_to(a[None, :, :], (n_rows, GRID_W, nf)).reshape(seq, nf)
    cr, sr, cc, sc = per_row(jnp.cos(ar)), per_row(jnp.sin(ar)), per_col(jnp.cos(ac)), per_col(jnp.sin(ac))
    c32 = jnp.concatenate([cr, cr, cc, cc], axis=-1)
    s32 = jnp.concatenate([sr, sr, sc, sc], axis=-1)
    one = jnp.ones((seq, MLA_NOPE), jnp.float32)
    z64 = jnp.zeros((seq, MLA_NOPE), jnp.float32)
    z32 = jnp.zeros((seq, LANES - MLA_NOPE - MLA_ROPE), jnp.float32)
    cq_tab = jnp.concatenate([one, c32, z32], axis=-1)
    sq_tab = jnp.concatenate([z64, s32, z32], axis=-1)
    csk_tab = jnp.concatenate([c32, s32, z64], axis=-1)
    return cq_tab, sq_tab, csk_tab


def kernel(x, c, w_ada, b_ada, g_attn_pre, g_attn_post, w_in, b_gate, na_rpb, q_norm_g, kv_norm_g,
           w_uq, w_ukv, w_na_up, w_mla_up, w_out, g_ffn_pre, g_ffn_post, w_router, b_router,
           w1, b1, w2, b2):
    B, S, D = x.shape
    T = B * S
    depth = w_ada.shape[0]
    E = w_router.shape[-1]
    n_rows = S // GRID_W
    assert S % (NA_GROUP_ROWS * GRID_W) == 0 and n_rows >= NA_KEY_ROWS
    assert E <= LANES and T % 2048 == 0

    bf = jnp.bfloat16
    cq_tab, sq_tab, csk_tab = _rope_tables(S)
    pm_np = np.zeros((LANES, LANES), np.float32)
    for cidx in range(MLA_ROPE):
        pm_np[cidx, MLA_NOPE + cidx] = 1.0
        pm_np[MLA_ROPE + cidx, MLA_NOPE + cidx] = 1.0
    pm = jnp.asarray(pm_np, bf)
    qscale = float((MLA_NOPE + MLA_ROPE) ** -0.5 * math.log2(math.e))

    col_na = 2 * D
    col_cq = col_na + 3 * NA_WIDTH
    col_ckv = col_cq + Q_LORA_PAD
    col_kr = col_ckv + MLA_KV_LORA
    assert col_cq % Q_LORA_PAD == 0 and col_ckv % MLA_KV_LORA == 0

    x2 = x.reshape(T, D)
    for l in range(depth):
        mod = _ada_mod(c, w_ada[l], b_ada[l])
        mod3 = mod.reshape(B, 1, 6 * D)

        o_na, o_cq, o_ckv, o_kr, o_g = np.cumsum([0, 3 * NA_WIDTH, MLA_Q_LORA, MLA_KV_LORA, MLA_ROPE]).tolist()
        wi = w_in[l]
        w_kr = wi[:, o_kr:o_g]
        w_ext = jnp.concatenate([
            wi[:, o_g:o_g + 2 * D],
            wi[:, o_na:o_cq],
            wi[:, o_cq:o_ckv], jnp.zeros((D, Q_LORA_PAD - MLA_Q_LORA), wi.dtype),
            wi[:, o_ckv:o_kr],
            w_kr, _rope_swap_cols(w_kr), jnp.zeros((D, LANES - 2 * MLA_ROPE), wi.dtype),
        ], axis=1).astype(bf)
        proj = _in_proj(x2, mod3, g_attn_pre[l], w_ext, b_gate[l], S)

        bias_tab = _na_bias_tables(na_rpb[l], n_rows)
        o_na_tok = _na_attention(proj, bias_tab, B, S, col_na, col_na + NA_WIDTH, col_na + 2 * NA_WIDTH)

        wq = w_uq[l]
        zq = jnp.zeros((MLA_Q_LORA, MLA_HEADS, LANES - MLA_NOPE - MLA_ROPE), wq.dtype)
        wqm = jnp.concatenate([wq, zq], axis=-1).reshape(MLA_Q_LORA, MLA_HEADS * LANES)
        wqs = jnp.concatenate([jnp.zeros((MLA_Q_LORA, MLA_HEADS, MLA_NOPE), wq.dtype),
                               _rope_swap_cols(wq[..., MLA_NOPE:]), zq], axis=-1
                              ).reshape(MLA_Q_LORA, MLA_HEADS * LANES)
        rpad = ((0, Q_LORA_PAD - MLA_Q_LORA), (0, 0))
        wqm = jnp.pad(wqm, rpad).astype(bf)
        wqs = jnp.pad(wqs, rpad).astype(bf)
        wkv = w_ukv[l]
        wk = jnp.concatenate([wkv[..., :MLA_NOPE], jnp.zeros((MLA_KV_LORA, MLA_HEADS, LANES - MLA_NOPE), wkv.dtype)],
                             axis=-1).reshape(MLA_KV_LORA, MLA_HEADS * LANES).astype(bf)
        wv = wkv[..., MLA_NOPE:].reshape(MLA_KV_LORA, MLA_WIDTH).astype(bf)
        gq = jnp.pad(q_norm_g[l], (0, Q_LORA_PAD - MLA_Q_LORA)).reshape(1, Q_LORA_PAD)
        gkv = kv_norm_g[l].reshape(1, MLA_KV_LORA)
        q_cat, k_cat, v_mla = _mla_prep(proj, col_cq, col_ckv, col_kr, gq, gkv, wqm, wqs, wk, wv, pm,
                                        cq_tab, sq_tab, csk_tab, S, qscale)
        o_mla_tok = _mla_attention(q_cat, k_cat, v_mla, B, S)

        w_r = jnp.pad(w_router[l], ((0, 0), (0, LANES - E))).astype(bf)
        b_r = jnp.concatenate([b_router[l], jnp.full((LANES - E,), NEG_BIG, jnp.float32)]).reshape(1, LANES)
        x1, h2, logits = _out_proj(o_na_tok, o_mla_tok, proj, x2, mod3, g_attn_post[l], g_ffn_pre[l],
                                   w_na_up[l].astype(bf), w_mla_up[l].astype(bf), w_out[l].astype(bf),
                                   w_r, b_r, S)

        route, counts = _route(logits)
        e_idx = route[:, 0:TOP_K].astype(jnp.int32)
        rank = route[:, TOP_K:2 * TOP_K].astype(jnp.int32)
        cnt = counts[0, :E].astype(jnp.int32)
        padded = ((cnt + MOE_BLOCK - 1) // MOE_BLOCK) * MOE_BLOCK
        pend = jnp.cumsum(padded)
        pstart = pend - padded
        n_items = T * TOP_K
        nblk = -(-n_items // MOE_BLOCK) + E
        cap = nblk * MOE_BLOCK
        dest = (rank + jnp.sum(jnp.where(e_idx[..., None] == jnp.arange(E, dtype=jnp.int32), pstart, 0), axis=-1)
                ).reshape(n_items)
        blk_off = jnp.arange(nblk, dtype=jnp.int32) * MOE_BLOCK
        blk_e = jnp.minimum(jnp.sum((pend[None, :] <= blk_off[:, None]).astype(jnp.int32), axis=-1), E - 1)
        nvalid = (pend[-1:] // MOE_BLOCK).astype(jnp.int32)

        last_blk = jnp.where(padded > 0, pend // MOE_BLOCK - 1, -1)
        tail_blk = nvalid[0] + jnp.arange(nblk - n_items // MOE_BLOCK, dtype=jnp.int32)
        zero_blk = jnp.concatenate([last_blk, jnp.where(tail_blk < nblk, tail_blk, -1)]).astype(jnp.int32)
        xs, slot_row = _dispatch(h2, dest, zero_blk, cap)
        slot_row = slot_row.reshape(nblk, 1, MOE_BLOCK)
        F2 = w1.shape[-1]
        b1p = b1[l].reshape(E, F2 // (2 * LANES), LANES, 2).transpose(0, 1, 3, 2).reshape(E, 1, F2)
        y_k = _moe_ffn(xs, slot_row, n_items, blk_e, nvalid, w1[l], b1p, w2[l], b2[l].reshape(E, 1, D))
        x2 = _combine(y_k, route, x1, mod3, g_ffn_post[l], S)
    return x2.reshape(B, S, D)
```

```python
import functools
import math

import numpy as np
import jax
import jax.numpy as jnp
from jax import lax
from jax.experimental import pallas as pl
from jax.experimental.pallas import tpu as pltpu

GRID_W = 64
NA_HEADS = 8
NA_HEAD_DIM = 64
NA_WIN_ROWS = 8
NA_WIN_COLS = 16
NA_WIDTH = NA_HEADS * NA_HEAD_DIM
MLA_HEADS = 8
MLA_Q_LORA = 384
MLA_KV_LORA = 256
MLA_NOPE = 64
MLA_ROPE = 32
MLA_V = 64
MLA_WIDTH = MLA_HEADS * MLA_V
MLA_KEY_CHUNK = 256
MLA_VROWS = MLA_V + 16
MLA_STEPS_PER_TRIP = 8
ROPE_THETA = 100.0
TOP_K = 4
SWIGLU_ALPHA = 1.702
SWIGLU_LIMIT = 7.0
MOE_BLOCK = 256
RMS_EPS = 1e-6
NEG_BIG = -1e30

LANES = 128
VMEM_LIMIT_BYTES = 56 * 1024 * 1024

Q_LORA_PAD = 512
COL_GATES = 0
NA_GROUP_ROWS = 8
NA_KEY_ROWS = 16


def _f32(x):
    return x.astype(jnp.float32)


def _bf16(x):
    return x.astype(jnp.bfloat16)


def _dot(a, b):
    return jnp.dot(a, b, preferred_element_type=jnp.float32)


def _dot_nt(a, b):
    return lax.dot_general(a, b, (((1,), (1,)), ((), ())), preferred_element_type=jnp.float32)


def _rms(x, g, n=None):
    n = x.shape[-1] if n is None else n
    ms = jnp.sum(x * x, axis=-1, keepdims=True) * (1.0 / n)
    return x * lax.rsqrt(ms + RMS_EPS) * g


SUBLANES = 8


def _store_row_tiles(ref, x, base=0):
    n = x.shape[0]
    for c in range(x.shape[1] // LANES):
        ref[pl.ds(base + c, n, stride=SUBLANES), :] = x[:, c * LANES:(c + 1) * LANES]


def _load_row_tiles(ref, n, base=0):
    chunks = [ref[pl.ds(base + c, n, stride=SUBLANES), :] for c in range(SUBLANES)]
    return jnp.concatenate(chunks, axis=-1)


def _cparams(sem):
    return pltpu.CompilerParams(dimension_semantics=sem, vmem_limit_bytes=VMEM_LIMIT_BYTES)


def _ada_kernel(c_ref, w_ref, b_ref, o_ref):
    c = c_ref[...]
    sc = c * jax.nn.sigmoid(c)
    o_ref[...] = _dot(_bf16(sc), _bf16(w_ref[...])) + b_ref[...]


def _ada_mod(c, w_ada, b_ada):
    B, D = c.shape
    n_out = w_ada.shape[1]
    return pl.pallas_call(
        _ada_kernel,
        out_shape=jax.ShapeDtypeStruct((B, n_out), jnp.float32),
        grid=(n_out // D,),
        in_specs=[
            pl.BlockSpec((B, D), lambda j: (0, 0)),
            pl.BlockSpec((D, D), lambda j: (0, j)),
            pl.BlockSpec((1, D), lambda j: (0, j)),
        ],
        out_specs=pl.BlockSpec((B, D), lambda j: (0, j)),
        compiler_params=_cparams(("arbitrary",)),
        name="ada_mod",
    )(c, w_ada, b_ada.reshape(1, n_out))


def _in_proj_kernel(x_ref, mod_ref, g_ref, w_ref, bg_ref, o_ref, *, d_model, n_gate, chunk):
    x = x_ref[...]
    shift = mod_ref[0, :, 0:d_model]
    scale = mod_ref[0, :, d_model:2 * d_model]
    h = _rms(x, g_ref[...]) * (1.0 + scale) + shift
    hb = _bf16(h)
    n_total = w_ref.shape[1]
    for c0 in range(0, n_total, chunk):
        c1 = min(c0 + chunk, n_total)
        acc = _dot(hb, w_ref[:, c0:c1])
        if c0 < n_gate:
            acc = jax.nn.sigmoid(acc + bg_ref[:, c0:c1])
        o_ref[:, c0:c1] = _bf16(acc)


def _in_proj(x2, mod3, g_pre, w_ext, b_gate, seq, tm=512):
    T, D = x2.shape
    n_total = w_ext.shape[1]
    n_gate = b_gate.shape[-1]
    kern = functools.partial(_in_proj_kernel, d_model=D, n_gate=n_gate, chunk=512)
    return pl.pallas_call(
        kern,
        out_shape=jax.ShapeDtypeStruct((T, n_total), jnp.bfloat16),
        grid=(T // tm,),
        in_specs=[
            pl.BlockSpec((tm, D), lambda i: (i, 0)),
            pl.BlockSpec((1, 1, mod3.shape[-1]), lambda i: ((i * tm) // seq, 0, 0)),
            pl.BlockSpec((1, D), lambda i: (0, 0)),
            pl.BlockSpec((D, n_total), lambda i: (0, 0)),
            pl.BlockSpec((1, n_gate), lambda i: (0, 0)),
        ],
        out_specs=pl.BlockSpec((tm, n_total), lambda i: (i, 0)),
        compiler_params=_cparams(("arbitrary",)),
        name="in_proj",
    )(x2, mod3, g_pre.reshape(1, D), w_ext, b_gate.reshape(1, n_gate))


def _na_kernel(q_ref, k_ref, v_ref, bias_ref, o_ref, *, n_rows):
    g = pl.program_id(2)
    kb = jnp.clip(g * NA_GROUP_ROWS - NA_WIN_ROWS // 2, 0, n_rows - NA_KEY_ROWS) * GRID_W
    kb = pl.multiple_of(kb, 256)
    nk = NA_KEY_ROWS * GRID_W
    q = q_ref[...]
    k = k_ref[pl.ds(kb, nk), :]
    v = v_ref[pl.ds(kb, nk), :]
    lane = lax.broadcasted_iota(jnp.int32, (1, LANES), 1)
    scale = NA_HEAD_DIM ** -0.5
    scores = []
    for h in range(2):
        in_head = (lane >= h * NA_HEAD_DIM) & (lane < (h + 1) * NA_HEAD_DIM)
        qh = jnp.where(in_head, q * scale, 0).astype(jnp.bfloat16)
        scores.append(_dot_nt(qh, k))
    outs = []
    for h in range(2):
        s = scores[h] + bias_ref[0, h]
        m = jnp.max(s, axis=-1, keepdims=True)
        p = jnp.exp(s - m)
        l = jnp.sum(p, axis=-1, keepdims=True)
        pv = _dot(_bf16(p), v)
        outs.append(pv / l)
    o_ref[...] = _bf16(jnp.where(lane < NA_HEAD_DIM, outs[0], outs[1]))


def _na_attention(proj, bias_tab, batch, seq, col_q, col_k, col_v):
    T = proj.shape[0]
    n_rows = seq // GRID_W
    n_groups = n_rows // NA_GROUP_ROWS
    tq = NA_GROUP_ROWS * GRID_W
    n_pairs = NA_HEADS // 2

    def cls(g):
        return jnp.where(g == 0, 0, jnp.where(g == n_groups - 1, 2, 1))

    kern = functools.partial(_na_kernel, n_rows=n_rows)
    return pl.pallas_call(
        kern,
        out_shape=jax.ShapeDtypeStruct((T, NA_WIDTH), jnp.bfloat16),
        grid=(n_pairs, batch, n_groups),
        in_specs=[
            pl.BlockSpec((tq, LANES), lambda hp, b, g: (b * n_groups + g, col_q // LANES + hp)),
            pl.BlockSpec((seq, LANES), lambda hp, b, g: (b, col_k // LANES + hp)),
            pl.BlockSpec((seq, LANES), lambda hp, b, g: (b, col_v // LANES + hp)),
            pl.BlockSpec((1, 2, tq, NA_KEY_ROWS * GRID_W), lambda hp, b, g: (cls(g), hp, 0, 0)),
        ],
        out_specs=pl.BlockSpec((tq, LANES), lambda hp, b, g: (b * n_groups + g, hp)),
        compiler_params=_cparams(("arbitrary", "arbitrary", "arbitrary")),
        name="na_attn",
    )(proj, proj, proj, bias_tab)


def _na_bias_tables(rpb, n_rows):
    H = rpb.shape[0]
    kw = NA_WIN_COLS
    W = GRID_W
    pad = W - kw
    rp = jnp.pad(rpb.astype(jnp.float32), ((0, 0), (0, 0), (pad, pad)), constant_values=NEG_BIG)
    toep = jnp.stack([rp[:, :, W - 1 - qc: 2 * W - 1 - qc] for qc in range(W)], axis=2)
    cidx = np.arange(W)
    col_start = np.clip(cidx - kw // 2, 0, W - kw)
    col_in = (cidx[None, :] >= col_start[:, None]) & (cidx[None, :] < col_start[:, None] + kw)
    toep = jnp.where(jnp.asarray(col_in)[None, None], toep, NEG_BIG)
    neg_blk = jnp.full((H, W, W), NEG_BIG, jnp.float32)
    tabs = []
    for c in range(3):
        rows_i = []
        for i in range(NA_GROUP_ROWS):
            if c == 0:
                j0, off = max(i - NA_WIN_ROWS // 2, 0), NA_WIN_ROWS - 1
            elif c == 1:
                j0, off = i, NA_WIN_ROWS - 1 - NA_WIN_ROWS // 2
            else:
                j0, off = min(i + NA_WIN_ROWS // 2, NA_KEY_ROWS - NA_WIN_ROWS), -1
            blks = []
            for j in range(NA_KEY_ROWS):
                if j0 <= j < j0 + NA_WIN_ROWS:
                    blks.append(toep[:, j - i + off])
                else:
                    blks.append(neg_blk)
            rows_i.append(jnp.concatenate(blks, axis=-1))
        tabs.append(jnp.concatenate(rows_i, axis=1))
    return jnp.stack(tabs, axis=0)


def _mla_prep_kernel(cq_ref, ckv_ref, kr_ref, gq_ref, gkv_ref, wqm_ref, wqs_ref, wk_ref, wv_ref,
                     pm_ref, cq_tab_ref, sq_tab_ref, csk_tab_ref, q_out, k_out, v_out, *, qscale):
    cq = _f32(cq_ref[...])
    cqn = _bf16(_rms(cq, gq_ref[...], n=MLA_Q_LORA))
    qm = _dot(cqn, wqm_ref[...])
    qs = _dot(cqn, wqs_ref[...])
    ctab = cq_tab_ref[...]
    stab = sq_tab_ref[...]
    for h in range(MLA_HEADS):
        sl = slice(h * LANES, (h + 1) * LANES)
        qh = (qm[:, sl] * ctab + qs[:, sl] * stab) * qscale
        q_out[0, h] = _bf16(qh.T)

    ckv = _f32(ckv_ref[...])
    ckvn = _bf16(_rms(ckv, gkv_ref[...]))
    kk = _dot(ckvn, wk_ref[...])
    vv = _dot(ckvn, wv_ref[...])
    n_chunk = v_out.shape[2]
    ones = jnp.ones((MLA_VROWS - MLA_V, MLA_KEY_CHUNK), jnp.bfloat16)
    for hp in range(MLA_HEADS // 2):
        vt = _bf16(vv[:, hp * LANES:(hp + 1) * LANES].T)
        for cc in range(n_chunk):
            ks = slice(cc * MLA_KEY_CHUNK, (cc + 1) * MLA_KEY_CHUNK)
            for h in range(2):
                r0 = h * MLA_VROWS
                v_out[0, hp, cc, r0:r0 + MLA_V, :] = vt[h * MLA_V:(h + 1) * MLA_V, ks]
                v_out[0, hp, cc, r0 + MLA_V:r0 + MLA_VROWS, :] = ones
    krr = _f32(kr_ref[...]) * csk_tab_ref[...]
    kplace = _dot(_bf16(krr), pm_ref[...])
    for h in range(MLA_HEADS):
        sl = slice(h * LANES, (h + 1) * LANES)
        k_out[:, sl] = _bf16(kk[:, sl] + kplace)


def _mla_prep(proj, col_cq, col_ckv, col_kr, gq, gkv, wqm, wqs, wk, wv, pm, cq_tab, sq_tab, csk_tab,
              seq, qscale, tm=512):
    T = proj.shape[0]
    batch = T // seq
    n_s = seq // tm
    n_pairs = MLA_HEADS // 2
    cpt = tm // MLA_KEY_CHUNK
    full = lambda a: pl.BlockSpec(a.shape, lambda i: (0,) * a.ndim)
    tab = pl.BlockSpec((tm, LANES), lambda i: (i % n_s, 0))
    kern = functools.partial(_mla_prep_kernel, qscale=qscale)
    return pl.pallas_call(
        kern,
        out_shape=(
            jax.ShapeDtypeStruct((batch, MLA_HEADS, LANES, seq), jnp.bfloat16),
            jax.ShapeDtypeStruct((T, MLA_HEADS * LANES), jnp.bfloat16),
            jax.ShapeDtypeStruct((batch, n_pairs, seq // MLA_KEY_CHUNK, 2 * MLA_VROWS, MLA_KEY_CHUNK),
                                 jnp.bfloat16),
        ),
        grid=(T // tm,),
        in_specs=[
            pl.BlockSpec((tm, Q_LORA_PAD), lambda i: (i, col_cq // Q_LORA_PAD)),
            pl.BlockSpec((tm, MLA_KV_LORA), lambda i: (i, col_ckv // MLA_KV_LORA)),
            pl.BlockSpec((tm, LANES), lambda i: (i, col_kr // LANES)),
            full(gq), full(gkv), full(wqm), full(wqs), full(wk), full(wv), full(pm),
            tab, tab, tab,
        ],
        out_specs=(
            pl.BlockSpec((1, MLA_HEADS, LANES, tm), lambda i: (i // n_s, 0, 0, i % n_s)),
            pl.BlockSpec((tm, MLA_HEADS * LANES), lambda i: (i, 0)),
            pl.BlockSpec((1, n_pairs, cpt, 2 * MLA_VROWS, MLA_KEY_CHUNK),
                         lambda i: (i // n_s, 0, i % n_s, 0, 0)),
        ),
        compiler_params=_cparams(("arbitrary",)),
        name="mla_prep",
    )(proj, proj, proj, gq, gkv, wqm, wqs, wk, wv, pm, cq_tab, sq_tab, csk_tab)


def _mla_attn_kernel(qt_ref, k_ref, vt_ref, o_ref, m_scr, acc_scr, st_scr):
    n_chunks = vt_ref.shape[2]
    tk = MLA_KEY_CHUNK
    m_scr[...] = jnp.full(m_scr.shape, -jnp.inf, jnp.float32)
    acc_scr[...] = jnp.zeros(acc_scr.shape, jnp.float32)

    def scores(c, slot):
        k0 = pl.multiple_of(c * tk, tk)
        for h in range(2):
            kc = k_ref[pl.ds(k0, tk), h * LANES:(h + 1) * LANES]
            st_scr[slot, h] = _dot(kc, qt_ref[0, h])

    def step(c, slot):
        scores(jnp.minimum(c + 1, n_chunks - 1), 1 - slot)
        vt = vt_ref[0, 0, c]
        pts, alphas = [], []
        for h in range(2):
            st = st_scr[slot, h]
            m_old = m_scr[h]
            m_new = jnp.maximum(m_old, jnp.max(st, axis=0, keepdims=True))
            alpha = jnp.exp2(m_old - m_new)
            pt = jnp.exp2(st - m_new)
            m_scr[h] = m_new
            pts.append(_bf16(pt))
            alphas.append(alpha)
        for h in range(2):
            acc_scr[h] = alphas[h] * acc_scr[h] + _dot(vt[h * MLA_VROWS:(h + 1) * MLA_VROWS, :], pts[h])

    def body(j, carry):
        for u in range(MLA_STEPS_PER_TRIP):
            step(MLA_STEPS_PER_TRIP * j + u, u % 2)
        return carry

    scores(0, 0)
    lax.fori_loop(0, n_chunks // MLA_STEPS_PER_TRIP, body, 0)
    outs = []
    for h in range(2):
        acc = acc_scr[h]
        outs.append(acc[0:MLA_V] / acc[MLA_V:MLA_V + 1])
    o_ref[...] = _bf16(jnp.concatenate(outs, axis=0).T)


def _mla_attention(q_t, k_cat, v_t, batch, seq, tq=512):
    T = k_cat.shape[0]
    n_q = seq // tq
    n_pairs = MLA_HEADS // 2
    n_chunks = seq // MLA_KEY_CHUNK
    assert n_chunks % MLA_STEPS_PER_TRIP == 0
    return pl.pallas_call(
        _mla_attn_kernel,
        out_shape=jax.ShapeDtypeStruct((T, MLA_WIDTH), jnp.bfloat16),
        grid=(batch, n_pairs, n_q),
        in_specs=[
            pl.BlockSpec((1, 2, LANES, tq), lambda b, hp, i: (b, hp, 0, i)),
            pl.BlockSpec((seq, 2 * LANES), lambda b, hp, i: (b, hp)),
            pl.BlockSpec((1, 1, n_chunks, 2 * MLA_VROWS, MLA_KEY_CHUNK), lambda b, hp, i: (b, hp, 0, 0, 0)),
        ],
        out_specs=pl.BlockSpec((tq, LANES), lambda b, hp, i: (b * n_q + i, hp)),
        scratch_shapes=[
            pltpu.VMEM((2, 1, tq), jnp.float32),
            pltpu.VMEM((2, MLA_VROWS, tq), jnp.float32),
            pltpu.VMEM((2, 2, MLA_KEY_CHUNK, tq), jnp.float32),
        ],
        compiler_params=_cparams(("arbitrary", "arbitrary", "arbitrary")),
        name="mla_attn",
    )(q_t, k_cat, v_t)


def _out_proj_kernel(ona_ref, omla_ref, gate_ref, x_ref, mod_ref, gpost_ref, gpre_ref, wna_ref,
                     wmla_ref, wout_ref, wr_ref, br_ref, x1_ref, h2_ref, lg_ref, *, d_model):
    D = d_model
    gates = gate_ref[...]
    merged = (_f32(gates[:, 0:D]) * _dot(ona_ref[...], wna_ref[...])
              + _f32(gates[:, D:2 * D]) * _dot(omla_ref[...], wmla_ref[...]))
    y = _dot(_bf16(merged), wout_ref[...])
    gate_a = mod_ref[0, :, 2 * D:3 * D]
    shift_f = mod_ref[0, :, 3 * D:4 * D]
    scale_f = mod_ref[0, :, 4 * D:5 * D]
    x1 = x_ref[...] + gate_a * _rms(y, gpost_ref[...])
    x1_ref[...] = x1
    h2 = _rms(x1, gpre_ref[...]) * (1.0 + scale_f) + shift_f
    _store_row_tiles(h2_ref, h2)
    lg_ref[...] = _dot(_bf16(h2), wr_ref[...]) + br_ref[...]


def _out_proj(o_na, o_mla, proj, x2, mod3, g_post, g_pre, w_na, w_mla, w_out, w_r, b_r, seq, tm=512):
    T, D = x2.shape
    full = lambda a: pl.BlockSpec(a.shape, lambda i: (0,) * a.ndim)
    row = lambda w: pl.BlockSpec((tm, w), lambda i: (i, 0))
    kern = functools.partial(_out_proj_kernel, d_model=D)
    g_post = g_post.reshape(1, D)
    g_pre = g_pre.reshape(1, D)
    return pl.pallas_call(
        kern,
        out_shape=(
            jax.ShapeDtypeStruct((T, D), jnp.float32),
            jax.ShapeDtypeStruct((T * (D // LANES), LANES), jnp.float32),
            jax.ShapeDtypeStruct((T, LANES), jnp.float32),
        ),
        grid=(T // tm,),
        in_specs=[
            row(NA_WIDTH), row(MLA_WIDTH),
            pl.BlockSpec((tm, 2 * D), lambda i: (i, COL_GATES // (2 * D))),
            row(D),
            pl.BlockSpec((1, 1, mod3.shape[-1]), lambda i: ((i * tm) // seq, 0, 0)),
            full(g_post), full(g_pre), full(w_na), full(w_mla), full(w_out), full(w_r), full(b_r),
        ],
        out_specs=(row(D), pl.BlockSpec((tm * (D // LANES), LANES), lambda i: (i, 0)), row(LANES)),
        compiler_params=_cparams(("arbitrary",)),
        name="out_proj",
    )(o_na, o_mla, proj, x2, mod3, g_post, g_pre, w_na, w_mla, w_out, w_r, b_r)


def _route_kernel(lg_ref, r_ref, cnt_ref, carry_scr, *, sub):
    @pl.when(pl.program_id(0) == 0)
    def _():
        carry_scr[...] = jnp.zeros(carry_scr.shape, jnp.float32)

    tr = lg_ref.shape[0]
    lane = lax.broadcasted_iota(jnp.int32, (sub, LANES), 1).astype(jnp.float32)
    ri = lax.broadcasted_iota(jnp.int32, (sub, sub), 0)
    ci = lax.broadcasted_iota(jnp.int32, (sub, sub), 1)
    tri = jnp.where(ri >= ci, 1.0, 0.0).astype(jnp.bfloat16)
    for s0 in range(0, tr, sub):
        work = lg_ref[s0:s0 + sub, :]
        sels, vals, idxs = [], [], []
        for _k in range(TOP_K):
            mk = jnp.max(work, axis=-1, keepdims=True)
            ik = jnp.min(jnp.where(work == mk, lane, float(LANES)), axis=-1, keepdims=True)
            sk = lane == ik
            work = jnp.where(sk, -jnp.inf, work)
            sels.append(sk)
            vals.append(mk)
            idxs.append(ik)
        es = [jnp.exp(v - vals[0]) for v in vals]
        denom = es[0] + es[1] + es[2] + es[3]
        onehot = jnp.zeros((sub, LANES), jnp.float32)
        for sk in sels:
            onehot = jnp.where(sk, 1.0, onehot)
        prefix = _dot(tri, _bf16(onehot))
        carry = carry_scr[...]
        rank_mat = carry + prefix - 1.0
        res = jnp.zeros((sub, LANES), jnp.float32)
        for kk in range(TOP_K):
            rank_k = jnp.sum(jnp.where(sels[kk], rank_mat, 0.0), axis=-1, keepdims=True)
            res = jnp.where(lane == kk, idxs[kk], res)
            res = jnp.where(lane == TOP_K + kk, rank_k, res)
            res = jnp.where(lane == 2 * TOP_K + kk, es[kk] / denom, res)
        r_ref[s0:s0 + sub, :] = res
        carry_scr[...] = carry + jnp.sum(onehot, axis=0, keepdims=True)
    cnt_ref[...] = carry_scr[...]


def _route(logits, tr=2048, sub=256):
    T = logits.shape[0]
    kern = functools.partial(_route_kernel, sub=sub)
    return pl.pallas_call(
        kern,
        out_shape=(
            jax.ShapeDtypeStruct((T, LANES), jnp.float32),
            jax.ShapeDtypeStruct((1, LANES), jnp.float32),
        ),
        grid=(T // tr,),
        in_specs=[pl.BlockSpec((tr, LANES), lambda i: (i, 0))],
        out_specs=(
            pl.BlockSpec((tr, LANES), lambda i: (i, 0)),
            pl.BlockSpec((1, LANES), lambda i: (0, 0)),
        ),
        scratch_shapes=[pltpu.VMEM((1, LANES), jnp.float32)],
        compiler_params=_cparams(("arbitrary",)),
        name="route",
    )(logits)


def _dispatch_kernel(zero_blk_ref, dest_ref, h_ref, row0_ref, xs_ref, row_ref, zbuf, sem, fill_sem,
                     *, n_tokens):
    i = pl.program_id(0)
    ts = h_ref.shape[0] // SUBLANES
    blk_rows = MOE_BLOCK * SUBLANES

    @pl.when(i == 0)
    def _():
        fill_row = pltpu.make_async_copy(row0_ref, row_ref, fill_sem.at[0])
        fill_row.start()
        zbuf[...] = jnp.zeros(zbuf.shape, zbuf.dtype)
        n_zero = zero_blk_ref.shape[0]

        def zero_copy(z):
            r0 = pl.multiple_of(zero_blk_ref[z] * blk_rows, blk_rows)
            return pltpu.make_async_copy(zbuf, xs_ref.at[pl.ds(r0, blk_rows), :], fill_sem.at[1])

        for z in range(n_zero):
            pl.when(zero_blk_ref[z] >= 0)(lambda z=z: zero_copy(z).start())
        for z in range(n_zero):
            pl.when(zero_blk_ref[z] >= 0)(lambda z=z: zero_copy(z).wait())
        fill_row.wait()

    def row_copy(j, d):
        dst = pl.ds(pl.multiple_of(d * SUBLANES, SUBLANES), SUBLANES)
        return pltpu.make_async_copy(h_ref.at[pl.ds(j * SUBLANES, SUBLANES), :], xs_ref.at[dst, :], sem)

    t0 = i * ts
    for j in range(ts):
        for kk in range(TOP_K):
            d = dest_ref[j * TOP_K + kk]
            row_copy(j, d).start(priority=kk % 2)
            row_ref[d] = (kk * n_tokens + j) + t0
    for _j in range(ts * TOP_K):
        row_copy(0, 0).wait()


def _dispatch(h2_tiles, dest, zero_blk, cap, ts=256):
    T = h2_tiles.shape[0] // SUBLANES
    kern = functools.partial(_dispatch_kernel, n_tokens=T)
    grid_spec = pltpu.PrefetchScalarGridSpec(
        num_scalar_prefetch=1,
        grid=(T // ts,),
        in_specs=[
            pl.BlockSpec((ts * TOP_K,), lambda i, zb: (i,), memory_space=pltpu.SMEM),
            pl.BlockSpec((ts * SUBLANES, LANES), lambda i, zb: (i, 0)),
            pl.BlockSpec(memory_space=pl.ANY),
        ],
        out_specs=(pl.BlockSpec(memory_space=pl.ANY), pl.BlockSpec(memory_space=pltpu.SMEM)),
        scratch_shapes=[
            pltpu.VMEM((MOE_BLOCK * SUBLANES, LANES), h2_tiles.dtype),
            pltpu.SemaphoreType.DMA(()),
            pltpu.SemaphoreType.DMA((2,)),
        ],
    )
    return pl.pallas_call(
        kern,
        out_shape=(jax.ShapeDtypeStruct((cap * SUBLANES, LANES), h2_tiles.dtype),
                   jax.ShapeDtypeStruct((cap,), jnp.int32)),
        grid_spec=grid_spec,
        compiler_params=_cparams(("arbitrary",)),
        name="dispatch",
    )(zero_blk, dest, h2_tiles, jnp.full((cap,), -1, jnp.int32))


def _moe_kernel(blk_e_ref, nvalid_ref, row_prev_ref, row_ref, x_ref, w1_ref, b1_ref, w2_ref, b2_ref,
                out_ref, w1p_scr, w2b_scr, ybuf0, ybuf1, ssem, *, n_items):
    i = pl.program_id(0)
    nvalid = nvalid_ref[0]
    d_ff = w2_ref.shape[1]
    n_groups = (2 * d_ff) // (2 * LANES)

    def tile(r):
        return pl.ds(pl.multiple_of(r * SUBLANES, SUBLANES), SUBLANES)

    ybufs = (ybuf0, ybuf1)

    def scatter_copy(j, r, slot):
        return pltpu.make_async_copy(ybufs[slot].at[pl.ds(j * SUBLANES, SUBLANES), :], out_ref.at[tile(r), :],
                                     ssem.at[slot])

    def wait_scatters(slot):
        for _j in range(MOE_BLOCK):
            scatter_copy(0, 0, slot).wait()

    def scatter_rows(rows_ref, slot):
        spare0 = n_items + slot * MOE_BLOCK
        for j in range(MOE_BLOCK):
            r = rows_ref[0, 0, j]
            scatter_copy(j, jnp.where(r >= 0, r, spare0 + j), slot).start(priority=j % 2)

    @pl.when(i == 0)
    def _():
        ybuf0[...] = jnp.zeros(ybuf0.shape, ybuf0.dtype)
        ybuf1[...] = jnp.zeros(ybuf1.shape, ybuf1.dtype)
        for j in range(MOE_BLOCK):
            scatter_copy(j, n_items + j, 0).start(priority=j % 2)

    @pl.when(i < nvalid)
    def _():
        e = blk_e_ref[i]
        e_prev = blk_e_ref[jnp.maximum(i - 1, 0)]

        @pl.when((i == 0) | (e != e_prev))
        def _():
            r = lax.broadcasted_iota(jnp.int32, (2 * LANES, 2 * LANES), 0)
            c = lax.broadcasted_iota(jnp.int32, (2 * LANES, 2 * LANES), 1)
            src = jnp.where(c < LANES, 2 * c, 2 * (c - LANES) + 1)
            perm = jnp.where(r == src, 1.0, 0.0).astype(jnp.bfloat16)
            for gI in range(n_groups):
                sl = slice(gI * 2 * LANES, (gI + 1) * 2 * LANES)
                w1p_scr[:, sl] = _bf16(_dot(_bf16(w1_ref[0, :, sl]), perm))
            w2b_scr[...] = _bf16(w2_ref[0])

        def block_step(slot):
            wait_scatters(slot)
            scatter_rows(row_prev_ref, 1 - slot)

            xb = _bf16(_load_row_tiles(x_ref, MOE_BLOCK))
            hcat = _dot(xb, w1p_scr[...]) + b1_ref[0]
            acts = []
            for gI in range(n_groups):
                glu = jnp.minimum(hcat[:, gI * 2 * LANES: gI * 2 * LANES + LANES], SWIGLU_LIMIT)
                lin = jnp.clip(hcat[:, gI * 2 * LANES + LANES: (gI + 1) * 2 * LANES],
                               -SWIGLU_LIMIT, SWIGLU_LIMIT)
                acts.append(_bf16(glu * jax.nn.sigmoid(SWIGLU_ALPHA * glu) * (lin + 1.0)))
            act = jnp.concatenate(acts, axis=-1)
            _store_row_tiles(ybufs[slot], _dot(act, w2b_scr[...]) + b2_ref[0])

            @pl.when(i == nvalid - 1)
            def _():
                scatter_rows(row_ref, slot)
                wait_scatters(1 - slot)
                wait_scatters(slot)

        parity = lax.rem(i, 2)
        for slot in range(2):
            pl.when(parity == slot)(functools.partial(block_step, slot))


def _moe_ffn(xs, slot_row, n_items, blk_e, nvalid, w1, b1p, w2, b2):
    D = w1.shape[1]
    assert D == SUBLANES * LANES
    E, _, F2 = w1.shape
    F = w2.shape[1]
    nblk = slot_row.shape[0]

    def cur(i, be, nv):
        return (jnp.minimum(i, nv[0] - 1), 0, 0)

    def prv(i, be, nv):
        return (jnp.where(i == 0, nblk, jnp.maximum(jnp.minimum(i, nv[0] - 1) - 1, 0)), 0, 0)

    slot_row = jnp.concatenate([slot_row, jnp.full((1, 1, MOE_BLOCK), -1, slot_row.dtype)], axis=0)
    slot_spec = lambda f: pl.BlockSpec((1, 1, MOE_BLOCK), f, memory_space=pltpu.SMEM)
    grid_spec = pltpu.PrefetchScalarGridSpec(
        num_scalar_prefetch=2,
        grid=(nblk,),
        in_specs=[
            slot_spec(prv), slot_spec(cur),
            pl.BlockSpec((MOE_BLOCK * SUBLANES, LANES), lambda i, be, nv: (jnp.minimum(i, nv[0] - 1), 0)),
            pl.BlockSpec((1, D, F2), lambda i, be, nv: (be[i], 0, 0)),
            pl.BlockSpec((1, 1, F2), lambda i, be, nv: (be[i], 0, 0)),
            pl.BlockSpec((1, F, D), lambda i, be, nv: (be[i], 0, 0)),
            pl.BlockSpec((1, 1, D), lambda i, be, nv: (be[i], 0, 0)),
        ],
        out_specs=pl.BlockSpec(memory_space=pl.ANY),
        scratch_shapes=[
            pltpu.VMEM((D, F2), jnp.bfloat16),
            pltpu.VMEM((F, D), jnp.bfloat16),
            pltpu.VMEM((MOE_BLOCK * SUBLANES, LANES), jnp.float32),
            pltpu.VMEM((MOE_BLOCK * SUBLANES, LANES), jnp.float32),
            pltpu.SemaphoreType.DMA((2,)),
        ],
    )
    kern = functools.partial(_moe_kernel, n_items=n_items)
    return pl.pallas_call(
        kern,
        out_shape=jax.ShapeDtypeStruct(((n_items + 2 * MOE_BLOCK) * SUBLANES, LANES), jnp.float32),
        grid_spec=grid_spec,
        compiler_params=_cparams(("arbitrary",)),
        name="moe_ffn",
    )(blk_e, nvalid, slot_row, slot_row, xs, w1, b1p, w2, b2)


def _combine_kernel(y0_ref, y1_ref, y2_ref, y3_ref, r_ref, x1_ref, mod_ref, g_ref, o_ref, *, d_model):
    D = d_model
    tc = x1_ref.shape[0]
    r = r_ref[...]
    moe = None
    for kk, y_ref in enumerate((y0_ref, y1_ref, y2_ref, y3_ref)):
        term = r[:, 2 * TOP_K + kk: 2 * TOP_K + kk + 1] * _load_row_tiles(y_ref, tc)
        moe = term if moe is None else moe + term
    gate_f = mod_ref[0, :, 5 * D:6 * D]
    o_ref[...] = x1_ref[...] + gate_f * _rms(moe, g_ref[...])


def _combine(y_k, route, x1, mod3, g_post, seq, tc=512):
    T, D = x1.shape
    n_t = T // tc
    kern = functools.partial(_combine_kernel, d_model=D)
    y_spec = lambda kk: pl.BlockSpec((tc * SUBLANES, LANES), lambda i: (kk * n_t + i, 0))
    return pl.pallas_call(
        kern,
        out_shape=jax.ShapeDtypeStruct((T, D), jnp.float32),
        grid=(n_t,),
        in_specs=[
            y_spec(0), y_spec(1), y_spec(2), y_spec(3),
            pl.BlockSpec((tc, LANES), lambda i: (i, 0)),
            pl.BlockSpec((tc, D), lambda i: (i, 0)),
            pl.BlockSpec((1, 1, mod3.shape[-1]), lambda i: ((i * tc) // seq, 0, 0)),
            pl.BlockSpec((1, D), lambda i: (0, 0)),
        ],
        out_specs=pl.BlockSpec((tc, D), lambda i: (i, 0)),
        compiler_params=_cparams(("arbitrary",)),
        name="combine",
    )(y_k, y_k, y_k, y_k, route, x1, mod3, g_post.reshape(1, D))


def _rope_swap_cols(w):
    nf = MLA_ROPE // 4
    return jnp.concatenate([-w[..., nf:2 * nf], w[..., 0:nf], -w[..., 3 * nf:4 * nf], w[..., 2 * nf:3 * nf]],
                           axis=-1)


def _rope_tables(seq):
    n_rows = seq // GRID_W
    nf = MLA_ROPE // 4
    inv = ROPE_THETA ** (-jnp.arange(nf, dtype=jnp.float32) / nf)
    ar = jnp.arange(n_rows, dtype=jnp.float32)[:, None] * inv
    ac = jnp.arange(GRID_W, dtype=jnp.float32)[:, None] * inv
    per_row = lambda a: jnp.broadcast_to(a[:, None, :], (n_rows, GRID_W, nf)).reshape(seq, nf)
    per_col = lambda a: jnp.broadcast_to(a[None, :, :], (n_rows, GRID_W, nf)).reshape(seq, nf)
    cr, sr, cc, sc = per_row(jnp.cos(ar)), per_row(jnp.sin(ar)), per_col(jnp.cos(ac)), per_col(jnp.sin(ac))
    c32 = jnp.concatenate([cr, cr, cc, cc], axis=-1)
    s32 = jnp.concatenate([sr, sr, sc, sc], axis=-1)
    one = jnp.ones((seq, MLA_NOPE), jnp.float32)
    z64 = jnp.zeros((seq, MLA_NOPE), jnp.float32)
    z32 = jnp.zeros((seq, LANES - MLA_NOPE - MLA_ROPE), jnp.float32)
    cq_tab = jnp.concatenate([one, c32, z32], axis=-1)
    sq_tab = jnp.concatenate([z64, s32, z32], axis=-1)
    csk_tab = jnp.concatenate([c32, s32, z64], axis=-1)
    return cq_tab, sq_tab, csk_tab


def kernel(x, c, w_ada, b_ada, g_attn_pre, g_attn_post, w_in, b_gate, na_rpb, q_norm_g, kv_norm_g,
           w_uq, w_ukv, w_na_up, w_mla_up, w_out, g_ffn_pre, g_ffn_post, w_router, b_router,
           w1, b1, w2, b2):
    B, S, D = x.shape
    T = B * S
    depth = w_ada.shape[0]
    E = w_router.shape[-1]
    n_rows = S // GRID_W
    assert S % (NA_GROUP_ROWS * GRID_W) == 0 and n_rows >= NA_KEY_ROWS
    assert E <= LANES and T % 2048 == 0

    bf = jnp.bfloat16
    cq_tab, sq_tab, csk_tab = _rope_tables(S)
    pm_np = np.zeros((LANES, LANES), np.float32)
    for cidx in range(MLA_ROPE):
        pm_np[cidx, MLA_NOPE + cidx] = 1.0
        pm_np[MLA_ROPE + cidx, MLA_NOPE + cidx] = 1.0
    pm = jnp.asarray(pm_np, bf)
    qscale = float((MLA_NOPE + MLA_ROPE) ** -0.5 * math.log2(math.e))

    col_na = 2 * D
    col_cq = col_na + 3 * NA_WIDTH
    col_ckv = col_cq + Q_LORA_PAD
    col_kr = col_ckv + MLA_KV_LORA
    assert col_cq % Q_LORA_PAD == 0 and col_ckv % MLA_KV_LORA == 0

    x2 = x.reshape(T, D)
    for l in range(depth):
        mod = _ada_mod(c, w_ada[l], b_ada[l])
        mod3 = mod.reshape(B, 1, 6 * D)

        o_na, o_cq, o_ckv, o_kr, o_g = np.cumsum([0, 3 * NA_WIDTH, MLA_Q_LORA, MLA_KV_LORA, MLA_ROPE]).tolist()
        wi = w_in[l]
        w_kr = wi[:, o_kr:o_g]
        w_ext = jnp.concatenate([
            wi[:, o_g:o_g + 2 * D],
            wi[:, o_na:o_cq],
            wi[:, o_cq:o_ckv], jnp.zeros((D, Q_LORA_PAD - MLA_Q_LORA), wi.dtype),
            wi[:, o_ckv:o_kr],
            w_kr, _rope_swap_cols(w_kr), jnp.zeros((D, LANES - 2 * MLA_ROPE), wi.dtype),
        ], axis=1).astype(bf)
        proj = _in_proj(x2, mod3, g_attn_pre[l], w_ext, b_gate[l], S)

        bias_tab = _na_bias_tables(na_rpb[l], n_rows)
        o_na_tok = _na_attention(proj, bias_tab, B, S, col_na, col_na + NA_WIDTH, col_na + 2 * NA_WIDTH)

        wq = w_uq[l]
        zq = jnp.zeros((MLA_Q_LORA, MLA_HEADS, LANES - MLA_NOPE - MLA_ROPE), wq.dtype)
        wqm = jnp.concatenate([wq, zq], axis=-1).reshape(MLA_Q_LORA, MLA_HEADS * LANES)
        wqs = jnp.concatenate([jnp.zeros((MLA_Q_LORA, MLA_HEADS, MLA_NOPE), wq.dtype),
                               _rope_swap_cols(wq[..., MLA_NOPE:]), zq], axis=-1
                              ).reshape(MLA_Q_LORA, MLA_HEADS * LANES)
        rpad = ((0, Q_LORA_PAD - MLA_Q_LORA), (0, 0))
        wqm = jnp.pad(wqm, rpad).astype(bf)
        wqs = jnp.pad(wqs, rpad).astype(bf)
        wkv = w_ukv[l]
        wk = jnp.concatenate([wkv[..., :MLA_NOPE], jnp.zeros((MLA_KV_LORA, MLA_HEADS, LANES - MLA_NOPE), wkv.dtype)],
                             axis=-1).reshape(MLA_KV_LORA, MLA_HEADS * LANES).astype(bf)
        wv = wkv[..., MLA_NOPE:].reshape(MLA_KV_LORA, MLA_WIDTH).astype(bf)
        gq = jnp.pad(q_norm_g[l], (0, Q_LORA_PAD - MLA_Q_LORA)).reshape(1, Q_LORA_PAD)
        gkv = kv_norm_g[l].reshape(1, MLA_KV_LORA)
        q_cat, k_cat, v_mla = _mla_prep(proj, col_cq, col_ckv, col_kr, gq, gkv, wqm, wqs, wk, wv, pm,
                                        cq_tab, sq_tab, csk_tab, S, qscale)
        o_mla_tok = _mla_attention(q_cat, k_cat, v_mla, B, S)

        w_r = jnp.pad(w_router[l], ((0, 0), (0, LANES - E))).astype(bf)
        b_r = jnp.concatenate([b_router[l], jnp.full((LANES - E,), NEG_BIG, jnp.float32)]).reshape(1, LANES)
        x1, h2, logits = _out_proj(o_na_tok, o_mla_tok, proj, x2, mod3, g_attn_post[l], g_ffn_pre[l],
                                   w_na_up[l].astype(bf), w_mla_up[l].astype(bf), w_out[l].astype(bf),
                                   w_r, b_r, S)

        route, counts = _route(logits)
        e_idx = route[:, 0:TOP_K].astype(jnp.int32)
        rank = route[:, TOP_K:2 * TOP_K].astype(jnp.int32)
        cnt = counts[0, :E].astype(jnp.int32)
        padded = ((cnt + MOE_BLOCK - 1) // MOE_BLOCK) * MOE_BLOCK
        pend = jnp.cumsum(padded)
        pstart = pend - padded
        n_items = T * TOP_K
        nblk = -(-n_items // MOE_BLOCK) + E
        cap = nblk * MOE_BLOCK
        dest = (rank + jnp.sum(jnp.where(e_idx[..., None] == jnp.arange(E, dtype=jnp.int32), pstart, 0), axis=-1)
                ).reshape(n_items)
        blk_off = jnp.arange(nblk, dtype=jnp.int32) * MOE_BLOCK
        blk_e = jnp.minimum(jnp.sum((pend[None, :] <= blk_off[:, None]).astype(jnp.int32), axis=-1), E - 1)
        nvalid = (pend[-1:] // MOE_BLOCK).astype(jnp.int32)

        last_blk = jnp.where(padded > 0, pend // MOE_BLOCK - 1, -1)
        tail_blk = nvalid[0] + jnp.arange(nblk - n_items // MOE_BLOCK, dtype=jnp.int32)
        zero_blk = jnp.concatenate([last_blk, jnp.where(tail_blk < nblk, tail_blk, -1)]).astype(jnp.int32)
        xs, slot_row = _dispatch(h2, dest, zero_blk, cap)
        slot_row = slot_row.reshape(nblk, 1, MOE_BLOCK)
        F2 = w1.shape[-1]
        b1p = b1[l].reshape(E, F2 // (2 * LANES), LANES, 2).transpose(0, 1, 3, 2).reshape(E, 1, F2)
        y_k = _moe_ffn(xs, slot_row, n_items, blk_e, nvalid, w1[l], b1p, w2[l], b2[l].reshape(E, 1, D))
        x2 = _combine(y_k, route, x1, mod3, g_ffn_post[l], S)
    return x2.reshape(B, S, D)
```

```python
import functools
import math

import numpy as np
import jax
import jax.numpy as jnp
from jax import lax
from jax.experimental import pallas as pl
from jax.experimental.pallas import tpu as pltpu

GRID_W = 64
NA_HEADS = 8
NA_HEAD_DIM = 64
NA_WIN_ROWS = 8
NA_WIN_COLS = 16
NA_WIDTH = NA_HEADS * NA_HEAD_DIM
MLA_HEADS = 8
MLA_Q_LORA = 384
MLA_KV_LORA = 256
MLA_NOPE = 64
MLA_ROPE = 32
MLA_V = 64
MLA_WIDTH = MLA_HEADS * MLA_V
MLA_KEY_CHUNK = 256
MLA_VROWS = MLA_V + 16
MLA_STEPS_PER_TRIP = 8
ROPE_THETA = 100.0
TOP_K = 4
SWIGLU_ALPHA = 1.702
SWIGLU_LIMIT = 7.0
MOE_BLOCK = 256
RMS_EPS = 1e-6
NEG_BIG = -1e30

LANES = 128
VMEM_LIMIT_BYTES = 56 * 1024 * 1024

Q_LORA_PAD = 512
COL_GATES = 0
NA_GROUP_ROWS = 8
NA_KEY_ROWS = 16


def _f32(x):
    return x.astype(jnp.float32)


def _bf16(x):
    return x.astype(jnp.bfloat16)


def _dot(a, b):
    return jnp.dot(a, b, preferred_element_type=jnp.float32)


def _dot_nt(a, b):
    return lax.dot_general(a, b, (((1,), (1,)), ((), ())), preferred_element_type=jnp.float32)


def _rms(x, g, n=None):
    n = x.shape[-1] if n is None else n
    ms = jnp.sum(x * x, axis=-1, keepdims=True) * (1.0 / n)
    return x * lax.rsqrt(ms + RMS_EPS) * g


SUBLANES = 8


def _store_row_tiles(ref, x, base=0):
    n = x.shape[0]
    for c in range(x.shape[1] // LANES):
        ref[pl.ds(base + c, n, stride=SUBLANES), :] = x[:, c * LANES:(c + 1) * LANES]


def _load_row_tiles(ref, n, base=0):
    chunks = [ref[pl.ds(base + c, n, stride=SUBLANES), :] for c in range(SUBLANES)]
    return jnp.concatenate(chunks, axis=-1)


def _cparams(sem):
    return pltpu.CompilerParams(dimension_semantics=sem, vmem_limit_bytes=VMEM_LIMIT_BYTES)


def _ada_kernel(c_ref, w_ref, b_ref, o_ref):
    c = c_ref[...]
    sc = c * jax.nn.sigmoid(c)
    o_ref[...] = _dot(_bf16(sc), _bf16(w_ref[...])) + b_ref[...]


def _ada_mod(c, w_ada, b_ada):
    B, D = c.shape
    n_out = w_ada.shape[1]
    return pl.pallas_call(
        _ada_kernel,
        out_shape=jax.ShapeDtypeStruct((B, n_out), jnp.float32),
        grid=(n_out // D,),
        in_specs=[
            pl.BlockSpec((B, D), lambda j: (0, 0)),
            pl.BlockSpec((D, D), lambda j: (0, j)),
            pl.BlockSpec((1, D), lambda j: (0, j)),
        ],
        out_specs=pl.BlockSpec((B, D), lambda j: (0, j)),
        compiler_params=_cparams(("arbitrary",)),
        name="ada_mod",
    )(c, w_ada, b_ada.reshape(1, n_out))


def _in_proj_kernel(x_ref, mod_ref, g_ref, w_ref, bg_ref, o_ref, qt_ref, vt_ref, *, d_model, n_gate,
                    chunk, col_q, col_v):
    x = x_ref[...]
    tm = x.shape[0]
    shift = mod_ref[0, :, 0:d_model]
    scale = mod_ref[0, :, d_model:2 * d_model]
    h = _rms(x, g_ref[...]) * (1.0 + scale) + shift
    hb = _bf16(h)
    n_total = w_ref.shape[1]
    n_pairs = NA_HEADS // 2
    for c0 in range(0, n_total, chunk):
        c1 = min(c0 + chunk, n_total)
        acc = _dot(hb, w_ref[:, c0:c1])
        if c0 < n_gate:
            acc = jax.nn.sigmoid(acc + bg_ref[:, c0:c1])
        o_ref[:, c0:c1] = _bf16(acc)
        if c0 == col_q:
            for hp in range(n_pairs):
                qt_ref[0, hp] = _bf16((acc[:, hp * LANES:(hp + 1) * LANES] * NA_HEAD_DIM ** -0.5).T)
        if c0 == col_v:
            for hp in range(n_pairs):
                vt = _bf16(acc[:, hp * LANES:(hp + 1) * LANES].T)
                for cc in range(tm // LANES):
                    vt_ref[0, hp, cc] = vt[:, cc * LANES:(cc + 1) * LANES]


def _in_proj(x2, mod3, g_pre, w_ext, b_gate, seq, col_q, col_v, tm=512):
    T, D = x2.shape
    batch = T // seq
    n_s = seq // tm
    n_total = w_ext.shape[1]
    n_gate = b_gate.shape[-1]
    n_pairs = NA_HEADS // 2
    chunk = n_pairs * LANES
    assert col_q % chunk == 0 and col_v % chunk == 0
    kern = functools.partial(_in_proj_kernel, d_model=D, n_gate=n_gate, chunk=chunk, col_q=col_q, col_v=col_v)
    return pl.pallas_call(
        kern,
        out_shape=(
            jax.ShapeDtypeStruct((T, n_total), jnp.bfloat16),
            jax.ShapeDtypeStruct((batch, n_pairs, LANES, seq), jnp.bfloat16),
            jax.ShapeDtypeStruct((batch, n_pairs, seq // LANES, LANES, LANES), jnp.bfloat16),
        ),
        grid=(T // tm,),
        in_specs=[
            pl.BlockSpec((tm, D), lambda i: (i, 0)),
            pl.BlockSpec((1, 1, mod3.shape[-1]), lambda i: ((i * tm) // seq, 0, 0)),
            pl.BlockSpec((1, D), lambda i: (0, 0)),
            pl.BlockSpec((D, n_total), lambda i: (0, 0)),
            pl.BlockSpec((1, n_gate), lambda i: (0, 0)),
        ],
        out_specs=(
            pl.BlockSpec((tm, n_total), lambda i: (i, 0)),
            pl.BlockSpec((1, n_pairs, LANES, tm), lambda i: (i // n_s, 0, 0, i % n_s)),
            pl.BlockSpec((1, n_pairs, tm // LANES, LANES, LANES), lambda i: (i // n_s, 0, i % n_s, 0, 0)),
        ),
        compiler_params=_cparams(("arbitrary",)),
        name="in_proj",
    )(x2, mod3, g_pre.reshape(1, D), w_ext, b_gate.reshape(1, n_gate))


def _na_class(c):
    half = NA_WIN_ROWS // 2
    if c == 0:
        return (lambda i: max(i - half, 0)), NA_WIN_ROWS - 1
    if c == 1:
        return (lambda i: i), NA_WIN_ROWS - 1 - half
    return (lambda i: min(i + half, NA_KEY_ROWS - NA_WIN_ROWS)), -1


NA_PAIR_KEY_ROWS = 10


def _na_kernel(qt_ref, k_ref, vt_ref, pt_ref, o_ref, *, n_rows):
    g = pl.program_id(2)
    n_groups = n_rows // NA_GROUP_ROWS
    kb_rows = jnp.clip(g * NA_GROUP_ROWS - NA_WIN_ROWS // 2, 0, n_rows - NA_KEY_ROWS)
    nkeys = NA_PAIR_KEY_ROWS * GRID_W
    row = lax.broadcasted_iota(jnp.int32, (LANES, LANES), 0)
    lane = lax.broadcasted_iota(jnp.int32, (GRID_W, LANES), 1)
    neg_blk = jnp.full((GRID_W, LANES), NEG_BIG, jnp.float32)

    def body(c):
        j0, off = _na_class(c)
        n_pair = NA_GROUP_ROWS // 2
        starts = [min(j0(2 * ip) - j0(2 * ip) % 2, NA_KEY_ROWS - NA_PAIR_KEY_ROWS) for ip in range(n_pair)]
        scores = []
        for ip in range(n_pair):
            kstart = pl.multiple_of((kb_rows + starts[ip]) * GRID_W, 2 * GRID_W)
            kwin = k_ref[pl.ds(kstart, nkeys), :]
            qp = qt_ref[0, 0, :, ip * LANES:(ip + 1) * LANES]
            zero = jnp.zeros_like(qp)
            wq = jnp.concatenate([jnp.where(row < NA_HEAD_DIM, qp, zero),
                                  jnp.where(row >= NA_HEAD_DIM, qp, zero)], axis=1)
            scores.append(_dot(kwin, wq))
        outs = []
        for ip in range(n_pair):
            i0 = 2 * ip
            w = starts[ip]
            st = scores[ip]
            blocks = []
            for jw in range(NA_PAIR_KEY_ROWS):
                j = w + jw
                v0 = j0(i0) <= j < j0(i0) + NA_WIN_ROWS
                v1 = j0(i0 + 1) <= j < j0(i0 + 1) + NA_WIN_ROWS
                dr0 = j - i0 + off
                halves = []
                for hl in range(2):
                    if not (v0 or v1):
                        halves.append(neg_blk)
                        continue
                    blk = st[jw * GRID_W:(jw + 1) * GRID_W, hl * LANES:(hl + 1) * LANES] + pt_ref[hl, dr0]
                    if not (v0 and v1):
                        keep = (lane < GRID_W) if v0 else (lane >= GRID_W)
                        blk = jnp.where(keep, blk, NEG_BIG)
                    halves.append(blk)
                blocks.append(jnp.concatenate(halves, axis=1))
            s = jnp.concatenate(blocks, axis=0)
            m = jnp.max(s, axis=0, keepdims=True)
            p = jnp.exp(s - m)
            l = jnp.sum(p, axis=0, keepdims=True)
            c0 = (kb_rows + w) // 2
            vwin = jnp.concatenate([vt_ref[0, 0, c0 + u] for u in range(NA_PAIR_KEY_ROWS // 2)], axis=1)
            on = _dot(vwin, _bf16(p)) / l
            outs.append(jnp.where(row < NA_HEAD_DIM, on[:, 0:LANES], on[:, LANES:2 * LANES]))
        o_ref[...] = _bf16(jnp.concatenate(outs, axis=1).T)

    cls = jnp.where(g == 0, 0, jnp.where(g == n_groups - 1, 2, 1))
    for c in range(3):
        pl.when(cls == c)(functools.partial(body, c))


def _na_attention(proj, q_t, v_t, pair_tab, batch, seq, col_k):
    T = proj.shape[0]
    n_rows = seq // GRID_W
    n_groups = n_rows // NA_GROUP_ROWS
    tq = NA_GROUP_ROWS * GRID_W
    n_pairs = NA_HEADS // 2
    kern = functools.partial(_na_kernel, n_rows=n_rows)
    return pl.pallas_call(
        kern,
        out_shape=jax.ShapeDtypeStruct((T, NA_WIDTH), jnp.bfloat16),
        grid=(n_pairs, batch, n_groups),
        in_specs=[
            pl.BlockSpec((1, 1, LANES, tq), lambda hp, b, g: (b, hp, 0, g)),
            pl.BlockSpec((seq, LANES), lambda hp, b, g: (b, col_k // LANES + hp)),
            pl.BlockSpec((1, 1, seq // LANES, LANES, LANES), lambda hp, b, g: (b, hp, 0, 0, 0)),
            pl.BlockSpec((2,) + pair_tab.shape[1:], lambda hp, b, g: (hp, 0, 0, 0)),
        ],
        out_specs=pl.BlockSpec((tq, LANES), lambda hp, b, g: (b * n_groups + g, hp)),
        compiler_params=_cparams(("arbitrary", "arbitrary", "arbitrary")),
        name="na_attn",
    )(q_t, proj, v_t, pair_tab)


def _na_pair_tables(rpb):
    H, n_dr, n_dc = rpb.shape
    kw = NA_WIN_COLS
    W = GRID_W
    ring = 2 * W - 1
    rp = jnp.pad(rpb.astype(jnp.float32), ((0, 0), (0, 0), (W - kw, W - kw)), constant_values=NEG_BIG)
    u = jnp.roll(rp, -(W - 1), axis=-1)
    circ = jnp.tile(u, (1, 1, W))[..., :W * (ring - 1)].reshape(H, n_dr, W, ring - 1)
    toep = circ[..., :W]
    cidx = np.arange(W)
    col_start = np.clip(cidx - kw // 2, 0, W - kw)
    col_in = (cidx[None, :] >= col_start[:, None]) & (cidx[None, :] < col_start[:, None] + kw)
    toep_t = jnp.where(jnp.asarray(col_in.T)[None, None], jnp.swapaxes(toep, 2, 3), NEG_BIG)
    neg = jnp.full((H, 1, W, W), NEG_BIG, jnp.float32)
    ext = jnp.concatenate([neg, toep_t, neg], axis=1)
    return jnp.concatenate([ext[:, 1:], ext[:, :-1]], axis=-1)


def _mla_prep_kernel(cq_ref, ckv_ref, kr_ref, gq_ref, gkv_ref, wqm_ref, wqs_ref, wk_ref, wv_ref,
                     pm_ref, cq_tab_ref, sq_tab_ref, csk_tab_ref, q_out, k_out, v_out, *, qscale):
    cq = _f32(cq_ref[...])
    cqn = _bf16(_rms(cq, gq_ref[...], n=MLA_Q_LORA))
    qm = _dot(cqn, wqm_ref[...])
    qs = _dot(cqn, wqs_ref[...])
    ctab = cq_tab_ref[...]
    stab = sq_tab_ref[...]
    for h in range(MLA_HEADS):
        sl = slice(h * LANES, (h + 1) * LANES)
        qh = (qm[:, sl] * ctab + qs[:, sl] * stab) * qscale
        q_out[0, h] = _bf16(qh.T)

    ckv = _f32(ckv_ref[...])
    ckvn = _bf16(_rms(ckv, gkv_ref[...]))
    kk = _dot(ckvn, wk_ref[...])
    vv = _dot(ckvn, wv_ref[...])
    n_chunk = v_out.shape[2]
    ones = jnp.ones((MLA_VROWS - MLA_V, MLA_KEY_CHUNK), jnp.bfloat16)
    for hp in range(MLA_HEADS // 2):
        vt = _bf16(vv[:, hp * LANES:(hp + 1) * LANES].T)
        for cc in range(n_chunk):
            ks = slice(cc * MLA_KEY_CHUNK, (cc + 1) * MLA_KEY_CHUNK)
            for h in range(2):
                r0 = h * MLA_VROWS
                v_out[0, hp, cc, r0:r0 + MLA_V, :] = vt[h * MLA_V:(h + 1) * MLA_V, ks]
                v_out[0, hp, cc, r0 + MLA_V:r0 + MLA_VROWS, :] = ones
    krr = _f32(kr_ref[...]) * csk_tab_ref[...]
    kplace = _dot(_bf16(krr), pm_ref[...])
    for h in range(MLA_HEADS):
        sl = slice(h * LANES, (h + 1) * LANES)
        k_out[:, sl] = _bf16(kk[:, sl] + kplace)


def _mla_prep(proj, col_cq, col_ckv, col_kr, gq, gkv, wqm, wqs, wk, wv, pm, cq_tab, sq_tab, csk_tab,
              seq, qscale, tm=512):
    T = proj.shape[0]
    batch = T // seq
    n_s = seq // tm
    n_pairs = MLA_HEADS // 2
    cpt = tm // MLA_KEY_CHUNK
    full = lambda a: pl.BlockSpec(a.shape, lambda i: (0,) * a.ndim)
    tab = pl.BlockSpec((tm, LANES), lambda i: (i % n_s, 0))
    kern = functools.partial(_mla_prep_kernel, qscale=qscale)
    return pl.pallas_call(
        kern,
        out_shape=(
            jax.ShapeDtypeStruct((batch, MLA_HEADS, LANES, seq), jnp.bfloat16),
            jax.ShapeDtypeStruct((T, MLA_HEADS * LANES), jnp.bfloat16),
            jax.ShapeDtypeStruct((batch, n_pairs, seq // MLA_KEY_CHUNK, 2 * MLA_VROWS, MLA_KEY_CHUNK),
                                 jnp.bfloat16),
        ),
        grid=(T // tm,),
        in_specs=[
            pl.BlockSpec((tm, Q_LORA_PAD), lambda i: (i, col_cq // Q_LORA_PAD)),
            pl.BlockSpec((tm, MLA_KV_LORA), lambda i: (i, col_ckv // MLA_KV_LORA)),
            pl.BlockSpec((tm, LANES), lambda i: (i, col_kr // LANES)),
            full(gq), full(gkv), full(wqm), full(wqs), full(wk), full(wv), full(pm),
            tab, tab, tab,
        ],
        out_specs=(
            pl.BlockSpec((1, MLA_HEADS, LANES, tm), lambda i: (i // n_s, 0, 0, i % n_s)),
            pl.BlockSpec((tm, MLA_HEADS * LANES), lambda i: (i, 0)),
            pl.BlockSpec((1, n_pairs, cpt, 2 * MLA_VROWS, MLA_KEY_CHUNK),
                         lambda i: (i // n_s, 0, i % n_s, 0, 0)),
        ),
        compiler_params=_cparams(("arbitrary",)),
        name="mla_prep",
    )(proj, proj, proj, gq, gkv, wqm, wqs, wk, wv, pm, cq_tab, sq_tab, csk_tab)


def _mla_attn_kernel(qt_ref, k_ref, vt_ref, o_ref, m_scr, acc_scr, st_scr):
    n_chunks = vt_ref.shape[2]
    tk = MLA_KEY_CHUNK
    m_scr[...] = jnp.full(m_scr.shape, -jnp.inf, jnp.float32)
    acc_scr[...] = jnp.zeros(acc_scr.shape, jnp.float32)

    def scores(c, slot):
        k0 = pl.multiple_of(c * tk, tk)
        for h in range(2):
            kc = k_ref[pl.ds(k0, tk), h * LANES:(h + 1) * LANES]
            st_scr[slot, h] = _dot(kc, qt_ref[0, h])

    def step(c, slot):
        scores(jnp.minimum(c + 1, n_chunks - 1), 1 - slot)
        vt = vt_ref[0, 0, c]
        pts, alphas = [], []
        for h in range(2):
            st = st_scr[slot, h]
            m_old = m_scr[h]
            m_new = jnp.maximum(m_old, jnp.max(st, axis=0, keepdims=True))
            alpha = jnp.exp2(m_old - m_new)
            pt = jnp.exp2(st - m_new)
            m_scr[h] = m_new
            pts.append(_bf16(pt))
            alphas.append(alpha)
        for h in range(2):
            acc_scr[h] = alphas[h] * acc_scr[h] + _dot(vt[h * MLA_VROWS:(h + 1) * MLA_VROWS, :], pts[h])

    def body(j, carry):
        for u in range(MLA_STEPS_PER_TRIP):
            step(MLA_STEPS_PER_TRIP * j + u, u % 2)
        return carry

    scores(0, 0)
    lax.fori_loop(0, n_chunks // MLA_STEPS_PER_TRIP, body, 0)
    outs = []
    for h in range(2):
        acc = acc_scr[h]
        outs.append(acc[0:MLA_V] / acc[MLA_V:MLA_V + 1])
    o_ref[...] = _bf16(jnp.concatenate(outs, axis=0).T)


def _mla_attention(q_t, k_cat, v_t, batch, seq, tq=512):
    T = k_cat.shape[0]
    n_q = seq // tq
    n_pairs = MLA_HEADS // 2
    n_chunks = seq // MLA_KEY_CHUNK
    assert n_chunks % MLA_STEPS_PER_TRIP == 0
    return pl.pallas_call(
        _mla_attn_kernel,
        out_shape=jax.ShapeDtypeStruct((T, MLA_WIDTH), jnp.bfloat16),
        grid=(batch, n_pairs, n_q),
        in_specs=[
            pl.BlockSpec((1, 2, LANES, tq), lambda b, hp, i: (b, hp, 0, i)),
            pl.BlockSpec((seq, 2 * LANES), lambda b, hp, i: (b, hp)),
            pl.BlockSpec((1, 1, n_chunks, 2 * MLA_VROWS, MLA_KEY_CHUNK), lambda b, hp, i: (b, hp, 0, 0, 0)),
        ],
        out_specs=pl.BlockSpec((tq, LANES), lambda b, hp, i: (b * n_q + i, hp)),
        scratch_shapes=[
            pltpu.VMEM((2, 1, tq), jnp.float32),
            pltpu.VMEM((2, MLA_VROWS, tq), jnp.float32),
            pltpu.VMEM((2, 2, MLA_KEY_CHUNK, tq), jnp.float32),
        ],
        compiler_params=_cparams(("arbitrary", "arbitrary", "arbitrary")),
        name="mla_attn",
    )(q_t, k_cat, v_t)


def _out_proj_kernel(ona_ref, omla_ref, gate_ref, x_ref, mod_ref, gpost_ref, gpre_ref, wna_ref,
                     wmla_ref, wout_ref, wr_ref, br_ref, x1_ref, h2_ref, lg_ref, *, d_model):
    D = d_model
    gates = gate_ref[...]
    merged = (_f32(gates[:, 0:D]) * _dot(ona_ref[...], wna_ref[...])
              + _f32(gates[:, D:2 * D]) * _dot(omla_ref[...], wmla_ref[...]))
    y = _dot(_bf16(merged), wout_ref[...])
    gate_a = mod_ref[0, :, 2 * D:3 * D]
    shift_f = mod_ref[0, :, 3 * D:4 * D]
    scale_f = mod_ref[0, :, 4 * D:5 * D]
    x1 = x_ref[...] + gate_a * _rms(y, gpost_ref[...])
    x1_ref[...] = x1
    h2 = _rms(x1, gpre_ref[...]) * (1.0 + scale_f) + shift_f
    _store_row_tiles(h2_ref, h2)
    lg_ref[...] = _dot(_bf16(h2), wr_ref[...]) + br_ref[...]


def _out_proj(o_na, o_mla, proj, x2, mod3, g_post, g_pre, w_na, w_mla, w_out, w_r, b_r, seq, tm=512):
    T, D = x2.shape
    full = lambda a: pl.BlockSpec(a.shape, lambda i: (0,) * a.ndim)
    row = lambda w: pl.BlockSpec((tm, w), lambda i: (i, 0))
    kern = functools.partial(_out_proj_kernel, d_model=D)
    g_post = g_post.reshape(1, D)
    g_pre = g_pre.reshape(1, D)
    return pl.pallas_call(
        kern,
        out_shape=(
            jax.ShapeDtypeStruct((T, D), jnp.float32),
            jax.ShapeDtypeStruct((T * (D // LANES), LANES), jnp.float32),
            jax.ShapeDtypeStruct((T, LANES), jnp.float32),
        ),
        grid=(T // tm,),
        in_specs=[
            row(NA_WIDTH), row(MLA_WIDTH),
            pl.BlockSpec((tm, 2 * D), lambda i: (i, COL_GATES // (2 * D))),
            row(D),
            pl.BlockSpec((1, 1, mod3.shape[-1]), lambda i: ((i * tm) // seq, 0, 0)),
            full(g_post), full(g_pre), full(w_na), full(w_mla), full(w_out), full(w_r), full(b_r),
        ],
        out_specs=(row(D), pl.BlockSpec((tm * (D // LANES), LANES), lambda i: (i, 0)), row(LANES)),
        compiler_params=_cparams(("arbitrary",)),
        name="out_proj",
    )(o_na, o_mla, proj, x2, mod3, g_post, g_pre, w_na, w_mla, w_out, w_r, b_r)


def _route_kernel(lg_ref, r_ref, cnt_ref, carry_scr, *, sub):
    @pl.when(pl.program_id(0) == 0)
    def _():
        carry_scr[...] = jnp.zeros(carry_scr.shape, jnp.float32)

    tr = lg_ref.shape[0]
    lane = lax.broadcasted_iota(jnp.int32, (sub, LANES), 1).astype(jnp.float32)
    ri = lax.broadcasted_iota(jnp.int32, (sub, sub), 0)
    ci = lax.broadcasted_iota(jnp.int32, (sub, sub), 1)
    tri = jnp.where(ri >= ci, 1.0, 0.0).astype(jnp.bfloat16)
    for s0 in range(0, tr, sub):
        work = lg_ref[s0:s0 + sub, :]
        sels, vals, idxs = [], [], []
        for _k in range(TOP_K):
            mk = jnp.max(work, axis=-1, keepdims=True)
            ik = jnp.min(jnp.where(work == mk, lane, float(LANES)), axis=-1, keepdims=True)
            sk = lane == ik
            work = jnp.where(sk, -jnp.inf, work)
            sels.append(sk)
            vals.append(mk)
            idxs.append(ik)
        es = [jnp.exp(v - vals[0]) for v in vals]
        denom = es[0] + es[1] + es[2] + es[3]
        onehot = jnp.zeros((sub, LANES), jnp.float32)
        for sk in sels:
            onehot = jnp.where(sk, 1.0, onehot)
        prefix = _dot(tri, _bf16(onehot))
        carry = carry_scr[...]
        rank_mat = carry + prefix - 1.0
        res = jnp.zeros((sub, LANES), jnp.float32)
        for kk in range(TOP_K):
            rank_k = jnp.sum(jnp.where(sels[kk], rank_mat, 0.0), axis=-1, keepdims=True)
            res = jnp.where(lane == kk, idxs[kk], res)
            res = jnp.where(lane == TOP_K + kk, rank_k, res)
            res = jnp.where(lane == 2 * TOP_K + kk, es[kk] / denom, res)
        r_ref[s0:s0 + sub, :] = res
        carry_scr[...] = carry + jnp.sum(onehot, axis=0, keepdims=True)
    cnt_ref[...] = carry_scr[...]


def _route(logits, tr=2048, sub=256):
    T = logits.shape[0]
    kern = functools.partial(_route_kernel, sub=sub)
    return pl.pallas_call(
        kern,
        out_shape=(
            jax.ShapeDtypeStruct((T, LANES), jnp.float32),
            jax.ShapeDtypeStruct((1, LANES), jnp.float32),
        ),
        grid=(T // tr,),
        in_specs=[pl.BlockSpec((tr, LANES), lambda i: (i, 0))],
        out_specs=(
            pl.BlockSpec((tr, LANES), lambda i: (i, 0)),
            pl.BlockSpec((1, LANES), lambda i: (0, 0)),
        ),
        scratch_shapes=[pltpu.VMEM((1, LANES), jnp.float32)],
        compiler_params=_cparams(("arbitrary",)),
        name="route",
    )(logits)


def _dispatch_kernel(zero_blk_ref, dest_ref, h_ref, row0_ref, xs_ref, row_ref, zbuf, sem, fill_sem,
                     *, n_tokens):
    i = pl.program_id(0)
    ts = h_ref.shape[0] // SUBLANES
    blk_rows = MOE_BLOCK * SUBLANES

    @pl.when(i == 0)
    def _():
        fill_row = pltpu.make_async_copy(row0_ref, row_ref, fill_sem.at[0])
        fill_row.start()
        zbuf[...] = jnp.zeros(zbuf.shape, zbuf.dtype)
        n_zero = zero_blk_ref.shape[0]

        def zero_copy(z):
            r0 = pl.multiple_of(zero_blk_ref[z] * blk_rows, blk_rows)
            return pltpu.make_async_copy(zbuf, xs_ref.at[pl.ds(r0, blk_rows), :], fill_sem.at[1])

        for z in range(n_zero):
            pl.when(zero_blk_ref[z] >= 0)(lambda z=z: zero_copy(z).start())
        for z in range(n_zero):
            pl.when(zero_blk_ref[z] >= 0)(lambda z=z: zero_copy(z).wait())
        fill_row.wait()

    def row_copy(j, d):
        dst = pl.ds(pl.multiple_of(d * SUBLANES, SUBLANES), SUBLANES)
        return pltpu.make_async_copy(h_ref.at[pl.ds(j * SUBLANES, SUBLANES), :], xs_ref.at[dst, :], sem)

    t0 = i * ts
    for j in range(ts):
        for kk in range(TOP_K):
            d = dest_ref[j * TOP_K + kk]
            row_copy(j, d).start(priority=kk % 2)
            row_ref[d] = (kk * n_tokens + j) + t0
    for _j in range(ts * TOP_K):
        row_copy(0, 0).wait()


def _dispatch(h2_tiles, dest, zero_blk, cap, ts=256):
    T = h2_tiles.shape[0] // SUBLANES
    kern = functools.partial(_dispatch_kernel, n_tokens=T)
    grid_spec = pltpu.PrefetchScalarGridSpec(
        num_scalar_prefetch=1,
        grid=(T // ts,),
        in_specs=[
            pl.BlockSpec((ts * TOP_K,), lambda i, zb: (i,), memory_space=pltpu.SMEM),
            pl.BlockSpec((ts * SUBLANES, LANES), lambda i, zb: (i, 0)),
            pl.BlockSpec(memory_space=pl.ANY),
        ],
        out_specs=(pl.BlockSpec(memory_space=pl.ANY), pl.BlockSpec(memory_space=pltpu.SMEM)),
        scratch_shapes=[
            pltpu.VMEM((MOE_BLOCK * SUBLANES, LANES), h2_tiles.dtype),
            pltpu.SemaphoreType.DMA(()),
            pltpu.SemaphoreType.DMA((2,)),
        ],
    )
    return pl.pallas_call(
        kern,
        out_shape=(jax.ShapeDtypeStruct((cap * SUBLANES, LANES), h2_tiles.dtype),
                   jax.ShapeDtypeStruct((cap,), jnp.int32)),
        grid_spec=grid_spec,
        compiler_params=_cparams(("arbitrary",)),
        name="dispatch",
    )(zero_blk, dest, h2_tiles, jnp.full((cap,), -1, jnp.int32))


def _moe_kernel(blk_e_ref, nvalid_ref, row_prev_ref, row_ref, x_ref, w1_ref, b1_ref, w2_ref, b2_ref,
                out_ref, w1p_scr, w2b_scr, ybuf0, ybuf1, ssem, *, n_items):
    i = pl.program_id(0)
    nvalid = nvalid_ref[0]
    d_ff = w2_ref.shape[1]
    n_groups = (2 * d_ff) // (2 * LANES)

    def tile(r):
        return pl.ds(pl.multiple_of(r * SUBLANES, SUBLANES), SUBLANES)

    ybufs = (ybuf0, ybuf1)

    def scatter_copy(j, r, slot):
        return pltpu.make_async_copy(ybufs[slot].at[pl.ds(j * SUBLANES, SUBLANES), :], out_ref.at[tile(r), :],
                                     ssem.at[slot])

    def wait_scatters(slot):
        for _j in range(MOE_BLOCK):
            scatter_copy(0, 0, slot).wait()

    def scatter_rows(rows_ref, slot):
        spare0 = n_items + slot * MOE_BLOCK
        for j in range(MOE_BLOCK):
            r = rows_ref[0, 0, j]
            scatter_copy(j, jnp.where(r >= 0, r, spare0 + j), slot).start(priority=j % 2)

    @pl.when(i == 0)
    def _():
        ybuf0[...] = jnp.zeros(ybuf0.shape, ybuf0.dtype)
        ybuf1[...] = jnp.zeros(ybuf1.shape, ybuf1.dtype)
        for j in range(MOE_BLOCK):
            scatter_copy(j, n_items + j, 0).start(priority=j % 2)

    @pl.when(i < nvalid)
    def _():
        e = blk_e_ref[i]
        e_prev = blk_e_ref[jnp.maximum(i - 1, 0)]

        @pl.when((i == 0) | (e != e_prev))
        def _():
            r = lax.broadcasted_iota(jnp.int32, (2 * LANES, 2 * LANES), 0)
            c = lax.broadcasted_iota(jnp.int32, (2 * LANES, 2 * LANES), 1)
            src = jnp.where(c < LANES, 2 * c, 2 * (c - LANES) + 1)
            perm = jnp.where(r == src, 1.0, 0.0).astype(jnp.bfloat16)
            for gI in range(n_groups):
                sl = slice(gI * 2 * LANES, (gI + 1) * 2 * LANES)
                w1p_scr[:, sl] = _bf16(_dot(_bf16(w1_ref[0, :, sl]), perm))
            w2b_scr[...] = _bf16(w2_ref[0])

        def block_step(slot):
            wait_scatters(slot)
            scatter_rows(row_prev_ref, 1 - slot)

            xb = _bf16(_load_row_tiles(x_ref, MOE_BLOCK))
            hcat = _dot(xb, w1p_scr[...]) + b1_ref[0]
            acts = []
            for gI in range(n_groups):
                glu = jnp.minimum(hcat[:, gI * 2 * LANES: gI * 2 * LANES + LANES], SWIGLU_LIMIT)
                lin = jnp.clip(hcat[:, gI * 2 * LANES + LANES: (gI + 1) * 2 * LANES],
                               -SWIGLU_LIMIT, SWIGLU_LIMIT)
                acts.append(_bf16(glu * jax.nn.sigmoid(SWIGLU_ALPHA * glu) * (lin + 1.0)))
            act = jnp.concatenate(acts, axis=-1)
            _store_row_tiles(ybufs[slot], _dot(act, w2b_scr[...]) + b2_ref[0])

            @pl.when(i == nvalid - 1)
            def _():
                scatter_rows(row_ref, slot)
                wait_scatters(1 - slot)
                wait_scatters(slot)

        parity = lax.rem(i, 2)
        for slot in range(2):
            pl.when(parity == slot)(functools.partial(block_step, slot))


def _moe_ffn(xs, slot_row, n_items, blk_e, nvalid, w1, b1p, w2, b2):
    D = w1.shape[1]
    assert D == SUBLANES * LANES
    E, _, F2 = w1.shape
    F = w2.shape[1]
    nblk = slot_row.shape[0]

    def cur(i, be, nv):
        return (jnp.minimum(i, nv[0] - 1), 0, 0)

    def prv(i, be, nv):
        return (jnp.where(i == 0, nblk, jnp.maximum(jnp.minimum(i, nv[0] - 1) - 1, 0)), 0, 0)

    slot_row = jnp.concatenate([slot_row, jnp.full((1, 1, MOE_BLOCK), -1, slot_row.dtype)], axis=0)
    slot_spec = lambda f: pl.BlockSpec((1, 1, MOE_BLOCK), f, memory_space=pltpu.SMEM)
    grid_spec = pltpu.PrefetchScalarGridSpec(
        num_scalar_prefetch=2,
        grid=(nblk,),
        in_specs=[
            slot_spec(prv), slot_spec(cur),
            pl.BlockSpec((MOE_BLOCK * SUBLANES, LANES), lambda i, be, nv: (jnp.minimum(i, nv[0] - 1), 0)),
            pl.BlockSpec((1, D, F2), lambda i, be, nv: (be[i], 0, 0)),
            pl.BlockSpec((1, 1, F2), lambda i, be, nv: (be[i], 0, 0)),
            pl.BlockSpec((1, F, D), lambda i, be, nv: (be[i], 0, 0)),
            pl.BlockSpec((1, 1, D), lambda i, be, nv: (be[i], 0, 0)),
        ],
        out_specs=pl.BlockSpec(memory_space=pl.ANY),
        scratch_shapes=[
            pltpu.VMEM((D, F2), jnp.bfloat16),
            pltpu.VMEM((F, D), jnp.bfloat16),
            pltpu.VMEM((MOE_BLOCK * SUBLANES, LANES), jnp.float32),
            pltpu.VMEM((MOE_BLOCK * SUBLANES, LANES), jnp.float32),
            pltpu.SemaphoreType.DMA((2,)),
        ],
    )
    kern = functools.partial(_moe_kernel, n_items=n_items)
    return pl.pallas_call(
        kern,
        out_shape=jax.ShapeDtypeStruct(((n_items + 2 * MOE_BLOCK) * SUBLANES, LANES), jnp.float32),
        grid_spec=grid_spec,
        compiler_params=_cparams(("arbitrary",)),
        name="moe_ffn",
    )(blk_e, nvalid, slot_row, slot_row, xs, w1, b1p, w2, b2)


def _combine_kernel(y0_ref, y1_ref, y2_ref, y3_ref, r_ref, x1_ref, mod_ref, g_ref, o_ref, *, d_model):
    D = d_model
    tc = x1_ref.shape[0]
    r = r_ref[...]
    moe = None
    for kk, y_ref in enumerate((y0_ref, y1_ref, y2_ref, y3_ref)):
        term = r[:, 2 * TOP_K + kk: 2 * TOP_K + kk + 1] * _load_row_tiles(y_ref, tc)
        moe = term if moe is None else moe + term
    gate_f = mod_ref[0, :, 5 * D:6 * D]
    o_ref[...] = x1_ref[...] + gate_f * _rms(moe, g_ref[...])


def _combine(y_k, route, x1, mod3, g_post, seq, tc=512):
    T, D = x1.shape
    n_t = T // tc
    kern = functools.partial(_combine_kernel, d_model=D)
    y_spec = lambda kk: pl.BlockSpec((tc * SUBLANES, LANES), lambda i: (kk * n_t + i, 0))
    return pl.pallas_call(
        kern,
        out_shape=jax.ShapeDtypeStruct((T, D), jnp.float32),
        grid=(n_t,),
        in_specs=[
            y_spec(0), y_spec(1), y_spec(2), y_spec(3),
            pl.BlockSpec((tc, LANES), lambda i: (i, 0)),
            pl.BlockSpec((tc, D), lambda i: (i, 0)),
            pl.BlockSpec((1, 1, mod3.shape[-1]), lambda i: ((i * tc) // seq, 0, 0)),
            pl.BlockSpec((1, D), lambda i: (0, 0)),
        ],
        out_specs=pl.BlockSpec((tc, D), lambda i: (i, 0)),
        compiler_params=_cparams(("arbitrary",)),
        name="combine",
    )(y_k, y_k, y_k, y_k, route, x1, mod3, g_post.reshape(1, D))


def _rope_swap_cols(w):
    nf = MLA_ROPE // 4
    return jnp.concatenate([-w[..., nf:2 * nf], w[..., 0:nf], -w[..., 3 * nf:4 * nf], w[..., 2 * nf:3 * nf]],
                           axis=-1)


def _rope_tables(seq):
    n_rows = seq // GRID_W
    nf = MLA_ROPE // 4
    inv = ROPE_THETA ** (-jnp.arange(nf, dtype=jnp.float32) / nf)
    ar = jnp.arange(n_rows, dtype=jnp.float32)[:, None] * inv
    ac = jnp.arange(GRID_W, dtype=jnp.float32)[:, None] * inv
    per_row = lambda a: jnp.broadcast_to(a[:, None, :], (n_rows, GRID_W, nf)).reshape(seq, nf)
    per_col = lambda a: jnp.broadcast_to(a[None, :, :], (n_rows, GRID_W, nf)).reshape(seq, nf)
    cr, sr, cc, sc = per_row(jnp.cos(ar)), per_row(jnp.sin(ar)), per_col(jnp.cos(ac)), per_col(jnp.sin(ac))
    c32 = jnp.concatenate([cr, cr, cc, cc], axis=-1)
    s32 = jnp.concatenate([sr, sr, sc, sc], axis=-1)
    one = jnp.ones((seq, MLA_NOPE), jnp.float32)
    z64 = jnp.zeros((seq, MLA_NOPE), jnp.float32)
    z32 = jnp.zeros((seq, LANES - MLA_NOPE - MLA_ROPE), jnp.float32)
    cq_tab = jnp.concatenate([one, c32, z32], axis=-1)
    sq_tab = jnp.concatenate([z64, s32, z32], axis=-1)
    csk_tab = jnp.concatenate([c32, s32, z64], axis=-1)
    return cq_tab, sq_tab, csk_tab


def kernel(x, c, w_ada, b_ada, g_attn_pre, g_attn_post, w_in, b_gate, na_rpb, q_norm_g, kv_norm_g,
           w_uq, w_ukv, w_na_up, w_mla_up, w_out, g_ffn_pre, g_ffn_post, w_router, b_router,
           w1, b1, w2, b2):
    B, S, D = x.shape
    T = B * S
    depth = w_ada.shape[0]
    E = w_router.shape[-1]
    n_rows = S // GRID_W
    assert S % (NA_GROUP_ROWS * GRID_W) == 0 and n_rows >= NA_KEY_ROWS
    assert E <= LANES and T % 2048 == 0

    bf = jnp.bfloat16
    cq_tab, sq_tab, csk_tab = _rope_tables(S)
    pm_np = np.zeros((LANES, LANES), np.float32)
    for cidx in range(MLA_ROPE):
        pm_np[cidx, MLA_NOPE + cidx] = 1.0
        pm_np[MLA_ROPE + cidx, MLA_NOPE + cidx] = 1.0
    pm = jnp.asarray(pm_np, bf)
    qscale = float((MLA_NOPE + MLA_ROPE) ** -0.5 * math.log2(math.e))

    col_na = 2 * D
    col_cq = col_na + 3 * NA_WIDTH
    col_ckv = col_cq + Q_LORA_PAD
    col_kr = col_ckv + MLA_KV_LORA
    assert col_cq % Q_LORA_PAD == 0 and col_ckv % MLA_KV_LORA == 0

    x2 = x.reshape(T, D)
    for l in range(depth):
        mod = _ada_mod(c, w_ada[l], b_ada[l])
        mod3 = mod.reshape(B, 1, 6 * D)

        o_na, o_cq, o_ckv, o_kr, o_g = np.cumsum([0, 3 * NA_WIDTH, MLA_Q_LORA, MLA_KV_LORA, MLA_ROPE]).tolist()
        wi = w_in[l]
        w_kr = wi[:, o_kr:o_g]
        w_ext = jnp.concatenate([
            wi[:, o_g:o_g + 2 * D],
            wi[:, o_na:o_cq],
            wi[:, o_cq:o_ckv], jnp.zeros((D, Q_LORA_PAD - MLA_Q_LORA), wi.dtype),
            wi[:, o_ckv:o_kr],
            w_kr, _rope_swap_cols(w_kr), jnp.zeros((D, LANES - 2 * MLA_ROPE), wi.dtype),
        ], axis=1).astype(bf)
        proj, na_qt, na_vt = _in_proj(x2, mod3, g_attn_pre[l], w_ext, b_gate[l], S,
                                      col_na, col_na + 2 * NA_WIDTH)

        o_na_tok = _na_attention(proj, na_qt, na_vt, _na_pair_tables(na_rpb[l]), B, S, col_na + NA_WIDTH)

        wq = w_uq[l]
        zq = jnp.zeros((MLA_Q_LORA, MLA_HEADS, LANES - MLA_NOPE - MLA_ROPE), wq.dtype)
        wqm = jnp.concatenate([wq, zq], axis=-1).reshape(MLA_Q_LORA, MLA_HEADS * LANES)
        wqs = jnp.concatenate([jnp.zeros((MLA_Q_LORA, MLA_HEADS, MLA_NOPE), wq.dtype),
                               _rope_swap_cols(wq[..., MLA_NOPE:]), zq], axis=-1
                              ).reshape(MLA_Q_LORA, MLA_HEADS * LANES)
        rpad = ((0, Q_LORA_PAD - MLA_Q_LORA), (0, 0))
        wqm = jnp.pad(wqm, rpad).astype(bf)
        wqs = jnp.pad(wqs, rpad).astype(bf)
        wkv = w_ukv[l]
        wk = jnp.concatenate([wkv[..., :MLA_NOPE], jnp.zeros((MLA_KV_LORA, MLA_HEADS, LANES - MLA_NOPE), wkv.dtype)],
                             axis=-1).reshape(MLA_KV_LORA, MLA_HEADS * LANES).astype(bf)
        wv = wkv[..., MLA_NOPE:].reshape(MLA_KV_LORA, MLA_WIDTH).astype(bf)
        gq = jnp.pad(q_norm_g[l], (0, Q_LORA_PAD - MLA_Q_LORA)).reshape(1, Q_LORA_PAD)
        gkv = kv_norm_g[l].reshape(1, MLA_KV_LORA)
        q_cat, k_cat, v_mla = _mla_prep(proj, col_cq, col_ckv, col_kr, gq, gkv, wqm, wqs, wk, wv, pm,
                                        cq_tab, sq_tab, csk_tab, S, qscale)
        o_mla_tok = _mla_attention(q_cat, k_cat, v_mla, B, S)

        w_r = jnp.pad(w_router[l], ((0, 0), (0, LANES - E))).astype(bf)
        b_r = jnp.concatenate([b_router[l], jnp.full((LANES - E,), NEG_BIG, jnp.float32)]).reshape(1, LANES)
        x1, h2, logits = _out_proj(o_na_tok, o_mla_tok, proj, x2, mod3, g_attn_post[l], g_ffn_pre[l],
                                   w_na_up[l].astype(bf), w_mla_up[l].astype(bf), w_out[l].astype(bf),
                                   w_r, b_r, S)

        route, counts = _route(logits)
        e_idx = route[:, 0:TOP_K].astype(jnp.int32)
        rank = route[:, TOP_K:2 * TOP_K].astype(jnp.int32)
        cnt = counts[0, :E].astype(jnp.int32)
        padded = ((cnt + MOE_BLOCK - 1) // MOE_BLOCK) * MOE_BLOCK
        pend = jnp.cumsum(padded)
        pstart = pend - padded
        n_items = T * TOP_K
        nblk = -(-n_items // MOE_BLOCK) + E
        cap = nblk * MOE_BLOCK
        dest = (rank + jnp.sum(jnp.where(e_idx[..., None] == jnp.arange(E, dtype=jnp.int32), pstart, 0), axis=-1)
                ).reshape(n_items)
        blk_off = jnp.arange(nblk, dtype=jnp.int32) * MOE_BLOCK
        blk_e = jnp.minimum(jnp.sum((pend[None, :] <= blk_off[:, None]).astype(jnp.int32), axis=-1), E - 1)
        nvalid = (pend[-1:] // MOE_BLOCK).astype(jnp.int32)

        last_blk = jnp.where(padded > 0, pend // MOE_BLOCK - 1, -1)
        tail_blk = nvalid[0] + jnp.arange(nblk - n_items // MOE_BLOCK, dtype=jnp.int32)
        zero_blk = jnp.concatenate([last_blk, jnp.where(tail_blk < nblk, tail_blk, -1)]).astype(jnp.int32)
        xs, slot_row = _dispatch(h2, dest, zero_blk, cap)
        slot_row = slot_row.reshape(nblk, 1, MOE_BLOCK)
        F2 = w1.shape[-1]
        b1p = b1[l].reshape(E, F2 // (2 * LANES), LANES, 2).transpose(0, 1, 3, 2).reshape(E, 1, F2)
        y_k = _moe_ffn(xs, slot_row, n_items, blk_e, nvalid, w1[l], b1p, w2[l], b2[l].reshape(E, 1, D))
        x2 = _combine(y_k, route, x1, mod3, g_ffn_post[l], S)
    return x2.reshape(B, S, D)
```

```python
import functools
import math

import numpy as np
import jax
import jax.numpy as jnp
from jax import lax
from jax.experimental import pallas as pl
from jax.experimental.pallas import tpu as pltpu

GRID_W = 64
NA_HEADS = 8
NA_HEAD_DIM = 64
NA_WIN_ROWS = 8
NA_WIN_COLS = 16
NA_WIDTH = NA_HEADS * NA_HEAD_DIM
MLA_HEADS = 8
MLA_Q_LORA = 384
MLA_KV_LORA = 256
MLA_NOPE = 64
MLA_ROPE = 32
MLA_V = 64
MLA_WIDTH = MLA_HEADS * MLA_V
MLA_KEY_CHUNK = 256
MLA_VROWS = MLA_V + 16
MLA_STEPS_PER_TRIP = 8
ROPE_THETA = 100.0
TOP_K = 4
SWIGLU_ALPHA = 1.702
SWIGLU_LIMIT = 7.0
MOE_BLOCK = 256
RMS_EPS = 1e-6
NEG_BIG = -1e30

LANES = 128
VMEM_LIMIT_BYTES = 56 * 1024 * 1024

Q_LORA_PAD = 512
COL_GATES = 0
NA_GROUP_ROWS = 8
NA_KEY_ROWS = 16


def _f32(x):
    return x.astype(jnp.float32)


def _bf16(x):
    return x.astype(jnp.bfloat16)


def _dot(a, b):
    return jnp.dot(a, b, preferred_element_type=jnp.float32)


def _dot_nt(a, b):
    return lax.dot_general(a, b, (((1,), (1,)), ((), ())), preferred_element_type=jnp.float32)


def _rms(x, g, n=None):
    n = x.shape[-1] if n is None else n
    ms = jnp.sum(x * x, axis=-1, keepdims=True) * (1.0 / n)
    return x * lax.rsqrt(ms + RMS_EPS) * g


SUBLANES = 8


def _store_row_tiles(ref, x, base=0):
    n = x.shape[0]
    for c in range(x.shape[1] // LANES):
        ref[pl.ds(base + c, n, stride=SUBLANES), :] = x[:, c * LANES:(c + 1) * LANES]


def _load_row_tiles(ref, n, base=0):
    chunks = [ref[pl.ds(base + c, n, stride=SUBLANES), :] for c in range(SUBLANES)]
    return jnp.concatenate(chunks, axis=-1)


def _cparams(sem):
    return pltpu.CompilerParams(dimension_semantics=sem, vmem_limit_bytes=VMEM_LIMIT_BYTES)


def _ada_kernel(c_ref, w_ref, b_ref, o_ref):
    c = c_ref[...]
    sc = c * jax.nn.sigmoid(c)
    o_ref[...] = _dot(_bf16(sc), _bf16(w_ref[...])) + b_ref[...]


def _ada_mod(c, w_ada, b_ada):
    B, D = c.shape
    n_out = w_ada.shape[1]
    return pl.pallas_call(
        _ada_kernel,
        out_shape=jax.ShapeDtypeStruct((B, n_out), jnp.float32),
        grid=(n_out // D,),
        in_specs=[
            pl.BlockSpec((B, D), lambda j: (0, 0)),
            pl.BlockSpec((D, D), lambda j: (0, j)),
            pl.BlockSpec((1, D), lambda j: (0, j)),
        ],
        out_specs=pl.BlockSpec((B, D), lambda j: (0, j)),
        compiler_params=_cparams(("arbitrary",)),
        name="ada_mod",
    )(c, w_ada, b_ada.reshape(1, n_out))


def _in_proj_kernel(x_ref, mod_ref, g_ref, w_ref, bg_ref, o_ref, qt_ref, vt_ref, *, d_model, n_gate,
                    chunk, col_q, col_v):
    x = x_ref[...]
    tm = x.shape[0]
    shift = mod_ref[0, :, 0:d_model]
    scale = mod_ref[0, :, d_model:2 * d_model]
    h = _rms(x, g_ref[...]) * (1.0 + scale) + shift
    hb = _bf16(h)
    n_total = w_ref.shape[1]
    n_pairs = NA_HEADS // 2
    for c0 in range(0, n_total, chunk):
        c1 = min(c0 + chunk, n_total)
        acc = _dot(hb, w_ref[:, c0:c1])
        if c0 < n_gate:
            acc = jax.nn.sigmoid(acc + bg_ref[:, c0:c1])
        o_ref[:, c0:c1] = _bf16(acc)
        if c0 == col_q:
            for hp in range(n_pairs):
                qt_ref[0, hp] = _bf16((acc[:, hp * LANES:(hp + 1) * LANES] * NA_HEAD_DIM ** -0.5).T)
        if c0 == col_v:
            for hp in range(n_pairs):
                vt = _bf16(acc[:, hp * LANES:(hp + 1) * LANES].T)
                for cc in range(tm // LANES):
                    vt_ref[0, hp, cc] = vt[:, cc * LANES:(cc + 1) * LANES]


def _in_proj(x2, mod3, g_pre, w_ext, b_gate, seq, col_q, col_v, tm=512):
    T, D = x2.shape
    batch = T // seq
    n_s = seq // tm
    n_total = w_ext.shape[1]
    n_gate = b_gate.shape[-1]
    n_pairs = NA_HEADS // 2
    chunk = n_pairs * LANES
    assert col_q % chunk == 0 and col_v % chunk == 0
    kern = functools.partial(_in_proj_kernel, d_model=D, n_gate=n_gate, chunk=chunk, col_q=col_q, col_v=col_v)
    return pl.pallas_call(
        kern,
        out_shape=(
            jax.ShapeDtypeStruct((T, n_total), jnp.bfloat16),
            jax.ShapeDtypeStruct((batch, n_pairs, LANES, seq), jnp.bfloat16),
            jax.ShapeDtypeStruct((batch, n_pairs, seq // LANES, LANES, LANES), jnp.bfloat16),
        ),
        grid=(T // tm,),
        in_specs=[
            pl.BlockSpec((tm, D), lambda i: (i, 0)),
            pl.BlockSpec((1, 1, mod3.shape[-1]), lambda i: ((i * tm) // seq, 0, 0)),
            pl.BlockSpec((1, D), lambda i: (0, 0)),
            pl.BlockSpec((D, n_total), lambda i: (0, 0)),
            pl.BlockSpec((1, n_gate), lambda i: (0, 0)),
        ],
        out_specs=(
            pl.BlockSpec((tm, n_total), lambda i: (i, 0)),
            pl.BlockSpec((1, n_pairs, LANES, tm), lambda i: (i // n_s, 0, 0, i % n_s)),
            pl.BlockSpec((1, n_pairs, tm // LANES, LANES, LANES), lambda i: (i // n_s, 0, i % n_s, 0, 0)),
        ),
        compiler_params=_cparams(("arbitrary",)),
        name="in_proj",
    )(x2, mod3, g_pre.reshape(1, D), w_ext, b_gate.reshape(1, n_gate))


def _na_class(c):
    half = NA_WIN_ROWS // 2
    if c == 0:
        return (lambda i: max(i - half, 0)), NA_WIN_ROWS - 1
    if c == 1:
        return (lambda i: i), NA_WIN_ROWS - 1 - half
    return (lambda i: min(i + half, NA_KEY_ROWS - NA_WIN_ROWS)), -1


NA_PAIR_KEY_ROWS = 10


def _na_kernel(qt_ref, k_ref, vt_ref, pt_ref, o_ref, *, n_rows):
    g = pl.program_id(2)
    n_groups = n_rows // NA_GROUP_ROWS
    kb_rows = jnp.clip(g * NA_GROUP_ROWS - NA_WIN_ROWS // 2, 0, n_rows - NA_KEY_ROWS)
    nkeys = NA_PAIR_KEY_ROWS * GRID_W
    row = lax.broadcasted_iota(jnp.int32, (LANES, LANES), 0)
    lane = lax.broadcasted_iota(jnp.int32, (GRID_W, LANES), 1)
    neg_blk = jnp.full((GRID_W, LANES), NEG_BIG, jnp.float32)

    def body(c):
        j0, off = _na_class(c)
        n_pair = NA_GROUP_ROWS // 2
        starts = [min(j0(2 * ip) - j0(2 * ip) % 2, NA_KEY_ROWS - NA_PAIR_KEY_ROWS) for ip in range(n_pair)]
        scores = []
        for ip in range(n_pair):
            kstart = pl.multiple_of((kb_rows + starts[ip]) * GRID_W, 2 * GRID_W)
            kwin = k_ref[pl.ds(kstart, nkeys), :]
            qp = qt_ref[0, 0, :, ip * LANES:(ip + 1) * LANES]
            zero = jnp.zeros_like(qp)
            wq = jnp.concatenate([jnp.where(row < NA_HEAD_DIM, qp, zero),
                                  jnp.where(row >= NA_HEAD_DIM, qp, zero)], axis=1)
            scores.append(_dot(kwin, wq))
        outs = []
        for ip in range(n_pair):
            i0 = 2 * ip
            w = starts[ip]
            st = scores[ip]
            blocks = []
            for jw in range(NA_PAIR_KEY_ROWS):
                j = w + jw
                v0 = j0(i0) <= j < j0(i0) + NA_WIN_ROWS
                v1 = j0(i0 + 1) <= j < j0(i0 + 1) + NA_WIN_ROWS
                dr0 = j - i0 + off
                halves = []
                for hl in range(2):
                    if not (v0 or v1):
                        halves.append(neg_blk)
                        continue
                    blk = st[jw * GRID_W:(jw + 1) * GRID_W, hl * LANES:(hl + 1) * LANES] + pt_ref[hl, dr0]
                    if not (v0 and v1):
                        keep = (lane < GRID_W) if v0 else (lane >= GRID_W)
                        blk = jnp.where(keep, blk, NEG_BIG)
                    halves.append(blk)
                blocks.append(jnp.concatenate(halves, axis=1))
            s = jnp.concatenate(blocks, axis=0)
            m = jnp.max(s, axis=0, keepdims=True)
            p = jnp.exp(s - m)
            l = jnp.sum(p, axis=0, keepdims=True)
            c0 = (kb_rows + w) // 2
            vwin = jnp.concatenate([vt_ref[0, 0, c0 + u] for u in range(NA_PAIR_KEY_ROWS // 2)], axis=1)
            on = _dot(vwin, _bf16(p)) / l
            outs.append(jnp.where(row < NA_HEAD_DIM, on[:, 0:LANES], on[:, LANES:2 * LANES]))
        o_ref[...] = _bf16(jnp.concatenate(outs, axis=1).T)

    cls = jnp.where(g == 0, 0, jnp.where(g == n_groups - 1, 2, 1))
    for c in range(3):
        pl.when(cls == c)(functools.partial(body, c))


def _na_attention(proj, q_t, v_t, pair_tab, batch, seq, col_k):
    T = proj.shape[0]
    n_rows = seq // GRID_W
    n_groups = n_rows // NA_GROUP_ROWS
    tq = NA_GROUP_ROWS * GRID_W
    n_pairs = NA_HEADS // 2
    kern = functools.partial(_na_kernel, n_rows=n_rows)
    return pl.pallas_call(
        kern,
        out_shape=jax.ShapeDtypeStruct((T, NA_WIDTH), jnp.bfloat16),
        grid=(n_pairs, batch, n_groups),
        in_specs=[
            pl.BlockSpec((1, 1, LANES, tq), lambda hp, b, g: (b, hp, 0, g)),
            pl.BlockSpec((seq, LANES), lambda hp, b, g: (b, col_k // LANES + hp)),
            pl.BlockSpec((1, 1, seq // LANES, LANES, LANES), lambda hp, b, g: (b, hp, 0, 0, 0)),
            pl.BlockSpec((2,) + pair_tab.shape[1:], lambda hp, b, g: (hp, 0, 0, 0)),
        ],
        out_specs=pl.BlockSpec((tq, LANES), lambda hp, b, g: (b * n_groups + g, hp)),
        compiler_params=_cparams(("arbitrary", "arbitrary", "arbitrary")),
        name="na_attn",
    )(q_t, proj, v_t, pair_tab)


def _na_pair_tables(rpb):
    H, n_dr, n_dc = rpb.shape
    kw = NA_WIN_COLS
    W = GRID_W
    ring = 2 * W - 1
    rp = jnp.pad(rpb.astype(jnp.float32), ((0, 0), (0, 0), (W - kw, W - kw)), constant_values=NEG_BIG)
    u = jnp.roll(rp, -(W - 1), axis=-1)
    circ = jnp.tile(u, (1, 1, W))[..., :W * (ring - 1)].reshape(H, n_dr, W, ring - 1)
    toep = circ[..., :W]
    cidx = np.arange(W)
    col_start = np.clip(cidx - kw // 2, 0, W - kw)
    col_in = (cidx[None, :] >= col_start[:, None]) & (cidx[None, :] < col_start[:, None] + kw)
    toep_t = jnp.where(jnp.asarray(col_in.T)[None, None], jnp.swapaxes(toep, 2, 3), NEG_BIG)
    neg = jnp.full((H, 1, W, W), NEG_BIG, jnp.float32)
    ext = jnp.concatenate([neg, toep_t, neg], axis=1)
    return jnp.concatenate([ext[:, 1:], ext[:, :-1]], axis=-1)


def _mla_prep_kernel(cq_ref, ckv_ref, kr_ref, gq_ref, gkv_ref, wqm_ref, wqs_ref, wk_ref, wv_ref,
                     pm_ref, tab_ref, q_out, k_out, v_out, *, qscale):
    cq = _f32(cq_ref[...])
    cqn = _bf16(_rms(cq, gq_ref[...], n=MLA_Q_LORA))
    qm = _dot(cqn, wqm_ref[...])
    qs = _dot(cqn, wqs_ref[...])
    ctab = tab_ref[:, 0:LANES]
    stab = tab_ref[:, LANES:2 * LANES]
    for h in range(MLA_HEADS):
        sl = slice(h * LANES, (h + 1) * LANES)
        qh = (qm[:, sl] * ctab + qs[:, sl] * stab) * qscale
        q_out[0, h] = _bf16(qh.T)

    ckv = _f32(ckv_ref[...])
    ckvn = _bf16(_rms(ckv, gkv_ref[...]))
    kk = _dot(ckvn, wk_ref[...])
    vv = _dot(ckvn, wv_ref[...])
    n_chunk = v_out.shape[2]
    ones = jnp.ones((MLA_VROWS - MLA_V, MLA_KEY_CHUNK), jnp.bfloat16)
    for hp in range(MLA_HEADS // 2):
        vt = _bf16(vv[:, hp * LANES:(hp + 1) * LANES].T)
        for cc in range(n_chunk):
            ks = slice(cc * MLA_KEY_CHUNK, (cc + 1) * MLA_KEY_CHUNK)
            for h in range(2):
                r0 = h * MLA_VROWS
                v_out[0, hp, cc, r0:r0 + MLA_V, :] = vt[h * MLA_V:(h + 1) * MLA_V, ks]
                v_out[0, hp, cc, r0 + MLA_V:r0 + MLA_VROWS, :] = ones
    krr = _f32(kr_ref[...]) * tab_ref[:, 2 * LANES:3 * LANES]
    kplace = _dot(_bf16(krr), pm_ref[...])
    for h in range(MLA_HEADS):
        sl = slice(h * LANES, (h + 1) * LANES)
        k_out[:, sl] = _bf16(kk[:, sl] + kplace)


def _mla_prep(proj, col_cq, col_ckv, col_kr, gq, gkv, wqm, wqs, wk, wv, pm, rope_tab, seq, qscale, tm=512):
    T = proj.shape[0]
    batch = T // seq
    n_s = seq // tm
    n_pairs = MLA_HEADS // 2
    cpt = tm // MLA_KEY_CHUNK
    full = lambda a: pl.BlockSpec(a.shape, lambda i: (0,) * a.ndim)
    tab = pl.BlockSpec((tm, 3 * LANES), lambda i: (i % n_s, 0))
    kern = functools.partial(_mla_prep_kernel, qscale=qscale)
    return pl.pallas_call(
        kern,
        out_shape=(
            jax.ShapeDtypeStruct((batch, MLA_HEADS, LANES, seq), jnp.bfloat16),
            jax.ShapeDtypeStruct((T, MLA_HEADS * LANES), jnp.bfloat16),
            jax.ShapeDtypeStruct((batch, n_pairs, seq // MLA_KEY_CHUNK, 2 * MLA_VROWS, MLA_KEY_CHUNK),
                                 jnp.bfloat16),
        ),
        grid=(T // tm,),
        in_specs=[
            pl.BlockSpec((tm, Q_LORA_PAD), lambda i: (i, col_cq // Q_LORA_PAD)),
            pl.BlockSpec((tm, MLA_KV_LORA), lambda i: (i, col_ckv // MLA_KV_LORA)),
            pl.BlockSpec((tm, LANES), lambda i: (i, col_kr // LANES)),
            full(gq), full(gkv), full(wqm), full(wqs), full(wk), full(wv), full(pm),
            tab,
        ],
        out_specs=(
            pl.BlockSpec((1, MLA_HEADS, LANES, tm), lambda i: (i // n_s, 0, 0, i % n_s)),
            pl.BlockSpec((tm, MLA_HEADS * LANES), lambda i: (i, 0)),
            pl.BlockSpec((1, n_pairs, cpt, 2 * MLA_VROWS, MLA_KEY_CHUNK),
                         lambda i: (i // n_s, 0, i % n_s, 0, 0)),
        ),
        compiler_params=_cparams(("arbitrary",)),
        name="mla_prep",
    )(proj, proj, proj, gq, gkv, wqm, wqs, wk, wv, pm, rope_tab)


def _mla_attn_kernel(qt_ref, k_ref, vt_ref, o_ref, m_scr, acc_scr, st_scr):
    n_chunks = vt_ref.shape[2]
    tk = MLA_KEY_CHUNK
    m_scr[...] = jnp.full(m_scr.shape, -jnp.inf, jnp.float32)
    acc_scr[...] = jnp.zeros(acc_scr.shape, jnp.float32)

    def scores(c, slot):
        k0 = pl.multiple_of(c * tk, tk)
        for h in range(2):
            kc = k_ref[pl.ds(k0, tk), h * LANES:(h + 1) * LANES]
            st_scr[slot, h] = _dot(kc, qt_ref[0, h])

    def step(c, slot):
        scores(jnp.minimum(c + 1, n_chunks - 1), 1 - slot)
        vt = vt_ref[0, 0, c]
        pts, alphas = [], []
        for h in range(2):
            st = st_scr[slot, h]
            m_old = m_scr[h]
            m_new = jnp.maximum(m_old, jnp.max(st, axis=0, keepdims=True))
            alpha = jnp.exp2(m_old - m_new)
            pt = jnp.exp2(st - m_new)
            m_scr[h] = m_new
            pts.append(_bf16(pt))
            alphas.append(alpha)
        for h in range(2):
            acc_scr[h] = alphas[h] * acc_scr[h] + _dot(vt[h * MLA_VROWS:(h + 1) * MLA_VROWS, :], pts[h])

    def body(j, carry):
        for u in range(MLA_STEPS_PER_TRIP):
            step(MLA_STEPS_PER_TRIP * j + u, u % 2)
        return carry

    scores(0, 0)
    lax.fori_loop(0, n_chunks // MLA_STEPS_PER_TRIP, body, 0)
    outs = []
    for h in range(2):
        acc = acc_scr[h]
        outs.append(acc[0:MLA_V] / acc[MLA_V:MLA_V + 1])
    o_ref[...] = _bf16(jnp.concatenate(outs, axis=0).T)


def _mla_attention(q_t, k_cat, v_t, batch, seq, tq=512):
    T = k_cat.shape[0]
    n_q = seq // tq
    n_pairs = MLA_HEADS // 2
    n_chunks = seq // MLA_KEY_CHUNK
    assert n_chunks % MLA_STEPS_PER_TRIP == 0
    return pl.pallas_call(
        _mla_attn_kernel,
        out_shape=jax.ShapeDtypeStruct((T, MLA_WIDTH), jnp.bfloat16),
        grid=(batch, n_pairs, n_q),
        in_specs=[
            pl.BlockSpec((1, 2, LANES, tq), lambda b, hp, i: (b, hp, 0, i)),
            pl.BlockSpec((seq, 2 * LANES), lambda b, hp, i: (b, hp)),
            pl.BlockSpec((1, 1, n_chunks, 2 * MLA_VROWS, MLA_KEY_CHUNK), lambda b, hp, i: (b, hp, 0, 0, 0)),
        ],
        out_specs=pl.BlockSpec((tq, LANES), lambda b, hp, i: (b * n_q + i, hp)),
        scratch_shapes=[
            pltpu.VMEM((2, 1, tq), jnp.float32),
            pltpu.VMEM((2, MLA_VROWS, tq), jnp.float32),
            pltpu.VMEM((2, 2, MLA_KEY_CHUNK, tq), jnp.float32),
        ],
        compiler_params=_cparams(("arbitrary", "arbitrary", "arbitrary")),
        name="mla_attn",
    )(q_t, k_cat, v_t)


def _out_proj_kernel(ona_ref, omla_ref, gate_ref, x_ref, mod_ref, gpost_ref, gpre_ref, wna_ref,
                     wmla_ref, wout_ref, wr_ref, br_ref, x1_ref, h2_ref, lg_ref, *, d_model):
    D = d_model
    tm = x_ref.shape[0]
    gate_a = mod_ref[0, :, 2 * D:3 * D]
    shift_f = mod_ref[0, :, 3 * D:4 * D]
    scale_f = mod_ref[0, :, 4 * D:5 * D]
    halves = [slice(0, tm // 2), slice(tm // 2, tm)]
    merged = []
    for rs in halves:
        up_na = _dot(ona_ref[rs, :], wna_ref[...])
        up_mla = _dot(omla_ref[rs, :], wmla_ref[...])
        merged.append(_bf16(_f32(gate_ref[rs, 0:D]) * up_na + _f32(gate_ref[rs, D:2 * D]) * up_mla))
    ys = [_dot(m, wout_ref[...]) for m in merged]
    h2s = []
    for rs, y in zip(halves, ys):
        x1 = x_ref[rs, :] + gate_a * _rms(y, gpost_ref[...])
        x1_ref[rs, :] = x1
        h2 = _rms(x1, gpre_ref[...]) * (1.0 + scale_f) + shift_f
        _store_row_tiles(h2_ref, h2, base=rs.start * SUBLANES)
        h2s.append(_bf16(h2))
    for rs, hb in zip(halves, h2s):
        lg_ref[rs, :] = _dot(hb, wr_ref[...]) + br_ref[...]


def _out_proj(o_na, o_mla, proj, x2, mod3, g_post, g_pre, w_na, w_mla, w_out, w_r, b_r, seq, tm=512):
    T, D = x2.shape
    full = lambda a: pl.BlockSpec(a.shape, lambda i: (0,) * a.ndim)
    row = lambda w: pl.BlockSpec((tm, w), lambda i: (i, 0))
    kern = functools.partial(_out_proj_kernel, d_model=D)
    g_post = g_post.reshape(1, D)
    g_pre = g_pre.reshape(1, D)
    return pl.pallas_call(
        kern,
        out_shape=(
            jax.ShapeDtypeStruct((T, D), jnp.float32),
            jax.ShapeDtypeStruct((T * (D // LANES), LANES), jnp.float32),
            jax.ShapeDtypeStruct((T, LANES), jnp.float32),
        ),
        grid=(T // tm,),
        in_specs=[
            row(NA_WIDTH), row(MLA_WIDTH),
            pl.BlockSpec((tm, 2 * D), lambda i: (i, COL_GATES // (2 * D))),
            row(D),
            pl.BlockSpec((1, 1, mod3.shape[-1]), lambda i: ((i * tm) // seq, 0, 0)),
            full(g_post), full(g_pre), full(w_na), full(w_mla), full(w_out), full(w_r), full(b_r),
        ],
        out_specs=(row(D), pl.BlockSpec((tm * (D // LANES), LANES), lambda i: (i, 0)), row(LANES)),
        compiler_params=_cparams(("arbitrary",)),
        name="out_proj",
    )(o_na, o_mla, proj, x2, mod3, g_post, g_pre, w_na, w_mla, w_out, w_r, b_r)


def _route_kernel(lg_ref, r_ref, cnt_ref, carry_scr, *, sub):
    @pl.when(pl.program_id(0) == 0)
    def _():
        carry_scr[...] = jnp.zeros(carry_scr.shape, jnp.float32)

    tr = lg_ref.shape[0]
    lane = lax.broadcasted_iota(jnp.int32, (sub, LANES), 1).astype(jnp.float32)
    ri = lax.broadcasted_iota(jnp.int32, (sub, sub), 0)
    ci = lax.broadcasted_iota(jnp.int32, (sub, sub), 1)
    tri = jnp.where(ri >= ci, 1.0, 0.0).astype(jnp.bfloat16)
    for s0 in range(0, tr, sub):
        work = lg_ref[s0:s0 + sub, :]
        sels, vals, idxs = [], [], []
        for _k in range(TOP_K):
            mk = jnp.max(work, axis=-1, keepdims=True)
            ik = jnp.min(jnp.where(work == mk, lane, float(LANES)), axis=-1, keepdims=True)
            sk = lane == ik
            work = jnp.where(sk, -jnp.inf, work)
            sels.append(sk)
            vals.append(mk)
            idxs.append(ik)
        es = [jnp.exp(v - vals[0]) for v in vals]
        denom = es[0] + es[1] + es[2] + es[3]
        onehot = jnp.zeros((sub, LANES), jnp.float32)
        for sk in sels:
            onehot = jnp.where(sk, 1.0, onehot)
        prefix = _dot(tri, _bf16(onehot))
        carry = carry_scr[...]
        rank_mat = carry + prefix - 1.0
        res = jnp.zeros((sub, LANES), jnp.float32)
        for kk in range(TOP_K):
            rank_k = jnp.sum(jnp.where(sels[kk], rank_mat, 0.0), axis=-1, keepdims=True)
            res = jnp.where(lane == kk, idxs[kk], res)
            res = jnp.where(lane == TOP_K + kk, rank_k, res)
            res = jnp.where(lane == 2 * TOP_K + kk, es[kk] / denom, res)
        r_ref[s0:s0 + sub, :] = res
        carry_scr[...] = carry + jnp.sum(onehot, axis=0, keepdims=True)
    cnt_ref[...] = carry_scr[...]


def _route(logits, tr=2048, sub=256):
    T = logits.shape[0]
    kern = functools.partial(_route_kernel, sub=sub)
    return pl.pallas_call(
        kern,
        out_shape=(
            jax.ShapeDtypeStruct((T, LANES), jnp.float32),
            jax.ShapeDtypeStruct((1, LANES), jnp.float32),
        ),
        grid=(T // tr,),
        in_specs=[pl.BlockSpec((tr, LANES), lambda i: (i, 0))],
        out_specs=(
            pl.BlockSpec((tr, LANES), lambda i: (i, 0)),
            pl.BlockSpec((1, LANES), lambda i: (0, 0)),
        ),
        scratch_shapes=[pltpu.VMEM((1, LANES), jnp.float32)],
        compiler_params=_cparams(("arbitrary",)),
        name="route",
    )(logits)


def _dispatch_kernel(zero_blk_ref, dest_ref, h_ref, row0_ref, xs_ref, row_ref, zbuf, sem, fill_sem,
                     *, n_tokens):
    i = pl.program_id(0)
    ts = h_ref.shape[0] // SUBLANES
    blk_rows = MOE_BLOCK * SUBLANES

    @pl.when(i == 0)
    def _():
        fill_row = pltpu.make_async_copy(row0_ref, row_ref, fill_sem.at[0])
        fill_row.start()
        zbuf[...] = jnp.zeros(zbuf.shape, zbuf.dtype)
        n_zero = zero_blk_ref.shape[0]

        def zero_copy(z):
            r0 = pl.multiple_of(zero_blk_ref[z] * blk_rows, blk_rows)
            return pltpu.make_async_copy(zbuf, xs_ref.at[pl.ds(r0, blk_rows), :], fill_sem.at[1])

        for z in range(n_zero):
            pl.when(zero_blk_ref[z] >= 0)(lambda z=z: zero_copy(z).start())
        for z in range(n_zero):
            pl.when(zero_blk_ref[z] >= 0)(lambda z=z: zero_copy(z).wait())
        fill_row.wait()

    def row_copy(j, d):
        dst = pl.ds(pl.multiple_of(d * SUBLANES, SUBLANES), SUBLANES)
        return pltpu.make_async_copy(h_ref.at[pl.ds(j * SUBLANES, SUBLANES), :], xs_ref.at[dst, :], sem)

    t0 = i * ts
    for j in range(ts):
        for kk in range(TOP_K):
            d = dest_ref[j * TOP_K + kk]
            row_copy(j, d).start(priority=kk % 2)
            row_ref[d] = (kk * n_tokens + j) + t0
    for _j in range(ts * TOP_K):
        row_copy(0, 0).wait()


def _dispatch(h2_tiles, dest, zero_blk, cap, ts=256):
    T = h2_tiles.shape[0] // SUBLANES
    kern = functools.partial(_dispatch_kernel, n_tokens=T)
    grid_spec = pltpu.PrefetchScalarGridSpec(
        num_scalar_prefetch=1,
        grid=(T // ts,),
        in_specs=[
            pl.BlockSpec((ts * TOP_K,), lambda i, zb: (i,), memory_space=pltpu.SMEM),
            pl.BlockSpec((ts * SUBLANES, LANES), lambda i, zb: (i, 0)),
            pl.BlockSpec(memory_space=pl.ANY),
        ],
        out_specs=(pl.BlockSpec(memory_space=pl.ANY), pl.BlockSpec(memory_space=pltpu.SMEM)),
        scratch_shapes=[
            pltpu.VMEM((MOE_BLOCK * SUBLANES, LANES), h2_tiles.dtype),
            pltpu.SemaphoreType.DMA(()),
            pltpu.SemaphoreType.DMA((2,)),
        ],
    )
    return pl.pallas_call(
        kern,
        out_shape=(jax.ShapeDtypeStruct((cap * SUBLANES, LANES), h2_tiles.dtype),
                   jax.ShapeDtypeStruct((cap,), jnp.int32)),
        grid_spec=grid_spec,
        compiler_params=_cparams(("arbitrary",)),
        name="dispatch",
    )(zero_blk, dest, h2_tiles, jnp.full((cap,), -1, jnp.int32))


def _moe_kernel(blk_e_ref, nvalid_ref, row_prev_ref, row_ref, x_ref, w1_ref, b1_ref, w2_ref, b2_ref,
                out_ref, w1p_scr, w2b_scr, ybuf0, ybuf1, ssem, *, n_items):
    i = pl.program_id(0)
    nvalid = nvalid_ref[0]
    d_ff = w2_ref.shape[1]
    n_groups = (2 * d_ff) // (2 * LANES)

    def tile(r):
        return pl.ds(pl.multiple_of(r * SUBLANES, SUBLANES), SUBLANES)

    ybufs = (ybuf0, ybuf1)

    def scatter_copy(j, r, slot):
        return pltpu.make_async_copy(ybufs[slot].at[pl.ds(j * SUBLANES, SUBLANES), :], out_ref.at[tile(r), :],
                                     ssem.at[slot])

    def wait_scatters(slot):
        for _j in range(MOE_BLOCK):
            scatter_copy(0, 0, slot).wait()

    def scatter_rows(rows_ref, slot):
        spare0 = n_items + slot * MOE_BLOCK
        for j in range(MOE_BLOCK):
            r = rows_ref[0, 0, j]
            scatter_copy(j, jnp.where(r >= 0, r, spare0 + j), slot).start(priority=j % 2)

    @pl.when(i == 0)
    def _():
        ybuf0[...] = jnp.zeros(ybuf0.shape, ybuf0.dtype)
        ybuf1[...] = jnp.zeros(ybuf1.shape, ybuf1.dtype)
        for j in range(MOE_BLOCK):
            scatter_copy(j, n_items + j, 0).start(priority=j % 2)

    @pl.when(i < nvalid)
    def _():
        e = blk_e_ref[i]
        e_prev = blk_e_ref[jnp.maximum(i - 1, 0)]

        @pl.when((i == 0) | (e != e_prev))
        def _():
            r = lax.broadcasted_iota(jnp.int32, (2 * LANES, 2 * LANES), 0)
            c = lax.broadcasted_iota(jnp.int32, (2 * LANES, 2 * LANES), 1)
            src = jnp.where(c < LANES, 2 * c, 2 * (c - LANES) + 1)
            perm = jnp.where(r == src, 1.0, 0.0).astype(jnp.bfloat16)
            for gI in range(n_groups):
                sl = slice(gI * 2 * LANES, (gI + 1) * 2 * LANES)
                w1p_scr[:, sl] = _bf16(_dot(_bf16(w1_ref[0, :, sl]), perm))
            w2b_scr[...] = _bf16(w2_ref[0])

        def block_step(slot):
            wait_scatters(slot)
            scatter_rows(row_prev_ref, 1 - slot)

            xb = _bf16(_load_row_tiles(x_ref, MOE_BLOCK))
            hcat = _dot(xb, w1p_scr[...]) + b1_ref[0]
            acts = []
            for gI in range(n_groups):
                glu = jnp.minimum(hcat[:, gI * 2 * LANES: gI * 2 * LANES + LANES], SWIGLU_LIMIT)
                lin = jnp.clip(hcat[:, gI * 2 * LANES + LANES: (gI + 1) * 2 * LANES],
                               -SWIGLU_LIMIT, SWIGLU_LIMIT)
                acts.append(_bf16(glu * jax.nn.sigmoid(SWIGLU_ALPHA * glu) * (lin + 1.0)))
            act = jnp.concatenate(acts, axis=-1)
            _store_row_tiles(ybufs[slot], _dot(act, w2b_scr[...]) + b2_ref[0])

            @pl.when(i == nvalid - 1)
            def _():
                scatter_rows(row_ref, slot)
                wait_scatters(1 - slot)
                wait_scatters(slot)

        parity = lax.rem(i, 2)
        for slot in range(2):
            pl.when(parity == slot)(functools.partial(block_step, slot))


def _moe_ffn(xs, slot_row, n_items, blk_e, nvalid, w1, b1p, w2, b2):
    D = w1.shape[1]
    assert D == SUBLANES * LANES
    E, _, F2 = w1.shape
    F = w2.shape[1]
    nblk = slot_row.shape[0]

    def cur(i, be, nv):
        return (jnp.minimum(i, nv[0] - 1), 0, 0)

    def prv(i, be, nv):
        return (jnp.where(i == 0, nblk, jnp.maximum(jnp.minimum(i, nv[0] - 1) - 1, 0)), 0, 0)

    slot_row = jnp.concatenate([slot_row, jnp.full((1, 1, MOE_BLOCK), -1, slot_row.dtype)], axis=0)
    slot_spec = lambda f: pl.BlockSpec((1, 1, MOE_BLOCK), f, memory_space=pltpu.SMEM)
    grid_spec = pltpu.PrefetchScalarGridSpec(
        num_scalar_prefetch=2,
        grid=(nblk,),
        in_specs=[
            slot_spec(prv), slot_spec(cur),
            pl.BlockSpec((MOE_BLOCK * SUBLANES, LANES), lambda i, be, nv: (jnp.minimum(i, nv[0] - 1), 0)),
            pl.BlockSpec((1, D, F2), lambda i, be, nv: (be[i], 0, 0)),
            pl.BlockSpec((1, 1, F2), lambda i, be, nv: (be[i], 0, 0)),
            pl.BlockSpec((1, F, D), lambda i, be, nv: (be[i], 0, 0)),
            pl.BlockSpec((1, 1, D), lambda i, be, nv: (be[i], 0, 0)),
        ],
        out_specs=pl.BlockSpec(memory_space=pl.ANY),
        scratch_shapes=[
            pltpu.VMEM((D, F2), jnp.bfloat16),
            pltpu.VMEM((F, D), jnp.bfloat16),
            pltpu.VMEM((MOE_BLOCK * SUBLANES, LANES), jnp.float32),
            pltpu.VMEM((MOE_BLOCK * SUBLANES, LANES), jnp.float32),
            pltpu.SemaphoreType.DMA((2,)),
        ],
    )
    kern = functools.partial(_moe_kernel, n_items=n_items)
    return pl.pallas_call(
        kern,
        out_shape=jax.ShapeDtypeStruct(((n_items + 2 * MOE_BLOCK) * SUBLANES, LANES), jnp.float32),
        grid_spec=grid_spec,
        compiler_params=_cparams(("arbitrary",)),
        name="moe_ffn",
    )(blk_e, nvalid, slot_row, slot_row, xs, w1, b1p, w2, b2)


def _combine_kernel(y0_ref, y1_ref, y2_ref, y3_ref, r_ref, x1_ref, mod_ref, g_ref, o_ref, *, d_model):
    D = d_model
    tc = x1_ref.shape[0]
    r = r_ref[...]
    moe = None
    for kk, y_ref in enumerate((y0_ref, y1_ref, y2_ref, y3_ref)):
        term = r[:, 2 * TOP_K + kk: 2 * TOP_K + kk + 1] * _load_row_tiles(y_ref, tc)
        moe = term if moe is None else moe + term
    gate_f = mod_ref[0, :, 5 * D:6 * D]
    o_ref[...] = x1_ref[...] + gate_f * _rms(moe, g_ref[...])


def _combine(y_k, route, x1, mod3, g_post, seq, tc=512):
    T, D = x1.shape
    n_t = T // tc
    kern = functools.partial(_combine_kernel, d_model=D)
    y_spec = lambda kk: pl.BlockSpec((tc * SUBLANES, LANES), lambda i: (kk * n_t + i, 0))
    return pl.pallas_call(
        kern,
        out_shape=jax.ShapeDtypeStruct((T, D), jnp.float32),
        grid=(n_t,),
        in_specs=[
            y_spec(0), y_spec(1), y_spec(2), y_spec(3),
            pl.BlockSpec((tc, LANES), lambda i: (i, 0)),
            pl.BlockSpec((tc, D), lambda i: (i, 0)),
            pl.BlockSpec((1, 1, mod3.shape[-1]), lambda i: ((i * tc) // seq, 0, 0)),
            pl.BlockSpec((1, D), lambda i: (0, 0)),
        ],
        out_specs=pl.BlockSpec((tc, D), lambda i: (i, 0)),
        compiler_params=_cparams(("arbitrary",)),
        name="combine",
    )(y_k, y_k, y_k, y_k, route, x1, mod3, g_post.reshape(1, D))


def _rope_swap_cols(w):
    nf = MLA_ROPE // 4
    return jnp.concatenate([-w[..., nf:2 * nf], w[..., 0:nf], -w[..., 3 * nf:4 * nf], w[..., 2 * nf:3 * nf]],
                           axis=-1)


def _rope_tables(seq):
    n_rows = seq // GRID_W
    nf = MLA_ROPE // 4
    inv = ROPE_THETA ** (-jnp.arange(nf, dtype=jnp.float32) / nf)
    ar = jnp.arange(n_rows, dtype=jnp.float32)[:, None] * inv
    ac = jnp.arange(GRID_W, dtype=jnp.float32)[:, None] * inv
    per_row = lambda a: jnp.broadcast_to(a[:, None, :], (n_rows, GRID_W, nf)).reshape(seq, nf)
    per_col = lambda a: jnp.broadcast_to(a[None, :, :], (n_rows, GRID_W, nf)).reshape(seq, nf)
    cr, sr, cc, sc = per_row(jnp.cos(ar)), per_row(jnp.sin(ar)), per_col(jnp.cos(ac)), per_col(jnp.sin(ac))
    c32 = jnp.concatenate([cr, cr, cc, cc], axis=-1)
    s32 = jnp.concatenate([sr, sr, sc, sc], axis=-1)
    one = jnp.ones((seq, MLA_NOPE), jnp.float32)
    z64 = jnp.zeros((seq, MLA_NOPE), jnp.float32)
    z32 = jnp.zeros((seq, LANES - MLA_NOPE - MLA_ROPE), jnp.float32)
    return jnp.concatenate([one, c32, z32, z64, s32, z32, c32, s32, z64], axis=-1)


def kernel(x, c, w_ada, b_ada, g_attn_pre, g_attn_post, w_in, b_gate, na_rpb, q_norm_g, kv_norm_g,
           w_uq, w_ukv, w_na_up, w_mla_up, w_out, g_ffn_pre, g_ffn_post, w_router, b_router,
           w1, b1, w2, b2):
    B, S, D = x.shape
    T = B * S
    depth = w_ada.shape[0]
    E = w_router.shape[-1]
    n_rows = S // GRID_W
    assert S % (NA_GROUP_ROWS * GRID_W) == 0 and n_rows >= NA_KEY_ROWS
    assert E <= LANES and T % 2048 == 0

    bf = jnp.bfloat16
    rope_tab = _rope_tables(S)
    pm_np = np.zeros((LANES, LANES), np.float32)
    for cidx in range(MLA_ROPE):
        pm_np[cidx, MLA_NOPE + cidx] = 1.0
        pm_np[MLA_ROPE + cidx, MLA_NOPE + cidx] = 1.0
    pm = jnp.asarray(pm_np, bf)
    qscale = float((MLA_NOPE + MLA_ROPE) ** -0.5 * math.log2(math.e))

    col_na = 2 * D
    col_cq = col_na + 3 * NA_WIDTH
    col_ckv = col_cq + Q_LORA_PAD
    col_kr = col_ckv + MLA_KV_LORA
    assert col_cq % Q_LORA_PAD == 0 and col_ckv % MLA_KV_LORA == 0

    x2 = x.reshape(T, D)
    for l in range(depth):
        mod = _ada_mod(c, w_ada[l], b_ada[l])
        mod3 = mod.reshape(B, 1, 6 * D)

        o_na, o_cq, o_ckv, o_kr, o_g = np.cumsum([0, 3 * NA_WIDTH, MLA_Q_LORA, MLA_KV_LORA, MLA_ROPE]).tolist()
        wi = w_in[l]
        w_kr = wi[:, o_kr:o_g]
        w_ext = jnp.concatenate([
            wi[:, o_g:o_g + 2 * D],
            wi[:, o_na:o_cq],
            wi[:, o_cq:o_ckv], jnp.zeros((D, Q_LORA_PAD - MLA_Q_LORA), wi.dtype),
            wi[:, o_ckv:o_kr],
            w_kr, _rope_swap_cols(w_kr), jnp.zeros((D, LANES - 2 * MLA_ROPE), wi.dtype),
        ], axis=1).astype(bf)
        proj, na_qt, na_vt = _in_proj(x2, mod3, g_attn_pre[l], w_ext, b_gate[l], S,
                                      col_na, col_na + 2 * NA_WIDTH)

        o_na_tok = _na_attention(proj, na_qt, na_vt, _na_pair_tables(na_rpb[l]), B, S, col_na + NA_WIDTH)

        wq = w_uq[l]
        zq = jnp.zeros((MLA_Q_LORA, MLA_HEADS, LANES - MLA_NOPE - MLA_ROPE), wq.dtype)
        wqm = jnp.concatenate([wq, zq], axis=-1).reshape(MLA_Q_LORA, MLA_HEADS * LANES)
        wqs = jnp.concatenate([jnp.zeros((MLA_Q_LORA, MLA_HEADS, MLA_NOPE), wq.dtype),
                               _rope_swap_cols(wq[..., MLA_NOPE:]), zq], axis=-1
                              ).reshape(MLA_Q_LORA, MLA_HEADS * LANES)
        rpad = ((0, Q_LORA_PAD - MLA_Q_LORA), (0, 0))
        wqm = jnp.pad(wqm, rpad).astype(bf)
        wqs = jnp.pad(wqs, rpad).astype(bf)
        wkv = w_ukv[l]
        wk = jnp.concatenate([wkv[..., :MLA_NOPE], jnp.zeros((MLA_KV_LORA, MLA_HEADS, LANES - MLA_NOPE), wkv.dtype)],
                             axis=-1).reshape(MLA_KV_LORA, MLA_HEADS * LANES).astype(bf)
        wv = wkv[..., MLA_NOPE:].reshape(MLA_KV_LORA, MLA_WIDTH).astype(bf)
        gq = jnp.pad(q_norm_g[l], (0, Q_LORA_PAD - MLA_Q_LORA)).reshape(1, Q_LORA_PAD)
        gkv = kv_norm_g[l].reshape(1, MLA_KV_LORA)
        q_cat, k_cat, v_mla = _mla_prep(proj, col_cq, col_ckv, col_kr, gq, gkv, wqm, wqs, wk, wv, pm,
                                        rope_tab, S, qscale)
        o_mla_tok = _mla_attention(q_cat, k_cat, v_mla, B, S)

        w_r = jnp.pad(w_router[l], ((0, 0), (0, LANES - E))).astype(bf)
        b_r = jnp.concatenate([b_router[l], jnp.full((LANES - E,), NEG_BIG, jnp.float32)]).reshape(1, LANES)
        x1, h2, logits = _out_proj(o_na_tok, o_mla_tok, proj, x2, mod3, g_attn_post[l], g_ffn_pre[l],
                                   w_na_up[l].astype(bf), w_mla_up[l].astype(bf), w_out[l].astype(bf),
                                   w_r, b_r, S)

        route, counts = _route(logits)
        e_idx = route[:, 0:TOP_K].astype(jnp.int32)
        rank = route[:, TOP_K:2 * TOP_K].astype(jnp.int32)
        cnt = counts[0, :E].astype(jnp.int32)
        padded = ((cnt + MOE_BLOCK - 1) // MOE_BLOCK) * MOE_BLOCK
        pend = jnp.cumsum(padded)
        pstart = pend - padded
        n_items = T * TOP_K
        nblk = -(-n_items // MOE_BLOCK) + E
        cap = nblk * MOE_BLOCK
        dest = (rank + jnp.sum(jnp.where(e_idx[..., None] == jnp.arange(E, dtype=jnp.int32), pstart, 0), axis=-1)
                ).reshape(n_items)
        blk_off = jnp.arange(nblk, dtype=jnp.int32) * MOE_BLOCK
        blk_e = jnp.minimum(jnp.sum((pend[None, :] <= blk_off[:, None]).astype(jnp.int32), axis=-1), E - 1)
        nvalid = (pend[-1:] // MOE_BLOCK).astype(jnp.int32)

        last_blk = jnp.where(padded > 0, pend // MOE_BLOCK - 1, -1)
        tail_blk = nvalid[0] + jnp.arange(nblk - n_items // MOE_BLOCK, dtype=jnp.int32)
        zero_blk = jnp.concatenate([last_blk, jnp.where(tail_blk < nblk, tail_blk, -1)]).astype(jnp.int32)
        xs, slot_row = _dispatch(h2, dest, zero_blk, cap)
        slot_row = slot_row.reshape(nblk, 1, MOE_BLOCK)
        F2 = w1.shape[-1]
        b1p = b1[l].reshape(E, F2 // (2 * LANES), LANES, 2).transpose(0, 1, 3, 2).reshape(E, 1, F2)
        y_k = _moe_ffn(xs, slot_row, n_items, blk_e, nvalid, w1[l], b1p, w2[l], b2[l].reshape(E, 1, D))
        x2 = _combine(y_k, route, x1, mod3, g_ffn_post[l], S)
    return x2.reshape(B, S, D)
```

```python
import functools
import math

import numpy as np
import jax
import jax.numpy as jnp
from jax import lax
from jax.experimental import pallas as pl
from jax.experimental.pallas import tpu as pltpu

GRID_W = 64
NA_HEADS = 8
NA_HEAD_DIM = 64
NA_WIN_ROWS = 8
NA_WIN_COLS = 16
NA_WIDTH = NA_HEADS * NA_HEAD_DIM
MLA_HEADS = 8
MLA_Q_LORA = 384
MLA_KV_LORA = 256
MLA_NOPE = 64
MLA_ROPE = 32
MLA_V = 64
MLA_WIDTH = MLA_HEADS * MLA_V
MLA_KEY_CHUNK = 256
MLA_VROWS = MLA_V + 16
MLA_STEPS_PER_TRIP = 8
ROPE_THETA = 100.0
TOP_K = 4
SWIGLU_ALPHA = 1.702
SWIGLU_LIMIT = 7.0
MOE_BLOCK = 256
RMS_EPS = 1e-6
NEG_BIG = -1e30

LANES = 128
VMEM_LIMIT_BYTES = 56 * 1024 * 1024

Q_LORA_PAD = 512
COL_GATES = 0
NA_GROUP_ROWS = 8
NA_KEY_ROWS = 16


def _f32(x):
    return x.astype(jnp.float32)


def _bf16(x):
    return x.astype(jnp.bfloat16)


def _dot(a, b):
    return jnp.dot(a, b, preferred_element_type=jnp.float32)


def _dot_nt(a, b):
    return lax.dot_general(a, b, (((1,), (1,)), ((), ())), preferred_element_type=jnp.float32)


def _rms(x, g, n=None):
    n = x.shape[-1] if n is None else n
    ms = jnp.sum(x * x, axis=-1, keepdims=True) * (1.0 / n)
    return x * lax.rsqrt(ms + RMS_EPS) * g


SUBLANES = 8


def _store_row_tiles(ref, x, base=0):
    n = x.shape[0]
    for c in range(x.shape[1] // LANES):
        ref[pl.ds(base + c, n, stride=SUBLANES), :] = x[:, c * LANES:(c + 1) * LANES]


def _load_row_tiles(ref, n, base=0):
    chunks = [ref[pl.ds(base + c, n, stride=SUBLANES), :] for c in range(SUBLANES)]
    return jnp.concatenate(chunks, axis=-1)


def _cparams(sem):
    return pltpu.CompilerParams(dimension_semantics=sem, vmem_limit_bytes=VMEM_LIMIT_BYTES)


def _ada_kernel(c_ref, w_ref, b_ref, o_ref):
    c = c_ref[...]
    sc = c * jax.nn.sigmoid(c)
    o_ref[...] = _dot(_bf16(sc), _bf16(w_ref[...])) + b_ref[...]


def _ada_mod(c, w_ada, b_ada):
    B, D = c.shape
    n_out = w_ada.shape[1]
    return pl.pallas_call(
        _ada_kernel,
        out_shape=jax.ShapeDtypeStruct((B, n_out), jnp.float32),
        grid=(n_out // D,),
        in_specs=[
            pl.BlockSpec((B, D), lambda j: (0, 0)),
            pl.BlockSpec((D, D), lambda j: (0, j)),
            pl.BlockSpec((1, D), lambda j: (0, j)),
        ],
        out_specs=pl.BlockSpec((B, D), lambda j: (0, j)),
        compiler_params=_cparams(("arbitrary",)),
        name="ada_mod",
    )(c, w_ada, b_ada.reshape(1, n_out))


def _in_proj_kernel(x_ref, mod_ref, g_ref, w_ref, bg_ref, o_ref, qt_ref, vt_ref, *, d_model, n_gate,
                    chunk, col_q, col_v):
    x = x_ref[...]
    tm = x.shape[0]
    shift = mod_ref[0, :, 0:d_model]
    scale = mod_ref[0, :, d_model:2 * d_model]
    h = _rms(x, g_ref[...]) * (1.0 + scale) + shift
    hb = _bf16(h)
    n_total = w_ref.shape[1]
    n_pairs = NA_HEADS // 2
    for c0 in range(0, n_total, chunk):
        c1 = min(c0 + chunk, n_total)
        acc = _dot(hb, w_ref[:, c0:c1])
        if c0 < n_gate:
            acc = jax.nn.sigmoid(acc + bg_ref[:, c0:c1])
        o_ref[:, c0:c1] = _bf16(acc)
        if c0 == col_q:
            for hp in range(n_pairs):
                qt_ref[0, hp] = _bf16((acc[:, hp * LANES:(hp + 1) * LANES] * NA_HEAD_DIM ** -0.5).T)
        if c0 == col_v:
            for hp in range(n_pairs):
                vt = _bf16(acc[:, hp * LANES:(hp + 1) * LANES].T)
                for cc in range(tm // LANES):
                    vt_ref[0, hp, cc] = vt[:, cc * LANES:(cc + 1) * LANES]


def _in_proj(x2, mod3, g_pre, w_ext, b_gate, seq, col_q, col_v, tm=512):
    T, D = x2.shape
    batch = T // seq
    n_s = seq // tm
    n_total = w_ext.shape[1]
    n_gate = b_gate.shape[-1]
    n_pairs = NA_HEADS // 2
    chunk = n_pairs * LANES
    assert col_q % chunk == 0 and col_v % chunk == 0
    kern = functools.partial(_in_proj_kernel, d_model=D, n_gate=n_gate, chunk=chunk, col_q=col_q, col_v=col_v)
    return pl.pallas_call(
        kern,
        out_shape=(
            jax.ShapeDtypeStruct((T, n_total), jnp.bfloat16),
            jax.ShapeDtypeStruct((batch, n_pairs, LANES, seq), jnp.bfloat16),
            jax.ShapeDtypeStruct((batch, n_pairs, seq // LANES, LANES, LANES), jnp.bfloat16),
        ),
        grid=(T // tm,),
        in_specs=[
            pl.BlockSpec((tm, D), lambda i: (i, 0)),
            pl.BlockSpec((1, 1, mod3.shape[-1]), lambda i: ((i * tm) // seq, 0, 0)),
            pl.BlockSpec((1, D), lambda i: (0, 0)),
            pl.BlockSpec((D, n_total), lambda i: (0, 0)),
            pl.BlockSpec((1, n_gate), lambda i: (0, 0)),
        ],
        out_specs=(
            pl.BlockSpec((tm, n_total), lambda i: (i, 0)),
            pl.BlockSpec((1, n_pairs, LANES, tm), lambda i: (i // n_s, 0, 0, i % n_s)),
            pl.BlockSpec((1, n_pairs, tm // LANES, LANES, LANES), lambda i: (i // n_s, 0, i % n_s, 0, 0)),
        ),
        compiler_params=_cparams(("arbitrary",)),
        name="in_proj",
    )(x2, mod3, g_pre.reshape(1, D), w_ext, b_gate.reshape(1, n_gate))


def _na_class(c):
    half = NA_WIN_ROWS // 2
    if c == 0:
        return (lambda i: max(i - half, 0)), NA_WIN_ROWS - 1
    if c == 1:
        return (lambda i: i), NA_WIN_ROWS - 1 - half
    return (lambda i: min(i + half, NA_KEY_ROWS - NA_WIN_ROWS)), -1


NA_PAIR_KEY_ROWS = 10


def _na_kernel(qt_ref, k_ref, vt_ref, pt_ref, o_ref, *, n_rows):
    g = pl.program_id(2)
    n_groups = n_rows // NA_GROUP_ROWS
    kb_rows = jnp.clip(g * NA_GROUP_ROWS - NA_WIN_ROWS // 2, 0, n_rows - NA_KEY_ROWS)
    nkeys = NA_PAIR_KEY_ROWS * GRID_W
    row = lax.broadcasted_iota(jnp.int32, (LANES, LANES), 0)
    lane = lax.broadcasted_iota(jnp.int32, (GRID_W, LANES), 1)
    neg_blk = jnp.full((GRID_W, LANES), NEG_BIG, jnp.float32)

    def body(c):
        j0, off = _na_class(c)
        n_pair = NA_GROUP_ROWS // 2
        starts = [min(j0(2 * ip) - j0(2 * ip) % 2, NA_KEY_ROWS - NA_PAIR_KEY_ROWS) for ip in range(n_pair)]
        scores = []
        for ip in range(n_pair):
            kstart = pl.multiple_of((kb_rows + starts[ip]) * GRID_W, 2 * GRID_W)
            kwin = k_ref[pl.ds(kstart, nkeys), :]
            qp = qt_ref[0, 0, :, ip * LANES:(ip + 1) * LANES]
            zero = jnp.zeros_like(qp)
            wq = jnp.concatenate([jnp.where(row < NA_HEAD_DIM, qp, zero),
                                  jnp.where(row >= NA_HEAD_DIM, qp, zero)], axis=1)
            scores.append(_dot(kwin, wq))
        outs = []
        for ip in range(n_pair):
            i0 = 2 * ip
            w = starts[ip]
            st = scores[ip]
            blocks = []
            for jw in range(NA_PAIR_KEY_ROWS):
                j = w + jw
                v0 = j0(i0) <= j < j0(i0) + NA_WIN_ROWS
                v1 = j0(i0 + 1) <= j < j0(i0 + 1) + NA_WIN_ROWS
                dr0 = j - i0 + off
                halves = []
                for hl in range(2):
                    if not (v0 or v1):
                        halves.append(neg_blk)
                        continue
                    blk = st[jw * GRID_W:(jw + 1) * GRID_W, hl * LANES:(hl + 1) * LANES] + pt_ref[hl, dr0]
                    if not (v0 and v1):
                        keep = (lane < GRID_W) if v0 else (lane >= GRID_W)
                        blk = jnp.where(keep, blk, NEG_BIG)
                    halves.append(blk)
                blocks.append(jnp.concatenate(halves, axis=1))
            s = jnp.concatenate(blocks, axis=0)
            m = jnp.max(s, axis=0, keepdims=True)
            p = jnp.exp(s - m)
            l = jnp.sum(p, axis=0, keepdims=True)
            c0 = (kb_rows + w) // 2
            vwin = jnp.concatenate([vt_ref[0, 0, c0 + u] for u in range(NA_PAIR_KEY_ROWS // 2)], axis=1)
            on = _dot(vwin, _bf16(p)) / l
            outs.append(jnp.where(row < NA_HEAD_DIM, on[:, 0:LANES], on[:, LANES:2 * LANES]))
        o_ref[...] = _bf16(jnp.concatenate(outs, axis=1).T)

    cls = jnp.where(g == 0, 0, jnp.where(g == n_groups - 1, 2, 1))
    for c in range(3):
        pl.when(cls == c)(functools.partial(body, c))


def _na_attention(proj, q_t, v_t, pair_tab, batch, seq, col_k):
    T = proj.shape[0]
    n_rows = seq // GRID_W
    n_groups = n_rows // NA_GROUP_ROWS
    tq = NA_GROUP_ROWS * GRID_W
    n_pairs = NA_HEADS // 2
    kern = functools.partial(_na_kernel, n_rows=n_rows)
    return pl.pallas_call(
        kern,
        out_shape=jax.ShapeDtypeStruct((T, NA_WIDTH), jnp.bfloat16),
        grid=(n_pairs, batch, n_groups),
        in_specs=[
            pl.BlockSpec((1, 1, LANES, tq), lambda hp, b, g: (b, hp, 0, g)),
            pl.BlockSpec((seq, LANES), lambda hp, b, g: (b, col_k // LANES + hp)),
            pl.BlockSpec((1, 1, seq // LANES, LANES, LANES), lambda hp, b, g: (b, hp, 0, 0, 0)),
            pl.BlockSpec((2,) + pair_tab.shape[1:], lambda hp, b, g: (hp, 0, 0, 0)),
        ],
        out_specs=pl.BlockSpec((tq, LANES), lambda hp, b, g: (b * n_groups + g, hp)),
        compiler_params=_cparams(("arbitrary", "arbitrary", "arbitrary")),
        name="na_attn",
    )(q_t, proj, v_t, pair_tab)


def _na_pair_tables(rpb):
    H, n_dr, n_dc = rpb.shape
    kw = NA_WIN_COLS
    W = GRID_W
    ring = 2 * W - 1
    rp = jnp.pad(rpb.astype(jnp.float32), ((0, 0), (0, 0), (W - kw, W - kw)), constant_values=NEG_BIG)
    u = jnp.roll(rp, -(W - 1), axis=-1)
    circ = jnp.tile(u, (1, 1, W))[..., :W * (ring - 1)].reshape(H, n_dr, W, ring - 1)
    toep = circ[..., :W]
    cidx = np.arange(W)
    col_start = np.clip(cidx - kw // 2, 0, W - kw)
    col_in = (cidx[None, :] >= col_start[:, None]) & (cidx[None, :] < col_start[:, None] + kw)
    toep_t = jnp.where(jnp.asarray(col_in.T)[None, None], jnp.swapaxes(toep, 2, 3), NEG_BIG)
    neg = jnp.full((H, 1, W, W), NEG_BIG, jnp.float32)
    ext = jnp.concatenate([neg, toep_t, neg], axis=1)
    return jnp.concatenate([ext[:, 1:], ext[:, :-1]], axis=-1)


def _mla_prep_kernel(cq_ref, ckv_ref, kr_ref, gq_ref, gkv_ref, wqm_ref, wqs_ref, wk_ref, wv_ref,
                     pm_ref, tab_ref, q_out, k_out, v_out, *, qscale):
    cq = _f32(cq_ref[...])
    cqn = _bf16(_rms(cq, gq_ref[...], n=MLA_Q_LORA))
    qm = _dot(cqn, wqm_ref[...])
    qs = _dot(cqn, wqs_ref[...])
    ctab = tab_ref[:, 0:LANES]
    stab = tab_ref[:, LANES:2 * LANES]
    for h in range(MLA_HEADS):
        sl = slice(h * LANES, (h + 1) * LANES)
        qh = (qm[:, sl] * ctab + qs[:, sl] * stab) * qscale
        q_out[0, h] = _bf16(qh.T)

    ckv = _f32(ckv_ref[...])
    ckvn = _bf16(_rms(ckv, gkv_ref[...]))
    kk = _dot(ckvn, wk_ref[...])
    vv = _dot(ckvn, wv_ref[...])
    n_chunk = v_out.shape[2]
    ones = jnp.ones((MLA_VROWS - MLA_V, MLA_KEY_CHUNK), jnp.bfloat16)
    for hp in range(MLA_HEADS // 2):
        vt = _bf16(vv[:, hp * LANES:(hp + 1) * LANES].T)
        for cc in range(n_chunk):
            ks = slice(cc * MLA_KEY_CHUNK, (cc + 1) * MLA_KEY_CHUNK)
            for h in range(2):
                r0 = h * MLA_VROWS
                v_out[0, hp, cc, r0:r0 + MLA_V, :] = vt[h * MLA_V:(h + 1) * MLA_V, ks]
                v_out[0, hp, cc, r0 + MLA_V:r0 + MLA_VROWS, :] = ones
    krr = _f32(kr_ref[...]) * tab_ref[:, 2 * LANES:3 * LANES]
    kplace = _dot(_bf16(krr), pm_ref[...])
    for h in range(MLA_HEADS):
        sl = slice(h * LANES, (h + 1) * LANES)
        k_out[:, sl] = _bf16(kk[:, sl] + kplace)


def _mla_prep(proj, col_cq, col_ckv, col_kr, gq, gkv, wqm, wqs, wk, wv, pm, rope_tab, seq, qscale, tm=512):
    T = proj.shape[0]
    batch = T // seq
    n_s = seq // tm
    n_pairs = MLA_HEADS // 2
    cpt = tm // MLA_KEY_CHUNK
    full = lambda a: pl.BlockSpec(a.shape, lambda i: (0,) * a.ndim)
    tab = pl.BlockSpec((tm, 3 * LANES), lambda i: (i % n_s, 0))
    kern = functools.partial(_mla_prep_kernel, qscale=qscale)
    return pl.pallas_call(
        kern,
        out_shape=(
            jax.ShapeDtypeStruct((batch, MLA_HEADS, LANES, seq), jnp.bfloat16),
            jax.ShapeDtypeStruct((T, MLA_HEADS * LANES), jnp.bfloat16),
            jax.ShapeDtypeStruct((batch, n_pairs, seq // MLA_KEY_CHUNK, 2 * MLA_VROWS, MLA_KEY_CHUNK),
                                 jnp.bfloat16),
        ),
        grid=(T // tm,),
        in_specs=[
            pl.BlockSpec((tm, Q_LORA_PAD), lambda i: (i, col_cq // Q_LORA_PAD)),
            pl.BlockSpec((tm, MLA_KV_LORA), lambda i: (i, col_ckv // MLA_KV_LORA)),
            pl.BlockSpec((tm, LANES), lambda i: (i, col_kr // LANES)),
            full(gq), full(gkv), full(wqm), full(wqs), full(wk), full(wv), full(pm),
            tab,
        ],
        out_specs=(
            pl.BlockSpec((1, MLA_HEADS, LANES, tm), lambda i: (i // n_s, 0, 0, i % n_s)),
            pl.BlockSpec((tm, MLA_HEADS * LANES), lambda i: (i, 0)),
            pl.BlockSpec((1, n_pairs, cpt, 2 * MLA_VROWS, MLA_KEY_CHUNK),
                         lambda i: (i // n_s, 0, i % n_s, 0, 0)),
        ),
        compiler_params=_cparams(("arbitrary",)),
        name="mla_prep",
    )(proj, proj, proj, gq, gkv, wqm, wqs, wk, wv, pm, rope_tab)


def _mla_attn_kernel(qt_ref, k_ref, vt_ref, o_ref, m_scr, acc_scr, st_scr):
    n_chunks = vt_ref.shape[2]
    tk = MLA_KEY_CHUNK
    m_scr[...] = jnp.full(m_scr.shape, -jnp.inf, jnp.float32)
    acc_scr[...] = jnp.zeros(acc_scr.shape, jnp.float32)

    def scores(c, slot):
        k0 = pl.multiple_of(c * tk, tk)
        for h in range(2):
            kc = k_ref[pl.ds(k0, tk), h * LANES:(h + 1) * LANES]
            st_scr[slot, h] = _dot(kc, qt_ref[0, h])

    def step(c, slot):
        scores(jnp.minimum(c + 1, n_chunks - 1), 1 - slot)
        vt = vt_ref[0, 0, c]
        pts, alphas = [], []
        for h in range(2):
            st = st_scr[slot, h]
            m_old = m_scr[h]
            m_new = jnp.maximum(m_old, jnp.max(st, axis=0, keepdims=True))
            alpha = jnp.exp2(m_old - m_new)
            pt = jnp.exp2(st - m_new)
            m_scr[h] = m_new
            pts.append(_bf16(pt))
            alphas.append(alpha)
        for h in range(2):
            acc_scr[h] = alphas[h] * acc_scr[h] + _dot(vt[h * MLA_VROWS:(h + 1) * MLA_VROWS, :], pts[h])

    def body(j, carry):
        for u in range(MLA_STEPS_PER_TRIP):
            step(MLA_STEPS_PER_TRIP * j + u, u % 2)
        return carry

    scores(0, 0)
    lax.fori_loop(0, n_chunks // MLA_STEPS_PER_TRIP, body, 0)
    outs = []
    for h in range(2):
        acc = acc_scr[h]
        outs.append(acc[0:MLA_V] / acc[MLA_V:MLA_V + 1])
    o_ref[...] = _bf16(jnp.concatenate(outs, axis=0).T)


def _mla_attention(q_t, k_cat, v_t, batch, seq, tq=512):
    T = k_cat.shape[0]
    n_q = seq // tq
    n_pairs = MLA_HEADS // 2
    n_chunks = seq // MLA_KEY_CHUNK
    assert n_chunks % MLA_STEPS_PER_TRIP == 0
    return pl.pallas_call(
        _mla_attn_kernel,
        out_shape=jax.ShapeDtypeStruct((T, MLA_WIDTH), jnp.bfloat16),
        grid=(batch, n_pairs, n_q),
        in_specs=[
            pl.BlockSpec((1, 2, LANES, tq), lambda b, hp, i: (b, hp, 0, i)),
            pl.BlockSpec((seq, 2 * LANES), lambda b, hp, i: (b, hp)),
            pl.BlockSpec((1, 1, n_chunks, 2 * MLA_VROWS, MLA_KEY_CHUNK), lambda b, hp, i: (b, hp, 0, 0, 0)),
        ],
        out_specs=pl.BlockSpec((tq, LANES), lambda b, hp, i: (b * n_q + i, hp)),
        scratch_shapes=[
            pltpu.VMEM((2, 1, tq), jnp.float32),
            pltpu.VMEM((2, MLA_VROWS, tq), jnp.float32),
            pltpu.VMEM((2, 2, MLA_KEY_CHUNK, tq), jnp.float32),
        ],
        compiler_params=_cparams(("arbitrary", "arbitrary", "arbitrary")),
        name="mla_attn",
    )(q_t, k_cat, v_t)


def _out_proj_kernel(ona_ref, omla_ref, gate_ref, x_ref, mod_ref, gpost_ref, gpre_ref, wna_ref,
                     wmla_ref, wout_ref, wr_ref, br_ref, x1_ref, h2_ref, lg_ref, *, d_model):
    D = d_model
    tm = x_ref.shape[0]
    gate_a = mod_ref[0, :, 2 * D:3 * D]
    shift_f = mod_ref[0, :, 3 * D:4 * D]
    scale_f = mod_ref[0, :, 4 * D:5 * D]
    halves = [slice(0, tm // 2), slice(tm // 2, tm)]
    merged = []
    for rs in halves:
        up_na = _dot(ona_ref[rs, :], wna_ref[...])
        up_mla = _dot(omla_ref[rs, :], wmla_ref[...])
        merged.append(_bf16(_f32(gate_ref[rs, 0:D]) * up_na + _f32(gate_ref[rs, D:2 * D]) * up_mla))
    ys = [_dot(m, wout_ref[...]) for m in merged]
    h2s = []
    for rs, y in zip(halves, ys):
        x1 = x_ref[rs, :] + gate_a * _rms(y, gpost_ref[...])
        x1_ref[rs, :] = x1
        h2 = _rms(x1, gpre_ref[...]) * (1.0 + scale_f) + shift_f
        _store_row_tiles(h2_ref, h2, base=rs.start * SUBLANES)
        h2s.append(_bf16(h2))
    for rs, hb in zip(halves, h2s):
        lg_ref[rs, :] = _dot(hb, wr_ref[...]) + br_ref[...]


def _out_proj(o_na, o_mla, proj, x2, mod3, g_post, g_pre, w_na, w_mla, w_out, w_r, b_r, seq, tm=512):
    T, D = x2.shape
    full = lambda a: pl.BlockSpec(a.shape, lambda i: (0,) * a.ndim)
    row = lambda w: pl.BlockSpec((tm, w), lambda i: (i, 0))
    kern = functools.partial(_out_proj_kernel, d_model=D)
    g_post = g_post.reshape(1, D)
    g_pre = g_pre.reshape(1, D)
    return pl.pallas_call(
        kern,
        out_shape=(
            jax.ShapeDtypeStruct((T, D), jnp.float32),
            jax.ShapeDtypeStruct((T * (D // LANES), LANES), jnp.float32),
            jax.ShapeDtypeStruct((T, LANES), jnp.float32),
        ),
        grid=(T // tm,),
        in_specs=[
            row(NA_WIDTH), row(MLA_WIDTH),
            pl.BlockSpec((tm, 2 * D), lambda i: (i, COL_GATES // (2 * D))),
            row(D),
            pl.BlockSpec((1, 1, mod3.shape[-1]), lambda i: ((i * tm) // seq, 0, 0)),
            full(g_post), full(g_pre), full(w_na), full(w_mla), full(w_out), full(w_r), full(b_r),
        ],
        out_specs=(row(D), pl.BlockSpec((tm * (D // LANES), LANES), lambda i: (i, 0)), row(LANES)),
        compiler_params=_cparams(("arbitrary",)),
        name="out_proj",
    )(o_na, o_mla, proj, x2, mod3, g_post, g_pre, w_na, w_mla, w_out, w_r, b_r)


def _route_kernel(lg_ref, r_ref, cnt_ref, carry_scr, *, sub):
    @pl.when(pl.program_id(0) == 0)
    def _():
        carry_scr[...] = jnp.zeros(carry_scr.shape, jnp.float32)

    tr = lg_ref.shape[0]
    lane = lax.broadcasted_iota(jnp.int32, (sub, LANES), 1).astype(jnp.float32)
    ri = lax.broadcasted_iota(jnp.int32, (sub, sub), 0)
    ci = lax.broadcasted_iota(jnp.int32, (sub, sub), 1)
    tri = jnp.where(ri >= ci, 1.0, 0.0).astype(jnp.bfloat16)
    for s0 in range(0, tr, sub):
        work = lg_ref[s0:s0 + sub, :]
        sels, vals, idxs = [], [], []
        for _k in range(TOP_K):
            mk = jnp.max(work, axis=-1, keepdims=True)
            ik = jnp.min(jnp.where(work == mk, lane, float(LANES)), axis=-1, keepdims=True)
            sk = lane == ik
            work = jnp.where(sk, -jnp.inf, work)
            sels.append(sk)
            vals.append(mk)
            idxs.append(ik)
        es = [jnp.exp(v - vals[0]) for v in vals]
        denom = es[0] + es[1] + es[2] + es[3]
        onehot = jnp.zeros((sub, LANES), jnp.float32)
        for sk in sels:
            onehot = jnp.where(sk, 1.0, onehot)
        prefix = _dot(tri, _bf16(onehot))
        carry = carry_scr[...]
        rank_mat = carry + prefix - 1.0
        res = jnp.zeros((sub, LANES), jnp.float32)
        for kk in range(TOP_K):
            rank_k = jnp.sum(jnp.where(sels[kk], rank_mat, 0.0), axis=-1, keepdims=True)
            res = jnp.where(lane == kk, idxs[kk], res)
            res = jnp.where(lane == TOP_K + kk, rank_k, res)
            res = jnp.where(lane == 2 * TOP_K + kk, es[kk] / denom, res)
        r_ref[s0:s0 + sub, :] = res
        carry_scr[...] = carry + jnp.sum(onehot, axis=0, keepdims=True)
    cnt_ref[...] = carry_scr[...]


def _route(logits, tr=2048, sub=256):
    T = logits.shape[0]
    kern = functools.partial(_route_kernel, sub=sub)
    return pl.pallas_call(
        kern,
        out_shape=(
            jax.ShapeDtypeStruct((T, LANES), jnp.float32),
            jax.ShapeDtypeStruct((1, LANES), jnp.float32),
        ),
        grid=(T // tr,),
        in_specs=[pl.BlockSpec((tr, LANES), lambda i: (i, 0))],
        out_specs=(
            pl.BlockSpec((tr, LANES), lambda i: (i, 0)),
            pl.BlockSpec((1, LANES), lambda i: (0, 0)),
        ),
        scratch_shapes=[pltpu.VMEM((1, LANES), jnp.float32)],
        compiler_params=_cparams(("arbitrary",)),
        name="route",
    )(logits)


def _dispatch_kernel(zero_blk_ref, dest_ref, h_ref, row0_ref, xs_ref, row_ref, zbuf, sem, fill_sem,
                     *, n_tokens):
    i = pl.program_id(0)
    ts = h_ref.shape[0] // SUBLANES
    blk_rows = MOE_BLOCK * SUBLANES

    @pl.when(i == 0)
    def _():
        fill_row = pltpu.make_async_copy(row0_ref, row_ref, fill_sem.at[0])
        fill_row.start()
        zbuf[...] = jnp.zeros(zbuf.shape, zbuf.dtype)
        n_zero = zero_blk_ref.shape[0]

        def zero_copy(z):
            r0 = pl.multiple_of(zero_blk_ref[z] * blk_rows, blk_rows)
            return pltpu.make_async_copy(zbuf, xs_ref.at[pl.ds(r0, blk_rows), :], fill_sem.at[1])

        for z in range(n_zero):
            pl.when(zero_blk_ref[z] >= 0)(lambda z=z: zero_copy(z).start())
        for z in range(n_zero):
            pl.when(zero_blk_ref[z] >= 0)(lambda z=z: zero_copy(z).wait())
        fill_row.wait()

    def row_copy(j, d):
        dst = pl.ds(pl.multiple_of(d * SUBLANES, SUBLANES), SUBLANES)
        return pltpu.make_async_copy(h_ref.at[pl.ds(j * SUBLANES, SUBLANES), :], xs_ref.at[dst, :], sem)

    t0 = i * ts
    for j in range(ts):
        for kk in range(TOP_K):
            d = dest_ref[j * TOP_K + kk]
            row_copy(j, d).start(priority=kk % 2)
            row_ref[d] = (kk * n_tokens + j) + t0
    for _j in range(ts * TOP_K):
        row_copy(0, 0).wait()


def _dispatch(h2_tiles, dest, zero_blk, cap, ts=256):
    T = h2_tiles.shape[0] // SUBLANES
    kern = functools.partial(_dispatch_kernel, n_tokens=T)
    grid_spec = pltpu.PrefetchScalarGridSpec(
        num_scalar_prefetch=1,
        grid=(T // ts,),
        in_specs=[
            pl.BlockSpec((ts * TOP_K,), lambda i, zb: (i,), memory_space=pltpu.SMEM),
            pl.BlockSpec((ts * SUBLANES, LANES), lambda i, zb: (i, 0)),
            pl.BlockSpec(memory_space=pl.ANY),
        ],
        out_specs=(pl.BlockSpec(memory_space=pl.ANY), pl.BlockSpec(memory_space=pltpu.SMEM)),
        scratch_shapes=[
            pltpu.VMEM((MOE_BLOCK * SUBLANES, LANES), h2_tiles.dtype),
            pltpu.SemaphoreType.DMA(()),
            pltpu.SemaphoreType.DMA((2,)),
        ],
    )
    return pl.pallas_call(
        kern,
        out_shape=(jax.ShapeDtypeStruct((cap * SUBLANES, LANES), h2_tiles.dtype),
                   jax.ShapeDtypeStruct((cap,), jnp.int32)),
        grid_spec=grid_spec,
        compiler_params=_cparams(("arbitrary",)),
        name="dispatch",
    )(zero_blk, dest, h2_tiles, jnp.full((cap,), -1, jnp.int32))


def _moe_kernel(blk_e_ref, nvalid_ref, row_prev_ref, row_ref, x_ref, w1_ref, b1_ref, w2_ref, b2_ref,
                out_ref, w1p_scr, w2b_scr, act_scr, ybuf0, ybuf1, ssem, *, n_items):
    i = pl.program_id(0)
    nvalid = nvalid_ref[0]
    d_ff = w2_ref.shape[1]
    n_groups = (2 * d_ff) // (2 * LANES)

    def tile(r):
        return pl.ds(pl.multiple_of(r * SUBLANES, SUBLANES), SUBLANES)

    ybufs = (ybuf0, ybuf1)

    def scatter_copy(j, r, slot):
        return pltpu.make_async_copy(ybufs[slot].at[pl.ds(j * SUBLANES, SUBLANES), :], out_ref.at[tile(r), :],
                                     ssem.at[slot])

    def wait_scatters(slot):
        for _j in range(MOE_BLOCK):
            scatter_copy(0, 0, slot).wait()

    def scatter_rows(rows_ref, slot, lo=0, hi=MOE_BLOCK):
        spare0 = n_items + slot * MOE_BLOCK
        for j in range(lo, hi):
            r = rows_ref[0, 0, j]
            scatter_copy(j, jnp.where(r >= 0, r, spare0 + j), slot).start(priority=j % 2)

    @pl.when(i == 0)
    def _():
        ybuf0[...] = jnp.zeros(ybuf0.shape, ybuf0.dtype)
        ybuf1[...] = jnp.zeros(ybuf1.shape, ybuf1.dtype)
        for j in range(MOE_BLOCK):
            scatter_copy(j, n_items + j, 0).start(priority=j % 2)

    @pl.when(i < nvalid)
    def _():
        e = blk_e_ref[i]
        e_prev = blk_e_ref[jnp.maximum(i - 1, 0)]

        @pl.when((i == 0) | (e != e_prev))
        def _():
            r = lax.broadcasted_iota(jnp.int32, (2 * LANES, 2 * LANES), 0)
            c = lax.broadcasted_iota(jnp.int32, (2 * LANES, 2 * LANES), 1)
            src = jnp.where(c < LANES, 2 * c, 2 * (c - LANES) + 1)
            perm = jnp.where(r == src, 1.0, 0.0).astype(jnp.bfloat16)
            for gI in range(n_groups):
                sl = slice(gI * 2 * LANES, (gI + 1) * 2 * LANES)
                w1p_scr[:, sl] = _bf16(_dot(_bf16(w1_ref[0, :, sl]), perm))
            w2b_scr[...] = _bf16(w2_ref[0])

        def block_step(slot):
            wait_scatters(slot)
            xb = _bf16(_load_row_tiles(x_ref, MOE_BLOCK))
            d_model = w2b_scr.shape[1]
            n_out_groups = d_model // (2 * LANES)
            share = MOE_BLOCK // (n_groups + n_out_groups)
            issued = 0
            for gI in range(n_groups):
                gs = slice(gI * 2 * LANES, (gI + 1) * 2 * LANES)
                hg = _dot(xb, w1p_scr[:, gs]) + b1_ref[0, :, gs]
                glu = jnp.minimum(hg[:, 0:LANES], SWIGLU_LIMIT)
                lin = jnp.clip(hg[:, LANES:2 * LANES], -SWIGLU_LIMIT, SWIGLU_LIMIT)
                act_scr[:, gI * LANES:(gI + 1) * LANES] = _bf16(
                    glu * jax.nn.sigmoid(SWIGLU_ALPHA * glu) * (lin + 1.0))
                scatter_rows(row_prev_ref, 1 - slot, issued, issued + share)
                issued += share
            act = act_scr[...]
            for gO in range(n_out_groups):
                gs = slice(gO * 2 * LANES, (gO + 1) * 2 * LANES)
                yg = _dot(act, w2b_scr[:, gs]) + b2_ref[0, :, gs]
                _store_row_tiles(ybufs[slot], yg, base=2 * gO)
                hi = MOE_BLOCK if gO == n_out_groups - 1 else issued + share
                scatter_rows(row_prev_ref, 1 - slot, issued, hi)
                issued = hi

            @pl.when(i == nvalid - 1)
            def _():
                scatter_rows(row_ref, slot)
                wait_scatters(1 - slot)
                wait_scatters(slot)

        parity = lax.rem(i, 2)
        for slot in range(2):
            pl.when(parity == slot)(functools.partial(block_step, slot))


def _moe_ffn(xs, slot_row, n_items, blk_e, nvalid, w1, b1p, w2, b2):
    D = w1.shape[1]
    assert D == SUBLANES * LANES
    E, _, F2 = w1.shape
    F = w2.shape[1]
    nblk = slot_row.shape[0]

    def cur(i, be, nv):
        return (jnp.minimum(i, nv[0] - 1), 0, 0)

    def prv(i, be, nv):
        return (jnp.where(i == 0, nblk, jnp.maximum(jnp.minimum(i, nv[0] - 1) - 1, 0)), 0, 0)

    slot_row = jnp.concatenate([slot_row, jnp.full((1, 1, MOE_BLOCK), -1, slot_row.dtype)], axis=0)
    slot_spec = lambda f: pl.BlockSpec((1, 1, MOE_BLOCK), f, memory_space=pltpu.SMEM)
    grid_spec = pltpu.PrefetchScalarGridSpec(
        num_scalar_prefetch=2,
        grid=(nblk,),
        in_specs=[
            slot_spec(prv), slot_spec(cur),
            pl.BlockSpec((MOE_BLOCK * SUBLANES, LANES), lambda i, be, nv: (jnp.minimum(i, nv[0] - 1), 0)),
            pl.BlockSpec((1, D, F2), lambda i, be, nv: (be[i], 0, 0)),
            pl.BlockSpec((1, 1, F2), lambda i, be, nv: (be[i], 0, 0)),
            pl.BlockSpec((1, F, D), lambda i, be, nv: (be[i], 0, 0)),
            pl.BlockSpec((1, 1, D), lambda i, be, nv: (be[i], 0, 0)),
        ],
        out_specs=pl.BlockSpec(memory_space=pl.ANY),
        scratch_shapes=[
            pltpu.VMEM((D, F2), jnp.bfloat16),
            pltpu.VMEM((F, D), jnp.bfloat16),
            pltpu.VMEM((MOE_BLOCK, F), jnp.bfloat16),
            pltpu.VMEM((MOE_BLOCK * SUBLANES, LANES), jnp.float32),
            pltpu.VMEM((MOE_BLOCK * SUBLANES, LANES), jnp.float32),
            pltpu.SemaphoreType.DMA((2,)),
        ],
    )
    kern = functools.partial(_moe_kernel, n_items=n_items)
    return pl.pallas_call(
        kern,
        out_shape=jax.ShapeDtypeStruct(((n_items + 2 * MOE_BLOCK) * SUBLANES, LANES), jnp.float32),
        grid_spec=grid_spec,
        compiler_params=_cparams(("arbitrary",)),
        name="moe_ffn",
    )(blk_e, nvalid, slot_row, slot_row, xs, w1, b1p, w2, b2)


def _combine_kernel(y0_ref, y1_ref, y2_ref, y3_ref, r_ref, x1_ref, mod_ref, g_ref, o_ref, *, d_model):
    D = d_model
    tc = x1_ref.shape[0]
    r = r_ref[...]
    moe = None
    for kk, y_ref in enumerate((y0_ref, y1_ref, y2_ref, y3_ref)):
        term = r[:, 2 * TOP_K + kk: 2 * TOP_K + kk + 1] * _load_row_tiles(y_ref, tc)
        moe = term if moe is None else moe + term
    gate_f = mod_ref[0, :, 5 * D:6 * D]
    o_ref[...] = x1_ref[...] + gate_f * _rms(moe, g_ref[...])


def _combine(y_k, route, x1, mod3, g_post, seq, tc=512):
    T, D = x1.shape
    n_t = T // tc
    kern = functools.partial(_combine_kernel, d_model=D)
    y_spec = lambda kk: pl.BlockSpec((tc * SUBLANES, LANES), lambda i: (kk * n_t + i, 0))
    return pl.pallas_call(
        kern,
        out_shape=jax.ShapeDtypeStruct((T, D), jnp.float32),
        grid=(n_t,),
        in_specs=[
            y_spec(0), y_spec(1), y_spec(2), y_spec(3),
            pl.BlockSpec((tc, LANES), lambda i: (i, 0)),
            pl.BlockSpec((tc, D), lambda i: (i, 0)),
            pl.BlockSpec((1, 1, mod3.shape[-1]), lambda i: ((i * tc) // seq, 0, 0)),
            pl.BlockSpec((1, D), lambda i: (0, 0)),
        ],
        out_specs=pl.BlockSpec((tc, D), lambda i: (i, 0)),
        compiler_params=_cparams(("arbitrary",)),
        name="combine",
    )(y_k, y_k, y_k, y_k, route, x1, mod3, g_post.reshape(1, D))


def _rope_swap_cols(w):
    nf = MLA_ROPE // 4
    return jnp.concatenate([-w[..., nf:2 * nf], w[..., 0:nf], -w[..., 3 * nf:4 * nf], w[..., 2 * nf:3 * nf]],
                           axis=-1)


def _rope_tables(seq):
    n_rows = seq // GRID_W
    nf = MLA_ROPE // 4
    inv = ROPE_THETA ** (-jnp.arange(nf, dtype=jnp.float32) / nf)
    ar = jnp.arange(n_rows, dtype=jnp.float32)[:, None] * inv
    ac = jnp.arange(GRID_W, dtype=jnp.float32)[:, None] * inv
    cr, sr, cc, sc = jnp.cos(ar), jnp.sin(ar), jnp.cos(ac), jnp.sin(ac)

    def lanes(parts, n):
        z = lambda w: jnp.zeros((n, w), jnp.float32)
        return jnp.concatenate([z(p) if isinstance(p, int) else p for p in parts], axis=-1)

    pad = LANES - MLA_NOPE - MLA_ROPE
    row_tab = lanes([MLA_NOPE, cr, cr, 2 * nf, pad, MLA_NOPE, sr, sr, 2 * nf, pad,
                     cr, cr, 2 * nf, sr, sr, 2 * nf, MLA_NOPE], n_rows)
    col_tab = lanes([jnp.ones((GRID_W, MLA_NOPE), jnp.float32), 2 * nf, cc, cc, pad, MLA_NOPE, 2 * nf, sc, sc, pad,
                     2 * nf, cc, cc, 2 * nf, sc, sc, MLA_NOPE], GRID_W)
    return (row_tab[:, None, :] + col_tab[None, :, :]).reshape(seq, 3 * LANES)


def kernel(x, c, w_ada, b_ada, g_attn_pre, g_attn_post, w_in, b_gate, na_rpb, q_norm_g, kv_norm_g,
           w_uq, w_ukv, w_na_up, w_mla_up, w_out, g_ffn_pre, g_ffn_post, w_router, b_router,
           w1, b1, w2, b2):
    B, S, D = x.shape
    T = B * S
    depth = w_ada.shape[0]
    E = w_router.shape[-1]
    n_rows = S // GRID_W
    assert S % (NA_GROUP_ROWS * GRID_W) == 0 and n_rows >= NA_KEY_ROWS
    assert E <= LANES and T % 2048 == 0

    bf = jnp.bfloat16
    rope_tab = _rope_tables(S)
    pm_np = np.zeros((LANES, LANES), np.float32)
    for cidx in range(MLA_ROPE):
        pm_np[cidx, MLA_NOPE + cidx] = 1.0
        pm_np[MLA_ROPE + cidx, MLA_NOPE + cidx] = 1.0
    pm = jnp.asarray(pm_np, bf)
    qscale = float((MLA_NOPE + MLA_ROPE) ** -0.5 * math.log2(math.e))

    col_na = 2 * D
    col_cq = col_na + 3 * NA_WIDTH
    col_ckv = col_cq + Q_LORA_PAD
    col_kr = col_ckv + MLA_KV_LORA
    assert col_cq % Q_LORA_PAD == 0 and col_ckv % MLA_KV_LORA == 0

    x2 = x.reshape(T, D)
    for l in range(depth):
        mod = _ada_mod(c, w_ada[l], b_ada[l])
        mod3 = mod.reshape(B, 1, 6 * D)

        o_na, o_cq, o_ckv, o_kr, o_g = np.cumsum([0, 3 * NA_WIDTH, MLA_Q_LORA, MLA_KV_LORA, MLA_ROPE]).tolist()
        wi = w_in[l]
        w_kr = wi[:, o_kr:o_g]
        w_ext = jnp.concatenate([
            wi[:, o_g:o_g + 2 * D],
            wi[:, o_na:o_cq],
            wi[:, o_cq:o_ckv], jnp.zeros((D, Q_LORA_PAD - MLA_Q_LORA), wi.dtype),
            wi[:, o_ckv:o_kr],
            w_kr, _rope_swap_cols(w_kr), jnp.zeros((D, LANES - 2 * MLA_ROPE), wi.dtype),
        ], axis=1).astype(bf)
        proj, na_qt, na_vt = _in_proj(x2, mod3, g_attn_pre[l], w_ext, b_gate[l], S,
                                      col_na, col_na + 2 * NA_WIDTH)

        o_na_tok = _na_attention(proj, na_qt, na_vt, _na_pair_tables(na_rpb[l]), B, S, col_na + NA_WIDTH)

        wq = w_uq[l]
        zq = jnp.zeros((MLA_Q_LORA, MLA_HEADS, LANES - MLA_NOPE - MLA_ROPE), wq.dtype)
        wqm = jnp.concatenate([wq, zq], axis=-1).reshape(MLA_Q_LORA, MLA_HEADS * LANES)
        wqs = jnp.concatenate([jnp.zeros((MLA_Q_LORA, MLA_HEADS, MLA_NOPE), wq.dtype),
                               _rope_swap_cols(wq[..., MLA_NOPE:]), zq], axis=-1
                              ).reshape(MLA_Q_LORA, MLA_HEADS * LANES)
        rpad = ((0, Q_LORA_PAD - MLA_Q_LORA), (0, 0))
        wqm = jnp.pad(wqm, rpad).astype(bf)
        wqs = jnp.pad(wqs, rpad).astype(bf)
        wkv = w_ukv[l]
        wk = jnp.concatenate([wkv[..., :MLA_NOPE], jnp.zeros((MLA_KV_LORA, MLA_HEADS, LANES - MLA_NOPE), wkv.dtype)],
                             axis=-1).reshape(MLA_KV_LORA, MLA_HEADS * LANES).astype(bf)
        wv = wkv[..., MLA_NOPE:].reshape(MLA_KV_LORA, MLA_WIDTH).astype(bf)
        gq = jnp.pad(q_norm_g[l], (0, Q_LORA_PAD - MLA_Q_LORA)).reshape(1, Q_LORA_PAD)
        gkv = kv_norm_g[l].reshape(1, MLA_KV_LORA)
        q_cat, k_cat, v_mla = _mla_prep(proj, col_cq, col_ckv, col_kr, gq, gkv, wqm, wqs, wk, wv, pm,
                                        rope_tab, S, qscale)
        o_mla_tok = _mla_attention(q_cat, k_cat, v_mla, B, S)

        w_r = jnp.pad(w_router[l], ((0, 0), (0, LANES - E))).astype(bf)
        b_r = jnp.concatenate([b_router[l], jnp.full((LANES - E,), NEG_BIG, jnp.float32)]).reshape(1, LANES)
        x1, h2, logits = _out_proj(o_na_tok, o_mla_tok, proj, x2, mod3, g_attn_post[l], g_ffn_pre[l],
                                   w_na_up[l].astype(bf), w_mla_up[l].astype(bf), w_out[l].astype(bf),
                                   w_r, b_r, S)

        route, counts = _route(logits)
        e_idx = route[:, 0:TOP_K].astype(jnp.int32)
        rank = route[:, TOP_K:2 * TOP_K].astype(jnp.int32)
        cnt = counts[0, :E].astype(jnp.int32)
        padded = ((cnt + MOE_BLOCK - 1) // MOE_BLOCK) * MOE_BLOCK
        pend = jnp.cumsum(padded)
        pstart = pend - padded
        n_items = T * TOP_K
        nblk = -(-n_items // MOE_BLOCK) + E
        cap = nblk * MOE_BLOCK
        dest = (rank + jnp.sum(jnp.where(e_idx[..., None] == jnp.arange(E, dtype=jnp.int32), pstart, 0), axis=-1)
                ).reshape(n_items)
        blk_off = jnp.arange(nblk, dtype=jnp.int32) * MOE_BLOCK
        blk_e = jnp.minimum(jnp.sum((pend[None, :] <= blk_off[:, None]).astype(jnp.int32), axis=-1), E - 1)
        nvalid = (pend[-1:] // MOE_BLOCK).astype(jnp.int32)

        last_blk = jnp.where(padded > 0, pend // MOE_BLOCK - 1, -1)
        tail_blk = nvalid[0] + jnp.arange(nblk - n_items // MOE_BLOCK, dtype=jnp.int32)
        zero_blk = jnp.concatenate([last_blk, jnp.where(tail_blk < nblk, tail_blk, -1)]).astype(jnp.int32)
        xs, slot_row = _dispatch(h2, dest, zero_blk, cap)
        slot_row = slot_row.reshape(nblk, 1, MOE_BLOCK)
        F2 = w1.shape[-1]
        b1p = b1[l].reshape(E, F2 // (2 * LANES), LANES, 2).transpose(0, 1, 3, 2).reshape(E, 1, F2)
        y_k = _moe_ffn(xs, slot_row, n_items, blk_e, nvalid, w1[l], b1p, w2[l], b2[l].reshape(E, 1, D))
        x2 = _combine(y_k, route, x1, mod3, g_ffn_post[l], S)
    return x2.reshape(B, S, D)
```

```python
import functools
import math

import numpy as np
import jax
import jax.numpy as jnp
from jax import lax
from jax.experimental import pallas as pl
from jax.experimental.pallas import tpu as pltpu

GRID_W = 64
NA_HEADS = 8
NA_HEAD_DIM = 64
NA_WIN_ROWS = 8
NA_WIN_COLS = 16
NA_WIDTH = NA_HEADS * NA_HEAD_DIM
MLA_HEADS = 8
MLA_Q_LORA = 384
MLA_KV_LORA = 256
MLA_NOPE = 64
MLA_ROPE = 32
MLA_V = 64
MLA_WIDTH = MLA_HEADS * MLA_V
MLA_KEY_CHUNK = 256
MLA_VROWS = MLA_V + 16
MLA_STEPS_PER_TRIP = 16
ROPE_THETA = 100.0
TOP_K = 4
SWIGLU_ALPHA = 1.702
SWIGLU_LIMIT = 7.0
MOE_BLOCK = 256
RMS_EPS = 1e-6
NEG_BIG = -1e30

LANES = 128
VMEM_LIMIT_BYTES = 56 * 1024 * 1024

Q_LORA_PAD = 512
COL_GATES = 0
NA_GROUP_ROWS = 8
NA_KEY_ROWS = 16


def _f32(x):
    return x.astype(jnp.float32)


def _bf16(x):
    return x.astype(jnp.bfloat16)


def _dot(a, b):
    return jnp.dot(a, b, preferred_element_type=jnp.float32)


def _dot_nt(a, b):
    return lax.dot_general(a, b, (((1,), (1,)), ((), ())), preferred_element_type=jnp.float32)


def _rms(x, g, n=None):
    n = x.shape[-1] if n is None else n
    ms = jnp.sum(x * x, axis=-1, keepdims=True) * (1.0 / n)
    return x * lax.rsqrt(ms + RMS_EPS) * g


SUBLANES = 8


def _store_row_tiles(ref, x, base=0):
    n = x.shape[0]
    for c in range(x.shape[1] // LANES):
        ref[pl.ds(base + c, n, stride=SUBLANES), :] = x[:, c * LANES:(c + 1) * LANES]


def _load_row_tiles(ref, n, base=0):
    chunks = [ref[pl.ds(base + c, n, stride=SUBLANES), :] for c in range(SUBLANES)]
    return jnp.concatenate(chunks, axis=-1)


def _cparams(sem):
    return pltpu.CompilerParams(dimension_semantics=sem, vmem_limit_bytes=VMEM_LIMIT_BYTES)


def _ada_kernel(c_ref, w_ref, b_ref, o_ref):
    c = c_ref[...]
    sc = c * jax.nn.sigmoid(c)
    o_ref[...] = _dot(_bf16(sc), _bf16(w_ref[...])) + b_ref[...]


def _ada_mod(c, w_ada, b_ada):
    B, D = c.shape
    n_out = w_ada.shape[1]
    return pl.pallas_call(
        _ada_kernel,
        out_shape=jax.ShapeDtypeStruct((B, n_out), jnp.float32),
        grid=(n_out // D,),
        in_specs=[
            pl.BlockSpec((B, D), lambda j: (0, 0)),
            pl.BlockSpec((D, D), lambda j: (0, j)),
            pl.BlockSpec((1, D), lambda j: (0, j)),
        ],
        out_specs=pl.BlockSpec((B, D), lambda j: (0, j)),
        compiler_params=_cparams(("arbitrary",)),
        name="ada_mod",
    )(c, w_ada, b_ada.reshape(1, n_out))


def _in_proj_kernel(x_ref, mod_ref, g_ref, w_ref, bg_ref, o_ref, qt_ref, vt_ref, *, d_model, n_gate,
                    chunk, col_q, col_v):
    x = x_ref[...]
    tm = x.shape[0]
    shift = mod_ref[0, :, 0:d_model]
    scale = mod_ref[0, :, d_model:2 * d_model]
    h = _rms(x, g_ref[...]) * (1.0 + scale) + shift
    hb = _bf16(h)
    n_total = w_ref.shape[1]
    n_pairs = NA_HEADS // 2
    for c0 in range(0, n_total, chunk):
        c1 = min(c0 + chunk, n_total)
        acc = _dot(hb, w_ref[:, c0:c1])
        if c0 < n_gate:
            acc = jax.nn.sigmoid(acc + bg_ref[:, c0:c1])
        o_ref[:, c0:c1] = _bf16(acc)
        if c0 == col_q:
            for hp in range(n_pairs):
                qt_ref[0, hp] = _bf16((acc[:, hp * LANES:(hp + 1) * LANES] * NA_HEAD_DIM ** -0.5).T)
        if c0 == col_v:
            for hp in range(n_pairs):
                vt = _bf16(acc[:, hp * LANES:(hp + 1) * LANES].T)
                for cc in range(tm // LANES):
                    vt_ref[0, hp, cc] = vt[:, cc * LANES:(cc + 1) * LANES]


def _in_proj(x2, mod3, g_pre, w_ext, b_gate, seq, col_q, col_v, tm=512):
    T, D = x2.shape
    batch = T // seq
    n_s = seq // tm
    n_total = w_ext.shape[1]
    n_gate = b_gate.shape[-1]
    n_pairs = NA_HEADS // 2
    chunk = n_pairs * LANES
    assert col_q % chunk == 0 and col_v % chunk == 0
    kern = functools.partial(_in_proj_kernel, d_model=D, n_gate=n_gate, chunk=chunk, col_q=col_q, col_v=col_v)
    return pl.pallas_call(
        kern,
        out_shape=(
            jax.ShapeDtypeStruct((T, n_total), jnp.bfloat16),
            jax.ShapeDtypeStruct((batch, n_pairs, LANES, seq), jnp.bfloat16),
            jax.ShapeDtypeStruct((batch, n_pairs, seq // LANES, LANES, LANES), jnp.bfloat16),
        ),
        grid=(T // tm,),
        in_specs=[
            pl.BlockSpec((tm, D), lambda i: (i, 0)),
            pl.BlockSpec((1, 1, mod3.shape[-1]), lambda i: ((i * tm) // seq, 0, 0)),
            pl.BlockSpec((1, D), lambda i: (0, 0)),
            pl.BlockSpec((D, n_total), lambda i: (0, 0)),
            pl.BlockSpec((1, n_gate), lambda i: (0, 0)),
        ],
        out_specs=(
            pl.BlockSpec((tm, n_total), lambda i: (i, 0)),
            pl.BlockSpec((1, n_pairs, LANES, tm), lambda i: (i // n_s, 0, 0, i % n_s)),
            pl.BlockSpec((1, n_pairs, tm // LANES, LANES, LANES), lambda i: (i // n_s, 0, i % n_s, 0, 0)),
        ),
        compiler_params=_cparams(("arbitrary",)),
        name="in_proj",
    )(x2, mod3, g_pre.reshape(1, D), w_ext, b_gate.reshape(1, n_gate))


def _na_class(c):
    half = NA_WIN_ROWS // 2
    if c == 0:
        return (lambda i: max(i - half, 0)), NA_WIN_ROWS - 1
    if c == 1:
        return (lambda i: i), NA_WIN_ROWS - 1 - half
    return (lambda i: min(i + half, NA_KEY_ROWS - NA_WIN_ROWS)), -1


NA_PAIR_KEY_ROWS = 10


def _na_kernel(qt_ref, k_ref, vt_ref, pt_ref, o_ref, *, n_rows):
    g = pl.program_id(2)
    n_groups = n_rows // NA_GROUP_ROWS
    kb_rows = jnp.clip(g * NA_GROUP_ROWS - NA_WIN_ROWS // 2, 0, n_rows - NA_KEY_ROWS)
    nkeys = NA_PAIR_KEY_ROWS * GRID_W
    row = lax.broadcasted_iota(jnp.int32, (LANES, LANES), 0)
    lane = lax.broadcasted_iota(jnp.int32, (GRID_W, LANES), 1)
    neg_blk = jnp.full((GRID_W, LANES), NEG_BIG, jnp.float32)

    def body(c):
        j0, off = _na_class(c)
        n_pair = NA_GROUP_ROWS // 2
        starts = [min(j0(2 * ip) - j0(2 * ip) % 2, NA_KEY_ROWS - NA_PAIR_KEY_ROWS) for ip in range(n_pair)]
        scores = []
        for ip in range(n_pair):
            kstart = pl.multiple_of((kb_rows + starts[ip]) * GRID_W, 2 * GRID_W)
            kwin = k_ref[pl.ds(kstart, nkeys), :]
            qp = qt_ref[0, 0, :, ip * LANES:(ip + 1) * LANES]
            zero = jnp.zeros_like(qp)
            wq = jnp.concatenate([jnp.where(row < NA_HEAD_DIM, qp, zero),
                                  jnp.where(row >= NA_HEAD_DIM, qp, zero)], axis=1)
            scores.append(_dot(kwin, wq))
        outs = []
        for ip in range(n_pair):
            i0 = 2 * ip
            w = starts[ip]
            st = scores[ip]
            blocks = []
            for jw in range(NA_PAIR_KEY_ROWS):
                j = w + jw
                v0 = j0(i0) <= j < j0(i0) + NA_WIN_ROWS
                v1 = j0(i0 + 1) <= j < j0(i0 + 1) + NA_WIN_ROWS
                dr0 = j - i0 + off
                halves = []
                for hl in range(2):
                    if not (v0 or v1):
                        halves.append(neg_blk)
                        continue
                    blk = st[jw * GRID_W:(jw + 1) * GRID_W, hl * LANES:(hl + 1) * LANES] + pt_ref[hl, dr0]
                    if not (v0 and v1):
                        keep = (lane < GRID_W) if v0 else (lane >= GRID_W)
                        blk = jnp.where(keep, blk, NEG_BIG)
                    halves.append(blk)
                blocks.append(jnp.concatenate(halves, axis=1))
            s = jnp.concatenate(blocks, axis=0)
            m = jnp.max(s, axis=0, keepdims=True)
            p = jnp.exp(s - m)
            l = jnp.sum(p, axis=0, keepdims=True)
            c0 = (kb_rows + w) // 2
            vwin = jnp.concatenate([vt_ref[0, 0, c0 + u] for u in range(NA_PAIR_KEY_ROWS // 2)], axis=1)
            on = _dot(vwin, _bf16(p)) / l
            outs.append(jnp.where(row < NA_HEAD_DIM, on[:, 0:LANES], on[:, LANES:2 * LANES]))
        o_ref[...] = _bf16(jnp.concatenate(outs, axis=1).T)

    cls = jnp.where(g == 0, 0, jnp.where(g == n_groups - 1, 2, 1))
    for c in range(3):
        pl.when(cls == c)(functools.partial(body, c))


def _na_attention(proj, q_t, v_t, pair_tab, batch, seq, col_k):
    T = proj.shape[0]
    n_rows = seq // GRID_W
    n_groups = n_rows // NA_GROUP_ROWS
    tq = NA_GROUP_ROWS * GRID_W
    n_pairs = NA_HEADS // 2
    kern = functools.partial(_na_kernel, n_rows=n_rows)
    return pl.pallas_call(
        kern,
        out_shape=jax.ShapeDtypeStruct((T, NA_WIDTH), jnp.bfloat16),
        grid=(n_pairs, batch, n_groups),
        in_specs=[
            pl.BlockSpec((1, 1, LANES, tq), lambda hp, b, g: (b, hp, 0, g)),
            pl.BlockSpec((seq, LANES), lambda hp, b, g: (b, col_k // LANES + hp)),
            pl.BlockSpec((1, 1, seq // LANES, LANES, LANES), lambda hp, b, g: (b, hp, 0, 0, 0)),
            pl.BlockSpec((2,) + pair_tab.shape[1:], lambda hp, b, g: (hp, 0, 0, 0)),
        ],
        out_specs=pl.BlockSpec((tq, LANES), lambda hp, b, g: (b * n_groups + g, hp)),
        compiler_params=_cparams(("arbitrary", "arbitrary", "arbitrary")),
        name="na_attn",
    )(q_t, proj, v_t, pair_tab)


def _na_pair_tables(rpb):
    H, n_dr, n_dc = rpb.shape
    kw = NA_WIN_COLS
    W = GRID_W
    ring = 2 * W - 1
    rp = jnp.pad(rpb.astype(jnp.float32), ((0, 0), (0, 0), (W - kw, W - kw)), constant_values=NEG_BIG)
    u = jnp.roll(rp, -(W - 1), axis=-1)
    circ = jnp.tile(u, (1, 1, W))[..., :W * (ring - 1)].reshape(H, n_dr, W, ring - 1)
    toep = circ[..., :W]
    cidx = np.arange(W)
    col_start = np.clip(cidx - kw // 2, 0, W - kw)
    col_in = (cidx[None, :] >= col_start[:, None]) & (cidx[None, :] < col_start[:, None] + kw)
    toep_t = jnp.where(jnp.asarray(col_in.T)[None, None], jnp.swapaxes(toep, 2, 3), NEG_BIG)
    neg = jnp.full((H, 1, W, W), NEG_BIG, jnp.float32)
    ext = jnp.concatenate([neg, toep_t, neg], axis=1)
    return jnp.concatenate([ext[:, 1:], ext[:, :-1]], axis=-1)


def _mla_prep_kernel(cq_ref, ckv_ref, kr_ref, gq_ref, gkv_ref, wqm_ref, wqs_ref, wk_ref, wv_ref,
                     pm_ref, tab_ref, q_out, k_out, v_out, *, qscale):
    cq = _f32(cq_ref[...])
    cqn = _bf16(_rms(cq, gq_ref[...], n=MLA_Q_LORA))
    qm = _dot(cqn, wqm_ref[...])
    qs = _dot(cqn, wqs_ref[...])
    ctab = tab_ref[:, 0:LANES]
    stab = tab_ref[:, LANES:2 * LANES]
    for h in range(MLA_HEADS):
        sl = slice(h * LANES, (h + 1) * LANES)
        qh = (qm[:, sl] * ctab + qs[:, sl] * stab) * qscale
        q_out[0, h] = _bf16(qh.T)

    ckv = _f32(ckv_ref[...])
    ckvn = _bf16(_rms(ckv, gkv_ref[...]))
    kk = _dot(ckvn, wk_ref[...])
    vv = _dot(ckvn, wv_ref[...])
    n_chunk = v_out.shape[2]
    ones = jnp.ones((MLA_VROWS - MLA_V, MLA_KEY_CHUNK), jnp.bfloat16)
    for hp in range(MLA_HEADS // 2):
        vt = _bf16(vv[:, hp * LANES:(hp + 1) * LANES].T)
        for cc in range(n_chunk):
            ks = slice(cc * MLA_KEY_CHUNK, (cc + 1) * MLA_KEY_CHUNK)
            for h in range(2):
                r0 = h * MLA_VROWS
                v_out[0, hp, cc, r0:r0 + MLA_V, :] = vt[h * MLA_V:(h + 1) * MLA_V, ks]
                v_out[0, hp, cc, r0 + MLA_V:r0 + MLA_VROWS, :] = ones
    krr = _f32(kr_ref[...]) * tab_ref[:, 2 * LANES:3 * LANES]
    kplace = _dot(_bf16(krr), pm_ref[...])
    for h in range(MLA_HEADS):
        sl = slice(h * LANES, (h + 1) * LANES)
        k_out[:, sl] = _bf16(kk[:, sl] + kplace)


def _mla_prep(proj, col_cq, col_ckv, col_kr, gq, gkv, wqm, wqs, wk, wv, pm, rope_tab, seq, qscale, tm=512):
    T = proj.shape[0]
    batch = T // seq
    n_s = seq // tm
    n_pairs = MLA_HEADS // 2
    cpt = tm // MLA_KEY_CHUNK
    full = lambda a: pl.BlockSpec(a.shape, lambda i: (0,) * a.ndim)
    tab = pl.BlockSpec((tm, 3 * LANES), lambda i: (i % n_s, 0))
    kern = functools.partial(_mla_prep_kernel, qscale=qscale)
    return pl.pallas_call(
        kern,
        out_shape=(
            jax.ShapeDtypeStruct((batch, MLA_HEADS, LANES, seq), jnp.bfloat16),
            jax.ShapeDtypeStruct((T, MLA_HEADS * LANES), jnp.bfloat16),
            jax.ShapeDtypeStruct((batch, n_pairs, seq // MLA_KEY_CHUNK, 2 * MLA_VROWS, MLA_KEY_CHUNK),
                                 jnp.bfloat16),
        ),
        grid=(T // tm,),
        in_specs=[
            pl.BlockSpec((tm, Q_LORA_PAD), lambda i: (i, col_cq // Q_LORA_PAD)),
            pl.BlockSpec((tm, MLA_KV_LORA), lambda i: (i, col_ckv // MLA_KV_LORA)),
            pl.BlockSpec((tm, LANES), lambda i: (i, col_kr // LANES)),
            full(gq), full(gkv), full(wqm), full(wqs), full(wk), full(wv), full(pm),
            tab,
        ],
        out_specs=(
            pl.BlockSpec((1, MLA_HEADS, LANES, tm), lambda i: (i // n_s, 0, 0, i % n_s)),
            pl.BlockSpec((tm, MLA_HEADS * LANES), lambda i: (i, 0)),
            pl.BlockSpec((1, n_pairs, cpt, 2 * MLA_VROWS, MLA_KEY_CHUNK),
                         lambda i: (i // n_s, 0, i % n_s, 0, 0)),
        ),
        compiler_params=_cparams(("arbitrary",)),
        name="mla_prep",
    )(proj, proj, proj, gq, gkv, wqm, wqs, wk, wv, pm, rope_tab)


def _mla_attn_kernel(qt_ref, k_ref, vt_ref, o_ref, m_scr, acc_scr, st_scr):
    n_chunks = vt_ref.shape[2]
    tk = MLA_KEY_CHUNK
    m_scr[...] = jnp.full(m_scr.shape, -jnp.inf, jnp.float32)
    acc_scr[...] = jnp.zeros(acc_scr.shape, jnp.float32)

    def scores(c, slot):
        k0 = pl.multiple_of(c * tk, tk)
        for h in range(2):
            kc = k_ref[pl.ds(k0, tk), h * LANES:(h + 1) * LANES]
            st_scr[slot, h] = _dot(kc, qt_ref[0, h])

    def step(c, slot):
        scores(jnp.minimum(c + 1, n_chunks - 1), 1 - slot)
        vt = vt_ref[0, 0, c]
        pts, alphas = [], []
        for h in range(2):
            st = st_scr[slot, h]
            m_old = m_scr[h]
            m_new = jnp.maximum(m_old, jnp.max(st, axis=0, keepdims=True))
            alpha = jnp.exp2(m_old - m_new)
            pt = jnp.exp2(st - m_new)
            m_scr[h] = m_new
            pts.append(_bf16(pt))
            alphas.append(alpha)
        for h in range(2):
            acc_scr[h] = alphas[h] * acc_scr[h] + _dot(vt[h * MLA_VROWS:(h + 1) * MLA_VROWS, :], pts[h])

    def body(j, carry):
        for u in range(MLA_STEPS_PER_TRIP):
            step(MLA_STEPS_PER_TRIP * j + u, u % 2)
        return carry

    scores(0, 0)
    lax.fori_loop(0, n_chunks // MLA_STEPS_PER_TRIP, body, 0)
    outs = []
    for h in range(2):
        acc = acc_scr[h]
        outs.append(acc[0:MLA_V] / acc[MLA_V:MLA_V + 1])
    o_ref[...] = _bf16(jnp.concatenate(outs, axis=0).T)


def _mla_attention(q_t, k_cat, v_t, batch, seq, tq=512):
    T = k_cat.shape[0]
    n_q = seq // tq
    n_pairs = MLA_HEADS // 2
    n_chunks = seq // MLA_KEY_CHUNK
    assert n_chunks % MLA_STEPS_PER_TRIP == 0
    return pl.pallas_call(
        _mla_attn_kernel,
        out_shape=jax.ShapeDtypeStruct((T, MLA_WIDTH), jnp.bfloat16),
        grid=(batch, n_pairs, n_q),
        in_specs=[
            pl.BlockSpec((1, 2, LANES, tq), lambda b, hp, i: (b, hp, 0, i)),
            pl.BlockSpec((seq, 2 * LANES), lambda b, hp, i: (b, hp)),
            pl.BlockSpec((1, 1, n_chunks, 2 * MLA_VROWS, MLA_KEY_CHUNK), lambda b, hp, i: (b, hp, 0, 0, 0)),
        ],
        out_specs=pl.BlockSpec((tq, LANES), lambda b, hp, i: (b * n_q + i, hp)),
        scratch_shapes=[
            pltpu.VMEM((2, 1, tq), jnp.float32),
            pltpu.VMEM((2, MLA_VROWS, tq), jnp.float32),
            pltpu.VMEM((2, 2, MLA_KEY_CHUNK, tq), jnp.float32),
        ],
        compiler_params=_cparams(("arbitrary", "arbitrary", "arbitrary")),
        name="mla_attn",
    )(q_t, k_cat, v_t)


def _out_proj_kernel(ona_ref, omla_ref, gate_ref, x_ref, mod_ref, gpost_ref, gpre_ref, wna_ref,
                     wmla_ref, wout_ref, wr_ref, br_ref, x1_ref, h2_ref, lg_ref, *, d_model):
    D = d_model
    tm = x_ref.shape[0]
    gate_a = mod_ref[0, :, 2 * D:3 * D]
    shift_f = mod_ref[0, :, 3 * D:4 * D]
    scale_f = mod_ref[0, :, 4 * D:5 * D]
    halves = [slice(0, tm // 2), slice(tm // 2, tm)]
    merged = []
    for rs in halves:
        up_na = _dot(ona_ref[rs, :], wna_ref[...])
        up_mla = _dot(omla_ref[rs, :], wmla_ref[...])
        merged.append(_bf16(_f32(gate_ref[rs, 0:D]) * up_na + _f32(gate_ref[rs, D:2 * D]) * up_mla))
    ys = [_dot(m, wout_ref[...]) for m in merged]
    h2s = []
    for rs, y in zip(halves, ys):
        x1 = x_ref[rs, :] + gate_a * _rms(y, gpost_ref[...])
        x1_ref[rs, :] = x1
        h2 = _rms(x1, gpre_ref[...]) * (1.0 + scale_f) + shift_f
        _store_row_tiles(h2_ref, h2, base=rs.start * SUBLANES)
        h2s.append(_bf16(h2))
    for rs, hb in zip(halves, h2s):
        lg_ref[rs, :] = _dot(hb, wr_ref[...]) + br_ref[...]


def _out_proj(o_na, o_mla, proj, x2, mod3, g_post, g_pre, w_na, w_mla, w_out, w_r, b_r, seq, tm=512):
    T, D = x2.shape
    full = lambda a: pl.BlockSpec(a.shape, lambda i: (0,) * a.ndim)
    row = lambda w: pl.BlockSpec((tm, w), lambda i: (i, 0))
    kern = functools.partial(_out_proj_kernel, d_model=D)
    g_post = g_post.reshape(1, D)
    g_pre = g_pre.reshape(1, D)
    return pl.pallas_call(
        kern,
        out_shape=(
            jax.ShapeDtypeStruct((T, D), jnp.float32),
            jax.ShapeDtypeStruct((T * (D // LANES), LANES), jnp.float32),
            jax.ShapeDtypeStruct((T, LANES), jnp.float32),
        ),
        grid=(T // tm,),
        in_specs=[
            row(NA_WIDTH), row(MLA_WIDTH),
            pl.BlockSpec((tm, 2 * D), lambda i: (i, COL_GATES // (2 * D))),
            row(D),
            pl.BlockSpec((1, 1, mod3.shape[-1]), lambda i: ((i * tm) // seq, 0, 0)),
            full(g_post), full(g_pre), full(w_na), full(w_mla), full(w_out), full(w_r), full(b_r),
        ],
        out_specs=(row(D), pl.BlockSpec((tm * (D // LANES), LANES), lambda i: (i, 0)), row(LANES)),
        compiler_params=_cparams(("arbitrary",)),
        name="out_proj",
    )(o_na, o_mla, proj, x2, mod3, g_post, g_pre, w_na, w_mla, w_out, w_r, b_r)


def _route_kernel(lg_ref, r_ref, cnt_ref, carry_scr, *, sub):
    @pl.when(pl.program_id(0) == 0)
    def _():
        carry_scr[...] = jnp.zeros(carry_scr.shape, jnp.float32)

    tr = lg_ref.shape[0]
    lane = lax.broadcasted_iota(jnp.int32, (sub, LANES), 1).astype(jnp.float32)
    ri = lax.broadcasted_iota(jnp.int32, (sub, sub), 0)
    ci = lax.broadcasted_iota(jnp.int32, (sub, sub), 1)
    tri = jnp.where(ri >= ci, 1.0, 0.0).astype(jnp.bfloat16)
    for s0 in range(0, tr, sub):
        work = lg_ref[s0:s0 + sub, :]
        sels, vals, idxs = [], [], []
        for _k in range(TOP_K):
            mk = jnp.max(work, axis=-1, keepdims=True)
            ik = jnp.min(jnp.where(work == mk, lane, float(LANES)), axis=-1, keepdims=True)
            sk = lane == ik
            work = jnp.where(sk, -jnp.inf, work)
            sels.append(sk)
            vals.append(mk)
            idxs.append(ik)
        es = [jnp.exp(v - vals[0]) for v in vals]
        denom = es[0] + es[1] + es[2] + es[3]
        onehot = jnp.zeros((sub, LANES), jnp.float32)
        for sk in sels:
            onehot = jnp.where(sk, 1.0, onehot)
        prefix = _dot(tri, _bf16(onehot))
        carry = carry_scr[...]
        rank_mat = carry + prefix - 1.0
        res = jnp.zeros((sub, LANES), jnp.float32)
        for kk in range(TOP_K):
            rank_k = jnp.sum(jnp.where(sels[kk], rank_mat, 0.0), axis=-1, keepdims=True)
            res = jnp.where(lane == kk, idxs[kk], res)
            res = jnp.where(lane == TOP_K + kk, rank_k, res)
            res = jnp.where(lane == 2 * TOP_K + kk, es[kk] / denom, res)
        r_ref[s0:s0 + sub, :] = res
        carry_scr[...] = carry + jnp.sum(onehot, axis=0, keepdims=True)
    cnt_ref[...] = carry_scr[...]


def _route(logits, tr=2048, sub=256):
    T = logits.shape[0]
    kern = functools.partial(_route_kernel, sub=sub)
    return pl.pallas_call(
        kern,
        out_shape=(
            jax.ShapeDtypeStruct((T, LANES), jnp.float32),
            jax.ShapeDtypeStruct((1, LANES), jnp.float32),
        ),
        grid=(T // tr,),
        in_specs=[pl.BlockSpec((tr, LANES), lambda i: (i, 0))],
        out_specs=(
            pl.BlockSpec((tr, LANES), lambda i: (i, 0)),
            pl.BlockSpec((1, LANES), lambda i: (0, 0)),
        ),
        scratch_shapes=[pltpu.VMEM((1, LANES), jnp.float32)],
        compiler_params=_cparams(("arbitrary",)),
        name="route",
    )(logits)


def _dispatch_kernel(zero_blk_ref, dest_ref, h_ref, row0_ref, xs_ref, row_ref, zbuf, sem, fill_sem,
                     *, n_tokens):
    i = pl.program_id(0)
    ts = h_ref.shape[0] // SUBLANES
    blk_rows = MOE_BLOCK * SUBLANES

    @pl.when(i == 0)
    def _():
        fill_row = pltpu.make_async_copy(row0_ref, row_ref, fill_sem.at[0])
        fill_row.start()
        zbuf[...] = jnp.zeros(zbuf.shape, zbuf.dtype)
        n_zero = zero_blk_ref.shape[0]

        def zero_copy(z):
            r0 = pl.multiple_of(zero_blk_ref[z] * blk_rows, blk_rows)
            return pltpu.make_async_copy(zbuf, xs_ref.at[pl.ds(r0, blk_rows), :], fill_sem.at[1])

        for z in range(n_zero):
            pl.when(zero_blk_ref[z] >= 0)(lambda z=z: zero_copy(z).start())
        for z in range(n_zero):
            pl.when(zero_blk_ref[z] >= 0)(lambda z=z: zero_copy(z).wait())
        fill_row.wait()

    def row_copy(j, d):
        dst = pl.ds(pl.multiple_of(d * SUBLANES, SUBLANES), SUBLANES)
        return pltpu.make_async_copy(h_ref.at[pl.ds(j * SUBLANES, SUBLANES), :], xs_ref.at[dst, :], sem)

    t0 = i * ts
    for j in range(ts):
        for kk in range(TOP_K):
            d = dest_ref[j * TOP_K + kk]
            row_copy(j, d).start(priority=kk % 2)
            row_ref[d] = (kk * n_tokens + j) + t0
    for _j in range(ts * TOP_K):
        row_copy(0, 0).wait()


def _dispatch(h2_tiles, dest, zero_blk, cap, ts=256):
    T = h2_tiles.shape[0] // SUBLANES
    kern = functools.partial(_dispatch_kernel, n_tokens=T)
    grid_spec = pltpu.PrefetchScalarGridSpec(
        num_scalar_prefetch=1,
        grid=(T // ts,),
        in_specs=[
            pl.BlockSpec((ts * TOP_K,), lambda i, zb: (i,), memory_space=pltpu.SMEM),
            pl.BlockSpec((ts * SUBLANES, LANES), lambda i, zb: (i, 0)),
            pl.BlockSpec(memory_space=pl.ANY),
        ],
        out_specs=(pl.BlockSpec(memory_space=pl.ANY), pl.BlockSpec(memory_space=pltpu.SMEM)),
        scratch_shapes=[
            pltpu.VMEM((MOE_BLOCK * SUBLANES, LANES), h2_tiles.dtype),
            pltpu.SemaphoreType.DMA(()),
            pltpu.SemaphoreType.DMA((2,)),
        ],
    )
    return pl.pallas_call(
        kern,
        out_shape=(jax.ShapeDtypeStruct((cap * SUBLANES, LANES), h2_tiles.dtype),
                   jax.ShapeDtypeStruct((cap,), jnp.int32)),
        grid_spec=grid_spec,
        compiler_params=_cparams(("arbitrary",)),
        name="dispatch",
    )(zero_blk, dest, h2_tiles, jnp.full((cap,), -1, jnp.int32))


def _moe_kernel(blk_e_ref, nvalid_ref, row_prev_ref, row_ref, x_ref, w1_ref, b1_ref, w2_ref, b2_ref,
                out_ref, w1p_scr, w2b_scr, ybuf0, ybuf1, ssem, *, n_items):
    i = pl.program_id(0)
    nvalid = nvalid_ref[0]
    d_ff = w2_ref.shape[1]
    n_groups = (2 * d_ff) // (2 * LANES)

    def tile(r):
        return pl.ds(pl.multiple_of(r * SUBLANES, SUBLANES), SUBLANES)

    ybufs = (ybuf0, ybuf1)

    def scatter_copy(j, r, slot):
        return pltpu.make_async_copy(ybufs[slot].at[pl.ds(j * SUBLANES, SUBLANES), :], out_ref.at[tile(r), :],
                                     ssem.at[slot])

    def wait_scatters(slot):
        for _j in range(MOE_BLOCK):
            scatter_copy(0, 0, slot).wait()

    def scatter_rows(rows_ref, slot):
        spare0 = n_items + slot * MOE_BLOCK
        for j in range(MOE_BLOCK):
            r = rows_ref[0, 0, j]
            scatter_copy(j, jnp.where(r >= 0, r, spare0 + j), slot).start(priority=j % 2)

    @pl.when(i == 0)
    def _():
        ybuf0[...] = jnp.zeros(ybuf0.shape, ybuf0.dtype)
        ybuf1[...] = jnp.zeros(ybuf1.shape, ybuf1.dtype)
        for j in range(MOE_BLOCK):
            scatter_copy(j, n_items + j, 0).start(priority=j % 2)

    @pl.when(i < nvalid)
    def _():
        e = blk_e_ref[i]
        e_prev = blk_e_ref[jnp.maximum(i - 1, 0)]

        @pl.when((i == 0) | (e != e_prev))
        def _():
            r = lax.broadcasted_iota(jnp.int32, (2 * LANES, 2 * LANES), 0)
            c = lax.broadcasted_iota(jnp.int32, (2 * LANES, 2 * LANES), 1)
            src = jnp.where(c < LANES, 2 * c, 2 * (c - LANES) + 1)
            perm = jnp.where(r == src, 1.0, 0.0).astype(jnp.bfloat16)
            for gI in range(n_groups):
                sl = slice(gI * 2 * LANES, (gI + 1) * 2 * LANES)
                w1p_scr[:, sl] = _bf16(_dot(_bf16(w1_ref[0, :, sl]), perm))
            w2b_scr[...] = _bf16(w2_ref[0])

        def block_step(slot):
            wait_scatters(slot)
            scatter_rows(row_prev_ref, 1 - slot)

            xb = _bf16(_load_row_tiles(x_ref, MOE_BLOCK))
            hcat = _dot(xb, w1p_scr[...]) + b1_ref[0]
            acts = []
            for gI in range(n_groups):
                glu = jnp.minimum(hcat[:, gI * 2 * LANES: gI * 2 * LANES + LANES], SWIGLU_LIMIT)
                lin = jnp.clip(hcat[:, gI * 2 * LANES + LANES: (gI + 1) * 2 * LANES],
                               -SWIGLU_LIMIT, SWIGLU_LIMIT)
                acts.append(_bf16(glu * jax.nn.sigmoid(SWIGLU_ALPHA * glu) * (lin + 1.0)))
            act = jnp.concatenate(acts, axis=-1)
            _store_row_tiles(ybufs[slot], _dot(act, w2b_scr[...]) + b2_ref[0])

            @pl.when(i == nvalid - 1)
            def _():
                scatter_rows(row_ref, slot)
                wait_scatters(1 - slot)
                wait_scatters(slot)

        parity = lax.rem(i, 2)
        for slot in range(2):
            pl.when(parity == slot)(functools.partial(block_step, slot))


def _moe_ffn(xs, slot_row, n_items, blk_e, nvalid, w1, b1p, w2, b2):
    D = w1.shape[1]
    assert D == SUBLANES * LANES
    E, _, F2 = w1.shape
    F = w2.shape[1]
    nblk = slot_row.shape[0]

    def cur(i, be, nv):
        return (jnp.minimum(i, nv[0] - 1), 0, 0)

    def prv(i, be, nv):
        return (jnp.where(i == 0, nblk, jnp.maximum(jnp.minimum(i, nv[0] - 1) - 1, 0)), 0, 0)

    slot_row = jnp.concatenate([slot_row, jnp.full((1, 1, MOE_BLOCK), -1, slot_row.dtype)], axis=0)
    slot_spec = lambda f: pl.BlockSpec((1, 1, MOE_BLOCK), f, memory_space=pltpu.SMEM)
    grid_spec = pltpu.PrefetchScalarGridSpec(
        num_scalar_prefetch=2,
        grid=(nblk,),
        in_specs=[
            slot_spec(prv), slot_spec(cur),
            pl.BlockSpec((MOE_BLOCK * SUBLANES, LANES), lambda i, be, nv: (jnp.minimum(i, nv[0] - 1), 0)),
            pl.BlockSpec((1, D, F2), lambda i, be, nv: (be[i], 0, 0)),
            pl.BlockSpec((1, 1, F2), lambda i, be, nv: (be[i], 0, 0)),
            pl.BlockSpec((1, F, D), lambda i, be, nv: (be[i], 0, 0)),
            pl.BlockSpec((1, 1, D), lambda i, be, nv: (be[i], 0, 0)),
        ],
        out_specs=pl.BlockSpec(memory_space=pl.ANY),
        scratch_shapes=[
            pltpu.VMEM((D, F2), jnp.bfloat16),
            pltpu.VMEM((F, D), jnp.bfloat16),
            pltpu.VMEM((MOE_BLOCK * SUBLANES, LANES), jnp.float32),
            pltpu.VMEM((MOE_BLOCK * SUBLANES, LANES), jnp.float32),
            pltpu.SemaphoreType.DMA((2,)),
        ],
    )
    kern = functools.partial(_moe_kernel, n_items=n_items)
    return pl.pallas_call(
        kern,
        out_shape=jax.ShapeDtypeStruct(((n_items + 2 * MOE_BLOCK) * SUBLANES, LANES), jnp.float32),
        grid_spec=grid_spec,
        compiler_params=_cparams(("arbitrary",)),
        name="moe_ffn",
    )(blk_e, nvalid, slot_row, slot_row, xs, w1, b1p, w2, b2)


def _combine_kernel(y0_ref, y1_ref, y2_ref, y3_ref, r_ref, x1_ref, mod_ref, g_ref, o_ref, *, d_model):
    D = d_model
    tc = x1_ref.shape[0]
    r = r_ref[...]
    moe = None
    for kk, y_ref in enumerate((y0_ref, y1_ref, y2_ref, y3_ref)):
        term = r[:, 2 * TOP_K + kk: 2 * TOP_K + kk + 1] * _load_row_tiles(y_ref, tc)
        moe = term if moe is None else moe + term
    gate_f = mod_ref[0, :, 5 * D:6 * D]
    o_ref[...] = x1_ref[...] + gate_f * _rms(moe, g_ref[...])


def _combine(y_k, route, x1, mod3, g_post, seq, tc=512):
    T, D = x1.shape
    n_t = T // tc
    kern = functools.partial(_combine_kernel, d_model=D)
    y_spec = lambda kk: pl.BlockSpec((tc * SUBLANES, LANES), lambda i: (kk * n_t + i, 0))
    return pl.pallas_call(
        kern,
        out_shape=jax.ShapeDtypeStruct((T, D), jnp.float32),
        grid=(n_t,),
        in_specs=[
            y_spec(0), y_spec(1), y_spec(2), y_spec(3),
            pl.BlockSpec((tc, LANES), lambda i: (i, 0)),
            pl.BlockSpec((tc, D), lambda i: (i, 0)),
            pl.BlockSpec((1, 1, mod3.shape[-1]), lambda i: ((i * tc) // seq, 0, 0)),
            pl.BlockSpec((1, D), lambda i: (0, 0)),
        ],
        out_specs=pl.BlockSpec((tc, D), lambda i: (i, 0)),
        compiler_params=_cparams(("arbitrary",)),
        name="combine",
    )(y_k, y_k, y_k, y_k, route, x1, mod3, g_post.reshape(1, D))


def _rope_swap_cols(w):
    nf = MLA_ROPE // 4
    return jnp.concatenate([-w[..., nf:2 * nf], w[..., 0:nf], -w[..., 3 * nf:4 * nf], w[..., 2 * nf:3 * nf]],
                           axis=-1)


def _rope_tables(seq):
    n_rows = seq // GRID_W
    nf = MLA_ROPE // 4
    inv = ROPE_THETA ** (-jnp.arange(nf, dtype=jnp.float32) / nf)
    ar = jnp.arange(n_rows, dtype=jnp.float32)[:, None] * inv
    ac = jnp.arange(GRID_W, dtype=jnp.float32)[:, None] * inv
    cr, sr, cc, sc = jnp.cos(ar), jnp.sin(ar), jnp.cos(ac), jnp.sin(ac)

    def lanes(parts, n):
        z = lambda w: jnp.zeros((n, w), jnp.float32)
        return jnp.concatenate([z(p) if isinstance(p, int) else p for p in parts], axis=-1)

    pad = LANES - MLA_NOPE - MLA_ROPE
    row_tab = lanes([MLA_NOPE, cr, cr, 2 * nf, pad, MLA_NOPE, sr, sr, 2 * nf, pad,
                     cr, cr, 2 * nf, sr, sr, 2 * nf, MLA_NOPE], n_rows)
    col_tab = lanes([jnp.ones((GRID_W, MLA_NOPE), jnp.float32), 2 * nf, cc, cc, pad, MLA_NOPE, 2 * nf, sc, sc, pad,
                     2 * nf, cc, cc, 2 * nf, sc, sc, MLA_NOPE], GRID_W)
    return (row_tab[:, None, :] + col_tab[None, :, :]).reshape(seq, 3 * LANES)


def kernel(x, c, w_ada, b_ada, g_attn_pre, g_attn_post, w_in, b_gate, na_rpb, q_norm_g, kv_norm_g,
           w_uq, w_ukv, w_na_up, w_mla_up, w_out, g_ffn_pre, g_ffn_post, w_router, b_router,
           w1, b1, w2, b2):
    B, S, D = x.shape
    T = B * S
    depth = w_ada.shape[0]
    E = w_router.shape[-1]
    n_rows = S // GRID_W
    assert S % (NA_GROUP_ROWS * GRID_W) == 0 and n_rows >= NA_KEY_ROWS
    assert E <= LANES and T % 2048 == 0

    bf = jnp.bfloat16
    rope_tab = _rope_tables(S)
    pm_np = np.zeros((LANES, LANES), np.float32)
    for cidx in range(MLA_ROPE):
        pm_np[cidx, MLA_NOPE + cidx] = 1.0
        pm_np[MLA_ROPE + cidx, MLA_NOPE + cidx] = 1.0
    pm = jnp.asarray(pm_np, bf)
    qscale = float((MLA_NOPE + MLA_ROPE) ** -0.5 * math.log2(math.e))

    col_na = 2 * D
    col_cq = col_na + 3 * NA_WIDTH
    col_ckv = col_cq + Q_LORA_PAD
    col_kr = col_ckv + MLA_KV_LORA
    assert col_cq % Q_LORA_PAD == 0 and col_ckv % MLA_KV_LORA == 0

    x2 = x.reshape(T, D)
    for l in range(depth):
        mod = _ada_mod(c, w_ada[l], b_ada[l])
        mod3 = mod.reshape(B, 1, 6 * D)

        o_na, o_cq, o_ckv, o_kr, o_g = np.cumsum([0, 3 * NA_WIDTH, MLA_Q_LORA, MLA_KV_LORA, MLA_ROPE]).tolist()
        wi = w_in[l]
        w_kr = wi[:, o_kr:o_g]
        w_ext = jnp.concatenate([
            wi[:, o_g:o_g + 2 * D],
            wi[:, o_na:o_cq],
            wi[:, o_cq:o_ckv], jnp.zeros((D, Q_LORA_PAD - MLA_Q_LORA), wi.dtype),
            wi[:, o_ckv:o_kr],
            w_kr, _rope_swap_cols(w_kr), jnp.zeros((D, LANES - 2 * MLA_ROPE), wi.dtype),
        ], axis=1).astype(bf)
        proj, na_qt, na_vt = _in_proj(x2, mod3, g_attn_pre[l], w_ext, b_gate[l], S,
                                      col_na, col_na + 2 * NA_WIDTH)

        o_na_tok = _na_attention(proj, na_qt, na_vt, _na_pair_tables(na_rpb[l]), B, S, col_na + NA_WIDTH)

        wq = w_uq[l]
        zq = jnp.zeros((MLA_Q_LORA, MLA_HEADS, LANES - MLA_NOPE - MLA_ROPE), wq.dtype)
        wqm = jnp.concatenate([wq, zq], axis=-1).reshape(MLA_Q_LORA, MLA_HEADS * LANES)
        wqs = jnp.concatenate([jnp.zeros((MLA_Q_LORA, MLA_HEADS, MLA_NOPE), wq.dtype),
                               _rope_swap_cols(wq[..., MLA_NOPE:]), zq], axis=-1
                              ).reshape(MLA_Q_LORA, MLA_HEADS * LANES)
        rpad = ((0, Q_LORA_PAD - MLA_Q_LORA), (0, 0))
        wqm = jnp.pad(wqm, rpad).astype(bf)
        wqs = jnp.pad(wqs, rpad).astype(bf)
        wkv = w_ukv[l]
        wk = jnp.concatenate([wkv[..., :MLA_NOPE], jnp.zeros((MLA_KV_LORA, MLA_HEADS, LANES - MLA_NOPE), wkv.dtype)],
                             axis=-1).reshape(MLA_KV_LORA, MLA_HEADS * LANES).astype(bf)
        wv = wkv[..., MLA_NOPE:].reshape(MLA_KV_LORA, MLA_WIDTH).astype(bf)
        gq = jnp.pad(q_norm_g[l], (0, Q_LORA_PAD - MLA_Q_LORA)).reshape(1, Q_LORA_PAD)
        gkv = kv_norm_g[l].reshape(1, MLA_KV_LORA)
        q_cat, k_cat, v_mla = _mla_prep(proj, col_cq, col_ckv, col_kr, gq, gkv, wqm, wqs, wk, wv, pm,
                                        rope_tab, S, qscale)
        o_mla_tok = _mla_attention(q_cat, k_cat, v_mla, B, S)

        w_r = jnp.pad(w_router[l], ((0, 0), (0, LANES - E))).astype(bf)
        b_r = jnp.concatenate([b_router[l], jnp.full((LANES - E,), NEG_BIG, jnp.float32)]).reshape(1, LANES)
        x1, h2, logits = _out_proj(o_na_tok, o_mla_tok, proj, x2, mod3, g_attn_post[l], g_ffn_pre[l],
                                   w_na_up[l].astype(bf), w_mla_up[l].astype(bf), w_out[l].astype(bf),
                                   w_r, b_r, S)

        route, counts = _route(logits)
        e_idx = route[:, 0:TOP_K].astype(jnp.int32)
        rank = route[:, TOP_K:2 * TOP_K].astype(jnp.int32)
        cnt = counts[0, :E].astype(jnp.int32)
        padded = ((cnt + MOE_BLOCK - 1) // MOE_BLOCK) * MOE_BLOCK
        pend = jnp.cumsum(padded)
        pstart = pend - padded
        n_items = T * TOP_K
        nblk = -(-n_items // MOE_BLOCK) + E
        cap = nblk * MOE_BLOCK
        dest = (rank + jnp.sum(jnp.where(e_idx[..., None] == jnp.arange(E, dtype=jnp.int32), pstart, 0), axis=-1)
                ).reshape(n_items)
        blk_off = jnp.arange(nblk, dtype=jnp.int32) * MOE_BLOCK
        blk_e = jnp.minimum(jnp.sum((pend[None, :] <= blk_off[:, None]).astype(jnp.int32), axis=-1), E - 1)
        nvalid = (pend[-1:] // MOE_BLOCK).astype(jnp.int32)

        last_blk = jnp.where(padded > 0, pend // MOE_BLOCK - 1, -1)
        tail_blk = nvalid[0] + jnp.arange(nblk - n_items // MOE_BLOCK, dtype=jnp.int32)
        zero_blk = jnp.concatenate([last_blk, jnp.where(tail_blk < nblk, tail_blk, -1)]).astype(jnp.int32)
        xs, slot_row = _dispatch(h2, dest, zero_blk, cap)
        slot_row = slot_row.reshape(nblk, 1, MOE_BLOCK)
        F2 = w1.shape[-1]
        b1p = b1[l].reshape(E, F2 // (2 * LANES), LANES, 2).transpose(0, 1, 3, 2).reshape(E, 1, F2)
        y_k = _moe_ffn(xs, slot_row, n_items, blk_e, nvalid, w1[l], b1p, w2[l], b2[l].reshape(E, 1, D))
        x2 = _combine(y_k, route, x1, mod3, g_ffn_post[l], S)
    return x2.reshape(B, S, D)
```

```python
import functools
import math

import numpy as np
import jax
import jax.numpy as jnp
from jax import lax
from jax.experimental import pallas as pl
from jax.experimental.pallas import tpu as pltpu

GRID_W = 64
NA_HEADS = 8
NA_HEAD_DIM = 64
NA_WIN_ROWS = 8
NA_WIN_COLS = 16
NA_WIDTH = NA_HEADS * NA_HEAD_DIM
MLA_HEADS = 8
MLA_Q_LORA = 384
MLA_KV_LORA = 256
MLA_NOPE = 64
MLA_ROPE = 32
MLA_V = 64
MLA_WIDTH = MLA_HEADS * MLA_V
MLA_KEY_CHUNK = 256
MLA_VROWS = MLA_V + 16
MLA_STEPS_PER_TRIP = 16
ROPE_THETA = 100.0
TOP_K = 4
SWIGLU_ALPHA = 1.702
SWIGLU_LIMIT = 7.0
MOE_BLOCK = 256
RMS_EPS = 1e-6
NEG_BIG = -1e30

LANES = 128
VMEM_LIMIT_BYTES = 56 * 1024 * 1024

Q_LORA_PAD = 512
COL_GATES = 0
NA_GROUP_ROWS = 8
NA_KEY_ROWS = 16


def _f32(x):
    return x.astype(jnp.float32)


def _bf16(x):
    return x.astype(jnp.bfloat16)


def _dot(a, b):
    return jnp.dot(a, b, preferred_element_type=jnp.float32)


def _rms(x, g, n=None):
    n = x.shape[-1] if n is None else n
    ms = jnp.sum(x * x, axis=-1, keepdims=True) * (1.0 / n)
    return x * lax.rsqrt(ms + RMS_EPS) * g


SUBLANES = 8


def _store_row_tiles(ref, x, base=0):
    n = x.shape[0]
    for c in range(x.shape[1] // LANES):
        ref[pl.ds(base + c, n, stride=SUBLANES), :] = x[:, c * LANES:(c + 1) * LANES]


def _load_row_tiles(ref, n, base=0):
    chunks = [ref[pl.ds(base + c, n, stride=SUBLANES), :] for c in range(SUBLANES)]
    return jnp.concatenate(chunks, axis=-1)


def _cparams(sem):
    return pltpu.CompilerParams(dimension_semantics=sem, vmem_limit_bytes=VMEM_LIMIT_BYTES)


def _ada_kernel(c_ref, w_ref, b_ref, o_ref):
    c = c_ref[...]
    sc = c * jax.nn.sigmoid(c)
    o_ref[...] = _dot(_bf16(sc), _bf16(w_ref[...])) + b_ref[...]


def _ada_mod(c, w_ada, b_ada):
    B, D = c.shape
    n_out = w_ada.shape[1]
    return pl.pallas_call(
        _ada_kernel,
        out_shape=jax.ShapeDtypeStruct((B, n_out), jnp.float32),
        grid=(n_out // D,),
        in_specs=[
            pl.BlockSpec((B, D), lambda j: (0, 0)),
            pl.BlockSpec((D, D), lambda j: (0, j)),
            pl.BlockSpec((1, D), lambda j: (0, j)),
        ],
        out_specs=pl.BlockSpec((B, D), lambda j: (0, j)),
        compiler_params=_cparams(("arbitrary",)),
        name="ada_mod",
    )(c, w_ada, b_ada.reshape(1, n_out))


def _in_proj_kernel(x_ref, mod_ref, g_ref, w_ref, bg_ref, o_ref, qt_ref, vt_ref, *, d_model, n_gate,
                    chunk, col_q, col_v):
    x = x_ref[...]
    tm = x.shape[0]
    shift = mod_ref[0, :, 0:d_model]
    scale = mod_ref[0, :, d_model:2 * d_model]
    h = _rms(x, g_ref[...]) * (1.0 + scale) + shift
    hb = _bf16(h)
    n_total = w_ref.shape[1]
    n_pairs = NA_HEADS // 2
    for c0 in range(0, n_total, chunk):
        c1 = min(c0 + chunk, n_total)
        acc = _dot(hb, w_ref[:, c0:c1])
        if c0 < n_gate:
            acc = jax.nn.sigmoid(acc + bg_ref[:, c0:c1])
        o_ref[:, c0:c1] = _bf16(acc)
        if c0 == col_q:
            for hp in range(n_pairs):
                qt_ref[0, hp] = _bf16((acc[:, hp * LANES:(hp + 1) * LANES] * NA_HEAD_DIM ** -0.5).T)
        if c0 == col_v:
            for hp in range(n_pairs):
                vt = _bf16(acc[:, hp * LANES:(hp + 1) * LANES].T)
                for cc in range(tm // LANES):
                    vt_ref[0, hp, cc] = vt[:, cc * LANES:(cc + 1) * LANES]


def _in_proj(x2, mod3, g_pre, w_ext, b_gate, seq, col_q, col_v, tm=512):
    T, D = x2.shape
    batch = T // seq
    n_s = seq // tm
    n_total = w_ext.shape[1]
    n_gate = b_gate.shape[-1]
    n_pairs = NA_HEADS // 2
    chunk = n_pairs * LANES
    assert col_q % chunk == 0 and col_v % chunk == 0
    kern = functools.partial(_in_proj_kernel, d_model=D, n_gate=n_gate, chunk=chunk, col_q=col_q, col_v=col_v)
    return pl.pallas_call(
        kern,
        out_shape=(
            jax.ShapeDtypeStruct((T, n_total), jnp.bfloat16),
            jax.ShapeDtypeStruct((batch, n_pairs, LANES, seq), jnp.bfloat16),
            jax.ShapeDtypeStruct((batch, n_pairs, seq // LANES, LANES, LANES), jnp.bfloat16),
        ),
        grid=(T // tm,),
        in_specs=[
            pl.BlockSpec((tm, D), lambda i: (i, 0)),
            pl.BlockSpec((1, 1, mod3.shape[-1]), lambda i: ((i * tm) // seq, 0, 0)),
            pl.BlockSpec((1, D), lambda i: (0, 0)),
            pl.BlockSpec((D, n_total), lambda i: (0, 0)),
            pl.BlockSpec((1, n_gate), lambda i: (0, 0)),
        ],
        out_specs=(
            pl.BlockSpec((tm, n_total), lambda i: (i, 0)),
            pl.BlockSpec((1, n_pairs, LANES, tm), lambda i: (i // n_s, 0, 0, i % n_s)),
            pl.BlockSpec((1, n_pairs, tm // LANES, LANES, LANES), lambda i: (i // n_s, 0, i % n_s, 0, 0)),
        ),
        compiler_params=_cparams(("arbitrary",)),
        name="in_proj",
    )(x2, mod3, g_pre.reshape(1, D), w_ext, b_gate.reshape(1, n_gate))


def _na_class(c):
    half = NA_WIN_ROWS // 2
    if c == 0:
        return (lambda i: max(i - half, 0)), NA_WIN_ROWS - 1
    if c == 1:
        return (lambda i: i), NA_WIN_ROWS - 1 - half
    return (lambda i: min(i + half, NA_KEY_ROWS - NA_WIN_ROWS)), -1


NA_PAIR_KEY_ROWS = 10


def _na_kernel(qt_ref, k_ref, vt_ref, pt_ref, o_ref, *, n_rows):
    g = pl.program_id(2)
    n_groups = n_rows // NA_GROUP_ROWS
    kb_rows = jnp.clip(g * NA_GROUP_ROWS - NA_WIN_ROWS // 2, 0, n_rows - NA_KEY_ROWS)
    nkeys = NA_PAIR_KEY_ROWS * GRID_W
    row = lax.broadcasted_iota(jnp.int32, (LANES, LANES), 0)
    lane = lax.broadcasted_iota(jnp.int32, (GRID_W, LANES), 1)
    neg_blk = jnp.full((GRID_W, LANES), NEG_BIG, jnp.float32)

    def body(c):
        j0, off = _na_class(c)
        n_pair = NA_GROUP_ROWS // 2
        starts = [min(j0(2 * ip) - j0(2 * ip) % 2, NA_KEY_ROWS - NA_PAIR_KEY_ROWS) for ip in range(n_pair)]
        scores = []
        for ip in range(n_pair):
            kstart = pl.multiple_of((kb_rows + starts[ip]) * GRID_W, 2 * GRID_W)
            kwin = k_ref[pl.ds(kstart, nkeys), :]
            qp = qt_ref[0, 0, :, ip * LANES:(ip + 1) * LANES]
            zero = jnp.zeros_like(qp)
            wq = jnp.concatenate([jnp.where(row < NA_HEAD_DIM, qp, zero),
                                  jnp.where(row >= NA_HEAD_DIM, qp, zero)], axis=1)
            scores.append(_dot(kwin, wq))
        outs = []
        for ip in range(n_pair):
            i0 = 2 * ip
            w = starts[ip]
            st = scores[ip]
            blocks = []
            for jw in range(NA_PAIR_KEY_ROWS):
                j = w + jw
                v0 = j0(i0) <= j < j0(i0) + NA_WIN_ROWS
                v1 = j0(i0 + 1) <= j < j0(i0 + 1) + NA_WIN_ROWS
                dr0 = j - i0 + off
                halves = []
                for hl in range(2):
                    if not (v0 or v1):
                        halves.append(neg_blk)
                        continue
                    blk = st[jw * GRID_W:(jw + 1) * GRID_W, hl * LANES:(hl + 1) * LANES] + pt_ref[hl, dr0]
                    if not (v0 and v1):
                        keep = (lane < GRID_W) if v0 else (lane >= GRID_W)
                        blk = jnp.where(keep, blk, NEG_BIG)
                    halves.append(blk)
                blocks.append(jnp.concatenate(halves, axis=1))
            s = jnp.concatenate(blocks, axis=0)
            m = jnp.max(s, axis=0, keepdims=True)
            p = jnp.exp(s - m)
            l = jnp.sum(p, axis=0, keepdims=True)
            c0 = (kb_rows + w) // 2
            vwin = jnp.concatenate([vt_ref[0, 0, c0 + u] for u in range(NA_PAIR_KEY_ROWS // 2)], axis=1)
            on = _dot(vwin, _bf16(p)) / l
            outs.append(jnp.where(row < NA_HEAD_DIM, on[:, 0:LANES], on[:, LANES:2 * LANES]))
        o_ref[...] = _bf16(jnp.concatenate(outs, axis=1).T)

    cls = jnp.where(g == 0, 0, jnp.where(g == n_groups - 1, 2, 1))
    for c in range(3):
        pl.when(cls == c)(functools.partial(body, c))


def _na_attention(proj, q_t, v_t, pair_tab, batch, seq, col_k):
    T = proj.shape[0]
    n_rows = seq // GRID_W
    n_groups = n_rows // NA_GROUP_ROWS
    tq = NA_GROUP_ROWS * GRID_W
    n_pairs = NA_HEADS // 2
    kern = functools.partial(_na_kernel, n_rows=n_rows)
    return pl.pallas_call(
        kern,
        out_shape=jax.ShapeDtypeStruct((T, NA_WIDTH), jnp.bfloat16),
        grid=(n_pairs, batch, n_groups),
        in_specs=[
            pl.BlockSpec((1, 1, LANES, tq), lambda hp, b, g: (b, hp, 0, g)),
            pl.BlockSpec((seq, LANES), lambda hp, b, g: (b, col_k // LANES + hp)),
            pl.BlockSpec((1, 1, seq // LANES, LANES, LANES), lambda hp, b, g: (b, hp, 0, 0, 0)),
            pl.BlockSpec((2,) + pair_tab.shape[1:], lambda hp, b, g: (hp, 0, 0, 0)),
        ],
        out_specs=pl.BlockSpec((tq, LANES), lambda hp, b, g: (b * n_groups + g, hp)),
        compiler_params=_cparams(("arbitrary", "arbitrary", "arbitrary")),
        name="na_attn",
    )(q_t, proj, v_t, pair_tab)


def _na_pair_tables(rpb):
    H, n_dr, n_dc = rpb.shape
    kw = NA_WIN_COLS
    W = GRID_W
    ring = 2 * W - 1
    rp = jnp.pad(rpb.astype(jnp.float32), ((0, 0), (0, 0), (W - kw, W - kw)), constant_values=NEG_BIG)
    u = jnp.roll(rp, -(W - 1), axis=-1)
    circ = jnp.tile(u, (1, 1, W))[..., :W * (ring - 1)].reshape(H, n_dr, W, ring - 1)
    toep = circ[..., :W]
    cidx = np.arange(W)
    col_start = np.clip(cidx - kw // 2, 0, W - kw)
    col_in = (cidx[None, :] >= col_start[:, None]) & (cidx[None, :] < col_start[:, None] + kw)
    toep_t = jnp.where(jnp.asarray(col_in.T)[None, None], jnp.swapaxes(toep, 2, 3), NEG_BIG)
    neg = jnp.full((H, 1, W, W), NEG_BIG, jnp.float32)
    ext = jnp.concatenate([neg, toep_t, neg], axis=1)
    return jnp.concatenate([ext[:, 1:], ext[:, :-1]], axis=-1)


def _mla_prep_kernel(cq_ref, ckv_ref, kr_ref, gq_ref, gkv_ref, wqm_ref, wqs_ref, wk_ref, wv_ref,
                     pm_ref, tab_ref, q_out, k_out, v_out, *, qscale):
    cq = _f32(cq_ref[...])
    cqn = _bf16(_rms(cq, gq_ref[...], n=MLA_Q_LORA))
    qm = _dot(cqn, wqm_ref[...])
    qs = _dot(cqn, wqs_ref[...])
    ctab = tab_ref[:, 0:LANES]
    stab = tab_ref[:, LANES:2 * LANES]
    for h in range(MLA_HEADS):
        sl = slice(h * LANES, (h + 1) * LANES)
        qh = (qm[:, sl] * ctab + qs[:, sl] * stab) * qscale
        q_out[0, h] = _bf16(qh.T)

    ckv = _f32(ckv_ref[...])
    ckvn = _bf16(_rms(ckv, gkv_ref[...]))
    kk = _dot(ckvn, wk_ref[...])
    vv = _dot(ckvn, wv_ref[...])
    n_chunk = v_out.shape[2]
    ones = jnp.ones((MLA_VROWS - MLA_V, MLA_KEY_CHUNK), jnp.bfloat16)
    for hp in range(MLA_HEADS // 2):
        vt = _bf16(vv[:, hp * LANES:(hp + 1) * LANES].T)
        for cc in range(n_chunk):
            ks = slice(cc * MLA_KEY_CHUNK, (cc + 1) * MLA_KEY_CHUNK)
            for h in range(2):
                r0 = h * MLA_VROWS
                v_out[0, hp, cc, r0:r0 + MLA_V, :] = vt[h * MLA_V:(h + 1) * MLA_V, ks]
                v_out[0, hp, cc, r0 + MLA_V:r0 + MLA_VROWS, :] = ones
    krr = _f32(kr_ref[...]) * tab_ref[:, 2 * LANES:3 * LANES]
    kplace = _dot(_bf16(krr), pm_ref[...])
    for h in range(MLA_HEADS):
        sl = slice(h * LANES, (h + 1) * LANES)
        k_out[:, sl] = _bf16(kk[:, sl] + kplace)


def _mla_prep(proj, col_cq, col_ckv, col_kr, gq, gkv, wqm, wqs, wk, wv, pm, rope_tab, seq, qscale, tm=512):
    T = proj.shape[0]
    batch = T // seq
    n_s = seq // tm
    n_pairs = MLA_HEADS // 2
    cpt = tm // MLA_KEY_CHUNK
    full = lambda a: pl.BlockSpec(a.shape, lambda i: (0,) * a.ndim)
    tab = pl.BlockSpec((tm, 3 * LANES), lambda i: (i % n_s, 0))
    kern = functools.partial(_mla_prep_kernel, qscale=qscale)
    return pl.pallas_call(
        kern,
        out_shape=(
            jax.ShapeDtypeStruct((batch, MLA_HEADS, LANES, seq), jnp.bfloat16),
            jax.ShapeDtypeStruct((T, MLA_HEADS * LANES), jnp.bfloat16),
            jax.ShapeDtypeStruct((batch, n_pairs, seq // MLA_KEY_CHUNK, 2 * MLA_VROWS, MLA_KEY_CHUNK),
                                 jnp.bfloat16),
        ),
        grid=(T // tm,),
        in_specs=[
            pl.BlockSpec((tm, Q_LORA_PAD), lambda i: (i, col_cq // Q_LORA_PAD)),
            pl.BlockSpec((tm, MLA_KV_LORA), lambda i: (i, col_ckv // MLA_KV_LORA)),
            pl.BlockSpec((tm, LANES), lambda i: (i, col_kr // LANES)),
            full(gq), full(gkv), full(wqm), full(wqs), full(wk), full(wv), full(pm),
            tab,
        ],
        out_specs=(
            pl.BlockSpec((1, MLA_HEADS, LANES, tm), lambda i: (i // n_s, 0, 0, i % n_s)),
            pl.BlockSpec((tm, MLA_HEADS * LANES), lambda i: (i, 0)),
            pl.BlockSpec((1, n_pairs, cpt, 2 * MLA_VROWS, MLA_KEY_CHUNK),
                         lambda i: (i // n_s, 0, i % n_s, 0, 0)),
        ),
        compiler_params=_cparams(("arbitrary",)),
        name="mla_prep",
    )(proj, proj, proj, gq, gkv, wqm, wqs, wk, wv, pm, rope_tab)


def _mla_attn_kernel(qt_ref, k_ref, vt_ref, o_ref, m_scr, acc_scr, st_scr):
    n_chunks = vt_ref.shape[2]
    tk = MLA_KEY_CHUNK
    m_scr[...] = jnp.full(m_scr.shape, -jnp.inf, jnp.float32)
    acc_scr[...] = jnp.zeros(acc_scr.shape, jnp.float32)

    def scores(c, slot):
        k0 = pl.multiple_of(c * tk, tk)
        for h in range(2):
            kc = k_ref[pl.ds(k0, tk), h * LANES:(h + 1) * LANES]
            st_scr[slot, h] = _dot(kc, qt_ref[0, h])

    def step(c, slot):
        scores(jnp.minimum(c + 1, n_chunks - 1), 1 - slot)
        vt = vt_ref[0, 0, c]
        pts, alphas = [], []
        for h in range(2):
            st = st_scr[slot, h]
            m_old = m_scr[h]
            m_new = jnp.maximum(m_old, jnp.max(st, axis=0, keepdims=True))
            alpha = jnp.exp2(m_old - m_new)
            pt = jnp.exp2(st - m_new)
            m_scr[h] = m_new
            pts.append(_bf16(pt))
            alphas.append(alpha)
        for h in range(2):
            acc_scr[h] = alphas[h] * acc_scr[h] + _dot(vt[h * MLA_VROWS:(h + 1) * MLA_VROWS, :], pts[h])

    def body(j, carry):
        for u in range(MLA_STEPS_PER_TRIP):
            step(MLA_STEPS_PER_TRIP * j + u, u % 2)
        return carry

    scores(0, 0)
    lax.fori_loop(0, n_chunks // MLA_STEPS_PER_TRIP, body, 0)
    outs = []
    for h in range(2):
        acc = acc_scr[h]
        outs.append(acc[0:MLA_V] / acc[MLA_V:MLA_V + 1])
    o_ref[...] = _bf16(jnp.concatenate(outs, axis=0).T)


def _mla_attention(q_t, k_cat, v_t, batch, seq, tq=512):
    T = k_cat.shape[0]
    n_q = seq // tq
    n_pairs = MLA_HEADS // 2
    n_chunks = seq // MLA_KEY_CHUNK
    assert n_chunks % MLA_STEPS_PER_TRIP == 0
    return pl.pallas_call(
        _mla_attn_kernel,
        out_shape=jax.ShapeDtypeStruct((T, MLA_WIDTH), jnp.bfloat16),
        grid=(batch, n_pairs, n_q),
        in_specs=[
            pl.BlockSpec((1, 2, LANES, tq), lambda b, hp, i: (b, hp, 0, i)),
            pl.BlockSpec((seq, 2 * LANES), lambda b, hp, i: (b, hp)),
            pl.BlockSpec((1, 1, n_chunks, 2 * MLA_VROWS, MLA_KEY_CHUNK), lambda b, hp, i: (b, hp, 0, 0, 0)),
        ],
        out_specs=pl.BlockSpec((tq, LANES), lambda b, hp, i: (b * n_q + i, hp)),
        scratch_shapes=[
            pltpu.VMEM((2, 1, tq), jnp.float32),
            pltpu.VMEM((2, MLA_VROWS, tq), jnp.float32),
            pltpu.VMEM((2, 2, MLA_KEY_CHUNK, tq), jnp.float32),
        ],
        compiler_params=_cparams(("arbitrary", "arbitrary", "arbitrary")),
        name="mla_attn",
    )(q_t, k_cat, v_t)


def _out_proj_kernel(ona_ref, omla_ref, gate_ref, x_ref, mod_ref, gpost_ref, gpre_ref, wna_ref,
                     wmla_ref, wout_ref, wr_ref, br_ref, x1_ref, h2_ref, lg_ref, *, d_model):
    D = d_model
    tm = x_ref.shape[0]
    gate_a = mod_ref[0, :, 2 * D:3 * D]
    shift_f = mod_ref[0, :, 3 * D:4 * D]
    scale_f = mod_ref[0, :, 4 * D:5 * D]
    halves = [slice(0, tm // 2), slice(tm // 2, tm)]
    merged = []
    for rs in halves:
        up_na = _dot(ona_ref[rs, :], wna_ref[...])
        up_mla = _dot(omla_ref[rs, :], wmla_ref[...])
        merged.append(_bf16(_f32(gate_ref[rs, 0:D]) * up_na + _f32(gate_ref[rs, D:2 * D]) * up_mla))
    ys = [_dot(m, wout_ref[...]) for m in merged]
    h2s = []
    for rs, y in zip(halves, ys):
        x1 = x_ref[rs, :] + gate_a * _rms(y, gpost_ref[...])
        x1_ref[rs, :] = x1
        h2 = _rms(x1, gpre_ref[...]) * (1.0 + scale_f) + shift_f
        _store_row_tiles(h2_ref, h2, base=rs.start * SUBLANES)
        h2s.append(_bf16(h2))
    for rs, hb in zip(halves, h2s):
        lg_ref[rs, :] = _dot(hb, wr_ref[...]) + br_ref[...]


def _out_proj(o_na, o_mla, proj, x2, mod3, g_post, g_pre, w_na, w_mla, w_out, w_r, b_r, seq, tm=512):
    T, D = x2.shape
    full = lambda a: pl.BlockSpec(a.shape, lambda i: (0,) * a.ndim)
    row = lambda w: pl.BlockSpec((tm, w), lambda i: (i, 0))
    kern = functools.partial(_out_proj_kernel, d_model=D)
    g_post = g_post.reshape(1, D)
    g_pre = g_pre.reshape(1, D)
    return pl.pallas_call(
        kern,
        out_shape=(
            jax.ShapeDtypeStruct((T, D), jnp.float32),
            jax.ShapeDtypeStruct((T * (D // LANES), LANES), jnp.float32),
            jax.ShapeDtypeStruct((T, LANES), jnp.float32),
        ),
        grid=(T // tm,),
        in_specs=[
            row(NA_WIDTH), row(MLA_WIDTH),
            pl.BlockSpec((tm, 2 * D), lambda i: (i, COL_GATES // (2 * D))),
            row(D),
            pl.BlockSpec((1, 1, mod3.shape[-1]), lambda i: ((i * tm) // seq, 0, 0)),
            full(g_post), full(g_pre), full(w_na), full(w_mla), full(w_out), full(w_r), full(b_r),
        ],
        out_specs=(row(D), pl.BlockSpec((tm * (D // LANES), LANES), lambda i: (i, 0)), row(LANES)),
        compiler_params=_cparams(("arbitrary",)),
        name="out_proj",
    )(o_na, o_mla, proj, x2, mod3, g_post, g_pre, w_na, w_mla, w_out, w_r, b_r)


def _route_kernel(lg_ref, r_ref, cnt_ref, carry_scr, *, sub):
    @pl.when(pl.program_id(0) == 0)
    def _():
        carry_scr[...] = jnp.zeros(carry_scr.shape, jnp.float32)

    tr = lg_ref.shape[0]
    lane = lax.broadcasted_iota(jnp.int32, (sub, LANES), 1).astype(jnp.float32)
    ri = lax.broadcasted_iota(jnp.int32, (sub, sub), 0)
    ci = lax.broadcasted_iota(jnp.int32, (sub, sub), 1)
    tri = jnp.where(ri >= ci, 1.0, 0.0).astype(jnp.bfloat16)
    for s0 in range(0, tr, sub):
        work = lg_ref[s0:s0 + sub, :]
        sels, vals, idxs = [], [], []
        for _k in range(TOP_K):
            mk = jnp.max(work, axis=-1, keepdims=True)
            ik = jnp.min(jnp.where(work == mk, lane, float(LANES)), axis=-1, keepdims=True)
            sk = lane == ik
            work = jnp.where(sk, -jnp.inf, work)
            sels.append(sk)
            vals.append(mk)
            idxs.append(ik)
        es = [jnp.exp(v - vals[0]) for v in vals]
        denom = es[0] + es[1] + es[2] + es[3]
        onehot = jnp.zeros((sub, LANES), jnp.float32)
        for sk in sels:
            onehot = jnp.where(sk, 1.0, onehot)
        prefix = _dot(tri, _bf16(onehot))
        carry = carry_scr[...]
        rank_mat = carry + prefix - 1.0
        res = jnp.zeros((sub, LANES), jnp.float32)
        for kk in range(TOP_K):
            rank_k = jnp.sum(jnp.where(sels[kk], rank_mat, 0.0), axis=-1, keepdims=True)
            res = jnp.where(lane == kk, idxs[kk], res)
            res = jnp.where(lane == TOP_K + kk, rank_k, res)
            res = jnp.where(lane == 2 * TOP_K + kk, es[kk] / denom, res)
        r_ref[s0:s0 + sub, :] = res
        carry_scr[...] = carry + jnp.sum(onehot, axis=0, keepdims=True)
    cnt_ref[...] = carry_scr[...]


def _route(logits, tr=2048, sub=256):
    T = logits.shape[0]
    kern = functools.partial(_route_kernel, sub=sub)
    return pl.pallas_call(
        kern,
        out_shape=(
            jax.ShapeDtypeStruct((T, LANES), jnp.float32),
            jax.ShapeDtypeStruct((1, LANES), jnp.float32),
        ),
        grid=(T // tr,),
        in_specs=[pl.BlockSpec((tr, LANES), lambda i: (i, 0))],
        out_specs=(
            pl.BlockSpec((tr, LANES), lambda i: (i, 0)),
            pl.BlockSpec((1, LANES), lambda i: (0, 0)),
        ),
        scratch_shapes=[pltpu.VMEM((1, LANES), jnp.float32)],
        compiler_params=_cparams(("arbitrary",)),
        name="route",
    )(logits)


def _dispatch_kernel(zero_blk_ref, dest_ref, h_ref, row0_ref, xs_ref, row_ref, zbuf, sem, fill_sem,
                     *, n_tokens):
    i = pl.program_id(0)
    ts = h_ref.shape[0] // SUBLANES
    blk_rows = MOE_BLOCK * SUBLANES

    @pl.when(i == 0)
    def _():
        fill_row = pltpu.make_async_copy(row0_ref, row_ref, fill_sem.at[0])
        fill_row.start()
        zbuf[...] = jnp.zeros(zbuf.shape, zbuf.dtype)
        n_zero = zero_blk_ref.shape[0]

        def zero_copy(z):
            r0 = pl.multiple_of(zero_blk_ref[z] * blk_rows, blk_rows)
            return pltpu.make_async_copy(zbuf, xs_ref.at[pl.ds(r0, blk_rows), :], fill_sem.at[1])

        for z in range(n_zero):
            pl.when(zero_blk_ref[z] >= 0)(lambda z=z: zero_copy(z).start())
        for z in range(n_zero):
            pl.when(zero_blk_ref[z] >= 0)(lambda z=z: zero_copy(z).wait())
        fill_row.wait()

    def row_copy(j, d):
        dst = pl.ds(pl.multiple_of(d * SUBLANES, SUBLANES), SUBLANES)
        return pltpu.make_async_copy(h_ref.at[pl.ds(j * SUBLANES, SUBLANES), :], xs_ref.at[dst, :], sem)

    t0 = i * ts
    for j in range(ts):
        for kk in range(TOP_K):
            d = dest_ref[j * TOP_K + kk]
            row_copy(j, d).start(priority=kk % 2)
            row_ref[d] = (kk * n_tokens + j) + t0
    for _j in range(ts * TOP_K):
        row_copy(0, 0).wait()


def _dispatch(h2_tiles, dest, zero_blk, cap, ts=256):
    T = h2_tiles.shape[0] // SUBLANES
    kern = functools.partial(_dispatch_kernel, n_tokens=T)
    grid_spec = pltpu.PrefetchScalarGridSpec(
        num_scalar_prefetch=1,
        grid=(T // ts,),
        in_specs=[
            pl.BlockSpec((ts * TOP_K,), lambda i, zb: (i,), memory_space=pltpu.SMEM),
            pl.BlockSpec((ts * SUBLANES, LANES), lambda i, zb: (i, 0)),
            pl.BlockSpec(memory_space=pl.ANY),
        ],
        out_specs=(pl.BlockSpec(memory_space=pl.ANY), pl.BlockSpec(memory_space=pltpu.SMEM)),
        scratch_shapes=[
            pltpu.VMEM((MOE_BLOCK * SUBLANES, LANES), h2_tiles.dtype),
            pltpu.SemaphoreType.DMA(()),
            pltpu.SemaphoreType.DMA((2,)),
        ],
    )
    return pl.pallas_call(
        kern,
        out_shape=(jax.ShapeDtypeStruct((cap * SUBLANES, LANES), h2_tiles.dtype),
                   jax.ShapeDtypeStruct((cap,), jnp.int32)),
        grid_spec=grid_spec,
        compiler_params=_cparams(("arbitrary",)),
        name="dispatch",
    )(zero_blk, dest, h2_tiles, jnp.full((cap,), -1, jnp.int32))


def _moe_kernel(blk_e_ref, nvalid_ref, row_prev_ref, row_ref, x_ref, w1_ref, b1_ref, w2_ref, b2_ref,
                out_ref, w1p_scr, w2b_scr, ybuf0, ybuf1, ssem, *, n_items):
    i = pl.program_id(0)
    nvalid = nvalid_ref[0]
    d_ff = w2_ref.shape[1]
    n_groups = (2 * d_ff) // (2 * LANES)

    def tile(r):
        return pl.ds(pl.multiple_of(r * SUBLANES, SUBLANES), SUBLANES)

    ybufs = (ybuf0, ybuf1)

    def scatter_copy(j, r, slot):
        return pltpu.make_async_copy(ybufs[slot].at[pl.ds(j * SUBLANES, SUBLANES), :], out_ref.at[tile(r), :],
                                     ssem.at[slot])

    def wait_scatters(slot):
        for _j in range(MOE_BLOCK):
            scatter_copy(0, 0, slot).wait()

    def scatter_rows(rows_ref, slot):
        spare0 = n_items + slot * MOE_BLOCK
        for j in range(MOE_BLOCK):
            r = rows_ref[0, 0, j]
            scatter_copy(j, jnp.where(r >= 0, r, spare0 + j), slot).start(priority=j % 2)

    @pl.when(i == 0)
    def _():
        ybuf0[...] = jnp.zeros(ybuf0.shape, ybuf0.dtype)
        ybuf1[...] = jnp.zeros(ybuf1.shape, ybuf1.dtype)
        for j in range(MOE_BLOCK):
            scatter_copy(j, n_items + j, 0).start(priority=j % 2)

    @pl.when(i < nvalid)
    def _():
        e = blk_e_ref[i]
        e_prev = blk_e_ref[jnp.maximum(i - 1, 0)]

        @pl.when((i == 0) | (e != e_prev))
        def _():
            r = lax.broadcasted_iota(jnp.int32, (2 * LANES, 2 * LANES), 0)
            c = lax.broadcasted_iota(jnp.int32, (2 * LANES, 2 * LANES), 1)
            src = jnp.where(c < LANES, 2 * c, 2 * (c - LANES) + 1)
            perm = jnp.where(r == src, 1.0, 0.0).astype(jnp.bfloat16)
            for gI in range(n_groups):
                sl = slice(gI * 2 * LANES, (gI + 1) * 2 * LANES)
                w1p_scr[:, sl] = _bf16(_dot(_bf16(w1_ref[0, :, sl]), perm))
            w2b_scr[...] = _bf16(w2_ref[0])

        def block_step(slot):
            wait_scatters(slot)
            scatter_rows(row_prev_ref, 1 - slot)

            xb = _bf16(_load_row_tiles(x_ref, MOE_BLOCK))
            hcat = _dot(xb, w1p_scr[...]) + b1_ref[0]
            acts = []
            for gI in range(n_groups):
                glu = jnp.minimum(hcat[:, gI * 2 * LANES: gI * 2 * LANES + LANES], SWIGLU_LIMIT)
                lin = jnp.clip(hcat[:, gI * 2 * LANES + LANES: (gI + 1) * 2 * LANES],
                               -SWIGLU_LIMIT, SWIGLU_LIMIT)
                acts.append(_bf16(glu * jax.nn.sigmoid(SWIGLU_ALPHA * glu) * (lin + 1.0)))
            act = jnp.concatenate(acts, axis=-1)
            _store_row_tiles(ybufs[slot], _dot(act, w2b_scr[...]) + b2_ref[0])

            @pl.when(i == nvalid - 1)
            def _():
                scatter_rows(row_ref, slot)
                wait_scatters(1 - slot)
                wait_scatters(slot)

        parity = lax.rem(i, 2)
        for slot in range(2):
            pl.when(parity == slot)(functools.partial(block_step, slot))


def _moe_ffn(xs, slot_row, n_items, blk_e, nvalid, w1, b1p, w2, b2):
    D = w1.shape[1]
    assert D == SUBLANES * LANES
    E, _, F2 = w1.shape
    F = w2.shape[1]
    nblk = slot_row.shape[0]

    def cur(i, be, nv):
        return (jnp.minimum(i, nv[0] - 1), 0, 0)

    def prv(i, be, nv):
        return (jnp.where(i == 0, nblk, jnp.maximum(jnp.minimum(i, nv[0] - 1) - 1, 0)), 0, 0)

    slot_row = jnp.concatenate([slot_row, jnp.full((1, 1, MOE_BLOCK), -1, slot_row.dtype)], axis=0)
    slot_spec = lambda f: pl.BlockSpec((1, 1, MOE_BLOCK), f, memory_space=pltpu.SMEM)
    grid_spec = pltpu.PrefetchScalarGridSpec(
        num_scalar_prefetch=2,
        grid=(nblk,),
        in_specs=[
            slot_spec(prv), slot_spec(cur),
            pl.BlockSpec((MOE_BLOCK * SUBLANES, LANES), lambda i, be, nv: (jnp.minimum(i, nv[0] - 1), 0)),
            pl.BlockSpec((1, D, F2), lambda i, be, nv: (be[i], 0, 0)),
            pl.BlockSpec((1, 1, F2), lambda i, be, nv: (be[i], 0, 0)),
            pl.BlockSpec((1, F, D), lambda i, be, nv: (be[i], 0, 0)),
            pl.BlockSpec((1, 1, D), lambda i, be, nv: (be[i], 0, 0)),
        ],
        out_specs=pl.BlockSpec(memory_space=pl.ANY),
        scratch_shapes=[
            pltpu.VMEM((D, F2), jnp.bfloat16),
            pltpu.VMEM((F, D), jnp.bfloat16),
            pltpu.VMEM((MOE_BLOCK * SUBLANES, LANES), jnp.float32),
            pltpu.VMEM((MOE_BLOCK * SUBLANES, LANES), jnp.float32),
            pltpu.SemaphoreType.DMA((2,)),
        ],
    )
    kern = functools.partial(_moe_kernel, n_items=n_items)
    return pl.pallas_call(
        kern,
        out_shape=jax.ShapeDtypeStruct(((n_items + 2 * MOE_BLOCK) * SUBLANES, LANES), jnp.float32),
        grid_spec=grid_spec,
        compiler_params=_cparams(("arbitrary",)),
        name="moe_ffn",
    )(blk_e, nvalid, slot_row, slot_row, xs, w1, b1p, w2, b2)


def _combine_kernel(y0_ref, y1_ref, y2_ref, y3_ref, r_ref, x1_ref, mod_ref, g_ref, o_ref, *, d_model):
    D = d_model
    tc = x1_ref.shape[0]
    r = r_ref[...]
    moe = None
    for kk, y_ref in enumerate((y0_ref, y1_ref, y2_ref, y3_ref)):
        term = r[:, 2 * TOP_K + kk: 2 * TOP_K + kk + 1] * _load_row_tiles(y_ref, tc)
        moe = term if moe is None else moe + term
    gate_f = mod_ref[0, :, 5 * D:6 * D]
    o_ref[...] = x1_ref[...] + gate_f * _rms(moe, g_ref[...])


def _combine(y_k, route, x1, mod3, g_post, seq, tc=512):
    T, D = x1.shape
    n_t = T // tc
    kern = functools.partial(_combine_kernel, d_model=D)
    y_spec = lambda kk: pl.BlockSpec((tc * SUBLANES, LANES), lambda i: (kk * n_t + i, 0))
    return pl.pallas_call(
        kern,
        out_shape=jax.ShapeDtypeStruct((T, D), jnp.float32),
        grid=(n_t,),
        in_specs=[
            y_spec(0), y_spec(1), y_spec(2), y_spec(3),
            pl.BlockSpec((tc, LANES), lambda i: (i, 0)),
            pl.BlockSpec((tc, D), lambda i: (i, 0)),
            pl.BlockSpec((1, 1, mod3.shape[-1]), lambda i: ((i * tc) // seq, 0, 0)),
            pl.BlockSpec((1, D), lambda i: (0, 0)),
        ],
        out_specs=pl.BlockSpec((tc, D), lambda i: (i, 0)),
        compiler_params=_cparams(("arbitrary",)),
        name="combine",
    )(y_k, y_k, y_k, y_k, route, x1, mod3, g_post.reshape(1, D))


def _rope_swap_cols(w):
    nf = MLA_ROPE // 4
    return jnp.concatenate([-w[..., nf:2 * nf], w[..., 0:nf], -w[..., 3 * nf:4 * nf], w[..., 2 * nf:3 * nf]],
                           axis=-1)


def _rope_tables(seq):
    n_rows = seq // GRID_W
    nf = MLA_ROPE // 4
    inv = ROPE_THETA ** (-jnp.arange(nf, dtype=jnp.float32) / nf)
    ar = jnp.arange(n_rows, dtype=jnp.float32)[:, None] * inv
    ac = jnp.arange(GRID_W, dtype=jnp.float32)[:, None] * inv
    cr, sr, cc, sc = jnp.cos(ar), jnp.sin(ar), jnp.cos(ac), jnp.sin(ac)

    def lanes(parts, n):
        z = lambda w: jnp.zeros((n, w), jnp.float32)
        return jnp.concatenate([z(p) if isinstance(p, int) else p for p in parts], axis=-1)

    pad = LANES - MLA_NOPE - MLA_ROPE
    row_tab = lanes([MLA_NOPE, cr, cr, 2 * nf, pad, MLA_NOPE, sr, sr, 2 * nf, pad,
                     cr, cr, 2 * nf, sr, sr, 2 * nf, MLA_NOPE], n_rows)
    col_tab = lanes([jnp.ones((GRID_W, MLA_NOPE), jnp.float32), 2 * nf, cc, cc, pad, MLA_NOPE, 2 * nf, sc, sc, pad,
                     2 * nf, cc, cc, 2 * nf, sc, sc, MLA_NOPE], GRID_W)
    return (row_tab[:, None, :] + col_tab[None, :, :]).reshape(seq, 3 * LANES)


def kernel(x, c, w_ada, b_ada, g_attn_pre, g_attn_post, w_in, b_gate, na_rpb, q_norm_g, kv_norm_g,
           w_uq, w_ukv, w_na_up, w_mla_up, w_out, g_ffn_pre, g_ffn_post, w_router, b_router,
           w1, b1, w2, b2):
    B, S, D = x.shape
    T = B * S
    depth = w_ada.shape[0]
    E = w_router.shape[-1]
    n_rows = S // GRID_W
    assert S % (NA_GROUP_ROWS * GRID_W) == 0 and n_rows >= NA_KEY_ROWS
    assert E <= LANES and T % 2048 == 0

    bf = jnp.bfloat16
    rope_tab = _rope_tables(S)
    pm_np = np.zeros((LANES, LANES), np.float32)
    for cidx in range(MLA_ROPE):
        pm_np[cidx, MLA_NOPE + cidx] = 1.0
        pm_np[MLA_ROPE + cidx, MLA_NOPE + cidx] = 1.0
    pm = jnp.asarray(pm_np, bf)
    qscale = float((MLA_NOPE + MLA_ROPE) ** -0.5 * math.log2(math.e))

    col_na = 2 * D
    col_cq = col_na + 3 * NA_WIDTH
    col_ckv = col_cq + Q_LORA_PAD
    col_kr = col_ckv + MLA_KV_LORA
    assert col_cq % Q_LORA_PAD == 0 and col_ckv % MLA_KV_LORA == 0

    x2 = x.reshape(T, D)
    for l in range(depth):
        mod = _ada_mod(c, w_ada[l], b_ada[l])
        mod3 = mod.reshape(B, 1, 6 * D)

        o_na, o_cq, o_ckv, o_kr, o_g = np.cumsum([0, 3 * NA_WIDTH, MLA_Q_LORA, MLA_KV_LORA, MLA_ROPE]).tolist()
        wi = w_in[l]
        w_kr = wi[:, o_kr:o_g]
        w_ext = jnp.concatenate([
            wi[:, o_g:o_g + 2 * D],
            wi[:, o_na:o_cq],
            wi[:, o_cq:o_ckv], jnp.zeros((D, Q_LORA_PAD - MLA_Q_LORA), wi.dtype),
            wi[:, o_ckv:o_kr],
            w_kr, _rope_swap_cols(w_kr), jnp.zeros((D, LANES - 2 * MLA_ROPE), wi.dtype),
        ], axis=1).astype(bf)
        proj, na_qt, na_vt = _in_proj(x2, mod3, g_attn_pre[l], w_ext, b_gate[l], S,
                                      col_na, col_na + 2 * NA_WIDTH)

        o_na_tok = _na_attention(proj, na_qt, na_vt, _na_pair_tables(na_rpb[l]), B, S, col_na + NA_WIDTH)

        wq = w_uq[l]
        zq = jnp.zeros((MLA_Q_LORA, MLA_HEADS, LANES - MLA_NOPE - MLA_ROPE), wq.dtype)
        wqm = jnp.concatenate([wq, zq], axis=-1).reshape(MLA_Q_LORA, MLA_HEADS * LANES)
        wqs = jnp.concatenate([jnp.zeros((MLA_Q_LORA, MLA_HEADS, MLA_NOPE), wq.dtype),
                               _rope_swap_cols(wq[..., MLA_NOPE:]), zq], axis=-1
                              ).reshape(MLA_Q_LORA, MLA_HEADS * LANES)
        rpad = ((0, Q_LORA_PAD - MLA_Q_LORA), (0, 0))
        wqm = jnp.pad(wqm, rpad).astype(bf)
        wqs = jnp.pad(wqs, rpad).astype(bf)
        wkv = w_ukv[l]
        wk = jnp.concatenate([wkv[..., :MLA_NOPE], jnp.zeros((MLA_KV_LORA, MLA_HEADS, LANES - MLA_NOPE), wkv.dtype)],
                             axis=-1).reshape(MLA_KV_LORA, MLA_HEADS * LANES).astype(bf)
        wv = wkv[..., MLA_NOPE:].reshape(MLA_KV_LORA, MLA_WIDTH).astype(bf)
        gq = jnp.pad(q_norm_g[l], (0, Q_LORA_PAD - MLA_Q_LORA)).reshape(1, Q_LORA_PAD)
        gkv = kv_norm_g[l].reshape(1, MLA_KV_LORA)
        q_cat, k_cat, v_mla = _mla_prep(proj, col_cq, col_ckv, col_kr, gq, gkv, wqm, wqs, wk, wv, pm,
                                        rope_tab, S, qscale)
        o_mla_tok = _mla_attention(q_cat, k_cat, v_mla, B, S)

        w_r = jnp.pad(w_router[l], ((0, 0), (0, LANES - E))).astype(bf)
        b_r = jnp.concatenate([b_router[l], jnp.full((LANES - E,), NEG_BIG, jnp.float32)]).reshape(1, LANES)
        x1, h2, logits = _out_proj(o_na_tok, o_mla_tok, proj, x2, mod3, g_attn_post[l], g_ffn_pre[l],
                                   w_na_up[l].astype(bf), w_mla_up[l].astype(bf), w_out[l].astype(bf),
                                   w_r, b_r, S)

        route, counts = _route(logits)
        e_idx = route[:, 0:TOP_K].astype(jnp.int32)
        rank = route[:, TOP_K:2 * TOP_K].astype(jnp.int32)
        cnt = counts[0, :E].astype(jnp.int32)
        padded = ((cnt + MOE_BLOCK - 1) // MOE_BLOCK) * MOE_BLOCK
        pend = jnp.cumsum(padded)
        pstart = pend - padded
        n_items = T * TOP_K
        nblk = -(-n_items // MOE_BLOCK) + E
        cap = nblk * MOE_BLOCK
        dest = (rank + jnp.sum(jnp.where(e_idx[..., None] == jnp.arange(E, dtype=jnp.int32), pstart, 0), axis=-1)
                ).reshape(n_items)
        blk_off = jnp.arange(nblk, dtype=jnp.int32) * MOE_BLOCK
        blk_e = jnp.minimum(jnp.sum((pend[None, :] <= blk_off[:, None]).astype(jnp.int32), axis=-1), E - 1)
        nvalid = (pend[-1:] // MOE_BLOCK).astype(jnp.int32)

        last_blk = jnp.where(padded > 0, pend // MOE_BLOCK - 1, -1)
        tail_blk = nvalid[0] + jnp.arange(nblk - n_items // MOE_BLOCK, dtype=jnp.int32)
        zero_blk = jnp.concatenate([last_blk, jnp.where(tail_blk < nblk, tail_blk, -1)]).astype(jnp.int32)
        xs, slot_row = _dispatch(h2, dest, zero_blk, cap)
        slot_row = slot_row.reshape(nblk, 1, MOE_BLOCK)
        F2 = w1.shape[-1]
        b1p = b1[l].reshape(E, F2 // (2 * LANES), LANES, 2).transpose(0, 1, 3, 2).reshape(E, 1, F2)
        y_k = _moe_ffn(xs, slot_row, n_items, blk_e, nvalid, w1[l], b1p, w2[l], b2[l].reshape(E, 1, D))
        x2 = _combine(y_k, route, x1, mod3, g_ffn_post[l], S)
    return x2.reshape(B, S, D)
```

```python
import functools
import math

import numpy as np
import jax
import jax.numpy as jnp
from jax import lax
from jax.experimental import pallas as pl
from jax.experimental.pallas import tpu as pltpu

GRID_W = 64
NA_HEADS = 8
NA_HEAD_DIM = 64
NA_WIN_ROWS = 8
NA_WIN_COLS = 16
NA_WIDTH = NA_HEADS * NA_HEAD_DIM
MLA_HEADS = 8
MLA_Q_LORA = 384
MLA_KV_LORA = 256
MLA_NOPE = 64
MLA_ROPE = 32
MLA_V = 64
MLA_WIDTH = MLA_HEADS * MLA_V
MLA_KEY_CHUNK = 256
MLA_VROWS = MLA_V + 16
MLA_STEPS_PER_TRIP = 16
ROPE_THETA = 100.0
TOP_K = 4
SWIGLU_ALPHA = 1.702
SWIGLU_LIMIT = 7.0
MOE_BLOCK = 256
RMS_EPS = 1e-6
NEG_BIG = -1e30

LANES = 128
VMEM_LIMIT_BYTES = 56 * 1024 * 1024

Q_LORA_PAD = 512
COL_GATES = 0
NA_GROUP_ROWS = 8
NA_KEY_ROWS = 16


def _f32(x):
    return x.astype(jnp.float32)


def _bf16(x):
    return x.astype(jnp.bfloat16)


def _dot(a, b):
    return jnp.dot(a, b, preferred_element_type=jnp.float32)


def _rms(x, g, n=None):
    n = x.shape[-1] if n is None else n
    ms = jnp.sum(x * x, axis=-1, keepdims=True) * (1.0 / n)
    return x * lax.rsqrt(ms + RMS_EPS) * g


SUBLANES = 8


def _store_row_tiles(ref, x, base=0):
    n = x.shape[0]
    for c in range(x.shape[1] // LANES):
        ref[pl.ds(base + c, n, stride=SUBLANES), :] = x[:, c * LANES:(c + 1) * LANES]


def _load_row_tiles(ref, n, base=0):
    chunks = [ref[pl.ds(base + c, n, stride=SUBLANES), :] for c in range(SUBLANES)]
    return jnp.concatenate(chunks, axis=-1)


def _cparams(sem):
    return pltpu.CompilerParams(dimension_semantics=sem, vmem_limit_bytes=VMEM_LIMIT_BYTES)


def _ada_kernel(c_ref, w_ref, b_ref, o_ref):
    c = c_ref[...]
    sc = c * jax.nn.sigmoid(c)
    o_ref[...] = _dot(_bf16(sc), _bf16(w_ref[...])) + b_ref[...]


def _ada_mod(c, w_ada, b_ada):
    B, D = c.shape
    n_out = w_ada.shape[1]
    return pl.pallas_call(
        _ada_kernel,
        out_shape=jax.ShapeDtypeStruct((B, n_out), jnp.float32),
        grid=(n_out // D,),
        in_specs=[
            pl.BlockSpec((B, D), lambda j: (0, 0)),
            pl.BlockSpec((D, D), lambda j: (0, j)),
            pl.BlockSpec((1, D), lambda j: (0, j)),
        ],
        out_specs=pl.BlockSpec((B, D), lambda j: (0, j)),
        compiler_params=_cparams(("arbitrary",)),
        name="ada_mod",
    )(c, w_ada, b_ada.reshape(1, n_out))


def _in_proj_kernel(x_ref, mod_ref, g_ref, w_ref, bg_ref, o_ref, qt_ref, vt_ref, *, d_model, n_gate,
                    chunk, col_q, col_v):
    x = x_ref[...]
    tm = x.shape[0]
    shift = mod_ref[0, :, 0:d_model]
    scale = mod_ref[0, :, d_model:2 * d_model]
    h = _rms(x, g_ref[...]) * (1.0 + scale) + shift
    hb = _bf16(h)
    n_total = w_ref.shape[1]
    n_pairs = NA_HEADS // 2
    for c0 in range(0, n_total, chunk):
        c1 = min(c0 + chunk, n_total)
        acc = _dot(hb, w_ref[:, c0:c1])
        if c0 < n_gate:
            acc = jax.nn.sigmoid(acc + bg_ref[:, c0:c1])
        o_ref[:, c0:c1] = _bf16(acc)
        if c0 == col_q:
            for hp in range(n_pairs):
                qt_ref[0, hp] = _bf16((acc[:, hp * LANES:(hp + 1) * LANES] * NA_HEAD_DIM ** -0.5).T)
        if c0 == col_v:
            for hp in range(n_pairs):
                vt = _bf16(acc[:, hp * LANES:(hp + 1) * LANES].T)
                for cc in range(tm // LANES):
                    vt_ref[0, hp, cc] = vt[:, cc * LANES:(cc + 1) * LANES]


def _in_proj(x2, mod3, g_pre, w_ext, b_gate, seq, col_q, col_v, tm=512):
    T, D = x2.shape
    batch = T // seq
    n_s = seq // tm
    n_total = w_ext.shape[1]
    n_gate = b_gate.shape[-1]
    n_pairs = NA_HEADS // 2
    chunk = n_pairs * LANES
    assert col_q % chunk == 0 and col_v % chunk == 0
    kern = functools.partial(_in_proj_kernel, d_model=D, n_gate=n_gate, chunk=chunk, col_q=col_q, col_v=col_v)
    return pl.pallas_call(
        kern,
        out_shape=(
            jax.ShapeDtypeStruct((T, n_total), jnp.bfloat16),
            jax.ShapeDtypeStruct((batch, n_pairs, LANES, seq), jnp.bfloat16),
            jax.ShapeDtypeStruct((batch, n_pairs, seq // LANES, LANES, LANES), jnp.bfloat16),
        ),
        grid=(T // tm,),
        in_specs=[
            pl.BlockSpec((tm, D), lambda i: (i, 0)),
            pl.BlockSpec((1, 1, mod3.shape[-1]), lambda i: ((i * tm) // seq, 0, 0)),
            pl.BlockSpec((1, D), lambda i: (0, 0)),
            pl.BlockSpec((D, n_total), lambda i: (0, 0)),
            pl.BlockSpec((1, n_gate), lambda i: (0, 0)),
        ],
        out_specs=(
            pl.BlockSpec((tm, n_total), lambda i: (i, 0)),
            pl.BlockSpec((1, n_pairs, LANES, tm), lambda i: (i // n_s, 0, 0, i % n_s)),
            pl.BlockSpec((1, n_pairs, tm // LANES, LANES, LANES), lambda i: (i // n_s, 0, i % n_s, 0, 0)),
        ),
        compiler_params=_cparams(("arbitrary",)),
        name="in_proj",
    )(x2, mod3, g_pre.reshape(1, D), w_ext, b_gate.reshape(1, n_gate))


def _na_class(c):
    half = NA_WIN_ROWS // 2
    if c == 0:
        return (lambda i: max(i - half, 0)), NA_WIN_ROWS - 1
    if c == 1:
        return (lambda i: i), NA_WIN_ROWS - 1 - half
    return (lambda i: min(i + half, NA_KEY_ROWS - NA_WIN_ROWS)), -1


NA_PAIR_KEY_ROWS = 10
NA_GROUPS_PER_STEP = 2


def _na_kernel(qt_ref, k_ref, vt_ref, pt_ref, o_ref, *, n_rows):
    n_groups = n_rows // NA_GROUP_ROWS
    tq = NA_GROUP_ROWS * GRID_W
    nkeys = NA_PAIR_KEY_ROWS * GRID_W
    row = lax.broadcasted_iota(jnp.int32, (LANES, LANES), 0)
    lane = lax.broadcasted_iota(jnp.int32, (GRID_W, LANES), 1)
    neg_blk = jnp.full((GRID_W, LANES), NEG_BIG, jnp.float32)

    def body(c, sub, kb_rows):
        j0, off = _na_class(c)
        n_pair = NA_GROUP_ROWS // 2
        starts = [min(j0(2 * ip) - j0(2 * ip) % 2, NA_KEY_ROWS - NA_PAIR_KEY_ROWS) for ip in range(n_pair)]
        scores = []
        for ip in range(n_pair):
            kstart = pl.multiple_of((kb_rows + starts[ip]) * GRID_W, 2 * GRID_W)
            kwin = k_ref[pl.ds(kstart, nkeys), :]
            q0 = sub * tq + ip * LANES
            qp = qt_ref[0, 0, :, q0:q0 + LANES]
            zero = jnp.zeros_like(qp)
            wq = jnp.concatenate([jnp.where(row < NA_HEAD_DIM, qp, zero),
                                  jnp.where(row >= NA_HEAD_DIM, qp, zero)], axis=1)
            scores.append(_dot(kwin, wq))
        outs = []
        for ip in range(n_pair):
            i0 = 2 * ip
            w = starts[ip]
            st = scores[ip]
            blocks = []
            for jw in range(NA_PAIR_KEY_ROWS):
                j = w + jw
                v0 = j0(i0) <= j < j0(i0) + NA_WIN_ROWS
                v1 = j0(i0 + 1) <= j < j0(i0 + 1) + NA_WIN_ROWS
                dr0 = j - i0 + off
                halves = []
                for hl in range(2):
                    if not (v0 or v1):
                        halves.append(neg_blk)
                        continue
                    blk = st[jw * GRID_W:(jw + 1) * GRID_W, hl * LANES:(hl + 1) * LANES] + pt_ref[hl, dr0]
                    if not (v0 and v1):
                        keep = (lane < GRID_W) if v0 else (lane >= GRID_W)
                        blk = jnp.where(keep, blk, NEG_BIG)
                    halves.append(blk)
                blocks.append(jnp.concatenate(halves, axis=1))
            s = jnp.concatenate(blocks, axis=0)
            m = jnp.max(s, axis=0, keepdims=True)
            p = jnp.exp(s - m)
            l = jnp.sum(p, axis=0, keepdims=True)
            c0 = (kb_rows + w) // 2
            vwin = jnp.concatenate([vt_ref[0, 0, c0 + u] for u in range(NA_PAIR_KEY_ROWS // 2)], axis=1)
            on = _dot(vwin, _bf16(p)) / l
            outs.append(jnp.where(row < NA_HEAD_DIM, on[:, 0:LANES], on[:, LANES:2 * LANES]))
        o_ref[sub * tq:(sub + 1) * tq, :] = _bf16(jnp.concatenate(outs, axis=1).T)

    for sub in range(NA_GROUPS_PER_STEP):
        g = pl.program_id(2) * NA_GROUPS_PER_STEP + sub
        kb_rows = jnp.clip(g * NA_GROUP_ROWS - NA_WIN_ROWS // 2, 0, n_rows - NA_KEY_ROWS)
        cls = jnp.where(g == 0, 0, jnp.where(g == n_groups - 1, 2, 1))
        for c in range(3):
            pl.when(cls == c)(functools.partial(body, c, sub, kb_rows))


def _na_attention(proj, q_t, v_t, pair_tab, batch, seq, col_k):
    T = proj.shape[0]
    n_rows = seq // GRID_W
    n_groups = n_rows // NA_GROUP_ROWS
    assert n_groups % NA_GROUPS_PER_STEP == 0
    n_steps = n_groups // NA_GROUPS_PER_STEP
    tq = NA_GROUPS_PER_STEP * NA_GROUP_ROWS * GRID_W
    n_pairs = NA_HEADS // 2
    kern = functools.partial(_na_kernel, n_rows=n_rows)
    return pl.pallas_call(
        kern,
        out_shape=jax.ShapeDtypeStruct((T, NA_WIDTH), jnp.bfloat16),
        grid=(n_pairs, batch, n_steps),
        in_specs=[
            pl.BlockSpec((1, 1, LANES, tq), lambda hp, b, g: (b, hp, 0, g)),
            pl.BlockSpec((seq, LANES), lambda hp, b, g: (b, col_k // LANES + hp)),
            pl.BlockSpec((1, 1, seq // LANES, LANES, LANES), lambda hp, b, g: (b, hp, 0, 0, 0)),
            pl.BlockSpec((2,) + pair_tab.shape[1:], lambda hp, b, g: (hp, 0, 0, 0)),
        ],
        out_specs=pl.BlockSpec((tq, LANES), lambda hp, b, g: (b * n_steps + g, hp)),
        compiler_params=_cparams(("arbitrary", "arbitrary", "arbitrary")),
        name="na_attn",
    )(q_t, proj, v_t, pair_tab)


def _na_pair_tables(rpb):
    H, n_dr, n_dc = rpb.shape
    kw = NA_WIN_COLS
    W = GRID_W
    ring = 2 * W - 1
    rp = jnp.pad(rpb.astype(jnp.float32), ((0, 0), (0, 0), (W - kw, W - kw)), constant_values=NEG_BIG)
    u = jnp.roll(rp, -(W - 1), axis=-1)
    circ = jnp.tile(u, (1, 1, W))[..., :W * (ring - 1)].reshape(H, n_dr, W, ring - 1)
    toep = circ[..., :W]
    cidx = np.arange(W)
    col_start = np.clip(cidx - kw // 2, 0, W - kw)
    col_in = (cidx[None, :] >= col_start[:, None]) & (cidx[None, :] < col_start[:, None] + kw)
    toep_t = jnp.where(jnp.asarray(col_in.T)[None, None], jnp.swapaxes(toep, 2, 3), NEG_BIG)
    neg = jnp.full((H, 1, W, W), NEG_BIG, jnp.float32)
    ext = jnp.concatenate([neg, toep_t, neg], axis=1)
    return jnp.concatenate([ext[:, 1:], ext[:, :-1]], axis=-1)


def _mla_prep_kernel(cq_ref, ckv_ref, kr_ref, gq_ref, gkv_ref, wqm_ref, wqs_ref, wk_ref, wv_ref,
                     pm_ref, tab_ref, q_out, k_out, v_out, *, qscale):
    cq = _f32(cq_ref[...])
    cqn = _bf16(_rms(cq, gq_ref[...], n=MLA_Q_LORA))
    qm = _dot(cqn, wqm_ref[...])
    qs = _dot(cqn, wqs_ref[...])
    ctab = tab_ref[:, 0:LANES]
    stab = tab_ref[:, LANES:2 * LANES]
    for h in range(MLA_HEADS):
        sl = slice(h * LANES, (h + 1) * LANES)
        qh = (qm[:, sl] * ctab + qs[:, sl] * stab) * qscale
        q_out[0, h] = _bf16(qh.T)

    ckv = _f32(ckv_ref[...])
    ckvn = _bf16(_rms(ckv, gkv_ref[...]))
    kk = _dot(ckvn, wk_ref[...])
    vv = _dot(ckvn, wv_ref[...])
    n_chunk = v_out.shape[2]
    ones = jnp.ones((MLA_VROWS - MLA_V, MLA_KEY_CHUNK), jnp.bfloat16)
    for hp in range(MLA_HEADS // 2):
        vt = _bf16(vv[:, hp * LANES:(hp + 1) * LANES].T)
        for cc in range(n_chunk):
            ks = slice(cc * MLA_KEY_CHUNK, (cc + 1) * MLA_KEY_CHUNK)
            for h in range(2):
                r0 = h * MLA_VROWS
                v_out[0, hp, cc, r0:r0 + MLA_V, :] = vt[h * MLA_V:(h + 1) * MLA_V, ks]
                v_out[0, hp, cc, r0 + MLA_V:r0 + MLA_VROWS, :] = ones
    krr = _f32(kr_ref[...]) * tab_ref[:, 2 * LANES:3 * LANES]
    kplace = _dot(_bf16(krr), pm_ref[...])
    for h in range(MLA_HEADS):
        sl = slice(h * LANES, (h + 1) * LANES)
        k_out[:, sl] = _bf16(kk[:, sl] + kplace)


def _mla_prep(proj, col_cq, col_ckv, col_kr, gq, gkv, wqm, wqs, wk, wv, pm, rope_tab, seq, qscale, tm=512):
    T = proj.shape[0]
    batch = T // seq
    n_s = seq // tm
    n_pairs = MLA_HEADS // 2
    cpt = tm // MLA_KEY_CHUNK
    full = lambda a: pl.BlockSpec(a.shape, lambda i: (0,) * a.ndim)
    tab = pl.BlockSpec((tm, 3 * LANES), lambda i: (i % n_s, 0))
    kern = functools.partial(_mla_prep_kernel, qscale=qscale)
    return pl.pallas_call(
        kern,
        out_shape=(
            jax.ShapeDtypeStruct((batch, MLA_HEADS, LANES, seq), jnp.bfloat16),
            jax.ShapeDtypeStruct((T, MLA_HEADS * LANES), jnp.bfloat16),
            jax.ShapeDtypeStruct((batch, n_pairs, seq // MLA_KEY_CHUNK, 2 * MLA_VROWS, MLA_KEY_CHUNK),
                                 jnp.bfloat16),
        ),
        grid=(T // tm,),
        in_specs=[
            pl.BlockSpec((tm, Q_LORA_PAD), lambda i: (i, col_cq // Q_LORA_PAD)),
            pl.BlockSpec((tm, MLA_KV_LORA), lambda i: (i, col_ckv // MLA_KV_LORA)),
            pl.BlockSpec((tm, LANES), lambda i: (i, col_kr // LANES)),
            full(gq), full(gkv), full(wqm), full(wqs), full(wk), full(wv), full(pm),
            tab,
        ],
        out_specs=(
            pl.BlockSpec((1, MLA_HEADS, LANES, tm), lambda i: (i // n_s, 0, 0, i % n_s)),
            pl.BlockSpec((tm, MLA_HEADS * LANES), lambda i: (i, 0)),
            pl.BlockSpec((1, n_pairs, cpt, 2 * MLA_VROWS, MLA_KEY_CHUNK),
                         lambda i: (i // n_s, 0, i % n_s, 0, 0)),
        ),
        compiler_params=_cparams(("arbitrary",)),
        name="mla_prep",
    )(proj, proj, proj, gq, gkv, wqm, wqs, wk, wv, pm, rope_tab)


def _mla_attn_kernel(qt_ref, k_ref, vt_ref, o_ref, m_scr, acc_scr, st_scr):
    n_chunks = vt_ref.shape[2]
    tk = MLA_KEY_CHUNK
    m_scr[...] = jnp.full(m_scr.shape, -jnp.inf, jnp.float32)
    acc_scr[...] = jnp.zeros(acc_scr.shape, jnp.float32)

    def scores(c, slot):
        k0 = pl.multiple_of(c * tk, tk)
        for h in range(2):
            kc = k_ref[pl.ds(k0, tk), h * LANES:(h + 1) * LANES]
            st_scr[slot, h] = _dot(kc, qt_ref[0, h])

    def step(c, slot):
        scores(jnp.minimum(c + 1, n_chunks - 1), 1 - slot)
        vt = vt_ref[0, 0, c]
        pts, alphas = [], []
        for h in range(2):
            st = st_scr[slot, h]
            m_old = m_scr[h]
            m_new = jnp.maximum(m_old, jnp.max(st, axis=0, keepdims=True))
            alpha = jnp.exp2(m_old - m_new)
            pt = jnp.exp2(st - m_new)
            m_scr[h] = m_new
            pts.append(_bf16(pt))
            alphas.append(alpha)
        for h in range(2):
            acc_scr[h] = alphas[h] * acc_scr[h] + _dot(vt[h * MLA_VROWS:(h + 1) * MLA_VROWS, :], pts[h])

    def body(j, carry):
        for u in range(MLA_STEPS_PER_TRIP):
            step(MLA_STEPS_PER_TRIP * j + u, u % 2)
        return carry

    scores(0, 0)
    lax.fori_loop(0, n_chunks // MLA_STEPS_PER_TRIP, body, 0)
    outs = []
    for h in range(2):
        acc = acc_scr[h]
        outs.append(acc[0:MLA_V] / acc[MLA_V:MLA_V + 1])
    o_ref[...] = _bf16(jnp.concatenate(outs, axis=0).T)


def _mla_attention(q_t, k_cat, v_t, batch, seq, tq=512):
    T = k_cat.shape[0]
    n_q = seq // tq
    n_pairs = MLA_HEADS // 2
    n_chunks = seq // MLA_KEY_CHUNK
    assert n_chunks % MLA_STEPS_PER_TRIP == 0
    return pl.pallas_call(
        _mla_attn_kernel,
        out_shape=jax.ShapeDtypeStruct((T, MLA_WIDTH), jnp.bfloat16),
        grid=(batch, n_pairs, n_q),
        in_specs=[
            pl.BlockSpec((1, 2, LANES, tq), lambda b, hp, i: (b, hp, 0, i)),
            pl.BlockSpec((seq, 2 * LANES), lambda b, hp, i: (b, hp)),
            pl.BlockSpec((1, 1, n_chunks, 2 * MLA_VROWS, MLA_KEY_CHUNK), lambda b, hp, i: (b, hp, 0, 0, 0)),
        ],
        out_specs=pl.BlockSpec((tq, LANES), lambda b, hp, i: (b * n_q + i, hp)),
        scratch_shapes=[
            pltpu.VMEM((2, 1, tq), jnp.float32),
            pltpu.VMEM((2, MLA_VROWS, tq), jnp.float32),
            pltpu.VMEM((2, 2, MLA_KEY_CHUNK, tq), jnp.float32),
        ],
        compiler_params=_cparams(("arbitrary", "arbitrary", "arbitrary")),
        name="mla_attn",
    )(q_t, k_cat, v_t)


def _out_proj_kernel(ona_ref, omla_ref, gate_ref, x_ref, mod_ref, gpost_ref, gpre_ref, wna_ref,
                     wmla_ref, wout_ref, wr_ref, br_ref, x1_ref, h2_ref, lg_ref, *, d_model):
    D = d_model
    tm = x_ref.shape[0]
    gate_a = mod_ref[0, :, 2 * D:3 * D]
    shift_f = mod_ref[0, :, 3 * D:4 * D]
    scale_f = mod_ref[0, :, 4 * D:5 * D]
    halves = [slice(0, tm // 2), slice(tm // 2, tm)]
    merged = []
    for rs in halves:
        up_na = _dot(ona_ref[rs, :], wna_ref[...])
        up_mla = _dot(omla_ref[rs, :], wmla_ref[...])
        merged.append(_bf16(_f32(gate_ref[rs, 0:D]) * up_na + _f32(gate_ref[rs, D:2 * D]) * up_mla))
    ys = [_dot(m, wout_ref[...]) for m in merged]
    h2s = []
    for rs, y in zip(halves, ys):
        x1 = x_ref[rs, :] + gate_a * _rms(y, gpost_ref[...])
        x1_ref[rs, :] = x1
        h2 = _rms(x1, gpre_ref[...]) * (1.0 + scale_f) + shift_f
        _store_row_tiles(h2_ref, h2, base=rs.start * SUBLANES)
        h2s.append(_bf16(h2))
    for rs, hb in zip(halves, h2s):
        lg_ref[rs, :] = _dot(hb, wr_ref[...]) + br_ref[...]


def _out_proj(o_na, o_mla, proj, x2, mod3, g_post, g_pre, w_na, w_mla, w_out, w_r, b_r, seq, tm=512):
    T, D = x2.shape
    full = lambda a: pl.BlockSpec(a.shape, lambda i: (0,) * a.ndim)
    row = lambda w: pl.BlockSpec((tm, w), lambda i: (i, 0))
    kern = functools.partial(_out_proj_kernel, d_model=D)
    g_post = g_post.reshape(1, D)
    g_pre = g_pre.reshape(1, D)
    return pl.pallas_call(
        kern,
        out_shape=(
            jax.ShapeDtypeStruct((T, D), jnp.float32),
            jax.ShapeDtypeStruct((T * (D // LANES), LANES), jnp.float32),
            jax.ShapeDtypeStruct((T, LANES), jnp.float32),
        ),
        grid=(T // tm,),
        in_specs=[
            row(NA_WIDTH), row(MLA_WIDTH),
            pl.BlockSpec((tm, 2 * D), lambda i: (i, COL_GATES // (2 * D))),
            row(D),
            pl.BlockSpec((1, 1, mod3.shape[-1]), lambda i: ((i * tm) // seq, 0, 0)),
            full(g_post), full(g_pre), full(w_na), full(w_mla), full(w_out), full(w_r), full(b_r),
        ],
        out_specs=(row(D), pl.BlockSpec((tm * (D // LANES), LANES), lambda i: (i, 0)), row(LANES)),
        compiler_params=_cparams(("arbitrary",)),
        name="out_proj",
    )(o_na, o_mla, proj, x2, mod3, g_post, g_pre, w_na, w_mla, w_out, w_r, b_r)


def _route_kernel(lg_ref, r_ref, cnt_ref, carry_scr, *, sub):
    @pl.when(pl.program_id(0) == 0)
    def _():
        carry_scr[...] = jnp.zeros(carry_scr.shape, jnp.float32)

    tr = lg_ref.shape[0]
    lane = lax.broadcasted_iota(jnp.int32, (sub, LANES), 1).astype(jnp.float32)
    ri = lax.broadcasted_iota(jnp.int32, (sub, sub), 0)
    ci = lax.broadcasted_iota(jnp.int32, (sub, sub), 1)
    tri = jnp.where(ri >= ci, 1.0, 0.0).astype(jnp.bfloat16)
    for s0 in range(0, tr, sub):
        work = lg_ref[s0:s0 + sub, :]
        sels, vals, idxs = [], [], []
        for _k in range(TOP_K):
            mk = jnp.max(work, axis=-1, keepdims=True)
            ik = jnp.min(jnp.where(work == mk, lane, float(LANES)), axis=-1, keepdims=True)
            sk = lane == ik
            work = jnp.where(sk, -jnp.inf, work)
            sels.append(sk)
            vals.append(mk)
            idxs.append(ik)
        es = [jnp.exp(v - vals[0]) for v in vals]
        denom = es[0] + es[1] + es[2] + es[3]
        onehot = jnp.zeros((sub, LANES), jnp.float32)
        for sk in sels:
            onehot = jnp.where(sk, 1.0, onehot)
        prefix = _dot(tri, _bf16(onehot))
        carry = carry_scr[...]
        rank_mat = carry + prefix - 1.0
        res = jnp.zeros((sub, LANES), jnp.float32)
        for kk in range(TOP_K):
            rank_k = jnp.sum(jnp.where(sels[kk], rank_mat, 0.0), axis=-1, keepdims=True)
            res = jnp.where(lane == kk, idxs[kk], res)
            res = jnp.where(lane == TOP_K + kk, rank_k, res)
            res = jnp.where(lane == 2 * TOP_K + kk, es[kk] / denom, res)
        r_ref[s0:s0 + sub, :] = res
        carry_scr[...] = carry + jnp.sum(onehot, axis=0, keepdims=True)
    cnt_ref[...] = carry_scr[...]


def _route(logits, tr=2048, sub=256):
    T = logits.shape[0]
    kern = functools.partial(_route_kernel, sub=sub)
    return pl.pallas_call(
        kern,
        out_shape=(
            jax.ShapeDtypeStruct((T, LANES), jnp.float32),
            jax.ShapeDtypeStruct((1, LANES), jnp.float32),
        ),
        grid=(T // tr,),
        in_specs=[pl.BlockSpec((tr, LANES), lambda i: (i, 0))],
        out_specs=(
            pl.BlockSpec((tr, LANES), lambda i: (i, 0)),
            pl.BlockSpec((1, LANES), lambda i: (0, 0)),
        ),
        scratch_shapes=[pltpu.VMEM((1, LANES), jnp.float32)],
        compiler_params=_cparams(("arbitrary",)),
        name="route",
    )(logits)


def _dispatch_kernel(zero_blk_ref, dest_ref, h_ref, row0_ref, xs_ref, row_ref, zbuf, sem, fill_sem,
                     *, n_tokens):
    i = pl.program_id(0)
    ts = h_ref.shape[0] // SUBLANES
    blk_rows = MOE_BLOCK * SUBLANES

    @pl.when(i == 0)
    def _():
        fill_row = pltpu.make_async_copy(row0_ref, row_ref, fill_sem.at[0])
        fill_row.start()
        zbuf[...] = jnp.zeros(zbuf.shape, zbuf.dtype)
        n_zero = zero_blk_ref.shape[0]

        def zero_copy(z):
            r0 = pl.multiple_of(zero_blk_ref[z] * blk_rows, blk_rows)
            return pltpu.make_async_copy(zbuf, xs_ref.at[pl.ds(r0, blk_rows), :], fill_sem.at[1])

        for z in range(n_zero):
            pl.when(zero_blk_ref[z] >= 0)(lambda z=z: zero_copy(z).start())
        for z in range(n_zero):
            pl.when(zero_blk_ref[z] >= 0)(lambda z=z: zero_copy(z).wait())
        fill_row.wait()

    def row_copy(j, d):
        dst = pl.ds(pl.multiple_of(d * SUBLANES, SUBLANES), SUBLANES)
        return pltpu.make_async_copy(h_ref.at[pl.ds(j * SUBLANES, SUBLANES), :], xs_ref.at[dst, :], sem)

    t0 = i * ts
    for j in range(ts):
        for kk in range(TOP_K):
            d = dest_ref[j * TOP_K + kk]
            row_copy(j, d).start(priority=kk % 2)
            row_ref[d] = (kk * n_tokens + j) + t0
    for _j in range(ts * TOP_K):
        row_copy(0, 0).wait()


def _dispatch(h2_tiles, dest, zero_blk, cap, ts=256):
    T = h2_tiles.shape[0] // SUBLANES
    kern = functools.partial(_dispatch_kernel, n_tokens=T)
    grid_spec = pltpu.PrefetchScalarGridSpec(
        num_scalar_prefetch=1,
        grid=(T // ts,),
        in_specs=[
            pl.BlockSpec((ts * TOP_K,), lambda i, zb: (i,), memory_space=pltpu.SMEM),
            pl.BlockSpec((ts * SUBLANES, LANES), lambda i, zb: (i, 0)),
            pl.BlockSpec(memory_space=pl.ANY),
        ],
        out_specs=(pl.BlockSpec(memory_space=pl.ANY), pl.BlockSpec(memory_space=pltpu.SMEM)),
        scratch_shapes=[
            pltpu.VMEM((MOE_BLOCK * SUBLANES, LANES), h2_tiles.dtype),
            pltpu.SemaphoreType.DMA(()),
            pltpu.SemaphoreType.DMA((2,)),
        ],
    )
    return pl.pallas_call(
        kern,
        out_shape=(jax.ShapeDtypeStruct((cap * SUBLANES, LANES), h2_tiles.dtype),
                   jax.ShapeDtypeStruct((cap,), jnp.int32)),
        grid_spec=grid_spec,
        compiler_params=_cparams(("arbitrary",)),
        name="dispatch",
    )(zero_blk, dest, h2_tiles, jnp.full((cap,), -1, jnp.int32))


def _moe_kernel(blk_e_ref, nvalid_ref, row_prev_ref, row_ref, x_ref, w1_ref, b1_ref, w2_ref, b2_ref,
                out_ref, w1p_scr, w2b_scr, ybuf0, ybuf1, ssem, *, n_items):
    i = pl.program_id(0)
    nvalid = nvalid_ref[0]
    d_ff = w2_ref.shape[1]
    n_groups = (2 * d_ff) // (2 * LANES)

    def tile(r):
        return pl.ds(pl.multiple_of(r * SUBLANES, SUBLANES), SUBLANES)

    ybufs = (ybuf0, ybuf1)

    def scatter_copy(j, r, slot):
        return pltpu.make_async_copy(ybufs[slot].at[pl.ds(j * SUBLANES, SUBLANES), :], out_ref.at[tile(r), :],
                                     ssem.at[slot])

    def wait_scatters(slot):
        for _j in range(MOE_BLOCK):
            scatter_copy(0, 0, slot).wait()

    def scatter_rows(rows_ref, slot):
        spare0 = n_items + slot * MOE_BLOCK
        for j in range(MOE_BLOCK):
            r = rows_ref[0, 0, j]
            scatter_copy(j, jnp.where(r >= 0, r, spare0 + j), slot).start(priority=j % 2)

    @pl.when(i == 0)
    def _():
        ybuf0[...] = jnp.zeros(ybuf0.shape, ybuf0.dtype)
        ybuf1[...] = jnp.zeros(ybuf1.shape, ybuf1.dtype)
        for j in range(MOE_BLOCK):
            scatter_copy(j, n_items + j, 0).start(priority=j % 2)

    @pl.when(i < nvalid)
    def _():
        e = blk_e_ref[i]
        e_prev = blk_e_ref[jnp.maximum(i - 1, 0)]

        @pl.when((i == 0) | (e != e_prev))
        def _():
            r = lax.broadcasted_iota(jnp.int32, (2 * LANES, 2 * LANES), 0)
            c = lax.broadcasted_iota(jnp.int32, (2 * LANES, 2 * LANES), 1)
            src = jnp.where(c < LANES, 2 * c, 2 * (c - LANES) + 1)
            perm = jnp.where(r == src, 1.0, 0.0).astype(jnp.bfloat16)
            for gI in range(n_groups):
                sl = slice(gI * 2 * LANES, (gI + 1) * 2 * LANES)
                w1p_scr[:, sl] = _bf16(_dot(_bf16(w1_ref[0, :, sl]), perm))
            w2b_scr[...] = _bf16(w2_ref[0])

        def block_step(slot):
            wait_scatters(slot)
            scatter_rows(row_prev_ref, 1 - slot)

            xb = _bf16(_load_row_tiles(x_ref, MOE_BLOCK))
            hcat = _dot(xb, w1p_scr[...]) + b1_ref[0]
            acts = []
            for gI in range(n_groups):
                glu = jnp.minimum(hcat[:, gI * 2 * LANES: gI * 2 * LANES + LANES], SWIGLU_LIMIT)
                lin = jnp.clip(hcat[:, gI * 2 * LANES + LANES: (gI + 1) * 2 * LANES],
                               -SWIGLU_LIMIT, SWIGLU_LIMIT)
                acts.append(_bf16(glu * jax.nn.sigmoid(SWIGLU_ALPHA * glu) * (lin + 1.0)))
            act = jnp.concatenate(acts, axis=-1)
            _store_row_tiles(ybufs[slot], _dot(act, w2b_scr[...]) + b2_ref[0])

            @pl.when(i == nvalid - 1)
            def _():
                scatter_rows(row_ref, slot)
                wait_scatters(1 - slot)
                wait_scatters(slot)

        parity = lax.rem(i, 2)
        for slot in range(2):
            pl.when(parity == slot)(functools.partial(block_step, slot))


def _moe_ffn(xs, slot_row, n_items, blk_e, nvalid, w1, b1p, w2, b2):
    D = w1.shape[1]
    assert D == SUBLANES * LANES
    E, _, F2 = w1.shape
    F = w2.shape[1]
    nblk = slot_row.shape[0]

    def cur(i, be, nv):
        return (jnp.minimum(i, nv[0] - 1), 0, 0)

    def prv(i, be, nv):
        return (jnp.where(i == 0, nblk, jnp.maximum(jnp.minimum(i, nv[0] - 1) - 1, 0)), 0, 0)

    slot_row = jnp.concatenate([slot_row, jnp.full((1, 1, MOE_BLOCK), -1, slot_row.dtype)], axis=0)
    slot_spec = lambda f: pl.BlockSpec((1, 1, MOE_BLOCK), f, memory_space=pltpu.SMEM)
    grid_spec = pltpu.PrefetchScalarGridSpec(
        num_scalar_prefetch=2,
        grid=(nblk,),
        in_specs=[
            slot_spec(prv), slot_spec(cur),
            pl.BlockSpec((MOE_BLOCK * SUBLANES, LANES), lambda i, be, nv: (jnp.minimum(i, nv[0] - 1), 0)),
            pl.BlockSpec((1, D, F2), lambda i, be, nv: (be[i], 0, 0)),
            pl.BlockSpec((1, 1, F2), lambda i, be, nv: (be[i], 0, 0)),
            pl.BlockSpec((1, F, D), lambda i, be, nv: (be[i], 0, 0)),
            pl.BlockSpec((1, 1, D), lambda i, be, nv: (be[i], 0, 0)),
        ],
        out_specs=pl.BlockSpec(memory_space=pl.ANY),
        scratch_shapes=[
            pltpu.VMEM((D, F2), jnp.bfloat16),
            pltpu.VMEM((F, D), jnp.bfloat16),
            pltpu.VMEM((MOE_BLOCK * SUBLANES, LANES), jnp.float32),
            pltpu.VMEM((MOE_BLOCK * SUBLANES, LANES), jnp.float32),
            pltpu.SemaphoreType.DMA((2,)),
        ],
    )
    kern = functools.partial(_moe_kernel, n_items=n_items)
    return pl.pallas_call(
        kern,
        out_shape=jax.ShapeDtypeStruct(((n_items + 2 * MOE_BLOCK) * SUBLANES, LANES), jnp.float32),
        grid_spec=grid_spec,
        compiler_params=_cparams(("arbitrary",)),
        name="moe_ffn",
    )(blk_e, nvalid, slot_row, slot_row, xs, w1, b1p, w2, b2)


def _combine_kernel(y0_ref, y1_ref, y2_ref, y3_ref, r_ref, x1_ref, mod_ref, g_ref, o_ref, *, d_model):
    D = d_model
    tc = x1_ref.shape[0]
    r = r_ref[...]
    moe = None
    for kk, y_ref in enumerate((y0_ref, y1_ref, y2_ref, y3_ref)):
        term = r[:, 2 * TOP_K + kk: 2 * TOP_K + kk + 1] * _load_row_tiles(y_ref, tc)
        moe = term if moe is None else moe + term
    gate_f = mod_ref[0, :, 5 * D:6 * D]
    o_ref[...] = x1_ref[...] + gate_f * _rms(moe, g_ref[...])


def _combine(y_k, route, x1, mod3, g_post, seq, tc=512):
    T, D = x1.shape
    n_t = T // tc
    kern = functools.partial(_combine_kernel, d_model=D)
    y_spec = lambda kk: pl.BlockSpec((tc * SUBLANES, LANES), lambda i: (kk * n_t + i, 0))
    return pl.pallas_call(
        kern,
        out_shape=jax.ShapeDtypeStruct((T, D), jnp.float32),
        grid=(n_t,),
        in_specs=[
            y_spec(0), y_spec(1), y_spec(2), y_spec(3),
            pl.BlockSpec((tc, LANES), lambda i: (i, 0)),
            pl.BlockSpec((tc, D), lambda i: (i, 0)),
            pl.BlockSpec((1, 1, mod3.shape[-1]), lambda i: ((i * tc) // seq, 0, 0)),
            pl.BlockSpec((1, D), lambda i: (0, 0)),
        ],
        out_specs=pl.BlockSpec((tc, D), lambda i: (i, 0)),
        compiler_params=_cparams(("arbitrary",)),
        name="combine",
    )(y_k, y_k, y_k, y_k, route, x1, mod3, g_post.reshape(1, D))


def _rope_swap_cols(w):
    nf = MLA_ROPE // 4
    return jnp.concatenate([-w[..., nf:2 * nf], w[..., 0:nf], -w[..., 3 * nf:4 * nf], w[..., 2 * nf:3 * nf]],
                           axis=-1)


def _rope_tables(seq):
    n_rows = seq // GRID_W
    nf = MLA_ROPE // 4
    inv = ROPE_THETA ** (-jnp.arange(nf, dtype=jnp.float32) / nf)
    ar = jnp.arange(n_rows, dtype=jnp.float32)[:, None] * inv
    ac = jnp.arange(GRID_W, dtype=jnp.float32)[:, None] * inv
    cr, sr, cc, sc = jnp.cos(ar), jnp.sin(ar), jnp.cos(ac), jnp.sin(ac)

    def lanes(parts, n):
        z = lambda w: jnp.zeros((n, w), jnp.float32)
        return jnp.concatenate([z(p) if isinstance(p, int) else p for p in parts], axis=-1)

    pad = LANES - MLA_NOPE - MLA_ROPE
    row_tab = lanes([MLA_NOPE, cr, cr, 2 * nf, pad, MLA_NOPE, sr, sr, 2 * nf, pad,
                     cr, cr, 2 * nf, sr, sr, 2 * nf, MLA_NOPE], n_rows)
    col_tab = lanes([jnp.ones((GRID_W, MLA_NOPE), jnp.float32), 2 * nf, cc, cc, pad, MLA_NOPE, 2 * nf, sc, sc, pad,
                     2 * nf, cc, cc, 2 * nf, sc, sc, MLA_NOPE], GRID_W)
    return (row_tab[:, None, :] + col_tab[None, :, :]).reshape(seq, 3 * LANES)


def kernel(x, c, w_ada, b_ada, g_attn_pre, g_attn_post, w_in, b_gate, na_rpb, q_norm_g, kv_norm_g,
           w_uq, w_ukv, w_na_up, w_mla_up, w_out, g_ffn_pre, g_ffn_post, w_router, b_router,
           w1, b1, w2, b2):
    B, S, D = x.shape
    T = B * S
    depth = w_ada.shape[0]
    E = w_router.shape[-1]
    n_rows = S // GRID_W
    assert S % (NA_GROUP_ROWS * GRID_W) == 0 and n_rows >= NA_KEY_ROWS
    assert E <= LANES and T % 2048 == 0

    bf = jnp.bfloat16
    rope_tab = _rope_tables(S)
    pm_np = np.zeros((LANES, LANES), np.float32)
    for cidx in range(MLA_ROPE):
        pm_np[cidx, MLA_NOPE + cidx] = 1.0
        pm_np[MLA_ROPE + cidx, MLA_NOPE + cidx] = 1.0
    pm = jnp.asarray(pm_np, bf)
    qscale = float((MLA_NOPE + MLA_ROPE) ** -0.5 * math.log2(math.e))

    col_na = 2 * D
    col_cq = col_na + 3 * NA_WIDTH
    col_ckv = col_cq + Q_LORA_PAD
    col_kr = col_ckv + MLA_KV_LORA
    assert col_cq % Q_LORA_PAD == 0 and col_ckv % MLA_KV_LORA == 0

    x2 = x.reshape(T, D)
    for l in range(depth):
        mod = _ada_mod(c, w_ada[l], b_ada[l])
        mod3 = mod.reshape(B, 1, 6 * D)

        o_na, o_cq, o_ckv, o_kr, o_g = np.cumsum([0, 3 * NA_WIDTH, MLA_Q_LORA, MLA_KV_LORA, MLA_ROPE]).tolist()
        wi = w_in[l]
        w_kr = wi[:, o_kr:o_g]
        w_ext = jnp.concatenate([
            wi[:, o_g:o_g + 2 * D],
            wi[:, o_na:o_cq],
            wi[:, o_cq:o_ckv], jnp.zeros((D, Q_LORA_PAD - MLA_Q_LORA), wi.dtype),
            wi[:, o_ckv:o_kr],
            w_kr, _rope_swap_cols(w_kr), jnp.zeros((D, LANES - 2 * MLA_ROPE), wi.dtype),
        ], axis=1).astype(bf)
        proj, na_qt, na_vt = _in_proj(x2, mod3, g_attn_pre[l], w_ext, b_gate[l], S,
                                      col_na, col_na + 2 * NA_WIDTH)

        o_na_tok = _na_attention(proj, na_qt, na_vt, _na_pair_tables(na_rpb[l]), B, S, col_na + NA_WIDTH)

        wq = w_uq[l]
        zq = jnp.zeros((MLA_Q_LORA, MLA_HEADS, LANES - MLA_NOPE - MLA_ROPE), wq.dtype)
        wqm = jnp.concatenate([wq, zq], axis=-1).reshape(MLA_Q_LORA, MLA_HEADS * LANES)
        wqs = jnp.concatenate([jnp.zeros((MLA_Q_LORA, MLA_HEADS, MLA_NOPE), wq.dtype),
                               _rope_swap_cols(wq[..., MLA_NOPE:]), zq], axis=-1
                              ).reshape(MLA_Q_LORA, MLA_HEADS * LANES)
        rpad = ((0, Q_LORA_PAD - MLA_Q_LORA), (0, 0))
        wqm = jnp.pad(wqm, rpad).astype(bf)
        wqs = jnp.pad(wqs, rpad).astype(bf)
        wkv = w_ukv[l]
        wk = jnp.concatenate([wkv[..., :MLA_NOPE], jnp.zeros((MLA_KV_LORA, MLA_HEADS, LANES - MLA_NOPE), wkv.dtype)],
                             axis=-1).reshape(MLA_KV_LORA, MLA_HEADS * LANES).astype(bf)
        wv = wkv[..., MLA_NOPE:].reshape(MLA_KV_LORA, MLA_WIDTH).astype(bf)
        gq = jnp.pad(q_norm_g[l], (0, Q_LORA_PAD - MLA_Q_LORA)).reshape(1, Q_LORA_PAD)
        gkv = kv_norm_g[l].reshape(1, MLA_KV_LORA)
        q_cat, k_cat, v_mla = _mla_prep(proj, col_cq, col_ckv, col_kr, gq, gkv, wqm, wqs, wk, wv, pm,
                                        rope_tab, S, qscale)
        o_mla_tok = _mla_attention(q_cat, k_cat, v_mla, B, S)

        w_r = jnp.pad(w_router[l], ((0, 0), (0, LANES - E))).astype(bf)
        b_r = jnp.concatenate([b_router[l], jnp.full((LANES - E,), NEG_BIG, jnp.float32)]).reshape(1, LANES)
        x1, h2, logits = _out_proj(o_na_tok, o_mla_tok, proj, x2, mod3, g_attn_post[l], g_ffn_pre[l],
                                   w_na_up[l].astype(bf), w_mla_up[l].astype(bf), w_out[l].astype(bf),
                                   w_r, b_r, S)

        route, counts = _route(logits)
        e_idx = route[:, 0:TOP_K].astype(jnp.int32)
        rank = route[:, TOP_K:2 * TOP_K].astype(jnp.int32)
        cnt = counts[0, :E].astype(jnp.int32)
        padded = ((cnt + MOE_BLOCK - 1) // MOE_BLOCK) * MOE_BLOCK
        pend = jnp.cumsum(padded)
        pstart = pend - padded
        n_items = T * TOP_K
        nblk = -(-n_items // MOE_BLOCK) + E
        cap = nblk * MOE_BLOCK
        dest = (rank + jnp.sum(jnp.where(e_idx[..., None] == jnp.arange(E, dtype=jnp.int32), pstart, 0), axis=-1)
                ).reshape(n_items)
        blk_off = jnp.arange(nblk, dtype=jnp.int32) * MOE_BLOCK
        blk_e = jnp.minimum(jnp.sum((pend[None, :] <= blk_off[:, None]).astype(jnp.int32), axis=-1), E - 1)
        nvalid = (pend[-1:] // MOE_BLOCK).astype(jnp.int32)

        last_blk = jnp.where(padded > 0, pend // MOE_BLOCK - 1, -1)
        tail_blk = nvalid[0] + jnp.arange(nblk - n_items // MOE_BLOCK, dtype=jnp.int32)
        zero_blk = jnp.concatenate([last_blk, jnp.where(tail_blk < nblk, tail_blk, -1)]).astype(jnp.int32)
        xs, slot_row = _dispatch(h2, dest, zero_blk, cap)
        slot_row = slot_row.reshape(nblk, 1, MOE_BLOCK)
        F2 = w1.shape[-1]
        b1p = b1[l].reshape(E, F2 // (2 * LANES), LANES, 2).transpose(0, 1, 3, 2).reshape(E, 1, F2)
        y_k = _moe_ffn(xs, slot_row, n_items, blk_e, nvalid, w1[l], b1p, w2[l], b2[l].reshape(E, 1, D))
        x2 = _combine(y_k, route, x1, mod3, g_ffn_post[l], S)
    return x2.reshape(B, S, D)
```

```python
import functools
import math

import numpy as np
import jax
import jax.numpy as jnp
from jax import lax
from jax.experimental import pallas as pl
from jax.experimental.pallas import tpu as pltpu

GRID_W = 64
NA_HEADS = 8
NA_HEAD_DIM = 64
NA_WIN_ROWS = 8
NA_WIN_COLS = 16
NA_WIDTH = NA_HEADS * NA_HEAD_DIM
MLA_HEADS = 8
MLA_Q_LORA = 384
MLA_KV_LORA = 256
MLA_NOPE = 64
MLA_ROPE = 32
MLA_V = 64
MLA_WIDTH = MLA_HEADS * MLA_V
MLA_KEY_CHUNK = 256
MLA_VROWS = MLA_V + 16
MLA_STEPS_PER_TRIP = 16
ROPE_THETA = 100.0
TOP_K = 4
SWIGLU_ALPHA = 1.702
SWIGLU_LIMIT = 7.0
MOE_BLOCK = 256
RMS_EPS = 1e-6
NEG_BIG = -1e30

LANES = 128
VMEM_LIMIT_BYTES = 56 * 1024 * 1024

Q_LORA_PAD = 512
COL_GATES = 0
NA_GROUP_ROWS = 8
NA_KEY_ROWS = 16


def _f32(x):
    return x.astype(jnp.float32)


def _bf16(x):
    return x.astype(jnp.bfloat16)


def _dot(a, b):
    return jnp.dot(a, b, preferred_element_type=jnp.float32)


def _rms(x, g, n=None):
    n = x.shape[-1] if n is None else n
    ms = jnp.sum(x * x, axis=-1, keepdims=True) * (1.0 / n)
    return x * lax.rsqrt(ms + RMS_EPS) * g


SUBLANES = 8


def _store_row_tiles(ref, x, base=0):
    n = x.shape[0]
    for c in range(x.shape[1] // LANES):
        ref[pl.ds(base + c, n, stride=SUBLANES), :] = x[:, c * LANES:(c + 1) * LANES]


def _load_row_tiles(ref, n, base=0):
    chunks = [ref[pl.ds(base + c, n, stride=SUBLANES), :] for c in range(SUBLANES)]
    return jnp.concatenate(chunks, axis=-1)


def _cparams(sem):
    return pltpu.CompilerParams(dimension_semantics=sem, vmem_limit_bytes=VMEM_LIMIT_BYTES)


def _ada_kernel(c_ref, w_ref, b_ref, o_ref):
    c = c_ref[...]
    sc = c * jax.nn.sigmoid(c)
    o_ref[...] = _dot(_bf16(sc), _bf16(w_ref[...])) + b_ref[...]


def _ada_mod(c, w_ada, b_ada):
    B, D = c.shape
    n_out = w_ada.shape[1]
    return pl.pallas_call(
        _ada_kernel,
        out_shape=jax.ShapeDtypeStruct((B, n_out), jnp.float32),
        grid=(n_out // D,),
        in_specs=[
            pl.BlockSpec((B, D), lambda j: (0, 0)),
            pl.BlockSpec((D, D), lambda j: (0, j)),
            pl.BlockSpec((1, D), lambda j: (0, j)),
        ],
        out_specs=pl.BlockSpec((B, D), lambda j: (0, j)),
        compiler_params=_cparams(("arbitrary",)),
        name="ada_mod",
    )(c, w_ada, b_ada.reshape(1, n_out))


def _in_proj_kernel(x_ref, mod_ref, g_ref, w_ref, bg_ref, o_ref, qt_ref, vt_ref, *, d_model, n_gate,
                    chunk, col_q, col_v):
    x = x_ref[...]
    tm = x.shape[0]
    shift = mod_ref[0, :, 0:d_model]
    scale = mod_ref[0, :, d_model:2 * d_model]
    h = _rms(x, g_ref[...]) * (1.0 + scale) + shift
    hb = _bf16(h)
    n_total = w_ref.shape[1]
    n_pairs = NA_HEADS // 2
    for c0 in range(0, n_total, chunk):
        c1 = min(c0 + chunk, n_total)
        acc = _dot(hb, w_ref[:, c0:c1])
        if c0 < n_gate:
            acc = jax.nn.sigmoid(acc + bg_ref[:, c0:c1])
        o_ref[:, c0:c1] = _bf16(acc)
        if c0 == col_q:
            for hp in range(n_pairs):
                qt_ref[0, hp] = _bf16((acc[:, hp * LANES:(hp + 1) * LANES] * NA_HEAD_DIM ** -0.5).T)
        if c0 == col_v:
            for hp in range(n_pairs):
                vt = _bf16(acc[:, hp * LANES:(hp + 1) * LANES].T)
                for cc in range(tm // LANES):
                    vt_ref[0, hp, cc] = vt[:, cc * LANES:(cc + 1) * LANES]


def _in_proj(x2, mod3, g_pre, w_ext, b_gate, seq, col_q, col_v, tm=512):
    T, D = x2.shape
    batch = T // seq
    n_s = seq // tm
    n_total = w_ext.shape[1]
    n_gate = b_gate.shape[-1]
    n_pairs = NA_HEADS // 2
    chunk = n_pairs * LANES
    assert col_q % chunk == 0 and col_v % chunk == 0
    kern = functools.partial(_in_proj_kernel, d_model=D, n_gate=n_gate, chunk=chunk, col_q=col_q, col_v=col_v)
    return pl.pallas_call(
        kern,
        out_shape=(
            jax.ShapeDtypeStruct((T, n_total), jnp.bfloat16),
            jax.ShapeDtypeStruct((batch, n_pairs, LANES, seq), jnp.bfloat16),
            jax.ShapeDtypeStruct((batch, n_pairs, seq // LANES, LANES, LANES), jnp.bfloat16),
        ),
        grid=(T // tm,),
        in_specs=[
            pl.BlockSpec((tm, D), lambda i: (i, 0)),
            pl.BlockSpec((1, 1, mod3.shape[-1]), lambda i: ((i * tm) // seq, 0, 0)),
            pl.BlockSpec((1, D), lambda i: (0, 0)),
            pl.BlockSpec((D, n_total), lambda i: (0, 0)),
            pl.BlockSpec((1, n_gate), lambda i: (0, 0)),
        ],
        out_specs=(
            pl.BlockSpec((tm, n_total), lambda i: (i, 0)),
            pl.BlockSpec((1, n_pairs, LANES, tm), lambda i: (i // n_s, 0, 0, i % n_s)),
            pl.BlockSpec((1, n_pairs, tm // LANES, LANES, LANES), lambda i: (i // n_s, 0, i % n_s, 0, 0)),
        ),
        compiler_params=_cparams(("arbitrary",)),
        name="in_proj",
    )(x2, mod3, g_pre.reshape(1, D), w_ext, b_gate.reshape(1, n_gate))


def _na_class(c):
    half = NA_WIN_ROWS // 2
    if c == 0:
        return (lambda i: max(i - half, 0)), NA_WIN_ROWS - 1
    if c == 1:
        return (lambda i: i), NA_WIN_ROWS - 1 - half
    return (lambda i: min(i + half, NA_KEY_ROWS - NA_WIN_ROWS)), -1


NA_PAIR_KEY_ROWS = 10
NA_GROUPS_PER_STEP = 4


def _na_kernel(qt_ref, k_ref, vt_ref, pt_ref, o_ref, *, n_rows):
    n_groups = n_rows // NA_GROUP_ROWS
    tq = NA_GROUP_ROWS * GRID_W
    nkeys = NA_PAIR_KEY_ROWS * GRID_W
    row = lax.broadcasted_iota(jnp.int32, (LANES, LANES), 0)
    lane = lax.broadcasted_iota(jnp.int32, (GRID_W, LANES), 1)
    neg_blk = jnp.full((GRID_W, LANES), NEG_BIG, jnp.float32)

    def body(c, sub, kb_rows):
        j0, off = _na_class(c)
        n_pair = NA_GROUP_ROWS // 2
        starts = [min(j0(2 * ip) - j0(2 * ip) % 2, NA_KEY_ROWS - NA_PAIR_KEY_ROWS) for ip in range(n_pair)]
        scores = []
        for ip in range(n_pair):
            kstart = pl.multiple_of((kb_rows + starts[ip]) * GRID_W, 2 * GRID_W)
            kwin = k_ref[pl.ds(kstart, nkeys), :]
            q0 = sub * tq + ip * LANES
            qp = qt_ref[0, 0, :, q0:q0 + LANES]
            zero = jnp.zeros_like(qp)
            wq = jnp.concatenate([jnp.where(row < NA_HEAD_DIM, qp, zero),
                                  jnp.where(row >= NA_HEAD_DIM, qp, zero)], axis=1)
            scores.append(_dot(kwin, wq))
        outs = []
        for ip in range(n_pair):
            i0 = 2 * ip
            w = starts[ip]
            st = scores[ip]
            blocks = []
            for jw in range(NA_PAIR_KEY_ROWS):
                j = w + jw
                v0 = j0(i0) <= j < j0(i0) + NA_WIN_ROWS
                v1 = j0(i0 + 1) <= j < j0(i0 + 1) + NA_WIN_ROWS
                dr0 = j - i0 + off
                halves = []
                for hl in range(2):
                    if not (v0 or v1):
                        halves.append(neg_blk)
                        continue
                    blk = st[jw * GRID_W:(jw + 1) * GRID_W, hl * LANES:(hl + 1) * LANES] + pt_ref[hl, dr0]
                    if not (v0 and v1):
                        keep = (lane < GRID_W) if v0 else (lane >= GRID_W)
                        blk = jnp.where(keep, blk, NEG_BIG)
                    halves.append(blk)
                blocks.append(jnp.concatenate(halves, axis=1))
            s = jnp.concatenate(blocks, axis=0)
            m = jnp.max(s, axis=0, keepdims=True)
            p = jnp.exp(s - m)
            l = jnp.sum(p, axis=0, keepdims=True)
            c0 = (kb_rows + w) // 2
            vwin = jnp.concatenate([vt_ref[0, 0, c0 + u] for u in range(NA_PAIR_KEY_ROWS // 2)], axis=1)
            on = _dot(vwin, _bf16(p)) / l
            outs.append(jnp.where(row < NA_HEAD_DIM, on[:, 0:LANES], on[:, LANES:2 * LANES]))
        o_ref[sub * tq:(sub + 1) * tq, :] = _bf16(jnp.concatenate(outs, axis=1).T)

    for sub in range(NA_GROUPS_PER_STEP):
        g = pl.program_id(2) * NA_GROUPS_PER_STEP + sub
        kb_rows = jnp.clip(g * NA_GROUP_ROWS - NA_WIN_ROWS // 2, 0, n_rows - NA_KEY_ROWS)
        cls = jnp.where(g == 0, 0, jnp.where(g == n_groups - 1, 2, 1))
        for c in range(3):
            pl.when(cls == c)(functools.partial(body, c, sub, kb_rows))


def _na_attention(proj, q_t, v_t, pair_tab, batch, seq, col_k):
    T = proj.shape[0]
    n_rows = seq // GRID_W
    n_groups = n_rows // NA_GROUP_ROWS
    assert n_groups % NA_GROUPS_PER_STEP == 0
    n_steps = n_groups // NA_GROUPS_PER_STEP
    tq = NA_GROUPS_PER_STEP * NA_GROUP_ROWS * GRID_W
    n_pairs = NA_HEADS // 2
    kern = functools.partial(_na_kernel, n_rows=n_rows)
    return pl.pallas_call(
        kern,
        out_shape=jax.ShapeDtypeStruct((T, NA_WIDTH), jnp.bfloat16),
        grid=(n_pairs, batch, n_steps),
        in_specs=[
            pl.BlockSpec((1, 1, LANES, tq), lambda hp, b, g: (b, hp, 0, g)),
            pl.BlockSpec((seq, LANES), lambda hp, b, g: (b, col_k // LANES + hp)),
            pl.BlockSpec((1, 1, seq // LANES, LANES, LANES), lambda hp, b, g: (b, hp, 0, 0, 0)),
            pl.BlockSpec((2,) + pair_tab.shape[1:], lambda hp, b, g: (hp, 0, 0, 0)),
        ],
        out_specs=pl.BlockSpec((tq, LANES), lambda hp, b, g: (b * n_steps + g, hp)),
        compiler_params=_cparams(("arbitrary", "arbitrary", "arbitrary")),
        name="na_attn",
    )(q_t, proj, v_t, pair_tab)


def _na_pair_tables(rpb):
    H, n_dr, n_dc = rpb.shape
    kw = NA_WIN_COLS
    W = GRID_W
    ring = 2 * W - 1
    rp = jnp.pad(rpb.astype(jnp.float32), ((0, 0), (0, 0), (W - kw, W - kw)), constant_values=NEG_BIG)
    u = jnp.roll(rp, -(W - 1), axis=-1)
    circ = jnp.tile(u, (1, 1, W))[..., :W * (ring - 1)].reshape(H, n_dr, W, ring - 1)
    toep = circ[..., :W]
    cidx = np.arange(W)
    col_start = np.clip(cidx - kw // 2, 0, W - kw)
    col_in = (cidx[None, :] >= col_start[:, None]) & (cidx[None, :] < col_start[:, None] + kw)
    toep_t = jnp.where(jnp.asarray(col_in.T)[None, None], jnp.swapaxes(toep, 2, 3), NEG_BIG)
    neg = jnp.full((H, 1, W, W), NEG_BIG, jnp.float32)
    ext = jnp.concatenate([neg, toep_t, neg], axis=1)
    return jnp.concatenate([ext[:, 1:], ext[:, :-1]], axis=-1)


def _mla_prep_kernel(cq_ref, ckv_ref, kr_ref, gq_ref, gkv_ref, wqm_ref, wqs_ref, wk_ref, wv_ref,
                     pm_ref, tab_ref, q_out, k_out, v_out, *, qscale):
    cq = _f32(cq_ref[...])
    cqn = _bf16(_rms(cq, gq_ref[...], n=MLA_Q_LORA))
    qm = _dot(cqn, wqm_ref[...])
    qs = _dot(cqn, wqs_ref[...])
    ctab = tab_ref[:, 0:LANES]
    stab = tab_ref[:, LANES:2 * LANES]
    for h in range(MLA_HEADS):
        sl = slice(h * LANES, (h + 1) * LANES)
        qh = (qm[:, sl] * ctab + qs[:, sl] * stab) * qscale
        q_out[0, h] = _bf16(qh.T)

    ckv = _f32(ckv_ref[...])
    ckvn = _bf16(_rms(ckv, gkv_ref[...]))
    kk = _dot(ckvn, wk_ref[...])
    vv = _dot(ckvn, wv_ref[...])
    n_chunk = v_out.shape[2]
    ones = jnp.ones((MLA_VROWS - MLA_V, MLA_KEY_CHUNK), jnp.bfloat16)
    for hp in range(MLA_HEADS // 2):
        vt = _bf16(vv[:, hp * LANES:(hp + 1) * LANES].T)
        for cc in range(n_chunk):
            ks = slice(cc * MLA_KEY_CHUNK, (cc + 1) * MLA_KEY_CHUNK)
            for h in range(2):
                r0 = h * MLA_VROWS
                v_out[0, hp, cc, r0:r0 + MLA_V, :] = vt[h * MLA_V:(h + 1) * MLA_V, ks]
                v_out[0, hp, cc, r0 + MLA_V:r0 + MLA_VROWS, :] = ones
    krr = _f32(kr_ref[...]) * tab_ref[:, 2 * LANES:3 * LANES]
    kplace = _dot(_bf16(krr), pm_ref[...])
    for h in range(MLA_HEADS):
        sl = slice(h * LANES, (h + 1) * LANES)
        k_out[:, sl] = _bf16(kk[:, sl] + kplace)


def _mla_prep(proj, col_cq, col_ckv, col_kr, gq, gkv, wqm, wqs, wk, wv, pm, rope_tab, seq, qscale, tm=512):
    T = proj.shape[0]
    batch = T // seq
    n_s = seq // tm
    n_pairs = MLA_HEADS // 2
    cpt = tm // MLA_KEY_CHUNK
    full = lambda a: pl.BlockSpec(a.shape, lambda i: (0,) * a.ndim)
    tab = pl.BlockSpec((tm, 3 * LANES), lambda i: (i % n_s, 0))
    kern = functools.partial(_mla_prep_kernel, qscale=qscale)
    return pl.pallas_call(
        kern,
        out_shape=(
            jax.ShapeDtypeStruct((batch, MLA_HEADS, LANES, seq), jnp.bfloat16),
            jax.ShapeDtypeStruct((T, MLA_HEADS * LANES), jnp.bfloat16),
            jax.ShapeDtypeStruct((batch, n_pairs, seq // MLA_KEY_CHUNK, 2 * MLA_VROWS, MLA_KEY_CHUNK),
                                 jnp.bfloat16),
        ),
        grid=(T // tm,),
        in_specs=[
            pl.BlockSpec((tm, Q_LORA_PAD), lambda i: (i, col_cq // Q_LORA_PAD)),
            pl.BlockSpec((tm, MLA_KV_LORA), lambda i: (i, col_ckv // MLA_KV_LORA)),
            pl.BlockSpec((tm, LANES), lambda i: (i, col_kr // LANES)),
            full(gq), full(gkv), full(wqm), full(wqs), full(wk), full(wv), full(pm),
            tab,
        ],
        out_specs=(
            pl.BlockSpec((1, MLA_HEADS, LANES, tm), lambda i: (i // n_s, 0, 0, i % n_s)),
            pl.BlockSpec((tm, MLA_HEADS * LANES), lambda i: (i, 0)),
            pl.BlockSpec((1, n_pairs, cpt, 2 * MLA_VROWS, MLA_KEY_CHUNK),
                         lambda i: (i // n_s, 0, i % n_s, 0, 0)),
        ),
        compiler_params=_cparams(("arbitrary",)),
        name="mla_prep",
    )(proj, proj, proj, gq, gkv, wqm, wqs, wk, wv, pm, rope_tab)


def _mla_attn_kernel(qt_ref, k_ref, vt_ref, o_ref, m_scr, acc_scr, st_scr):
    n_chunks = vt_ref.shape[2]
    tk = MLA_KEY_CHUNK
    m_scr[...] = jnp.full(m_scr.shape, -jnp.inf, jnp.float32)
    acc_scr[...] = jnp.zeros(acc_scr.shape, jnp.float32)

    def scores(c, slot):
        k0 = pl.multiple_of(c * tk, tk)
        for h in range(2):
            kc = k_ref[pl.ds(k0, tk), h * LANES:(h + 1) * LANES]
            st_scr[slot, h] = _dot(kc, qt_ref[0, h])

    def step(c, slot):
        scores(jnp.minimum(c + 1, n_chunks - 1), 1 - slot)
        vt = vt_ref[0, 0, c]
        pts, alphas = [], []
        for h in range(2):
            st = st_scr[slot, h]
            m_old = m_scr[h]
            m_new = jnp.maximum(m_old, jnp.max(st, axis=0, keepdims=True))
            alpha = jnp.exp2(m_old - m_new)
            pt = jnp.exp2(st - m_new)
            m_scr[h] = m_new
            pts.append(_bf16(pt))
            alphas.append(alpha)
        for h in range(2):
            acc_scr[h] = alphas[h] * acc_scr[h] + _dot(vt[h * MLA_VROWS:(h + 1) * MLA_VROWS, :], pts[h])

    def body(j, carry):
        for u in range(MLA_STEPS_PER_TRIP):
            step(MLA_STEPS_PER_TRIP * j + u, u % 2)
        return carry

    scores(0, 0)
    lax.fori_loop(0, n_chunks // MLA_STEPS_PER_TRIP, body, 0)
    outs = []
    for h in range(2):
        acc = acc_scr[h]
        outs.append(acc[0:MLA_V] / acc[MLA_V:MLA_V + 1])
    o_ref[...] = _bf16(jnp.concatenate(outs, axis=0).T)


def _mla_attention(q_t, k_cat, v_t, batch, seq, tq=512):
    T = k_cat.shape[0]
    n_q = seq // tq
    n_pairs = MLA_HEADS // 2
    n_chunks = seq // MLA_KEY_CHUNK
    assert n_chunks % MLA_STEPS_PER_TRIP == 0
    return pl.pallas_call(
        _mla_attn_kernel,
        out_shape=jax.ShapeDtypeStruct((T, MLA_WIDTH), jnp.bfloat16),
        grid=(batch, n_pairs, n_q),
        in_specs=[
            pl.BlockSpec((1, 2, LANES, tq), lambda b, hp, i: (b, hp, 0, i)),
            pl.BlockSpec((seq, 2 * LANES), lambda b, hp, i: (b, hp)),
            pl.BlockSpec((1, 1, n_chunks, 2 * MLA_VROWS, MLA_KEY_CHUNK), lambda b, hp, i: (b, hp, 0, 0, 0)),
        ],
        out_specs=pl.BlockSpec((tq, LANES), lambda b, hp, i: (b * n_q + i, hp)),
        scratch_shapes=[
            pltpu.VMEM((2, 1, tq), jnp.float32),
            pltpu.VMEM((2, MLA_VROWS, tq), jnp.float32),
            pltpu.VMEM((2, 2, MLA_KEY_CHUNK, tq), jnp.float32),
        ],
        compiler_params=_cparams(("arbitrary", "arbitrary", "arbitrary")),
        name="mla_attn",
    )(q_t, k_cat, v_t)


def _out_proj_kernel(ona_ref, omla_ref, gate_ref, x_ref, mod_ref, gpost_ref, gpre_ref, wna_ref,
                     wmla_ref, wout_ref, wr_ref, br_ref, x1_ref, h2_ref, lg_ref, *, d_model):
    D = d_model
    tm = x_ref.shape[0]
    gate_a = mod_ref[0, :, 2 * D:3 * D]
    shift_f = mod_ref[0, :, 3 * D:4 * D]
    scale_f = mod_ref[0, :, 4 * D:5 * D]
    halves = [slice(0, tm // 2), slice(tm // 2, tm)]
    merged = []
    for rs in halves:
        up_na = _dot(ona_ref[rs, :], wna_ref[...])
        up_mla = _dot(omla_ref[rs, :], wmla_ref[...])
        merged.append(_bf16(_f32(gate_ref[rs, 0:D]) * up_na + _f32(gate_ref[rs, D:2 * D]) * up_mla))
    ys = [_dot(m, wout_ref[...]) for m in merged]
    h2s = []
    for rs, y in zip(halves, ys):
        x1 = x_ref[rs, :] + gate_a * _rms(y, gpost_ref[...])
        x1_ref[rs, :] = x1
        h2 = _rms(x1, gpre_ref[...]) * (1.0 + scale_f) + shift_f
        _store_row_tiles(h2_ref, h2, base=rs.start * SUBLANES)
        h2s.append(_bf16(h2))
    for rs, hb in zip(halves, h2s):
        lg_ref[rs, :] = _dot(hb, wr_ref[...]) + br_ref[...]


def _out_proj(o_na, o_mla, proj, x2, mod3, g_post, g_pre, w_na, w_mla, w_out, w_r, b_r, seq, tm=512):
    T, D = x2.shape
    full = lambda a: pl.BlockSpec(a.shape, lambda i: (0,) * a.ndim)
    row = lambda w: pl.BlockSpec((tm, w), lambda i: (i, 0))
    kern = functools.partial(_out_proj_kernel, d_model=D)
    g_post = g_post.reshape(1, D)
    g_pre = g_pre.reshape(1, D)
    return pl.pallas_call(
        kern,
        out_shape=(
            jax.ShapeDtypeStruct((T, D), jnp.float32),
            jax.ShapeDtypeStruct((T * (D // LANES), LANES), jnp.float32),
            jax.ShapeDtypeStruct((T, LANES), jnp.float32),
        ),
        grid=(T // tm,),
        in_specs=[
            row(NA_WIDTH), row(MLA_WIDTH),
            pl.BlockSpec((tm, 2 * D), lambda i: (i, COL_GATES // (2 * D))),
            row(D),
            pl.BlockSpec((1, 1, mod3.shape[-1]), lambda i: ((i * tm) // seq, 0, 0)),
            full(g_post), full(g_pre), full(w_na), full(w_mla), full(w_out), full(w_r), full(b_r),
        ],
        out_specs=(row(D), pl.BlockSpec((tm * (D // LANES), LANES), lambda i: (i, 0)), row(LANES)),
        compiler_params=_cparams(("arbitrary",)),
        name="out_proj",
    )(o_na, o_mla, proj, x2, mod3, g_post, g_pre, w_na, w_mla, w_out, w_r, b_r)


def _route_kernel(lg_ref, r_ref, cnt_ref, carry_scr, *, sub):
    @pl.when(pl.program_id(0) == 0)
    def _():
        carry_scr[...] = jnp.zeros(carry_scr.shape, jnp.float32)

    tr = lg_ref.shape[0]
    lane = lax.broadcasted_iota(jnp.int32, (sub, LANES), 1).astype(jnp.float32)
    ri = lax.broadcasted_iota(jnp.int32, (sub, sub), 0)
    ci = lax.broadcasted_iota(jnp.int32, (sub, sub), 1)
    tri = jnp.where(ri >= ci, 1.0, 0.0).astype(jnp.bfloat16)
    for s0 in range(0, tr, sub):
        work = lg_ref[s0:s0 + sub, :]
        sels, vals, idxs = [], [], []
        for _k in range(TOP_K):
            mk = jnp.max(work, axis=-1, keepdims=True)
            ik = jnp.min(jnp.where(work == mk, lane, float(LANES)), axis=-1, keepdims=True)
            sk = lane == ik
            work = jnp.where(sk, -jnp.inf, work)
            sels.append(sk)
            vals.append(mk)
            idxs.append(ik)
        es = [jnp.exp(v - vals[0]) for v in vals]
        denom = es[0] + es[1] + es[2] + es[3]
        onehot = jnp.zeros((sub, LANES), jnp.float32)
        for sk in sels:
            onehot = jnp.where(sk, 1.0, onehot)
        prefix = _dot(tri, _bf16(onehot))
        carry = carry_scr[...]
        rank_mat = carry + prefix - 1.0
        res = jnp.zeros((sub, LANES), jnp.float32)
        for kk in range(TOP_K):
            rank_k = jnp.sum(jnp.where(sels[kk], rank_mat, 0.0), axis=-1, keepdims=True)
            res = jnp.where(lane == kk, idxs[kk], res)
            res = jnp.where(lane == TOP_K + kk, rank_k, res)
            res = jnp.where(lane == 2 * TOP_K + kk, es[kk] / denom, res)
        r_ref[s0:s0 + sub, :] = res
        carry_scr[...] = carry + jnp.sum(onehot, axis=0, keepdims=True)
    cnt_ref[...] = carry_scr[...]


def _route(logits, tr=2048, sub=256):
    T = logits.shape[0]
    kern = functools.partial(_route_kernel, sub=sub)
    return pl.pallas_call(
        kern,
        out_shape=(
            jax.ShapeDtypeStruct((T, LANES), jnp.float32),
            jax.ShapeDtypeStruct((1, LANES), jnp.float32),
        ),
        grid=(T // tr,),
        in_specs=[pl.BlockSpec((tr, LANES), lambda i: (i, 0))],
        out_specs=(
            pl.BlockSpec((tr, LANES), lambda i: (i, 0)),
            pl.BlockSpec((1, LANES), lambda i: (0, 0)),
        ),
        scratch_shapes=[pltpu.VMEM((1, LANES), jnp.float32)],
        compiler_params=_cparams(("arbitrary",)),
        name="route",
    )(logits)


def _dispatch_kernel(zero_blk_ref, dest_ref, h_ref, row0_ref, xs_ref, row_ref, zbuf, sem, fill_sem,
                     *, n_tokens):
    i = pl.program_id(0)
    ts = h_ref.shape[0] // SUBLANES
    blk_rows = MOE_BLOCK * SUBLANES

    @pl.when(i == 0)
    def _():
        fill_row = pltpu.make_async_copy(row0_ref, row_ref, fill_sem.at[0])
        fill_row.start()
        zbuf[...] = jnp.zeros(zbuf.shape, zbuf.dtype)
        n_zero = zero_blk_ref.shape[0]

        def zero_copy(z):
            r0 = pl.multiple_of(zero_blk_ref[z] * blk_rows, blk_rows)
            return pltpu.make_async_copy(zbuf, xs_ref.at[pl.ds(r0, blk_rows), :], fill_sem.at[1])

        for z in range(n_zero):
            pl.when(zero_blk_ref[z] >= 0)(lambda z=z: zero_copy(z).start())
        for z in range(n_zero):
            pl.when(zero_blk_ref[z] >= 0)(lambda z=z: zero_copy(z).wait())
        fill_row.wait()

    def row_copy(j, d):
        dst = pl.ds(pl.multiple_of(d * SUBLANES, SUBLANES), SUBLANES)
        return pltpu.make_async_copy(h_ref.at[pl.ds(j * SUBLANES, SUBLANES), :], xs_ref.at[dst, :], sem)

    t0 = i * ts
    for j in range(ts):
        for kk in range(TOP_K):
            d = dest_ref[j * TOP_K + kk]
            row_copy(j, d).start(priority=kk % 2)
            row_ref[d] = (kk * n_tokens + j) + t0
    for _j in range(ts * TOP_K):
        row_copy(0, 0).wait()


def _dispatch(h2_tiles, dest, zero_blk, cap, ts=256):
    T = h2_tiles.shape[0] // SUBLANES
    kern = functools.partial(_dispatch_kernel, n_tokens=T)
    grid_spec = pltpu.PrefetchScalarGridSpec(
        num_scalar_prefetch=1,
        grid=(T // ts,),
        in_specs=[
            pl.BlockSpec((ts * TOP_K,), lambda i, zb: (i,), memory_space=pltpu.SMEM),
            pl.BlockSpec((ts * SUBLANES, LANES), lambda i, zb: (i, 0)),
            pl.BlockSpec(memory_space=pl.ANY),
        ],
        out_specs=(pl.BlockSpec(memory_space=pl.ANY), pl.BlockSpec(memory_space=pltpu.SMEM)),
        scratch_shapes=[
            pltpu.VMEM((MOE_BLOCK * SUBLANES, LANES), h2_tiles.dtype),
            pltpu.SemaphoreType.DMA(()),
            pltpu.SemaphoreType.DMA((2,)),
        ],
    )
    return pl.pallas_call(
        kern,
        out_shape=(jax.ShapeDtypeStruct((cap * SUBLANES, LANES), h2_tiles.dtype),
                   jax.ShapeDtypeStruct((cap,), jnp.int32)),
        grid_spec=grid_spec,
        compiler_params=_cparams(("arbitrary",)),
        name="dispatch",
    )(zero_blk, dest, h2_tiles, jnp.full((cap,), -1, jnp.int32))


def _moe_kernel(blk_e_ref, nvalid_ref, row_prev_ref, row_ref, x_ref, w1_ref, b1_ref, w2_ref, b2_ref,
                out_ref, w1p_scr, w2b_scr, ybuf0, ybuf1, ssem, *, n_items):
    i = pl.program_id(0)
    nvalid = nvalid_ref[0]
    d_ff = w2_ref.shape[1]
    n_groups = (2 * d_ff) // (2 * LANES)

    def tile(r):
        return pl.ds(pl.multiple_of(r * SUBLANES, SUBLANES), SUBLANES)

    ybufs = (ybuf0, ybuf1)

    def scatter_copy(j, r, slot):
        return pltpu.make_async_copy(ybufs[slot].at[pl.ds(j * SUBLANES, SUBLANES), :], out_ref.at[tile(r), :],
                                     ssem.at[slot])

    def wait_scatters(slot):
        for _j in range(MOE_BLOCK):
            scatter_copy(0, 0, slot).wait()

    def scatter_rows(rows_ref, slot):
        spare0 = n_items + slot * MOE_BLOCK
        for j in range(MOE_BLOCK):
            r = rows_ref[0, 0, j]
            scatter_copy(j, jnp.where(r >= 0, r, spare0 + j), slot).start(priority=j % 2)

    @pl.when(i == 0)
    def _():
        ybuf0[...] = jnp.zeros(ybuf0.shape, ybuf0.dtype)
        ybuf1[...] = jnp.zeros(ybuf1.shape, ybuf1.dtype)
        for j in range(MOE_BLOCK):
            scatter_copy(j, n_items + j, 0).start(priority=j % 2)

    @pl.when(i < nvalid)
    def _():
        e = blk_e_ref[i]
        e_prev = blk_e_ref[jnp.maximum(i - 1, 0)]

        @pl.when((i == 0) | (e != e_prev))
        def _():
            r = lax.broadcasted_iota(jnp.int32, (2 * LANES, 2 * LANES), 0)
            c = lax.broadcasted_iota(jnp.int32, (2 * LANES, 2 * LANES), 1)
            src = jnp.where(c < LANES, 2 * c, 2 * (c - LANES) + 1)
            perm = jnp.where(r == src, 1.0, 0.0).astype(jnp.bfloat16)
            for gI in range(n_groups):
                sl = slice(gI * 2 * LANES, (gI + 1) * 2 * LANES)
                w1p_scr[:, sl] = _bf16(_dot(_bf16(w1_ref[0, :, sl]), perm))
            w2b_scr[...] = _bf16(w2_ref[0])

        def block_step(slot):
            wait_scatters(slot)
            scatter_rows(row_prev_ref, 1 - slot)

            xb = _bf16(_load_row_tiles(x_ref, MOE_BLOCK))
            hcat = _dot(xb, w1p_scr[...]) + b1_ref[0]
            acts = []
            for gI in range(n_groups):
                glu = jnp.minimum(hcat[:, gI * 2 * LANES: gI * 2 * LANES + LANES], SWIGLU_LIMIT)
                lin = jnp.clip(hcat[:, gI * 2 * LANES + LANES: (gI + 1) * 2 * LANES],
                               -SWIGLU_LIMIT, SWIGLU_LIMIT)
                acts.append(_bf16(glu * jax.nn.sigmoid(SWIGLU_ALPHA * glu) * (lin + 1.0)))
            act = jnp.concatenate(acts, axis=-1)
            _store_row_tiles(ybufs[slot], _dot(act, w2b_scr[...]) + b2_ref[0])

            @pl.when(i == nvalid - 1)
            def _():
                scatter_rows(row_ref, slot)
                wait_scatters(1 - slot)
                wait_scatters(slot)

        parity = lax.rem(i, 2)
        for slot in range(2):
            pl.when(parity == slot)(functools.partial(block_step, slot))


def _moe_ffn(xs, slot_row, n_items, blk_e, nvalid, w1, b1p, w2, b2):
    D = w1.shape[1]
    assert D == SUBLANES * LANES
    E, _, F2 = w1.shape
    F = w2.shape[1]
    nblk = slot_row.shape[0]

    def cur(i, be, nv):
        return (jnp.minimum(i, nv[0] - 1), 0, 0)

    def prv(i, be, nv):
        return (jnp.where(i == 0, nblk, jnp.maximum(jnp.minimum(i, nv[0] - 1) - 1, 0)), 0, 0)

    slot_row = jnp.concatenate([slot_row, jnp.full((1, 1, MOE_BLOCK), -1, slot_row.dtype)], axis=0)
    slot_spec = lambda f: pl.BlockSpec((1, 1, MOE_BLOCK), f, memory_space=pltpu.SMEM)
    grid_spec = pltpu.PrefetchScalarGridSpec(
        num_scalar_prefetch=2,
        grid=(nblk,),
        in_specs=[
            slot_spec(prv), slot_spec(cur),
            pl.BlockSpec((MOE_BLOCK * SUBLANES, LANES), lambda i, be, nv: (jnp.minimum(i, nv[0] - 1), 0)),
            pl.BlockSpec((1, D, F2), lambda i, be, nv: (be[i], 0, 0)),
            pl.BlockSpec((1, 1, F2), lambda i, be, nv: (be[i], 0, 0)),
            pl.BlockSpec((1, F, D), lambda i, be, nv: (be[i], 0, 0)),
            pl.BlockSpec((1, 1, D), lambda i, be, nv: (be[i], 0, 0)),
        ],
        out_specs=pl.BlockSpec(memory_space=pl.ANY),
        scratch_shapes=[
            pltpu.VMEM((D, F2), jnp.bfloat16),
            pltpu.VMEM((F, D), jnp.bfloat16),
            pltpu.VMEM((MOE_BLOCK * SUBLANES, LANES), jnp.float32),
            pltpu.VMEM((MOE_BLOCK * SUBLANES, LANES), jnp.float32),
            pltpu.SemaphoreType.DMA((2,)),
        ],
    )
    kern = functools.partial(_moe_kernel, n_items=n_items)
    return pl.pallas_call(
        kern,
        out_shape=jax.ShapeDtypeStruct(((n_items + 2 * MOE_BLOCK) * SUBLANES, LANES), jnp.float32),
        grid_spec=grid_spec,
        compiler_params=_cparams(("arbitrary",)),
        name="moe_ffn",
    )(blk_e, nvalid, slot_row, slot_row, xs, w1, b1p, w2, b2)


def _combine_kernel(y0_ref, y1_ref, y2_ref, y3_ref, r_ref, x1_ref, mod_ref, g_ref, o_ref, *, d_model):
    D = d_model
    tc = x1_ref.shape[0]
    r = r_ref[...]
    moe = None
    for kk, y_ref in enumerate((y0_ref, y1_ref, y2_ref, y3_ref)):
        term = r[:, 2 * TOP_K + kk: 2 * TOP_K + kk + 1] * _load_row_tiles(y_ref, tc)
        moe = term if moe is None else moe + term
    gate_f = mod_ref[0, :, 5 * D:6 * D]
    o_ref[...] = x1_ref[...] + gate_f * _rms(moe, g_ref[...])


def _combine(y_k, route, x1, mod3, g_post, seq, tc=512):
    T, D = x1.shape
    n_t = T // tc
    kern = functools.partial(_combine_kernel, d_model=D)
    y_spec = lambda kk: pl.BlockSpec((tc * SUBLANES, LANES), lambda i: (kk * n_t + i, 0))
    return pl.pallas_call(
        kern,
        out_shape=jax.ShapeDtypeStruct((T, D), jnp.float32),
        grid=(n_t,),
        in_specs=[
            y_spec(0), y_spec(1), y_spec(2), y_spec(3),
            pl.BlockSpec((tc, LANES), lambda i: (i, 0)),
            pl.BlockSpec((tc, D), lambda i: (i, 0)),
            pl.BlockSpec((1, 1, mod3.shape[-1]), lambda i: ((i * tc) // seq, 0, 0)),
            pl.BlockSpec((1, D), lambda i: (0, 0)),
        ],
        out_specs=pl.BlockSpec((tc, D), lambda i: (i, 0)),
        compiler_params=_cparams(("arbitrary",)),
        name="combine",
    )(y_k, y_k, y_k, y_k, route, x1, mod3, g_post.reshape(1, D))


def _rope_swap_cols(w):
    nf = MLA_ROPE // 4
    return jnp.concatenate([-w[..., nf:2 * nf], w[..., 0:nf], -w[..., 3 * nf:4 * nf], w[..., 2 * nf:3 * nf]],
                           axis=-1)


def _rope_tables(seq):
    n_rows = seq // GRID_W
    nf = MLA_ROPE // 4
    inv = ROPE_THETA ** (-jnp.arange(nf, dtype=jnp.float32) / nf)
    ar = jnp.arange(n_rows, dtype=jnp.float32)[:, None] * inv
    ac = jnp.arange(GRID_W, dtype=jnp.float32)[:, None] * inv
    cr, sr, cc, sc = jnp.cos(ar), jnp.sin(ar), jnp.cos(ac), jnp.sin(ac)

    def lanes(parts, n):
        z = lambda w: jnp.zeros((n, w), jnp.float32)
        return jnp.concatenate([z(p) if isinstance(p, int) else p for p in parts], axis=-1)

    pad = LANES - MLA_NOPE - MLA_ROPE
    row_tab = lanes([MLA_NOPE, cr, cr, 2 * nf, pad, MLA_NOPE, sr, sr, 2 * nf, pad,
                     cr, cr, 2 * nf, sr, sr, 2 * nf, MLA_NOPE], n_rows)
    col_tab = lanes([jnp.ones((GRID_W, MLA_NOPE), jnp.float32), 2 * nf, cc, cc, pad, MLA_NOPE, 2 * nf, sc, sc, pad,
                     2 * nf, cc, cc, 2 * nf, sc, sc, MLA_NOPE], GRID_W)
    return (row_tab[:, None, :] + col_tab[None, :, :]).reshape(seq, 3 * LANES)


def kernel(x, c, w_ada, b_ada, g_attn_pre, g_attn_post, w_in, b_gate, na_rpb, q_norm_g, kv_norm_g,
           w_uq, w_ukv, w_na_up, w_mla_up, w_out, g_ffn_pre, g_ffn_post, w_router, b_router,
           w1, b1, w2, b2):
    B, S, D = x.shape
    T = B * S
    depth = w_ada.shape[0]
    E = w_router.shape[-1]
    n_rows = S // GRID_W
    assert S % (NA_GROUP_ROWS * GRID_W) == 0 and n_rows >= NA_KEY_ROWS
    assert E <= LANES and T % 2048 == 0

    bf = jnp.bfloat16
    rope_tab = _rope_tables(S)
    pm_np = np.zeros((LANES, LANES), np.float32)
    for cidx in range(MLA_ROPE):
        pm_np[cidx, MLA_NOPE + cidx] = 1.0
        pm_np[MLA_ROPE + cidx, MLA_NOPE + cidx] = 1.0
    pm = jnp.asarray(pm_np, bf)
    qscale = float((MLA_NOPE + MLA_ROPE) ** -0.5 * math.log2(math.e))

    col_na = 2 * D
    col_cq = col_na + 3 * NA_WIDTH
    col_ckv = col_cq + Q_LORA_PAD
    col_kr = col_ckv + MLA_KV_LORA
    assert col_cq % Q_LORA_PAD == 0 and col_ckv % MLA_KV_LORA == 0

    x2 = x.reshape(T, D)
    for l in range(depth):
        mod = _ada_mod(c, w_ada[l], b_ada[l])
        mod3 = mod.reshape(B, 1, 6 * D)

        o_na, o_cq, o_ckv, o_kr, o_g = np.cumsum([0, 3 * NA_WIDTH, MLA_Q_LORA, MLA_KV_LORA, MLA_ROPE]).tolist()
        wi = w_in[l]
        w_kr = wi[:, o_kr:o_g]
        w_ext = jnp.concatenate([
            wi[:, o_g:o_g + 2 * D],
            wi[:, o_na:o_cq],
            wi[:, o_cq:o_ckv], jnp.zeros((D, Q_LORA_PAD - MLA_Q_LORA), wi.dtype),
            wi[:, o_ckv:o_kr],
            w_kr, _rope_swap_cols(w_kr), jnp.zeros((D, LANES - 2 * MLA_ROPE), wi.dtype),
        ], axis=1).astype(bf)
        proj, na_qt, na_vt = _in_proj(x2, mod3, g_attn_pre[l], w_ext, b_gate[l], S,
                                      col_na, col_na + 2 * NA_WIDTH)

        o_na_tok = _na_attention(proj, na_qt, na_vt, _na_pair_tables(na_rpb[l]), B, S, col_na + NA_WIDTH)

        wq = w_uq[l]
        zq = jnp.zeros((MLA_Q_LORA, MLA_HEADS, LANES - MLA_NOPE - MLA_ROPE), wq.dtype)
        wqm = jnp.concatenate([wq, zq], axis=-1).reshape(MLA_Q_LORA, MLA_HEADS * LANES)
        wqs = jnp.concatenate([jnp.zeros((MLA_Q_LORA, MLA_HEADS, MLA_NOPE), wq.dtype),
                               _rope_swap_cols(wq[..., MLA_NOPE:]), zq], axis=-1
                              ).reshape(MLA_Q_LORA, MLA_HEADS * LANES)
        rpad = ((0, Q_LORA_PAD - MLA_Q_LORA), (0, 0))
        wqm = jnp.pad(wqm, rpad).astype(bf)
        wqs = jnp.pad(wqs, rpad).astype(bf)
        wkv = w_ukv[l]
        wk = jnp.concatenate([wkv[..., :MLA_NOPE], jnp.zeros((MLA_KV_LORA, MLA_HEADS, LANES - MLA_NOPE), wkv.dtype)],
                             axis=-1).reshape(MLA_KV_LORA, MLA_HEADS * LANES).astype(bf)
        wv = wkv[..., MLA_NOPE:].reshape(MLA_KV_LORA, MLA_WIDTH).astype(bf)
        gq = jnp.pad(q_norm_g[l], (0, Q_LORA_PAD - MLA_Q_LORA)).reshape(1, Q_LORA_PAD)
        gkv = kv_norm_g[l].reshape(1, MLA_KV_LORA)
        q_cat, k_cat, v_mla = _mla_prep(proj, col_cq, col_ckv, col_kr, gq, gkv, wqm, wqs, wk, wv, pm,
                                        rope_tab, S, qscale)
        o_mla_tok = _mla_attention(q_cat, k_cat, v_mla, B, S)

        w_r = jnp.pad(w_router[l], ((0, 0), (0, LANES - E))).astype(bf)
        b_r = jnp.concatenate([b_router[l], jnp.full((LANES - E,), NEG_BIG, jnp.float32)]).reshape(1, LANES)
        x1, h2, logits = _out_proj(o_na_tok, o_mla_tok, proj, x2, mod3, g_attn_post[l], g_ffn_pre[l],
                                   w_na_up[l].astype(bf), w_mla_up[l].astype(bf), w_out[l].astype(bf),
                                   w_r, b_r, S)

        route, counts = _route(logits)
        e_idx = route[:, 0:TOP_K].astype(jnp.int32)
        rank = route[:, TOP_K:2 * TOP_K].astype(jnp.int32)
        cnt = counts[0, :E].astype(jnp.int32)
        padded = ((cnt + MOE_BLOCK - 1) // MOE_BLOCK) * MOE_BLOCK
        pend = jnp.cumsum(padded)
        pstart = pend - padded
        n_items = T * TOP_K
        nblk = -(-n_items // MOE_BLOCK) + E
        cap = nblk * MOE_BLOCK
        dest = (rank + jnp.sum(jnp.where(e_idx[..., None] == jnp.arange(E, dtype=jnp.int32), pstart, 0), axis=-1)
                ).reshape(n_items)
        blk_off = jnp.arange(nblk, dtype=jnp.int32) * MOE_BLOCK
        blk_e = jnp.minimum(jnp.sum((pend[None, :] <= blk_off[:, None]).astype(jnp.int32), axis=-1), E - 1)
        nvalid = (pend[-1:] // MOE_BLOCK).astype(jnp.int32)

        last_blk = jnp.where(padded > 0, pend // MOE_BLOCK - 1, -1)
        tail_blk = nvalid[0] + jnp.arange(nblk - n_items // MOE_BLOCK, dtype=jnp.int32)
        zero_blk = jnp.concatenate([last_blk, jnp.where(tail_blk < nblk, tail_blk, -1)]).astype(jnp.int32)
        xs, slot_row = _dispatch(h2, dest, zero_blk, cap)
        slot_row = slot_row.reshape(nblk, 1, MOE_BLOCK)
        F2 = w1.shape[-1]
        b1p = b1[l].reshape(E, F2 // (2 * LANES), LANES, 2).transpose(0, 1, 3, 2).reshape(E, 1, F2)
        y_k = _moe_ffn(xs, slot_row, n_items, blk_e, nvalid, w1[l], b1p, w2[l], b2[l].reshape(E, 1, D))
        x2 = _combine(y_k, route, x1, mod3, g_ffn_post[l], S)
    return x2.reshape(B, S, D)
```

```python
import functools
import math

import numpy as np
import jax
import jax.numpy as jnp
from jax import lax
from jax.experimental import pallas as pl
from jax.experimental.pallas import tpu as pltpu

GRID_W = 64
NA_HEADS = 8
NA_HEAD_DIM = 64
NA_WIN_ROWS = 8
NA_WIN_COLS = 16
NA_WIDTH = NA_HEADS * NA_HEAD_DIM
MLA_HEADS = 8
MLA_Q_LORA = 384
MLA_KV_LORA = 256
MLA_NOPE = 64
MLA_ROPE = 32
MLA_V = 64
MLA_WIDTH = MLA_HEADS * MLA_V
MLA_KEY_CHUNK = 256
MLA_VROWS = MLA_V + 16
MLA_STEPS_PER_TRIP = 16
ROPE_THETA = 100.0
TOP_K = 4
SWIGLU_ALPHA = 1.702
SWIGLU_LIMIT = 7.0
MOE_BLOCK = 256
RMS_EPS = 1e-6
NEG_BIG = -1e30

LANES = 128
VMEM_LIMIT_BYTES = 56 * 1024 * 1024

Q_LORA_PAD = 512
COL_GATES = 0
NA_GROUP_ROWS = 8
NA_KEY_ROWS = 16


def _f32(x):
    return x.astype(jnp.float32)


def _bf16(x):
    return x.astype(jnp.bfloat16)


def _dot(a, b):
    return jnp.dot(a, b, preferred_element_type=jnp.float32)


def _rms(x, g, n=None):
    n = x.shape[-1] if n is None else n
    ms = jnp.sum(x * x, axis=-1, keepdims=True) * (1.0 / n)
    return x * lax.rsqrt(ms + RMS_EPS) * g


SUBLANES = 8


def _store_row_tiles(ref, x, base=0):
    n = x.shape[0]
    for c in range(x.shape[1] // LANES):
        ref[pl.ds(base + c, n, stride=SUBLANES), :] = x[:, c * LANES:(c + 1) * LANES]


def _load_row_tiles(ref, n, base=0):
    chunks = [ref[pl.ds(base + c, n, stride=SUBLANES), :] for c in range(SUBLANES)]
    return jnp.concatenate(chunks, axis=-1)


def _cparams(sem):
    return pltpu.CompilerParams(dimension_semantics=sem, vmem_limit_bytes=VMEM_LIMIT_BYTES)


def _ada_kernel(c_ref, w_ref, b_ref, o_ref):
    c = c_ref[...]
    sc = c * jax.nn.sigmoid(c)
    o_ref[...] = _dot(_bf16(sc), _bf16(w_ref[...])) + b_ref[...]


def _ada_mod(c, w_ada, b_ada):
    B, D = c.shape
    n_out = w_ada.shape[1]
    return pl.pallas_call(
        _ada_kernel,
        out_shape=jax.ShapeDtypeStruct((B, n_out), jnp.float32),
        grid=(n_out // D,),
        in_specs=[
            pl.BlockSpec((B, D), lambda j: (0, 0)),
            pl.BlockSpec((D, D), lambda j: (0, j)),
            pl.BlockSpec((1, D), lambda j: (0, j)),
        ],
        out_specs=pl.BlockSpec((B, D), lambda j: (0, j)),
        compiler_params=_cparams(("arbitrary",)),
        name="ada_mod",
    )(c, w_ada, b_ada.reshape(1, n_out))


def _in_proj_kernel(x_ref, mod_ref, g_ref, w_ref, bg_ref, o_ref, qt_ref, vt_ref, *, d_model, n_gate,
                    chunk, col_q, col_v):
    x = x_ref[...]
    tm = x.shape[0]
    shift = mod_ref[0, :, 0:d_model]
    scale = mod_ref[0, :, d_model:2 * d_model]
    h = _rms(x, g_ref[...]) * (1.0 + scale) + shift
    hb = _bf16(h)
    n_total = w_ref.shape[1]
    n_pairs = NA_HEADS // 2
    for c0 in range(0, n_total, chunk):
        c1 = min(c0 + chunk, n_total)
        acc = _dot(hb, w_ref[:, c0:c1])
        if c0 < n_gate:
            acc = jax.nn.sigmoid(acc + bg_ref[:, c0:c1])
        o_ref[:, c0:c1] = _bf16(acc)
        if c0 == col_q:
            for hp in range(n_pairs):
                qt_ref[0, hp] = _bf16((acc[:, hp * LANES:(hp + 1) * LANES] * NA_HEAD_DIM ** -0.5).T)
        if c0 == col_v:
            for hp in range(n_pairs):
                vt = _bf16(acc[:, hp * LANES:(hp + 1) * LANES].T)
                for cc in range(tm // LANES):
                    vt_ref[0, hp, cc] = vt[:, cc * LANES:(cc + 1) * LANES]


def _in_proj(x2, mod3, g_pre, w_ext, b_gate, seq, col_q, col_v, tm=512):
    T, D = x2.shape
    batch = T // seq
    n_s = seq // tm
    n_total = w_ext.shape[1]
    n_gate = b_gate.shape[-1]
    n_pairs = NA_HEADS // 2
    chunk = n_pairs * LANES
    assert col_q % chunk == 0 and col_v % chunk == 0
    kern = functools.partial(_in_proj_kernel, d_model=D, n_gate=n_gate, chunk=chunk, col_q=col_q, col_v=col_v)
    return pl.pallas_call(
        kern,
        out_shape=(
            jax.ShapeDtypeStruct((T, n_total), jnp.bfloat16),
            jax.ShapeDtypeStruct((batch, n_pairs, LANES, seq), jnp.bfloat16),
            jax.ShapeDtypeStruct((batch, n_pairs, seq // LANES, LANES, LANES), jnp.bfloat16),
        ),
        grid=(T // tm,),
        in_specs=[
            pl.BlockSpec((tm, D), lambda i: (i, 0)),
            pl.BlockSpec((1, 1, mod3.shape[-1]), lambda i: ((i * tm) // seq, 0, 0)),
            pl.BlockSpec((1, D), lambda i: (0, 0)),
            pl.BlockSpec((D, n_total), lambda i: (0, 0)),
            pl.BlockSpec((1, n_gate), lambda i: (0, 0)),
        ],
        out_specs=(
            pl.BlockSpec((tm, n_total), lambda i: (i, 0)),
            pl.BlockSpec((1, n_pairs, LANES, tm), lambda i: (i // n_s, 0, 0, i % n_s)),
            pl.BlockSpec((1, n_pairs, tm // LANES, LANES, LANES), lambda i: (i // n_s, 0, i % n_s, 0, 0)),
        ),
        compiler_params=_cparams(("arbitrary",)),
        name="in_proj",
    )(x2, mod3, g_pre.reshape(1, D), w_ext, b_gate.reshape(1, n_gate))


def _na_class(c):
    half = NA_WIN_ROWS // 2
    if c == 0:
        return (lambda i: max(i - half, 0)), NA_WIN_ROWS - 1
    if c == 1:
        return (lambda i: i), NA_WIN_ROWS - 1 - half
    return (lambda i: min(i + half, NA_KEY_ROWS - NA_WIN_ROWS)), -1


NA_PAIR_KEY_ROWS = 10
NA_GROUPS_PER_STEP = 2


def _na_kernel(qt_ref, k_ref, vt_ref, pt_ref, o_ref, *, n_rows):
    n_groups = n_rows // NA_GROUP_ROWS
    tq = NA_GROUP_ROWS * GRID_W
    nkeys = NA_PAIR_KEY_ROWS * GRID_W
    row = lax.broadcasted_iota(jnp.int32, (LANES, LANES), 0)
    lane = lax.broadcasted_iota(jnp.int32, (GRID_W, LANES), 1)
    neg_blk = jnp.full((GRID_W, LANES), NEG_BIG, jnp.float32)

    def body(c, sub, kb_rows):
        j0, off = _na_class(c)
        n_pair = NA_GROUP_ROWS // 2
        starts = [min(j0(2 * ip) - j0(2 * ip) % 2, NA_KEY_ROWS - NA_PAIR_KEY_ROWS) for ip in range(n_pair)]
        scores = []
        for ip in range(n_pair):
            kstart = pl.multiple_of((kb_rows + starts[ip]) * GRID_W, 2 * GRID_W)
            kwin = k_ref[pl.ds(kstart, nkeys), :]
            q0 = sub * tq + ip * LANES
            qp = qt_ref[0, 0, :, q0:q0 + LANES]
            zero = jnp.zeros_like(qp)
            wq = jnp.concatenate([jnp.where(row < NA_HEAD_DIM, qp, zero),
                                  jnp.where(row >= NA_HEAD_DIM, qp, zero)], axis=1)
            scores.append(_dot(kwin, wq))
        outs = []
        for ip in range(n_pair):
            i0 = 2 * ip
            w = starts[ip]
            st = scores[ip]
            blocks = []
            for jw in range(NA_PAIR_KEY_ROWS):
                j = w + jw
                v0 = j0(i0) <= j < j0(i0) + NA_WIN_ROWS
                v1 = j0(i0 + 1) <= j < j0(i0 + 1) + NA_WIN_ROWS
                dr0 = j - i0 + off
                halves = []
                for hl in range(2):
                    if not (v0 or v1):
                        halves.append(neg_blk)
                        continue
                    blk = st[jw * GRID_W:(jw + 1) * GRID_W, hl * LANES:(hl + 1) * LANES] + pt_ref[hl, dr0]
                    if not (v0 and v1):
                        keep = (lane < GRID_W) if v0 else (lane >= GRID_W)
                        blk = jnp.where(keep, blk, NEG_BIG)
                    halves.append(blk)
                blocks.append(jnp.concatenate(halves, axis=1))
            s = jnp.concatenate(blocks, axis=0)
            m = jnp.max(s, axis=0, keepdims=True)
            p = jnp.exp(s - m)
            l = jnp.sum(p, axis=0, keepdims=True)
            c0 = (kb_rows + w) // 2
            vwin = jnp.concatenate([vt_ref[0, 0, c0 + u] for u in range(NA_PAIR_KEY_ROWS // 2)], axis=1)
            on = _dot(vwin, _bf16(p)) / l
            outs.append(jnp.where(row < NA_HEAD_DIM, on[:, 0:LANES], on[:, LANES:2 * LANES]))
        o_ref[sub * tq:(sub + 1) * tq, :] = _bf16(jnp.concatenate(outs, axis=1).T)

    for sub in range(NA_GROUPS_PER_STEP):
        g = pl.program_id(2) * NA_GROUPS_PER_STEP + sub
        kb_rows = jnp.clip(g * NA_GROUP_ROWS - NA_WIN_ROWS // 2, 0, n_rows - NA_KEY_ROWS)
        cls = jnp.where(g == 0, 0, jnp.where(g == n_groups - 1, 2, 1))
        for c in range(3):
            pl.when(cls == c)(functools.partial(body, c, sub, kb_rows))


def _na_attention(proj, q_t, v_t, pair_tab, batch, seq, col_k):
    T = proj.shape[0]
    n_rows = seq // GRID_W
    n_groups = n_rows // NA_GROUP_ROWS
    assert n_groups % NA_GROUPS_PER_STEP == 0
    n_steps = n_groups // NA_GROUPS_PER_STEP
    tq = NA_GROUPS_PER_STEP * NA_GROUP_ROWS * GRID_W
    n_pairs = NA_HEADS // 2
    kern = functools.partial(_na_kernel, n_rows=n_rows)
    return pl.pallas_call(
        kern,
        out_shape=jax.ShapeDtypeStruct((T, NA_WIDTH), jnp.bfloat16),
        grid=(n_pairs, batch, n_steps),
        in_specs=[
            pl.BlockSpec((1, 1, LANES, tq), lambda hp, b, g: (b, hp, 0, g)),
            pl.BlockSpec((seq, LANES), lambda hp, b, g: (b, col_k // LANES + hp)),
            pl.BlockSpec((1, 1, seq // LANES, LANES, LANES), lambda hp, b, g: (b, hp, 0, 0, 0)),
            pl.BlockSpec((2,) + pair_tab.shape[1:], lambda hp, b, g: (hp, 0, 0, 0)),
        ],
        out_specs=pl.BlockSpec((tq, LANES), lambda hp, b, g: (b * n_steps + g, hp)),
        compiler_params=_cparams(("arbitrary", "arbitrary", "arbitrary")),
        name="na_attn",
    )(q_t, proj, v_t, pair_tab)


def _na_pair_tables(rpb):
    H, n_dr, n_dc = rpb.shape
    kw = NA_WIN_COLS
    W = GRID_W
    ring = 2 * W - 1
    rp = jnp.pad(rpb.astype(jnp.float32), ((0, 0), (0, 0), (W - kw, W - kw)), constant_values=NEG_BIG)
    u = jnp.roll(rp, -(W - 1), axis=-1)
    circ = jnp.tile(u, (1, 1, W))[..., :W * (ring - 1)].reshape(H, n_dr, W, ring - 1)
    toep = circ[..., :W]
    cidx = np.arange(W)
    col_start = np.clip(cidx - kw // 2, 0, W - kw)
    col_in = (cidx[None, :] >= col_start[:, None]) & (cidx[None, :] < col_start[:, None] + kw)
    toep_t = jnp.where(jnp.asarray(col_in.T)[None, None], jnp.swapaxes(toep, 2, 3), NEG_BIG)
    neg = jnp.full((H, 1, W, W), NEG_BIG, jnp.float32)
    ext = jnp.concatenate([neg, toep_t, neg], axis=1)
    return jnp.concatenate([ext[:, 1:], ext[:, :-1]], axis=-1)


def _mla_prep_kernel(cq_ref, ckv_ref, kr_ref, gq_ref, gkv_ref, wqm_ref, wqs_ref, wk_ref, wv_ref,
                     pm_ref, tab_ref, q_out, k_out, v_out, *, qscale):
    cq = _f32(cq_ref[...])
    cqn = _bf16(_rms(cq, gq_ref[...], n=MLA_Q_LORA))
    qm = _dot(cqn, wqm_ref[...])
    qs = _dot(cqn, wqs_ref[...])
    ctab = tab_ref[:, 0:LANES]
    stab = tab_ref[:, LANES:2 * LANES]
    for h in range(MLA_HEADS):
        sl = slice(h * LANES, (h + 1) * LANES)
        qh = (qm[:, sl] * ctab + qs[:, sl] * stab) * qscale
        q_out[0, h] = _bf16(qh.T)

    ckv = _f32(ckv_ref[...])
    ckvn = _bf16(_rms(ckv, gkv_ref[...]))
    kk = _dot(ckvn, wk_ref[...])
    vv = _dot(ckvn, wv_ref[...])
    n_chunk = v_out.shape[2]
    ones = jnp.ones((MLA_VROWS - MLA_V, MLA_KEY_CHUNK), jnp.bfloat16)
    for hp in range(MLA_HEADS // 2):
        vt = _bf16(vv[:, hp * LANES:(hp + 1) * LANES].T)
        for cc in range(n_chunk):
            ks = slice(cc * MLA_KEY_CHUNK, (cc + 1) * MLA_KEY_CHUNK)
            for h in range(2):
                r0 = h * MLA_VROWS
                v_out[0, hp, cc, r0:r0 + MLA_V, :] = vt[h * MLA_V:(h + 1) * MLA_V, ks]
                v_out[0, hp, cc, r0 + MLA_V:r0 + MLA_VROWS, :] = ones
    krr = _f32(kr_ref[...]) * tab_ref[:, 2 * LANES:3 * LANES]
    kplace = _dot(_bf16(krr), pm_ref[...])
    for h in range(MLA_HEADS):
        sl = slice(h * LANES, (h + 1) * LANES)
        k_out[:, sl] = _bf16(kk[:, sl] + kplace)


def _mla_prep(proj, col_cq, col_ckv, col_kr, gq, gkv, wqm, wqs, wk, wv, pm, rope_tab, seq, qscale, tm=512):
    T = proj.shape[0]
    batch = T // seq
    n_s = seq // tm
    n_pairs = MLA_HEADS // 2
    cpt = tm // MLA_KEY_CHUNK
    full = lambda a: pl.BlockSpec(a.shape, lambda i: (0,) * a.ndim)
    tab = pl.BlockSpec((tm, 3 * LANES), lambda i: (i % n_s, 0))
    kern = functools.partial(_mla_prep_kernel, qscale=qscale)
    return pl.pallas_call(
        kern,
        out_shape=(
            jax.ShapeDtypeStruct((batch, MLA_HEADS, LANES, seq), jnp.bfloat16),
            jax.ShapeDtypeStruct((T, MLA_HEADS * LANES), jnp.bfloat16),
            jax.ShapeDtypeStruct((batch, n_pairs, seq // MLA_KEY_CHUNK, 2 * MLA_VROWS, MLA_KEY_CHUNK),
                                 jnp.bfloat16),
        ),
        grid=(T // tm,),
        in_specs=[
            pl.BlockSpec((tm, Q_LORA_PAD), lambda i: (i, col_cq // Q_LORA_PAD)),
            pl.BlockSpec((tm, MLA_KV_LORA), lambda i: (i, col_ckv // MLA_KV_LORA)),
            pl.BlockSpec((tm, LANES), lambda i: (i, col_kr // LANES)),
            full(gq), full(gkv), full(wqm), full(wqs), full(wk), full(wv), full(pm),
            tab,
        ],
        out_specs=(
            pl.BlockSpec((1, MLA_HEADS, LANES, tm), lambda i: (i // n_s, 0, 0, i % n_s)),
            pl.BlockSpec((tm, MLA_HEADS * LANES), lambda i: (i, 0)),
            pl.BlockSpec((1, n_pairs, cpt, 2 * MLA_VROWS, MLA_KEY_CHUNK),
                         lambda i: (i // n_s, 0, i % n_s, 0, 0)),
        ),
        compiler_params=_cparams(("arbitrary",)),
        name="mla_prep",
    )(proj, proj, proj, gq, gkv, wqm, wqs, wk, wv, pm, rope_tab)


def _mla_attn_kernel(qt_ref, k_ref, vt_ref, o_ref, m_scr, acc_scr, st_scr):
    n_chunks = vt_ref.shape[2]
    tk = MLA_KEY_CHUNK
    m_scr[...] = jnp.full(m_scr.shape, -jnp.inf, jnp.float32)
    acc_scr[...] = jnp.zeros(acc_scr.shape, jnp.float32)

    def scores(c, slot):
        k0 = pl.multiple_of(c * tk, tk)
        for h in range(2):
            kc = k_ref[pl.ds(k0, tk), h * LANES:(h + 1) * LANES]
            st_scr[slot, h] = _dot(kc, qt_ref[0, h])

    def step(c, slot, trip_end):
        if trip_end:
            pl.when(c + 1 < n_chunks)(lambda: scores(c + 1, 1 - slot))
        else:
            scores(c + 1, 1 - slot)
        vt = vt_ref[0, 0, c]
        pts, alphas = [], []
        for h in range(2):
            st = st_scr[slot, h]
            m_old = m_scr[h]
            m_new = jnp.maximum(m_old, jnp.max(st, axis=0, keepdims=True))
            alpha = jnp.exp2(m_old - m_new)
            pt = jnp.exp2(st - m_new)
            m_scr[h] = m_new
            pts.append(_bf16(pt))
            alphas.append(alpha)
        for h in range(2):
            acc_scr[h] = alphas[h] * acc_scr[h] + _dot(vt[h * MLA_VROWS:(h + 1) * MLA_VROWS, :], pts[h])

    def body(j, carry):
        for u in range(MLA_STEPS_PER_TRIP):
            step(MLA_STEPS_PER_TRIP * j + u, u % 2, u == MLA_STEPS_PER_TRIP - 1)
        return carry

    scores(0, 0)
    lax.fori_loop(0, n_chunks // MLA_STEPS_PER_TRIP, body, 0)
    outs = []
    for h in range(2):
        acc = acc_scr[h]
        outs.append(acc[0:MLA_V] / acc[MLA_V:MLA_V + 1])
    o_ref[...] = _bf16(jnp.concatenate(outs, axis=0).T)


def _mla_attention(q_t, k_cat, v_t, batch, seq, tq=512):
    T = k_cat.shape[0]
    n_q = seq // tq
    n_pairs = MLA_HEADS // 2
    n_chunks = seq // MLA_KEY_CHUNK
    assert n_chunks % MLA_STEPS_PER_TRIP == 0
    return pl.pallas_call(
        _mla_attn_kernel,
        out_shape=jax.ShapeDtypeStruct((T, MLA_WIDTH), jnp.bfloat16),
        grid=(batch, n_pairs, n_q),
        in_specs=[
            pl.BlockSpec((1, 2, LANES, tq), lambda b, hp, i: (b, hp, 0, i)),
            pl.BlockSpec((seq, 2 * LANES), lambda b, hp, i: (b, hp)),
            pl.BlockSpec((1, 1, n_chunks, 2 * MLA_VROWS, MLA_KEY_CHUNK), lambda b, hp, i: (b, hp, 0, 0, 0)),
        ],
        out_specs=pl.BlockSpec((tq, LANES), lambda b, hp, i: (b * n_q + i, hp)),
        scratch_shapes=[
            pltpu.VMEM((2, 1, tq), jnp.float32),
            pltpu.VMEM((2, MLA_VROWS, tq), jnp.float32),
            pltpu.VMEM((2, 2, MLA_KEY_CHUNK, tq), jnp.float32),
        ],
        compiler_params=_cparams(("arbitrary", "arbitrary", "arbitrary")),
        name="mla_attn",
    )(q_t, k_cat, v_t)


def _out_proj_kernel(ona_ref, omla_ref, gate_ref, x_ref, mod_ref, gpost_ref, gpre_ref, wna_ref,
                     wmla_ref, wout_ref, wr_ref, br_ref, x1_ref, h2_ref, lg_ref, *, d_model):
    D = d_model
    tm = x_ref.shape[0]
    gate_a = mod_ref[0, :, 2 * D:3 * D]
    shift_f = mod_ref[0, :, 3 * D:4 * D]
    scale_f = mod_ref[0, :, 4 * D:5 * D]
    halves = [slice(0, tm // 2), slice(tm // 2, tm)]
    merged = []
    for rs in halves:
        up_na = _dot(ona_ref[rs, :], wna_ref[...])
        up_mla = _dot(omla_ref[rs, :], wmla_ref[...])
        merged.append(_bf16(_f32(gate_ref[rs, 0:D]) * up_na + _f32(gate_ref[rs, D:2 * D]) * up_mla))
    ys = [_dot(m, wout_ref[...]) for m in merged]
    h2s = []
    for rs, y in zip(halves, ys):
        x1 = x_ref[rs, :] + gate_a * _rms(y, gpost_ref[...])
        x1_ref[rs, :] = x1
        h2 = _rms(x1, gpre_ref[...]) * (1.0 + scale_f) + shift_f
        _store_row_tiles(h2_ref, h2, base=rs.start * SUBLANES)
        h2s.append(_bf16(h2))
    for rs, hb in zip(halves, h2s):
        lg_ref[rs, :] = _dot(hb, wr_ref[...]) + br_ref[...]


def _out_proj(o_na, o_mla, proj, x2, mod3, g_post, g_pre, w_na, w_mla, w_out, w_r, b_r, seq, tm=512):
    T, D = x2.shape
    full = lambda a: pl.BlockSpec(a.shape, lambda i: (0,) * a.ndim)
    row = lambda w: pl.BlockSpec((tm, w), lambda i: (i, 0))
    kern = functools.partial(_out_proj_kernel, d_model=D)
    g_post = g_post.reshape(1, D)
    g_pre = g_pre.reshape(1, D)
    return pl.pallas_call(
        kern,
        out_shape=(
            jax.ShapeDtypeStruct((T, D), jnp.float32),
            jax.ShapeDtypeStruct((T * (D // LANES), LANES), jnp.float32),
            jax.ShapeDtypeStruct((T, LANES), jnp.float32),
        ),
        grid=(T // tm,),
        in_specs=[
            row(NA_WIDTH), row(MLA_WIDTH),
            pl.BlockSpec((tm, 2 * D), lambda i: (i, COL_GATES // (2 * D))),
            row(D),
            pl.BlockSpec((1, 1, mod3.shape[-1]), lambda i: ((i * tm) // seq, 0, 0)),
            full(g_post), full(g_pre), full(w_na), full(w_mla), full(w_out), full(w_r), full(b_r),
        ],
        out_specs=(row(D), pl.BlockSpec((tm * (D // LANES), LANES), lambda i: (i, 0)), row(LANES)),
        compiler_params=_cparams(("arbitrary",)),
        name="out_proj",
    )(o_na, o_mla, proj, x2, mod3, g_post, g_pre, w_na, w_mla, w_out, w_r, b_r)


def _route_kernel(lg_ref, r_ref, cnt_ref, carry_scr, *, sub):
    @pl.when(pl.program_id(0) == 0)
    def _():
        carry_scr[...] = jnp.zeros(carry_scr.shape, jnp.float32)

    tr = lg_ref.shape[0]
    lane = lax.broadcasted_iota(jnp.int32, (sub, LANES), 1).astype(jnp.float32)
    ri = lax.broadcasted_iota(jnp.int32, (sub, sub), 0)
    ci = lax.broadcasted_iota(jnp.int32, (sub, sub), 1)
    tri = jnp.where(ri >= ci, 1.0, 0.0).astype(jnp.bfloat16)
    for s0 in range(0, tr, sub):
        work = lg_ref[s0:s0 + sub, :]
        sels, vals, idxs = [], [], []
        for _k in range(TOP_K):
            mk = jnp.max(work, axis=-1, keepdims=True)
            ik = jnp.min(jnp.where(work == mk, lane, float(LANES)), axis=-1, keepdims=True)
            sk = lane == ik
            work = jnp.where(sk, -jnp.inf, work)
            sels.append(sk)
            vals.append(mk)
            idxs.append(ik)
        es = [jnp.exp(v - vals[0]) for v in vals]
        denom = es[0] + es[1] + es[2] + es[3]
        onehot = jnp.zeros((sub, LANES), jnp.float32)
        for sk in sels:
            onehot = jnp.where(sk, 1.0, onehot)
        prefix = _dot(tri, _bf16(onehot))
        carry = carry_scr[...]
        rank_mat = carry + prefix - 1.0
        res = jnp.zeros((sub, LANES), jnp.float32)
        for kk in range(TOP_K):
            rank_k = jnp.sum(jnp.where(sels[kk], rank_mat, 0.0), axis=-1, keepdims=True)
            res = jnp.where(lane == kk, idxs[kk], res)
            res = jnp.where(lane == TOP_K + kk, rank_k, res)
            res = jnp.where(lane == 2 * TOP_K + kk, es[kk] / denom, res)
        r_ref[s0:s0 + sub, :] = res
        carry_scr[...] = carry + jnp.sum(onehot, axis=0, keepdims=True)
    cnt_ref[...] = carry_scr[...]


def _route(logits, tr=2048, sub=256):
    T = logits.shape[0]
    kern = functools.partial(_route_kernel, sub=sub)
    return pl.pallas_call(
        kern,
        out_shape=(
            jax.ShapeDtypeStruct((T, LANES), jnp.float32),
            jax.ShapeDtypeStruct((1, LANES), jnp.float32),
        ),
        grid=(T // tr,),
        in_specs=[pl.BlockSpec((tr, LANES), lambda i: (i, 0))],
        out_specs=(
            pl.BlockSpec((tr, LANES), lambda i: (i, 0)),
            pl.BlockSpec((1, LANES), lambda i: (0, 0)),
        ),
        scratch_shapes=[pltpu.VMEM((1, LANES), jnp.float32)],
        compiler_params=_cparams(("arbitrary",)),
        name="route",
    )(logits)


def _dispatch_kernel(zero_blk_ref, dest_ref, h_ref, row0_ref, xs_ref, row_ref, zbuf, sem, fill_sem,
                     *, n_tokens):
    i = pl.program_id(0)
    ts = h_ref.shape[0] // SUBLANES
    blk_rows = MOE_BLOCK * SUBLANES

    @pl.when(i == 0)
    def _():
        fill_row = pltpu.make_async_copy(row0_ref, row_ref, fill_sem.at[0])
        fill_row.start()
        zbuf[...] = jnp.zeros(zbuf.shape, zbuf.dtype)
        n_zero = zero_blk_ref.shape[0]

        def zero_copy(z):
            r0 = pl.multiple_of(zero_blk_ref[z] * blk_rows, blk_rows)
            return pltpu.make_async_copy(zbuf, xs_ref.at[pl.ds(r0, blk_rows), :], fill_sem.at[1])

        for z in range(n_zero):
            pl.when(zero_blk_ref[z] >= 0)(lambda z=z: zero_copy(z).start())
        for z in range(n_zero):
            pl.when(zero_blk_ref[z] >= 0)(lambda z=z: zero_copy(z).wait())
        fill_row.wait()

    def row_copy(j, d):
        dst = pl.ds(pl.multiple_of(d * SUBLANES, SUBLANES), SUBLANES)
        return pltpu.make_async_copy(h_ref.at[pl.ds(j * SUBLANES, SUBLANES), :], xs_ref.at[dst, :], sem)

    t0 = i * ts
    for j in range(ts):
        for kk in range(TOP_K):
            d = dest_ref[j * TOP_K + kk]
            row_copy(j, d).start(priority=kk % 2)
            row_ref[d] = (kk * n_tokens + j) + t0
    for _j in range(ts * TOP_K):
        row_copy(0, 0).wait()


def _dispatch(h2_tiles, dest, zero_blk, cap, ts=256):
    T = h2_tiles.shape[0] // SUBLANES
    kern = functools.partial(_dispatch_kernel, n_tokens=T)
    grid_spec = pltpu.PrefetchScalarGridSpec(
        num_scalar_prefetch=1,
        grid=(T // ts,),
        in_specs=[
            pl.BlockSpec((ts * TOP_K,), lambda i, zb: (i,), memory_space=pltpu.SMEM),
            pl.BlockSpec((ts * SUBLANES, LANES), lambda i, zb: (i, 0)),
            pl.BlockSpec(memory_space=pl.ANY),
        ],
        out_specs=(pl.BlockSpec(memory_space=pl.ANY), pl.BlockSpec(memory_space=pltpu.SMEM)),
        scratch_shapes=[
            pltpu.VMEM((MOE_BLOCK * SUBLANES, LANES), h2_tiles.dtype),
            pltpu.SemaphoreType.DMA(()),
            pltpu.SemaphoreType.DMA((2,)),
        ],
    )
    return pl.pallas_call(
        kern,
        out_shape=(jax.ShapeDtypeStruct((cap * SUBLANES, LANES), h2_tiles.dtype),
                   jax.ShapeDtypeStruct((cap,), jnp.int32)),
        grid_spec=grid_spec,
        compiler_params=_cparams(("arbitrary",)),
        name="dispatch",
    )(zero_blk, dest, h2_tiles, jnp.full((cap,), -1, jnp.int32))


def _moe_kernel(blk_e_ref, nvalid_ref, row_prev_ref, row_ref, x_ref, w1_ref, b1_ref, w2_ref, b2_ref,
                out_ref, w1p_scr, w2b_scr, ybuf0, ybuf1, ssem, *, n_items):
    i = pl.program_id(0)
    nvalid = nvalid_ref[0]
    d_ff = w2_ref.shape[1]
    n_groups = (2 * d_ff) // (2 * LANES)

    def tile(r):
        return pl.ds(pl.multiple_of(r * SUBLANES, SUBLANES), SUBLANES)

    ybufs = (ybuf0, ybuf1)

    def scatter_copy(j, r, slot):
        return pltpu.make_async_copy(ybufs[slot].at[pl.ds(j * SUBLANES, SUBLANES), :], out_ref.at[tile(r), :],
                                     ssem.at[slot])

    def wait_scatters(slot):
        for _j in range(MOE_BLOCK):
            scatter_copy(0, 0, slot).wait()

    def scatter_rows(rows_ref, slot):
        spare0 = n_items + slot * MOE_BLOCK
        for j in range(MOE_BLOCK):
            r = rows_ref[0, 0, j]
            scatter_copy(j, jnp.where(r >= 0, r, spare0 + j), slot).start(priority=j % 2)

    @pl.when(i == 0)
    def _():
        ybuf0[...] = jnp.zeros(ybuf0.shape, ybuf0.dtype)
        ybuf1[...] = jnp.zeros(ybuf1.shape, ybuf1.dtype)
        for j in range(MOE_BLOCK):
            scatter_copy(j, n_items + j, 0).start(priority=j % 2)

    @pl.when(i < nvalid)
    def _():
        e = blk_e_ref[i]
        e_prev = blk_e_ref[jnp.maximum(i - 1, 0)]

        @pl.when((i == 0) | (e != e_prev))
        def _():
            r = lax.broadcasted_iota(jnp.int32, (2 * LANES, 2 * LANES), 0)
            c = lax.broadcasted_iota(jnp.int32, (2 * LANES, 2 * LANES), 1)
            src = jnp.where(c < LANES, 2 * c, 2 * (c - LANES) + 1)
            perm = jnp.where(r == src, 1.0, 0.0).astype(jnp.bfloat16)
            for gI in range(n_groups):
                sl = slice(gI * 2 * LANES, (gI + 1) * 2 * LANES)
                w1p_scr[:, sl] = _bf16(_dot(_bf16(w1_ref[0, :, sl]), perm))
            w2b_scr[...] = _bf16(w2_ref[0])

        def block_step(slot):
            wait_scatters(slot)
            scatter_rows(row_prev_ref, 1 - slot)

            xb = _bf16(_load_row_tiles(x_ref, MOE_BLOCK))
            hcat = _dot(xb, w1p_scr[...]) + b1_ref[0]
            acts = []
            for gI in range(n_groups):
                glu = jnp.minimum(hcat[:, gI * 2 * LANES: gI * 2 * LANES + LANES], SWIGLU_LIMIT)
                lin = jnp.clip(hcat[:, gI * 2 * LANES + LANES: (gI + 1) * 2 * LANES],
                               -SWIGLU_LIMIT, SWIGLU_LIMIT)
                acts.append(_bf16(glu * jax.nn.sigmoid(SWIGLU_ALPHA * glu) * (lin + 1.0)))
            act = jnp.concatenate(acts, axis=-1)
            _store_row_tiles(ybufs[slot], _dot(act, w2b_scr[...]) + b2_ref[0])

            @pl.when(i == nvalid - 1)
            def _():
                scatter_rows(row_ref, slot)
                wait_scatters(1 - slot)
                wait_scatters(slot)

        parity = lax.rem(i, 2)
        for slot in range(2):
            pl.when(parity == slot)(functools.partial(block_step, slot))


def _moe_ffn(xs, slot_row, n_items, blk_e, nvalid, w1, b1p, w2, b2):
    D = w1.shape[1]
    assert D == SUBLANES * LANES
    E, _, F2 = w1.shape
    F = w2.shape[1]
    nblk = slot_row.shape[0]

    def cur(i, be, nv):
        return (jnp.minimum(i, nv[0] - 1), 0, 0)

    def prv(i, be, nv):
        return (jnp.where(i == 0, nblk, jnp.maximum(jnp.minimum(i, nv[0] - 1) - 1, 0)), 0, 0)

    slot_row = jnp.concatenate([slot_row, jnp.full((1, 1, MOE_BLOCK), -1, slot_row.dtype)], axis=0)
    slot_spec = lambda f: pl.BlockSpec((1, 1, MOE_BLOCK), f, memory_space=pltpu.SMEM)
    grid_spec = pltpu.PrefetchScalarGridSpec(
        num_scalar_prefetch=2,
        grid=(nblk,),
        in_specs=[
            slot_spec(prv), slot_spec(cur),
            pl.BlockSpec((MOE_BLOCK * SUBLANES, LANES), lambda i, be, nv: (jnp.minimum(i, nv[0] - 1), 0)),
            pl.BlockSpec((1, D, F2), lambda i, be, nv: (be[i], 0, 0)),
            pl.BlockSpec((1, 1, F2), lambda i, be, nv: (be[i], 0, 0)),
            pl.BlockSpec((1, F, D), lambda i, be, nv: (be[i], 0, 0)),
            pl.BlockSpec((1, 1, D), lambda i, be, nv: (be[i], 0, 0)),
        ],
        out_specs=pl.BlockSpec(memory_space=pl.ANY),
        scratch_shapes=[
            pltpu.VMEM((D, F2), jnp.bfloat16),
            pltpu.VMEM((F, D), jnp.bfloat16),
            pltpu.VMEM((MOE_BLOCK * SUBLANES, LANES), jnp.float32),
            pltpu.VMEM((MOE_BLOCK * SUBLANES, LANES), jnp.float32),
            pltpu.SemaphoreType.DMA((2,)),
        ],
    )
    kern = functools.partial(_moe_kernel, n_items=n_items)
    return pl.pallas_call(
        kern,
        out_shape=jax.ShapeDtypeStruct(((n_items + 2 * MOE_BLOCK) * SUBLANES, LANES), jnp.float32),
        grid_spec=grid_spec,
        compiler_params=_cparams(("arbitrary",)),
        name="moe_ffn",
    )(blk_e, nvalid, slot_row, slot_row, xs, w1, b1p, w2, b2)


def _combine_kernel(y0_ref, y1_ref, y2_ref, y3_ref, r_ref, x1_ref, mod_ref, g_ref, o_ref, *, d_model):
    D = d_model
    tc = x1_ref.shape[0]
    r = r_ref[...]
    moe = None
    for kk, y_ref in enumerate((y0_ref, y1_ref, y2_ref, y3_ref)):
        term = r[:, 2 * TOP_K + kk: 2 * TOP_K + kk + 1] * _load_row_tiles(y_ref, tc)
        moe = term if moe is None else moe + term
    gate_f = mod_ref[0, :, 5 * D:6 * D]
    o_ref[...] = x1_ref[...] + gate_f * _rms(moe, g_ref[...])


def _combine(y_k, route, x1, mod3, g_post, seq, tc=512):
    T, D = x1.shape
    n_t = T // tc
    kern = functools.partial(_combine_kernel, d_model=D)
    y_spec = lambda kk: pl.BlockSpec((tc * SUBLANES, LANES), lambda i: (kk * n_t + i, 0))
    return pl.pallas_call(
        kern,
        out_shape=jax.ShapeDtypeStruct((T, D), jnp.float32),
        grid=(n_t,),
        in_specs=[
            y_spec(0), y_spec(1), y_spec(2), y_spec(3),
            pl.BlockSpec((tc, LANES), lambda i: (i, 0)),
            pl.BlockSpec((tc, D), lambda i: (i, 0)),
            pl.BlockSpec((1, 1, mod3.shape[-1]), lambda i: ((i * tc) // seq, 0, 0)),
            pl.BlockSpec((1, D), lambda i: (0, 0)),
        ],
        out_specs=pl.BlockSpec((tc, D), lambda i: (i, 0)),
        compiler_params=_cparams(("arbitrary",)),
        name="combine",
    )(y_k, y_k, y_k, y_k, route, x1, mod3, g_post.reshape(1, D))


def _rope_swap_cols(w):
    nf = MLA_ROPE // 4
    return jnp.concatenate([-w[..., nf:2 * nf], w[..., 0:nf], -w[..., 3 * nf:4 * nf], w[..., 2 * nf:3 * nf]],
                           axis=-1)


def _rope_tables(seq):
    n_rows = seq // GRID_W
    nf = MLA_ROPE // 4
    inv = ROPE_THETA ** (-jnp.arange(nf, dtype=jnp.float32) / nf)
    ar = jnp.arange(n_rows, dtype=jnp.float32)[:, None] * inv
    ac = jnp.arange(GRID_W, dtype=jnp.float32)[:, None] * inv
    cr, sr, cc, sc = jnp.cos(ar), jnp.sin(ar), jnp.cos(ac), jnp.sin(ac)

    def lanes(parts, n):
        z = lambda w: jnp.zeros((n, w), jnp.float32)
        return jnp.concatenate([z(p) if isinstance(p, int) else p for p in parts], axis=-1)

    pad = LANES - MLA_NOPE - MLA_ROPE
    row_tab = lanes([MLA_NOPE, cr, cr, 2 * nf, pad, MLA_NOPE, sr, sr, 2 * nf, pad,
                     cr, cr, 2 * nf, sr, sr, 2 * nf, MLA_NOPE], n_rows)
    col_tab = lanes([jnp.ones((GRID_W, MLA_NOPE), jnp.float32), 2 * nf, cc, cc, pad, MLA_NOPE, 2 * nf, sc, sc, pad,
                     2 * nf, cc, cc, 2 * nf, sc, sc, MLA_NOPE], GRID_W)
    return (row_tab[:, None, :] + col_tab[None, :, :]).reshape(seq, 3 * LANES)


def kernel(x, c, w_ada, b_ada, g_attn_pre, g_attn_post, w_in, b_gate, na_rpb, q_norm_g, kv_norm_g,
           w_uq, w_ukv, w_na_up, w_mla_up, w_out, g_ffn_pre, g_ffn_post, w_router, b_router,
           w1, b1, w2, b2):
    B, S, D = x.shape
    T = B * S
    depth = w_ada.shape[0]
    E = w_router.shape[-1]
    n_rows = S // GRID_W
    assert S % (NA_GROUP_ROWS * GRID_W) == 0 and n_rows >= NA_KEY_ROWS
    assert E <= LANES and T % 2048 == 0

    bf = jnp.bfloat16
    rope_tab = _rope_tables(S)
    pm_np = np.zeros((LANES, LANES), np.float32)
    for cidx in range(MLA_ROPE):
        pm_np[cidx, MLA_NOPE + cidx] = 1.0
        pm_np[MLA_ROPE + cidx, MLA_NOPE + cidx] = 1.0
    pm = jnp.asarray(pm_np, bf)
    qscale = float((MLA_NOPE + MLA_ROPE) ** -0.5 * math.log2(math.e))

    col_na = 2 * D
    col_cq = col_na + 3 * NA_WIDTH
    col_ckv = col_cq + Q_LORA_PAD
    col_kr = col_ckv + MLA_KV_LORA
    assert col_cq % Q_LORA_PAD == 0 and col_ckv % MLA_KV_LORA == 0

    x2 = x.reshape(T, D)
    for l in range(depth):
        mod = _ada_mod(c, w_ada[l], b_ada[l])
        mod3 = mod.reshape(B, 1, 6 * D)

        o_na, o_cq, o_ckv, o_kr, o_g = np.cumsum([0, 3 * NA_WIDTH, MLA_Q_LORA, MLA_KV_LORA, MLA_ROPE]).tolist()
        wi = w_in[l]
        w_kr = wi[:, o_kr:o_g]
        w_ext = jnp.concatenate([
            wi[:, o_g:o_g + 2 * D],
            wi[:, o_na:o_cq],
            wi[:, o_cq:o_ckv], jnp.zeros((D, Q_LORA_PAD - MLA_Q_LORA), wi.dtype),
            wi[:, o_ckv:o_kr],
            w_kr, _rope_swap_cols(w_kr), jnp.zeros((D, LANES - 2 * MLA_ROPE), wi.dtype),
        ], axis=1).astype(bf)
        proj, na_qt, na_vt = _in_proj(x2, mod3, g_attn_pre[l], w_ext, b_gate[l], S,
                                      col_na, col_na + 2 * NA_WIDTH)

        o_na_tok = _na_attention(proj, na_qt, na_vt, _na_pair_tables(na_rpb[l]), B, S, col_na + NA_WIDTH)

        wq = w_uq[l]
        zq = jnp.zeros((MLA_Q_LORA, MLA_HEADS, LANES - MLA_NOPE - MLA_ROPE), wq.dtype)
        wqm = jnp.concatenate([wq, zq], axis=-1).reshape(MLA_Q_LORA, MLA_HEADS * LANES)
        wqs = jnp.concatenate([jnp.zeros((MLA_Q_LORA, MLA_HEADS, MLA_NOPE), wq.dtype),
                               _rope_swap_cols(wq[..., MLA_NOPE:]), zq], axis=-1
                              ).reshape(MLA_Q_LORA, MLA_HEADS * LANES)
        rpad = ((0, Q_LORA_PAD - MLA_Q_LORA), (0, 0))
        wqm = jnp.pad(wqm, rpad).astype(bf)
        wqs = jnp.pad(wqs, rpad).astype(bf)
        wkv = w_ukv[l]
        wk = jnp.concatenate([wkv[..., :MLA_NOPE], jnp.zeros((MLA_KV_LORA, MLA_HEADS, LANES - MLA_NOPE), wkv.dtype)],
                             axis=-1).reshape(MLA_KV_LORA, MLA_HEADS * LANES).astype(bf)
        wv = wkv[..., MLA_NOPE:].reshape(MLA_KV_LORA, MLA_WIDTH).astype(bf)
        gq = jnp.pad(q_norm_g[l], (0, Q_LORA_PAD - MLA_Q_LORA)).reshape(1, Q_LORA_PAD)
        gkv = kv_norm_g[l].reshape(1, MLA_KV_LORA)
        q_cat, k_cat, v_mla = _mla_prep(proj, col_cq, col_ckv, col_kr, gq, gkv, wqm, wqs, wk, wv, pm,
                                        rope_tab, S, qscale)
        o_mla_tok = _mla_attention(q_cat, k_cat, v_mla, B, S)

        w_r = jnp.pad(w_router[l], ((0, 0), (0, LANES - E))).astype(bf)
        b_r = jnp.concatenate([b_router[l], jnp.full((LANES - E,), NEG_BIG, jnp.float32)]).reshape(1, LANES)
        x1, h2, logits = _out_proj(o_na_tok, o_mla_tok, proj, x2, mod3, g_attn_post[l], g_ffn_pre[l],
                                   w_na_up[l].astype(bf), w_mla_up[l].astype(bf), w_out[l].astype(bf),
                                   w_r, b_r, S)

        route, counts = _route(logits)
        e_idx = route[:, 0:TOP_K].astype(jnp.int32)
        rank = route[:, TOP_K:2 * TOP_K].astype(jnp.int32)
        cnt = counts[0, :E].astype(jnp.int32)
        padded = ((cnt + MOE_BLOCK - 1) // MOE_BLOCK) * MOE_BLOCK
        pend = jnp.cumsum(padded)
        pstart = pend - padded
        n_items = T * TOP_K
        nblk = -(-n_items // MOE_BLOCK) + E
        cap = nblk * MOE_BLOCK
        dest = (rank + jnp.sum(jnp.where(e_idx[..., None] == jnp.arange(E, dtype=jnp.int32), pstart, 0), axis=-1)
                ).reshape(n_items)
        blk_off = jnp.arange(nblk, dtype=jnp.int32) * MOE_BLOCK
        blk_e = jnp.minimum(jnp.sum((pend[None, :] <= blk_off[:, None]).astype(jnp.int32), axis=-1), E - 1)
        nvalid = (pend[-1:] // MOE_BLOCK).astype(jnp.int32)

        last_blk = jnp.where(padded > 0, pend // MOE_BLOCK - 1, -1)
        tail_blk = nvalid[0] + jnp.arange(nblk - n_items // MOE_BLOCK, dtype=jnp.int32)
        zero_blk = jnp.concatenate([last_blk, jnp.where(tail_blk < nblk, tail_blk, -1)]).astype(jnp.int32)
        xs, slot_row = _dispatch(h2, dest, zero_blk, cap)
        slot_row = slot_row.reshape(nblk, 1, MOE_BLOCK)
        F2 = w1.shape[-1]
        b1p = b1[l].reshape(E, F2 // (2 * LANES), LANES, 2).transpose(0, 1, 3, 2).reshape(E, 1, F2)
        y_k = _moe_ffn(xs, slot_row, n_items, blk_e, nvalid, w1[l], b1p, w2[l], b2[l].reshape(E, 1, D))
        x2 = _combine(y_k, route, x1, mod3, g_ffn_post[l], S)
    return x2.reshape(B, S, D)
```
